```python
import math
import jax, jax.numpy as jnp
from jax import lax
import numpy as np

D_MODEL = 2048
BATCH = 8
SEQ = 2048
DEPTH = 4

CTX_LEN = 256
GRID_W = 64
EPS = 1e-6
N_ADA = 6

GLA_HEADS = 4
GLA_DK = D_MODEL // 2
GLA_DV = D_MODEL
GLA_DK_HEAD = GLA_DK // GLA_HEADS
GLA_DV_HEAD = GLA_DV // GLA_HEADS
GLA_GATE_RANK = 16
GLA_GATE_TEMP = 16.0
GLA_CHUNK = 64
GLA_IN = 2 * GLA_DK + 2 * GLA_DV

HY_BANDS = 16
HY_EMB_DIM = 1 + 2 * HY_BANDS
HY_FILTER_HIDDEN = 64
HY_SIN_FREQ = 1.0
HY_FAST_DECAY_PCT = 0.3
HY_SLOW_DECAY_PCT = 1.5
HY_DECAY_TARGET = 1e-2

N_EXPERTS = 16
EXPERT_FF = D_MODEL // 2
CAPACITY_FACTOR = 2

N_GLA = (DEPTH + 1) // 2
N_HYENA = DEPTH // 2

kernel_name = "hybrid_gla_hyena_ecmoe_dit"


def rms_norm(x, g):
    xf = x.astype(jnp.float32)
    y = xf * lax.rsqrt(jnp.mean(xf * xf, axis=-1, keepdims=True) + EPS)
    return (y * g.astype(jnp.float32)).astype(x.dtype)


def modulate(h, shift, scale):
    return h * (1 + scale) + shift


def gla_project(h, w_in, w_a1, w_a2, b_a):
    B, L, _ = h.shape
    proj = h @ w_in
    q, k, v, g_out = jnp.split(proj, [GLA_DK, 2 * GLA_DK, 2 * GLA_DK + GLA_DV], axis=-1)

    def heads(t, d):
        return t.reshape(B, L, GLA_HEADS, d).transpose(0, 2, 1, 3).astype(jnp.float32)

    low = jnp.einsum('bld,zdr->zblr', h, w_a1)
    logit = jnp.einsum('zblr,zrk->zblk', low, w_a2) + b_a[:, None, None, :]
    log_alpha = jax.nn.log_sigmoid(logit.astype(jnp.float32)) / GLA_GATE_TEMP
    q = heads(q, GLA_DK_HEAD) * (GLA_DK_HEAD ** -0.5)
    return (q, heads(k, GLA_DK_HEAD), heads(v, GLA_DV_HEAD),
            heads(log_alpha[0], GLA_DK_HEAD), heads(log_alpha[1], GLA_DK_HEAD), g_out)


def gla_scan(q, k, v, la, s0):
    B, H, L, _ = q.shape
    dv = v.shape[-1]
    n = L // GLA_CHUNK

    def to_chunks(t):
        return t.reshape(B, H, n, GLA_CHUNK, t.shape[-1]).transpose(2, 0, 1, 3, 4)

    mask = jnp.tril(jnp.ones((GLA_CHUNK, GLA_CHUNK), dtype=bool))[:, :, None]

    def step(S, inp):
        qc, kc, vc, gc = inp
        b = jnp.cumsum(gc, axis=-2)
        o_inter = jnp.einsum('bhcd,bhde->bhce', qc * jnp.exp(b), S)
        diff = b[:, :, :, None, :] - b[:, :, None, :, :]
        decay = jnp.exp(jnp.where(mask, diff, -jnp.inf))
        A = jnp.einsum('bhid,bhjd,bhijd->bhij', qc, kc, decay)
        o_intra = jnp.einsum('bhij,bhje->bhie', A, vc)
        b_last = b[:, :, -1:, :]
        S_new = jnp.exp(b_last[:, :, 0, :])[..., None] * S + jnp.einsum(
            'bhjd,bhje->bhde', kc * jnp.exp(b_last - b), vc)
        return S_new, o_inter + o_intra

    s_fin, o = lax.scan(step, s0, (to_chunks(q), to_chunks(k), to_chunks(v), to_chunks(la)))
    return s_fin, o.transpose(1, 2, 0, 3, 4).reshape(B, H, L, dv)


def gla_bidir(q, k, v, la_f, la_b, s0_f, s0_b):
    s_f, o_f = gla_scan(q, k, v, la_f, s0_f)
    fl = lambda t: jnp.flip(t, axis=2)
    s_b, o_b = gla_scan(fl(q), fl(k), fl(v), fl(la_b), s0_b)
    return o_f + fl(o_b), s_f, s_b


def gla_output(o, g_out, head_norm, w_out, dtype):
    B, H, L, _ = o.shape
    o = o * lax.rsqrt(jnp.mean(o * o, axis=-1, keepdims=True) + EPS) * head_norm.astype(jnp.float32)
    o = o.transpose(0, 2, 1, 3).reshape(B, L, H * GLA_DV_HEAD).astype(dtype)
    return (o * jax.nn.silu(g_out)) @ w_out


def gla_mixer(h_lat, h_ctx, w_in, w_a1, w_a2, b_a, head_norm, w_out, need_ctx):
    B = h_lat.shape[0]
    zeros = jnp.zeros((B, GLA_HEADS, GLA_DK_HEAD, GLA_DV_HEAD), jnp.float32)
    qc, kc, vc, lfc, lbc, gc = gla_project(h_ctx, w_in, w_a1, w_a2, b_a)
    o_ctx, s_f, s_b = gla_bidir(qc, kc, vc, lfc, lbc, zeros, zeros)
    ql, kl, vl, lfl, lbl, gl = gla_project(h_lat, w_in, w_a1, w_a2, b_a)
    o_lat, _, _ = gla_bidir(ql, kl, vl, lfl, lbl, s_f, s_b)
    out_lat = gla_output(o_lat, gl, head_norm, w_out, h_lat.dtype)
    out_ctx = gla_output(o_ctx, gc, head_norm, w_out, h_ctx.dtype) if need_ctx else None
    return out_lat, out_ctx


def conv3_centred(u, w, b):
    up = jnp.pad(u, [(0, 0)] * (u.ndim - 2) + [(1, 1), (0, 0)])
    return up[..., :-2, :] * w[0] + up[..., 1:-1, :] * w[1] + up[..., 2:, :] * w[2] + b


def hyena_filter(L, f_w1, f_b1, f_w2, f_b2, f_w3, f_b3, f_w4):
    f32 = jnp.float32
    t = jnp.linspace(0.0, 1.0, L, dtype=f32)[:, None]
    w = 2 * math.pi * jnp.arange(L, dtype=f32)[:, None] / L
    f = jnp.linspace(1e-4, HY_BANDS - 1, HY_BANDS, dtype=f32)[None, :]
    z = jnp.concatenate([t, jnp.cos(f * w), -jnp.sin(f * w)], axis=-1)
    hdn = jnp.sin(HY_SIN_FREQ * (z @ f_w1.astype(f32) + f_b1.astype(f32)))
    hdn = jnp.sin(HY_SIN_FREQ * (hdn @ f_w2.astype(f32) + f_b2.astype(f32)))
    hdn = jnp.sin(HY_SIN_FREQ * (hdn @ f_w3.astype(f32) + f_b3.astype(f32)))
    filt = hdn @ f_w4.astype(f32)
    D = filt.shape[-1] // 2
    max_decay = math.log(HY_DECAY_TARGET) / HY_FAST_DECAY_PCT
    min_decay = math.log(HY_DECAY_TARGET) / HY_SLOW_DECAY_PCT
    deltas = jnp.abs(jnp.linspace(min_decay, max_decay, D, dtype=f32))
    decay = jnp.exp(-t * deltas[None, :])
    h_fwd = filt[:, :D] * decay
    h_bwd = filt[:, D:] * decay
    k_full = jnp.concatenate([h_fwd, jnp.zeros((1, D), f32), h_bwd[1:][::-1]], axis=0)
    return k_full / jnp.sum(jnp.abs(k_full), axis=0, keepdims=True)


def fft_long_conv(u, k_full, skip):
    L = u.shape[1]
    uf = u.astype(jnp.float32)
    U = jnp.fft.rfft(uf, n=2 * L, axis=1)
    K = jnp.fft.rfft(k_full, n=2 * L, axis=0)
    y = jnp.fft.irfft(U * K[None], n=2 * L, axis=1)[:, :L]
    return (y + uf * skip.astype(jnp.float32)).astype(u.dtype)


def hyena_mixer(h, w_in, b_in, conv_w, conv_b, f_w1, f_b1, f_w2, f_b2, f_w3, f_b3, f_w4, skip,
                w_out, b_out, grid):
    B, L, D = h.shape
    proj = h @ w_in + b_in
    if grid:
        rows = L // GRID_W
        proj = conv3_centred(proj.reshape(B, rows, GRID_W, 3 * D), conv_w, conv_b).reshape(B, L, 3 * D)
    else:
        proj = conv3_centred(proj, conv_w, conv_b)
    x0, x1, v = jnp.split(proj, 3, axis=-1)
    k_full = hyena_filter(L, f_w1, f_b1, f_w2, f_b2, f_w3, f_b3, f_w4)
    y = x0 * fft_long_conv(x1 * v, k_full, skip)
    return y @ w_out + b_out


def ec_moe(h, w_router, w1, w3, w2):
    B, L, D = h.shape
    cap = CAPACITY_FACTOR * L // N_EXPERTS
    s = jax.nn.softmax((h @ w_router).astype(jnp.float32), axis=-1)
    top_s, idx = lax.top_k(jnp.swapaxes(s, 1, 2), cap)
    xs = jax.vmap(lambda hb, ib: hb[ib])(h, idx)
    a = jax.nn.silu(jnp.einsum('becd,edf->becf', xs, w1)) * jnp.einsum('becd,edf->becf', xs, w3)
    y = jnp.einsum('becf,efd->becd', a, w2) * top_s[..., None].astype(h.dtype)
    return jax.vmap(lambda ib, yb: jnp.zeros((L, D), yb.dtype).at[ib.reshape(-1)].add(yb.reshape(-1, D)))(idx, y)


def setup_inputs(seed: int = 0) -> dict:
    key = jax.random.key(seed)
    it = iter(jax.random.split(key, 40))
    D, F, E = D_MODEL, EXPERT_FF, N_EXPERTS
    nrm = lambda shape, scale: jax.random.normal(next(it), shape, jnp.float32) * scale
    return {
        "x": nrm((BATCH, SEQ, D), 1.0),
        "c": nrm((BATCH, D), 1.0),
        "ctx": nrm((BATCH, CTX_LEN, D), 1.0),
        "c_ctx": nrm((D,), 1.0),
        "w_ada": nrm((DEPTH, D, N_ADA * D), 0.5 * D ** -0.5),
        "b_ada": nrm((DEPTH, N_ADA * D), 0.02),
        "norm_mix": 1.0 + nrm((DEPTH, D), 0.02),
        "norm_ffn": 1.0 + nrm((DEPTH, D), 0.02),
        "norm_final": 1.0 + nrm((D,), 0.02),
        "gla_w_in": nrm((N_GLA, D, GLA_IN), D ** -0.5),
        "gla_w_a1": nrm((N_GLA, 2, D, GLA_GATE_RANK), D ** -0.5),
        "gla_w_a2": nrm((N_GLA, 2, GLA_GATE_RANK, GLA_DK), GLA_GATE_RANK ** -0.5),
        "gla_b_a": nrm((N_GLA, 2, GLA_DK), 0.1),
        "gla_head_norm": 1.0 + nrm((N_GLA, GLA_DV_HEAD), 0.02),
        "gla_w_out": nrm((N_GLA, GLA_DV, D), GLA_DV ** -0.5),
        "hy_w_in": nrm((N_HYENA, D, 3 * D), D ** -0.5),
        "hy_b_in": nrm((N_HYENA, 3 * D), 0.02),
        "hy_conv_w": nrm((N_HYENA, 3, 3 * D), 3 ** -0.5),
        "hy_conv_b": nrm((N_HYENA, 3 * D), 0.02),
        "hy_f_w1": nrm((N_HYENA, HY_EMB_DIM, HY_FILTER_HIDDEN), HY_EMB_DIM ** -0.5),
        "hy_f_b1": nrm((N_HYENA, HY_FILTER_HIDDEN), 0.1),
        "hy_f_w2": nrm((N_HYENA, HY_FILTER_HIDDEN, HY_FILTER_HIDDEN), HY_FILTER_HIDDEN ** -0.5),
        "hy_f_b2": nrm((N_HYENA, HY_FILTER_HIDDEN), 0.1),
        "hy_f_w3": nrm((N_HYENA, HY_FILTER_HIDDEN, HY_FILTER_HIDDEN), HY_FILTER_HIDDEN ** -0.5),
        "hy_f_b3": nrm((N_HYENA, HY_FILTER_HIDDEN), 0.1),
        "hy_f_w4": nrm((N_HYENA, HY_FILTER_HIDDEN, 2 * D), HY_FILTER_HIDDEN ** -0.5),
        "hy_skip": nrm((N_HYENA, D), 0.1),
        "hy_w_out": nrm((N_HYENA, D, D), D ** -0.5),
        "hy_b_out": nrm((N_HYENA, D), 0.02),
        "moe_router": nrm((DEPTH, D, E), D ** -0.5),
        "moe_w1": nrm((DEPTH, E, D, F), D ** -0.5),
        "moe_w3": nrm((DEPTH, E, D, F), D ** -0.5),
        "moe_w2": nrm((DEPTH, E, F, D), F ** -0.5),
    }


def reference(x, c, ctx, c_ctx, w_ada, b_ada, norm_mix, norm_ffn, norm_final,
              gla_w_in, gla_w_a1, gla_w_a2, gla_b_a, gla_head_norm, gla_w_out,
              hy_w_in, hy_b_in, hy_conv_w, hy_conv_b, hy_f_w1, hy_f_b1, hy_f_w2, hy_f_b2,
              hy_f_w3, hy_f_b3, hy_f_w4, hy_skip, hy_w_out, hy_b_out,
              moe_router, moe_w1, moe_w3, moe_w2):
    silu_c = jax.nn.silu(c)
    silu_cc = jax.nn.silu(c_ctx)
    for i in range(DEPTH):
        need_ctx = i < DEPTH - 1
        mod = silu_c @ w_ada[i] + b_ada[i]
        mod_c = silu_cc @ w_ada[i] + b_ada[i]
        sh1, sc1, g1, sh2, sc2, g2 = jnp.split(mod[:, None, :], N_ADA, axis=-1)
        csh1, csc1, cg1, csh2, csc2, cg2 = jnp.split(mod_c, N_ADA, axis=-1)

        h = modulate(rms_norm(x, norm_mix[i]), sh1, sc1)
        hc = modulate(rms_norm(ctx, norm_mix[i]), csh1, csc1)
        j = i // 2
        if i % 2 == 0:
            o_lat, o_ctx = gla_mixer(h, hc, gla_w_in[j], gla_w_a1[j], gla_w_a2[j], gla_b_a[j],
                                     gla_head_norm[j], gla_w_out[j], need_ctx)
        else:
            hy = (hy_w_in[j], hy_b_in[j], hy_conv_w[j], hy_conv_b[j], hy_f_w1[j], hy_f_b1[j],
                  hy_f_w2[j], hy_f_b2[j], hy_f_w3[j], hy_f_b3[j], hy_f_w4[j], hy_skip[j],
                  hy_w_out[j], hy_b_out[j])
            o_lat = hyena_mixer(h, *hy, grid=True)
            o_ctx = hyena_mixer(hc, *hy, grid=False) if need_ctx else None
        x = x + g1 * o_lat
        if need_ctx:
            ctx = ctx + cg1 * o_ctx

        h = modulate(rms_norm(x, norm_ffn[i]), sh2, sc2)
        x = x + g2 * ec_moe(h, moe_router[i], moe_w1[i], moe_w3[i], moe_w2[i])
        if need_ctx:
            hc = modulate(rms_norm(ctx, norm_ffn[i]), csh2, csc2)
            ctx = ctx + cg2 * ec_moe(hc, moe_router[i], moe_w1[i], moe_w3[i], moe_w2[i])
    return rms_norm(x, norm_final)
```

```python
import functools
import math

import jax
import jax.numpy as jnp
import numpy as np
from jax import lax
from jax.experimental import pallas as pl
from jax.experimental.pallas import tpu as pltpu

D_MODEL = 2048
BATCH = 8
SEQ = 2048
DEPTH = 4
CTX_LEN = 256
GRID_W = 64
EPS = 1e-6
N_ADA = 6

GLA_HEADS = 4
GLA_DK = D_MODEL // 2
GLA_DV = D_MODEL
GLA_DK_HEAD = GLA_DK // GLA_HEADS
GLA_DV_HEAD = GLA_DV // GLA_HEADS
GLA_GATE_RANK = 16
GLA_GATE_TEMP = 16.0
GLA_IN = 2 * GLA_DK + 2 * GLA_DV

HY_BANDS = 16
HY_SIN_FREQ = 1.0
HY_FAST_DECAY_PCT = 0.3
HY_SLOW_DECAY_PCT = 1.5
HY_DECAY_TARGET = 1e-2

N_EXPERTS = 16
EXPERT_FF = D_MODEL // 2
CAPACITY_FACTOR = 2

S_ROWS = CTX_LEN + SEQ
M_ROWS = BATCH * S_ROWS
CTX_GROUP = BATCH
N_GROUPS = 16
ROW_BLK = 256
BLKS = S_ROWS // ROW_BLK
MM_TM = S_ROWS // 2
GLA_CHUNK = 64
GLA_LEVELS = 6
CAP_LAT = CAPACITY_FACTOR * SEQ // N_EXPERTS
CAP_CTX = CAPACITY_FACTOR * CTX_LEN // N_EXPERTS
EXP_ROWS = BATCH * (CAP_LAT + CAP_CTX)
FFN_TM = EXP_ROWS // 4

F32 = jnp.float32
BF16 = jnp.bfloat16
MIB = 1024 * 1024


def _cparams(sem, vmem_mib):
    return pltpu.CompilerParams(dimension_semantics=sem, vmem_limit_bytes=vmem_mib * MIB)


def _ada_kernel(c_ref, w_ref, b_ref, o_ref):
    c = c_ref[...]
    s = (c * jax.nn.sigmoid(c)).astype(BF16)
    o_ref[0] = jnp.dot(s, w_ref[0].astype(BF16), preferred_element_type=F32) + b_ref[0]


def _ada(cond, w_ada, b_ada):
    tn = 1024
    n = N_ADA * D_MODEL
    return pl.pallas_call(
        _ada_kernel,
        grid=(DEPTH, n // tn),
        in_specs=[
            pl.BlockSpec((N_GROUPS, D_MODEL), lambda l, j: (0, 0)),
            pl.BlockSpec((1, D_MODEL, tn), lambda l, j: (l, 0, j)),
            pl.BlockSpec((1, 1, tn), lambda l, j: (l, 0, j)),
        ],
        out_specs=pl.BlockSpec((1, N_GROUPS, tn), lambda l, j: (l, 0, j)),
        out_shape=jax.ShapeDtypeStruct((DEPTH, N_GROUPS, n), F32),
        compiler_params=_cparams(("arbitrary", "arbitrary"), 40),
        name="ada",
    )(cond, w_ada, b_ada.reshape(DEPTH, 1, n))


def _norm_mod_kernel(x_ref, g_ref, m_ref, o_ref, *, sh, sc):
    x = x_ref[0]
    y = x * lax.rsqrt(jnp.mean(x * x, axis=-1, keepdims=True) + EPS)
    y = y * g_ref[...]
    o_ref[0] = (y * (1.0 + m_ref[0, sc:sc + 1, :]) + m_ref[0, sh:sh + 1, :]).astype(o_ref.dtype)


def _group_of_block(b, t):
    return jnp.where(t == 0, CTX_GROUP, b)


def _norm_mod(xs, g, mod, sh, sc):
    return pl.pallas_call(
        functools.partial(_norm_mod_kernel, sh=sh, sc=sc),
        grid=(BATCH, BLKS),
        in_specs=[
            pl.BlockSpec((1, ROW_BLK, D_MODEL), lambda b, t: (b, t, 0)),
            pl.BlockSpec((1, D_MODEL), lambda b, t: (0, 0)),
            pl.BlockSpec((1, N_ADA, D_MODEL), lambda b, t: (_group_of_block(b, t), 0, 0)),
        ],
        out_specs=pl.BlockSpec((1, ROW_BLK, D_MODEL), lambda b, t: (b, t, 0)),
        out_shape=jax.ShapeDtypeStruct((BATCH, S_ROWS, D_MODEL), BF16),
        compiler_params=_cparams(("arbitrary", "arbitrary"), 32),
        name="norm_mod",
    )(xs, g.reshape(1, D_MODEL), mod)


def _final_norm_kernel(x_ref, g_ref, o_ref):
    x = x_ref[0]
    y = x * lax.rsqrt(jnp.mean(x * x, axis=-1, keepdims=True) + EPS)
    o_ref[0] = y * g_ref[...]


def _final_norm(xs, g):
    off = CTX_LEN // ROW_BLK
    return pl.pallas_call(
        _final_norm_kernel,
        grid=(BATCH, SEQ // ROW_BLK),
        in_specs=[
            pl.BlockSpec((1, ROW_BLK, D_MODEL), lambda b, t: (b, t + off, 0)),
            pl.BlockSpec((1, D_MODEL), lambda b, t: (0, 0)),
        ],
        out_specs=pl.BlockSpec((1, ROW_BLK, D_MODEL), lambda b, t: (b, t, 0)),
        out_shape=jax.ShapeDtypeStruct((BATCH, SEQ, D_MODEL), F32),
        compiler_params=_cparams(("arbitrary", "arbitrary"), 32),
        name="final_norm",
    )(xs, g.reshape(1, D_MODEL))


def _mm_kernel(*refs, has_bias, gate_idx, tm):
    a_ref, w_ref = refs[0], refs[1]
    pos = 2
    acc = jnp.dot(a_ref[...], w_ref[...], preferred_element_type=F32)
    if has_bias:
        acc = acc + refs[pos][...]
        pos += 1
    if gate_idx is None:
        o_ref = refs[pos]
        o_ref[...] = acc.astype(o_ref.dtype)
    else:
        res_ref, mb_ref, mc_ref, o_ref = refs[pos:pos + 4]
        gate_b = mb_ref[0, gate_idx:gate_idx + 1, :]
        gate_c = mc_ref[0, gate_idx:gate_idx + 1, :]
        rows = lax.broadcasted_iota(jnp.int32, (tm, 1), 0)
        first_half = pl.program_id(1) % (S_ROWS // tm) == 0
        is_ctx = jnp.logical_and(first_half, rows < CTX_LEN)
        o_ref[...] = res_ref[...] + jnp.where(is_ctx, gate_c, gate_b) * acc


def _mm(a, w, bias=None, out_dtype=F32, tn=1024, res=None, mod=None, gate_idx=None):
    m, k = a.shape
    n = w.shape[1]
    tm = MM_TM
    tn = min(tn, n)
    in_specs = [
        pl.BlockSpec((tm, k), lambda j, i: (i, 0)),
        pl.BlockSpec((k, tn), lambda j, i: (0, j)),
    ]
    args = [a, w]
    if bias is not None:
        in_specs.append(pl.BlockSpec((1, tn), lambda j, i: (0, j)))
        args.append(bias.reshape(1, n))
    if gate_idx is not None:
        per = S_ROWS // tm
        in_specs += [
            pl.BlockSpec((tm, tn), lambda j, i: (i, j)),
            pl.BlockSpec((1, N_ADA, tn), lambda j, i: (i // per, 0, j)),
            pl.BlockSpec((1, N_ADA, tn), lambda j, i: (CTX_GROUP, 0, j)),
        ]
        args += [res, mod, mod]
    return pl.pallas_call(
        functools.partial(_mm_kernel, has_bias=bias is not None, gate_idx=gate_idx, tm=tm),
        grid=(n // tn, m // tm),
        in_specs=in_specs,
        out_specs=pl.BlockSpec((tm, tn), lambda j, i: (i, j)),
        out_shape=jax.ShapeDtypeStruct((m, n), out_dtype),
        compiler_params=_cparams(("arbitrary", "arbitrary"), 52),
        name="mm",
    )(*args)


def _gla_la_kernel(low_ref, w2_ref, b_ref, o_ref):
    low = low_ref[...]
    for z in range(2):
        lz = low[:, z * GLA_GATE_RANK:(z + 1) * GLA_GATE_RANK].astype(BF16)
        logit = jnp.dot(lz, w2_ref[z].astype(BF16), preferred_element_type=F32) + b_ref[z]
        log_sig = jnp.minimum(logit, 0.0) - jnp.log1p(jnp.exp(-jnp.abs(logit)))
        o_ref[z] = log_sig / GLA_GATE_TEMP


def _gla_la(low, w_a2, b_a):
    tm = MM_TM
    return pl.pallas_call(
        _gla_la_kernel,
        grid=(M_ROWS // tm,),
        in_specs=[
            pl.BlockSpec((tm, 128), lambda i: (i, 0)),
            pl.BlockSpec((2, GLA_GATE_RANK, GLA_DK), lambda i: (0, 0, 0)),
            pl.BlockSpec((2, 1, GLA_DK), lambda i: (0, 0, 0)),
        ],
        out_specs=pl.BlockSpec((2, tm, GLA_DK), lambda i: (0, i, 0)),
        out_shape=jax.ShapeDtypeStruct((2, M_ROWS, GLA_DK), F32),
        compiler_params=_cparams(("arbitrary",), 48),
        name="gla_la",
    )(low, w_a2, b_a.reshape(2, 1, GLA_DK))


def _gla_tables():
    c = GLA_CHUNK
    t = np.arange(c)
    blocks = [(t[None, :] <= t[:, None]), (t[None, :] > t[:, None])]
    pair = np.full((c, c), GLA_LEVELS + 1, np.int32)
    pair[t, t] = GLA_LEVELS
    for lvl in range(GLA_LEVELS):
        s = c >> (lvl + 1)
        mid = (t // (2 * s)) * 2 * s + s - 1
        second = (t % (2 * s)) >= s
        dq = (t[None, :] > mid[:, None]) & (t[None, :] <= t[:, None])
        dk = (t[None, :] > t[:, None]) & (t[None, :] <= mid[:, None])
        blocks.append(np.where(second[:, None], dq, dk))
        same = (t[:, None] // (2 * s)) == (t[None, :] // (2 * s))
        pair[same & second[:, None] & ~second[None, :]] = lvl
    d_f = np.concatenate(blocks, axis=0).astype(np.float32)
    d_b = np.concatenate([blk[::-1, ::-1] for blk in blocks], axis=0).astype(np.float32)
    return np.stack([d_f, d_b]), np.stack([pair, pair[::-1, ::-1]])


def _gla_chunk(q_ref, k_ref, v_ref, la_ref, d_ref, pair_ref, st_ref, o_ref, r, last_row):
    c = GLA_CHUNK
    rows = pl.ds(r, c)
    la = la_ref[0, 0, rows, :]
    hi = la.astype(BF16)
    lo = (la - hi.astype(F32)).astype(BF16)
    dmat = d_ref[0]
    p = jnp.exp(jnp.dot(dmat, hi, preferred_element_type=F32) + jnp.dot(dmat, lo, preferred_element_type=F32))
    q = q_ref[0, rows, :].astype(F32) * (GLA_DK_HEAD ** -0.5)
    k = k_ref[0, rows, :].astype(F32)
    v = v_ref[0, rows, :]
    nt = (((1,), (1,)), ((), ()))
    st = st_ref[...]
    o = lax.dot_general((q * p[0:c]).astype(BF16), st.astype(BF16), nt, preferred_element_type=F32)
    pair = pair_ref[0]
    a = jnp.where(pair == GLA_LEVELS,
                  lax.dot_general(q.astype(BF16), k.astype(BF16), nt, preferred_element_type=F32), 0.0)
    for lvl in range(GLA_LEVELS):
        pl_ = p[(2 + lvl) * c:(3 + lvl) * c]
        a_l = lax.dot_general((q * pl_).astype(BF16), (k * pl_).astype(BF16), nt, preferred_element_type=F32)
        a = jnp.where(pair == lvl, a_l, a)
    o = o + jnp.dot(a.astype(BF16), v, preferred_element_type=F32)
    o_ref[0, rows, :] = o
    kt = (k * p[c:2 * c]).astype(BF16)
    tn = (((0,), (0,)), ((), ()))
    decay = p[last_row:last_row + 1]
    st_ref[...] = st * decay + lax.dot_general(v, kt, tn, preferred_element_type=F32)


def _gla_scan_kernel(qf, kf, vf, laf, qb, kb, vb, lab, d_ref, pair_ref, of_ref, ob_ref, sf_ref, sb_ref):
    @pl.when(pl.program_id(2) == 0)
    def _():
        sf_ref[...] = jnp.zeros_like(sf_ref)
        sb_ref[...] = jnp.zeros_like(sb_ref)

    n = ROW_BLK // GLA_CHUNK

    def step(i, carry):
        rf = pl.multiple_of(i * GLA_CHUNK, GLA_CHUNK)
        rb = pl.multiple_of((n - 1 - i) * GLA_CHUNK, GLA_CHUNK)
        _gla_chunk(qf, kf, vf, laf, d_ref.at[0:1], pair_ref.at[0:1], sf_ref, of_ref, rf, GLA_CHUNK - 1)
        _gla_chunk(qb, kb, vb, lab, d_ref.at[1:2], pair_ref.at[1:2], sb_ref, ob_ref, rb, 0)
        return carry

    lax.fori_loop(0, n, step, 0)


def _bwd_block(t):
    return jnp.where(t == 0, 0, BLKS - t)


def _gla_scan(proj, la):
    d_np, pair_np = _gla_tables()
    dmat = jnp.asarray(d_np, BF16)
    pair = jnp.asarray(pair_np, jnp.int32)
    kq, kv = GLA_DK_HEAD, GLA_DV_HEAD
    k_off = GLA_DK // kq
    v_off = 2 * GLA_DK // kv
    c = GLA_CHUNK
    spec_f = [
        pl.BlockSpec((1, ROW_BLK, kq), lambda b, h, t: (b, t, h)),
        pl.BlockSpec((1, ROW_BLK, kq), lambda b, h, t: (b, t, k_off + h)),
        pl.BlockSpec((1, ROW_BLK, kv), lambda b, h, t: (b, t, v_off + h)),
        pl.BlockSpec((1, 1, ROW_BLK, kq), lambda b, h, t: (0, b, t, h)),
    ]
    spec_b = [
        pl.BlockSpec((1, ROW_BLK, kq), lambda b, h, t: (b, _bwd_block(t), h)),
        pl.BlockSpec((1, ROW_BLK, kq), lambda b, h, t: (b, _bwd_block(t), k_off + h)),
        pl.BlockSpec((1, ROW_BLK, kv), lambda b, h, t: (b, _bwd_block(t), v_off + h)),
        pl.BlockSpec((1, 1, ROW_BLK, kq), lambda b, h, t: (1, b, _bwd_block(t), h)),
    ]
    const = [
        pl.BlockSpec((2, (2 + GLA_LEVELS) * c, c), lambda b, h, t: (0, 0, 0)),
        pl.BlockSpec((2, c, c), lambda b, h, t: (0, 0, 0)),
    ]
    out_sd = jax.ShapeDtypeStruct((BATCH, S_ROWS, GLA_DV), F32)
    return pl.pallas_call(
        _gla_scan_kernel,
        grid=(BATCH, GLA_HEADS, BLKS),
        in_specs=spec_f + spec_b + const,
        out_specs=[
            pl.BlockSpec((1, ROW_BLK, kv), lambda b, h, t: (b, t, h)),
            pl.BlockSpec((1, ROW_BLK, kv), lambda b, h, t: (b, _bwd_block(t), h)),
        ],
        out_shape=[out_sd, out_sd],
        scratch_shapes=[pltpu.VMEM((kv, kq), F32), pltpu.VMEM((kv, kq), F32)],
        compiler_params=_cparams(("arbitrary", "arbitrary", "arbitrary"), 32),
        name="gla_scan",
    )(proj, proj, proj, la, proj, proj, proj, la, dmat, pair)


def _gla_post_kernel(of_ref, ob_ref, g_ref, hn_ref, o_ref):
    hn = hn_ref[...]
    for h in range(GLA_HEADS):
        cols = slice(h * GLA_DV_HEAD, (h + 1) * GLA_DV_HEAD)
        o = of_ref[0, :, cols] + ob_ref[0, :, cols]
        o = o * lax.rsqrt(jnp.mean(o * o, axis=-1, keepdims=True) + EPS) * hn
        g = g_ref[0, :, cols].astype(F32)
        o_ref[0, :, cols] = (o.astype(BF16) * (g * jax.nn.sigmoid(g)).astype(BF16))


def _gla_post(o_f, o_b, proj, head_norm):
    g_off = 2 * GLA_DK // GLA_DV + 1
    blk = pl.BlockSpec((1, ROW_BLK, GLA_DV), lambda b, t: (b, t, 0))
    return pl.pallas_call(
        _gla_post_kernel,
        grid=(BATCH, BLKS),
        in_specs=[blk, blk,
                  pl.BlockSpec((1, ROW_BLK, GLA_DV), lambda b, t: (b, t, g_off)),
                  pl.BlockSpec((1, GLA_DV_HEAD), lambda b, t: (0, 0))],
        out_specs=blk,
        out_shape=jax.ShapeDtypeStruct((BATCH, S_ROWS, GLA_DV), BF16),
        compiler_params=_cparams(("arbitrary", "arbitrary"), 32),
        name="gla_post",
    )(o_f, o_b, proj, head_norm.reshape(1, GLA_DV_HEAD))


def _gla_layer(xs, h, mod, w_in, w_a1, w_a2, b_a, head_norm, w_out):
    hf = h.reshape(M_ROWS, D_MODEL)
    proj = _mm(hf, w_in.astype(BF16), out_dtype=BF16).reshape(BATCH, S_ROWS, GLA_IN)
    w_low = jnp.zeros((D_MODEL, 128), F32)
    w_low = w_low.at[:, :GLA_GATE_RANK].set(w_a1[0]).at[:, GLA_GATE_RANK:2 * GLA_GATE_RANK].set(w_a1[1])
    low = _mm(hf, w_low.astype(BF16), tn=128)
    la = _gla_la(low, w_a2, b_a).reshape(2, BATCH, S_ROWS, GLA_DK)
    o_f, o_b = _gla_scan(proj, la)
    y = _gla_post(o_f, o_b, proj, head_norm)
    out = _mm(y.reshape(M_ROWS, GLA_DV), w_out.astype(BF16), res=xs.reshape(M_ROWS, D_MODEL), mod=mod, gate_idx=2)
    return out.reshape(BATCH, S_ROWS, D_MODEL)


def _hy_pre_kernel(p0, p1, p2, w0, w1, w2, b0, b1, b2, x0_ref, u_ref):
    period = jnp.where(pl.program_id(1) == 0, CTX_LEN, GRID_W)
    pos = lax.broadcasted_iota(jnp.int32, (ROW_BLK, 1), 0) % period
    has_prev = pos != 0
    has_next = pos != period - 1

    def conv(p_ref, w_ref, b_ref):
        u = p_ref[0]
        prev = jnp.where(has_prev, pltpu.roll(u, 1, 0), 0.0)
        nxt = jnp.where(has_next, pltpu.roll(u, ROW_BLK - 1, 0), 0.0)
        return prev * w_ref[0:1, :] + u * w_ref[1:2, :] + nxt * w_ref[2:3, :] + b_ref[...]

    x0_ref[0] = conv(p0, w0, b0)
    u_ref[0] = conv(p1, w1, b1) * conv(p2, w2, b2)


def _hy_pre(proj, conv_w, conv_b):
    tn = 512
    nj = D_MODEL // tn
    pspec = lambda k: pl.BlockSpec((1, ROW_BLK, tn), lambda b, t, j: (b, t, k * nj + j))
    wspec = lambda k: pl.BlockSpec((3, tn), lambda b, t, j: (0, k * nj + j))
    bspec = lambda k: pl.BlockSpec((1, tn), lambda b, t, j: (0, k * nj + j))
    ospec = pl.BlockSpec((1, ROW_BLK, tn), lambda b, t, j: (b, t, j))
    out_sd = jax.ShapeDtypeStruct((BATCH, S_ROWS, D_MODEL), F32)
    cb = conv_b.reshape(1, 3 * D_MODEL)
    return pl.pallas_call(
        _hy_pre_kernel,
        grid=(BATCH, BLKS, nj),
        in_specs=[pspec(0), pspec(1), pspec(2), wspec(0), wspec(1), wspec(2), bspec(0), bspec(1), bspec(2)],
        out_specs=[ospec, ospec],
        out_shape=[out_sd, out_sd],
        compiler_params=_cparams(("arbitrary", "arbitrary", "arbitrary"), 32),
        name="hy_pre",
    )(proj, proj, proj, conv_w, conv_w, conv_w, cb, cb, cb)


def _hyena_filter(L, f_w1, f_b1, f_w2, f_b2, f_w3, f_b3, f_w4):
    t = jnp.linspace(0.0, 1.0, L, dtype=F32)[:, None]
    w = 2 * math.pi * jnp.arange(L, dtype=F32)[:, None] / L
    f = jnp.linspace(1e-4, HY_BANDS - 1, HY_BANDS, dtype=F32)[None, :]
    z = jnp.concatenate([t, jnp.cos(f * w), -jnp.sin(f * w)], axis=-1)
    hdn = jnp.sin(HY_SIN_FREQ * (z @ f_w1 + f_b1))
    hdn = jnp.sin(HY_SIN_FREQ * (hdn @ f_w2 + f_b2))
    hdn = jnp.sin(HY_SIN_FREQ * (hdn @ f_w3 + f_b3))
    filt = hdn @ f_w4
    d = filt.shape[-1] // 2
    max_decay = math.log(HY_DECAY_TARGET) / HY_FAST_DECAY_PCT
    min_decay = math.log(HY_DECAY_TARGET) / HY_SLOW_DECAY_PCT
    deltas = jnp.abs(jnp.linspace(min_decay, max_decay, d, dtype=F32))
    decay = jnp.exp(-t * deltas[None, :])
    h_fwd = filt[:, :d] * decay
    h_bwd = filt[:, d:] * decay
    k_full = jnp.concatenate([h_fwd, jnp.zeros((1, d), F32), h_bwd[1:][::-1]], axis=0)
    return k_full / jnp.sum(jnp.abs(k_full), axis=0, keepdims=True)


def _fft_long_conv(u, k_full):
    L = u.shape[1]
    uf = jnp.fft.rfft(u, n=2 * L, axis=1)
    kf = jnp.fft.rfft(k_full, n=2 * L, axis=0)
    return jnp.fft.irfft(uf * kf[None], n=2 * L, axis=1)[:, :L]


def _hyena_layer(xs, h, mod, w_in, b_in, conv_w, conv_b, fw, skip, w_out, b_out):
    hf = h.reshape(M_ROWS, D_MODEL)
    proj = _mm(hf, w_in.astype(BF16), bias=b_in).reshape(BATCH, S_ROWS, 3 * D_MODEL)
    x0, u = _hy_pre(proj, conv_w, conv_b)
    y_ctx = _fft_long_conv(u[:, :CTX_LEN], _hyena_filter(CTX_LEN, *fw))
    y_lat = _fft_long_conv(u[:, CTX_LEN:], _hyena_filter(SEQ, *fw))
    y = x0 * (jnp.concatenate([y_ctx, y_lat], axis=1) + u * skip)
    out = _mm(y.astype(BF16).reshape(M_ROWS, D_MODEL), w_out.astype(BF16), bias=b_out,
              res=xs.reshape(M_ROWS, D_MODEL), mod=mod, gate_idx=2)
    return out.reshape(BATCH, S_ROWS, D_MODEL)


def _ffn_kernel(x_ref, w1_ref, w3_ref, w2_ref, s_ref, o_ref):
    x = x_ref[0]
    h1 = jnp.dot(x, w1_ref[0], preferred_element_type=F32)
    h3 = jnp.dot(x, w3_ref[0], preferred_element_type=F32)
    a = (h1 * jax.nn.sigmoid(h1) * h3).astype(BF16)
    o_ref[0] = jnp.dot(a, w2_ref[0], preferred_element_type=F32) * s_ref[0]


def _ffn(xe, w1, w3, w2, score):
    tm = FFN_TM
    return pl.pallas_call(
        _ffn_kernel,
        grid=(N_EXPERTS, EXP_ROWS // tm),
        in_specs=[
            pl.BlockSpec((1, tm, D_MODEL), lambda e, i: (e, i, 0)),
            pl.BlockSpec((1, D_MODEL, EXPERT_FF), lambda e, i: (e, 0, 0)),
            pl.BlockSpec((1, D_MODEL, EXPERT_FF), lambda e, i: (e, 0, 0)),
            pl.BlockSpec((1, EXPERT_FF, D_MODEL), lambda e, i: (e, 0, 0)),
            pl.BlockSpec((1, tm, 1), lambda e, i: (e, i, 0)),
        ],
        out_specs=pl.BlockSpec((1, tm, D_MODEL), lambda e, i: (e, i, 0)),
        out_shape=jax.ShapeDtypeStruct((N_EXPERTS, EXP_ROWS, D_MODEL), F32),
        compiler_params=_cparams(("arbitrary", "arbitrary"), 56),
        name="moe_ffn",
    )(xe, w1, w3, w2, score)


def _moe_layer(xs, h, mod, w_router, w1, w3, w2):
    hf = h.reshape(M_ROWS, D_MODEL)
    wr = jnp.zeros((D_MODEL, 128), F32).at[:, :N_EXPERTS].set(w_router)
    logits = _mm(hf, wr.astype(BF16), tn=128)[:, :N_EXPERTS].reshape(BATCH, S_ROWS, N_EXPERTS)
    s = jax.nn.softmax(logits, axis=-1)
    top_c, idx_c = lax.top_k(jnp.swapaxes(s[:, :CTX_LEN], 1, 2), CAP_CTX)
    top_l, idx_l = lax.top_k(jnp.swapaxes(s[:, CTX_LEN:], 1, 2), CAP_LAT)
    base = (jnp.arange(BATCH, dtype=jnp.int32) * S_ROWS)[:, None, None]

    def per_expert(t):
        return jnp.swapaxes(t, 0, 1).reshape(N_EXPERTS, -1)

    rows = jnp.concatenate([per_expert(idx_l + base + CTX_LEN), per_expert(idx_c + base)], axis=1)
    score = jnp.concatenate([per_expert(top_l), per_expert(top_c)], axis=1)
    xe = jnp.take(hf, rows, axis=0)
    y = _ffn(xe, w1.astype(BF16), w3.astype(BF16), w2.astype(BF16), score[..., None])
    moe = jnp.zeros((M_ROWS, D_MODEL), F32).at[rows.reshape(-1)].add(y.reshape(-1, D_MODEL))
    gate = jnp.concatenate([jnp.broadcast_to(mod[CTX_GROUP, 5], (BATCH, CTX_LEN, D_MODEL)),
                            jnp.broadcast_to(mod[:BATCH, 5][:, None, :], (BATCH, SEQ, D_MODEL))], axis=1)
    return xs + gate * moe.reshape(BATCH, S_ROWS, D_MODEL)


def kernel(x, c, ctx, c_ctx, w_ada, b_ada, norm_mix, norm_ffn, norm_final, gla_w_in, gla_w_a1, gla_w_a2, gla_b_a, gla_head_norm, gla_w_out, hy_w_in, hy_b_in, hy_conv_w, hy_conv_b, hy_f_w1, hy_f_b1, hy_f_w2, hy_f_b2, hy_f_w3, hy_f_b3, hy_f_w4, hy_skip, hy_w_out, hy_b_out, moe_router, moe_w1, moe_w3, moe_w2):
    cond = jnp.zeros((N_GROUPS, D_MODEL), F32).at[:BATCH].set(c).at[CTX_GROUP].set(c_ctx)
    mods = _ada(cond, w_ada, b_ada).reshape(DEPTH, N_GROUPS, N_ADA, D_MODEL)
    xs = jnp.concatenate([ctx, x], axis=1)
    for i in range(DEPTH):
        mod = mods[i]
        j = i // 2
        h = _norm_mod(xs, norm_mix[i], mod, 0, 1)
        if i % 2 == 0:
            xs = _gla_layer(xs, h, mod, gla_w_in[j], gla_w_a1[j], gla_w_a2[j], gla_b_a[j], gla_head_norm[j],
                            gla_w_out[j])
        else:
            fw = (hy_f_w1[j], hy_f_b1[j], hy_f_w2[j], hy_f_b2[j], hy_f_w3[j], hy_f_b3[j], hy_f_w4[j])
            xs = _hyena_layer(xs, h, mod, hy_w_in[j], hy_b_in[j], hy_conv_w[j], hy_conv_b[j], fw, hy_skip[j],
                              hy_w_out[j], hy_b_out[j])
        h = _norm_mod(xs, norm_ffn[i], mod, 3, 4)
        xs = _moe_layer(xs, h, mod, moe_router[i], moe_w1[i], moe_w3[i], moe_w2[i])
    return _final_norm(xs, norm_final)
```

```python
import functools
import math

import jax
import jax.numpy as jnp
import numpy as np
from jax import lax
from jax.experimental import pallas as pl
from jax.experimental.pallas import tpu as pltpu

D_MODEL = 2048
BATCH = 8
SEQ = 2048
DEPTH = 4
CTX_LEN = 256
GRID_W = 64
EPS = 1e-6
N_ADA = 6

GLA_HEADS = 4
GLA_DK = D_MODEL // 2
GLA_DV = D_MODEL
GLA_DK_HEAD = GLA_DK // GLA_HEADS
GLA_DV_HEAD = GLA_DV // GLA_HEADS
GLA_GATE_RANK = 16
GLA_GATE_TEMP = 16.0
GLA_IN = 2 * GLA_DK + 2 * GLA_DV

HY_BANDS = 16
HY_SIN_FREQ = 1.0
HY_FAST_DECAY_PCT = 0.3
HY_SLOW_DECAY_PCT = 1.5
HY_DECAY_TARGET = 1e-2

N_EXPERTS = 16
EXPERT_FF = D_MODEL // 2
CAPACITY_FACTOR = 2

S_ROWS = CTX_LEN + SEQ
M_ROWS = BATCH * S_ROWS
CTX_GROUP = BATCH
N_GROUPS = 16
ROW_BLK = 256
BLKS = S_ROWS // ROW_BLK
MM_TM = S_ROWS // 2
GLA_CHUNK = 64
GLA_LEVELS = 6
CAP_LAT = CAPACITY_FACTOR * SEQ // N_EXPERTS
CAP_CTX = CAPACITY_FACTOR * CTX_LEN // N_EXPERTS
EXP_ROWS = BATCH * (CAP_LAT + CAP_CTX)
FFN_TM = EXP_ROWS // 2

F32 = jnp.float32
BF16 = jnp.bfloat16
MIB = 1024 * 1024


def _cparams(sem, vmem_mib):
    return pltpu.CompilerParams(dimension_semantics=sem, vmem_limit_bytes=vmem_mib * MIB)


def _ada_kernel(c_ref, w_ref, b_ref, o_ref):
    c = c_ref[...]
    s = (c * jax.nn.sigmoid(c)).astype(BF16)
    o_ref[0] = jnp.dot(s, w_ref[0].astype(BF16), preferred_element_type=F32) + b_ref[0]


def _ada(cond, w_ada, b_ada):
    tn = 1024
    n = N_ADA * D_MODEL
    return pl.pallas_call(
        _ada_kernel,
        grid=(DEPTH, n // tn),
        in_specs=[
            pl.BlockSpec((N_GROUPS, D_MODEL), lambda l, j: (0, 0)),
            pl.BlockSpec((1, D_MODEL, tn), lambda l, j: (l, 0, j)),
            pl.BlockSpec((1, 1, tn), lambda l, j: (l, 0, j)),
        ],
        out_specs=pl.BlockSpec((1, N_GROUPS, tn), lambda l, j: (l, 0, j)),
        out_shape=jax.ShapeDtypeStruct((DEPTH, N_GROUPS, n), F32),
        compiler_params=_cparams(("arbitrary", "arbitrary"), 40),
        name="ada",
    )(cond, w_ada, b_ada.reshape(DEPTH, 1, n))


def _norm_mod_kernel(x_ref, g_ref, m_ref, o_ref, *, sh, sc):
    x = x_ref[0]
    y = x * lax.rsqrt(jnp.mean(x * x, axis=-1, keepdims=True) + EPS)
    y = y * g_ref[...]
    o_ref[0] = (y * (1.0 + m_ref[0, sc:sc + 1, :]) + m_ref[0, sh:sh + 1, :]).astype(o_ref.dtype)


def _group_of_block(b, t):
    return jnp.where(t == BLKS - 1, CTX_GROUP, b)


def _norm_mod(xs, g, mod, sh, sc):
    return pl.pallas_call(
        functools.partial(_norm_mod_kernel, sh=sh, sc=sc),
        grid=(BATCH, BLKS),
        in_specs=[
            pl.BlockSpec((1, ROW_BLK, D_MODEL), lambda b, t: (b, t, 0)),
            pl.BlockSpec((1, D_MODEL), lambda b, t: (0, 0)),
            pl.BlockSpec((1, N_ADA, D_MODEL), lambda b, t: (_group_of_block(b, t), 0, 0)),
        ],
        out_specs=pl.BlockSpec((1, ROW_BLK, D_MODEL), lambda b, t: (b, t, 0)),
        out_shape=jax.ShapeDtypeStruct((BATCH, S_ROWS, D_MODEL), BF16),
        compiler_params=_cparams(("arbitrary", "arbitrary"), 32),
        name="norm_mod",
    )(xs, g.reshape(1, D_MODEL), mod)


def _final_norm_kernel(x_ref, g_ref, o_ref):
    x = x_ref[0]
    y = x * lax.rsqrt(jnp.mean(x * x, axis=-1, keepdims=True) + EPS)
    o_ref[0] = y * g_ref[...]


def _final_norm(xs, g):
    return pl.pallas_call(
        _final_norm_kernel,
        grid=(BATCH, SEQ // ROW_BLK),
        in_specs=[
            pl.BlockSpec((1, ROW_BLK, D_MODEL), lambda b, t: (b, t, 0)),
            pl.BlockSpec((1, D_MODEL), lambda b, t: (0, 0)),
        ],
        out_specs=pl.BlockSpec((1, ROW_BLK, D_MODEL), lambda b, t: (b, t, 0)),
        out_shape=jax.ShapeDtypeStruct((BATCH, SEQ, D_MODEL), F32),
        compiler_params=_cparams(("arbitrary", "arbitrary"), 32),
        name="final_norm",
    )(xs, g.reshape(1, D_MODEL))


def _mm_kernel(*refs, has_bias, gate_idx, tm):
    a_ref, w_ref = refs[0], refs[1]
    pos = 2
    acc = jnp.dot(a_ref[...], w_ref[...], preferred_element_type=F32)
    if has_bias:
        acc = acc + refs[pos][...]
        pos += 1
    if gate_idx is None:
        o_ref = refs[pos]
        o_ref[...] = acc.astype(o_ref.dtype)
    else:
        res_ref, mb_ref, mc_ref, o_ref = refs[pos:pos + 4]
        gate_b = mb_ref[0, gate_idx:gate_idx + 1, :]
        gate_c = mc_ref[0, gate_idx:gate_idx + 1, :]
        rows = lax.broadcasted_iota(jnp.int32, (tm, 1), 0)
        last_tile = pl.program_id(1) % (S_ROWS // tm) == S_ROWS // tm - 1
        is_ctx = jnp.logical_and(last_tile, rows >= tm - CTX_LEN)
        o_ref[...] = res_ref[...] + jnp.where(is_ctx, gate_c, gate_b) * acc


def _mm(a, w, bias=None, out_dtype=F32, tn=1024, res=None, mod=None, gate_idx=None):
    m, k = a.shape
    n = w.shape[1]
    tm = MM_TM
    tn = min(tn, n)
    in_specs = [
        pl.BlockSpec((tm, k), lambda j, i: (i, 0)),
        pl.BlockSpec((k, tn), lambda j, i: (0, j)),
    ]
    args = [a, w]
    if bias is not None:
        in_specs.append(pl.BlockSpec((1, tn), lambda j, i: (0, j)))
        args.append(bias.reshape(1, n))
    if gate_idx is not None:
        per = S_ROWS // tm
        in_specs += [
            pl.BlockSpec((tm, tn), lambda j, i: (i, j)),
            pl.BlockSpec((1, N_ADA, tn), lambda j, i: (i // per, 0, j)),
            pl.BlockSpec((1, N_ADA, tn), lambda j, i: (CTX_GROUP, 0, j)),
        ]
        args += [res, mod, mod]
    return pl.pallas_call(
        functools.partial(_mm_kernel, has_bias=bias is not None, gate_idx=gate_idx, tm=tm),
        grid=(n // tn, m // tm),
        in_specs=in_specs,
        out_specs=pl.BlockSpec((tm, tn), lambda j, i: (i, j)),
        out_shape=jax.ShapeDtypeStruct((m, n), out_dtype),
        compiler_params=_cparams(("arbitrary", "arbitrary"), 52),
        name="mm",
    )(*args)


def _gla_la_kernel(low_ref, w2_ref, b_ref, o_ref):
    low = low_ref[...]
    for z in range(2):
        lz = low[:, z * GLA_GATE_RANK:(z + 1) * GLA_GATE_RANK].astype(BF16)
        logit = jnp.dot(lz, w2_ref[z].astype(BF16), preferred_element_type=F32) + b_ref[z]
        log_sig = jnp.minimum(logit, 0.0) - jnp.log1p(jnp.exp(-jnp.abs(logit)))
        o_ref[z] = log_sig / GLA_GATE_TEMP


def _gla_la(low, w_a2, b_a):
    tm = MM_TM
    return pl.pallas_call(
        _gla_la_kernel,
        grid=(M_ROWS // tm,),
        in_specs=[
            pl.BlockSpec((tm, 128), lambda i: (i, 0)),
            pl.BlockSpec((2, GLA_GATE_RANK, GLA_DK), lambda i: (0, 0, 0)),
            pl.BlockSpec((2, 1, GLA_DK), lambda i: (0, 0, 0)),
        ],
        out_specs=pl.BlockSpec((2, tm, GLA_DK), lambda i: (0, i, 0)),
        out_shape=jax.ShapeDtypeStruct((2, M_ROWS, GLA_DK), F32),
        compiler_params=_cparams(("arbitrary",), 48),
        name="gla_la",
    )(low, w_a2, b_a.reshape(2, 1, GLA_DK))


def _gla_tables():
    c = GLA_CHUNK
    t = np.arange(c)
    blocks = [(t[None, :] <= t[:, None]), (t[None, :] > t[:, None])]
    pair = np.full((c, c), GLA_LEVELS + 1, np.int32)
    pair[t, t] = GLA_LEVELS
    for lvl in range(GLA_LEVELS):
        s = c >> (lvl + 1)
        mid = (t // (2 * s)) * 2 * s + s - 1
        second = (t % (2 * s)) >= s
        dq = (t[None, :] > mid[:, None]) & (t[None, :] <= t[:, None])
        dk = (t[None, :] > t[:, None]) & (t[None, :] <= mid[:, None])
        blocks.append(np.where(second[:, None], dq, dk))
        same = (t[:, None] // (2 * s)) == (t[None, :] // (2 * s))
        pair[same & second[:, None] & ~second[None, :]] = lvl
    d_f = np.concatenate(blocks, axis=0).astype(np.float32)
    d_b = np.concatenate([blk[::-1, ::-1] for blk in blocks], axis=0).astype(np.float32)
    return np.stack([d_f, d_b]), np.stack([pair, pair[::-1, ::-1]])


def _gla_chunk(q_ref, k_ref, v_ref, la_ref, d_ref, pair_ref, st_ref, o_ref, r, last_row):
    c = GLA_CHUNK
    rows = pl.ds(r, c)
    la = la_ref[0, 0, rows, :]
    hi = la.astype(BF16)
    lo = (la - hi.astype(F32)).astype(BF16)
    dmat = d_ref[0]
    p = jnp.exp(jnp.dot(dmat, hi, preferred_element_type=F32) + jnp.dot(dmat, lo, preferred_element_type=F32))
    q = q_ref[0, rows, :].astype(F32) * (GLA_DK_HEAD ** -0.5)
    k = k_ref[0, rows, :].astype(F32)
    v = v_ref[0, rows, :]
    nt = (((1,), (1,)), ((), ()))
    st = st_ref[...]
    o = lax.dot_general((q * p[0:c]).astype(BF16), st.astype(BF16), nt, preferred_element_type=F32)
    pair = pair_ref[0]
    a = jnp.where(pair == GLA_LEVELS,
                  lax.dot_general(q.astype(BF16), k.astype(BF16), nt, preferred_element_type=F32), 0.0)
    for lvl in range(GLA_LEVELS):
        pl_ = p[(2 + lvl) * c:(3 + lvl) * c]
        a_l = lax.dot_general((q * pl_).astype(BF16), (k * pl_).astype(BF16), nt, preferred_element_type=F32)
        a = jnp.where(pair == lvl, a_l, a)
    o = o + jnp.dot(a.astype(BF16), v, preferred_element_type=F32)
    o_ref[0, rows, :] = o
    kt = (k * p[c:2 * c]).astype(BF16)
    tn = (((0,), (0,)), ((), ()))
    decay = p[last_row:last_row + 1]
    st_ref[...] = st * decay + lax.dot_general(v, kt, tn, preferred_element_type=F32)


def _gla_scan_kernel(qf, kf, vf, laf, qb, kb, vb, lab, d_ref, pair_ref, of_ref, ob_ref, sf_ref, sb_ref):
    @pl.when(pl.program_id(2) == 0)
    def _():
        sf_ref[...] = jnp.zeros_like(sf_ref)
        sb_ref[...] = jnp.zeros_like(sb_ref)

    n = ROW_BLK // GLA_CHUNK

    def step(i, carry):
        rf = pl.multiple_of(i * GLA_CHUNK, GLA_CHUNK)
        rb = pl.multiple_of((n - 1 - i) * GLA_CHUNK, GLA_CHUNK)
        _gla_chunk(qf, kf, vf, laf, d_ref.at[0:1], pair_ref.at[0:1], sf_ref, of_ref, rf, GLA_CHUNK - 1)
        _gla_chunk(qb, kb, vb, lab, d_ref.at[1:2], pair_ref.at[1:2], sb_ref, ob_ref, rb, 0)
        return carry

    lax.fori_loop(0, n, step, 0)


def _fwd_block(t):
    return jnp.where(t == 0, BLKS - 1, t - 1)


def _bwd_block(t):
    return jnp.where(t == 0, BLKS - 1, BLKS - 1 - t)


def _gla_scan(proj, la):
    d_np, pair_np = _gla_tables()
    dmat = jnp.asarray(d_np, BF16)
    pair = jnp.asarray(pair_np, jnp.int32)
    kq, kv = GLA_DK_HEAD, GLA_DV_HEAD
    k_off = GLA_DK // kq
    v_off = 2 * GLA_DK // kv
    c = GLA_CHUNK
    spec_f = [
        pl.BlockSpec((1, ROW_BLK, kq), lambda b, h, t: (b, _fwd_block(t), h)),
        pl.BlockSpec((1, ROW_BLK, kq), lambda b, h, t: (b, _fwd_block(t), k_off + h)),
        pl.BlockSpec((1, ROW_BLK, kv), lambda b, h, t: (b, _fwd_block(t), v_off + h)),
        pl.BlockSpec((1, 1, ROW_BLK, kq), lambda b, h, t: (0, b, _fwd_block(t), h)),
    ]
    spec_b = [
        pl.BlockSpec((1, ROW_BLK, kq), lambda b, h, t: (b, _bwd_block(t), h)),
        pl.BlockSpec((1, ROW_BLK, kq), lambda b, h, t: (b, _bwd_block(t), k_off + h)),
        pl.BlockSpec((1, ROW_BLK, kv), lambda b, h, t: (b, _bwd_block(t), v_off + h)),
        pl.BlockSpec((1, 1, ROW_BLK, kq), lambda b, h, t: (1, b, _bwd_block(t), h)),
    ]
    const = [
        pl.BlockSpec((2, (2 + GLA_LEVELS) * c, c), lambda b, h, t: (0, 0, 0)),
        pl.BlockSpec((2, c, c), lambda b, h, t: (0, 0, 0)),
    ]
    out_sd = jax.ShapeDtypeStruct((BATCH, S_ROWS, GLA_DV), F32)
    return pl.pallas_call(
        _gla_scan_kernel,
        grid=(BATCH, GLA_HEADS, BLKS),
        in_specs=spec_f + spec_b + const,
        out_specs=[
            pl.BlockSpec((1, ROW_BLK, kv), lambda b, h, t: (b, _fwd_block(t), h)),
            pl.BlockSpec((1, ROW_BLK, kv), lambda b, h, t: (b, _bwd_block(t), h)),
        ],
        out_shape=[out_sd, out_sd],
        scratch_shapes=[pltpu.VMEM((kv, kq), F32), pltpu.VMEM((kv, kq), F32)],
        compiler_params=_cparams(("arbitrary", "arbitrary", "arbitrary"), 32),
        name="gla_scan",
    )(proj, proj, proj, la, proj, proj, proj, la, dmat, pair)


def _gla_post_kernel(of_ref, ob_ref, g_ref, hn_ref, o_ref):
    hn = hn_ref[...]
    for h in range(GLA_HEADS):
        cols = slice(h * GLA_DV_HEAD, (h + 1) * GLA_DV_HEAD)
        o = of_ref[0, :, cols] + ob_ref[0, :, cols]
        o = o * lax.rsqrt(jnp.mean(o * o, axis=-1, keepdims=True) + EPS) * hn
        g = g_ref[0, :, cols].astype(F32)
        o_ref[0, :, cols] = (o.astype(BF16) * (g * jax.nn.sigmoid(g)).astype(BF16))


def _gla_post(o_f, o_b, proj, head_norm):
    g_off = 2 * GLA_DK // GLA_DV + 1
    blk = pl.BlockSpec((1, ROW_BLK, GLA_DV), lambda b, t: (b, t, 0))
    return pl.pallas_call(
        _gla_post_kernel,
        grid=(BATCH, BLKS),
        in_specs=[blk, blk,
                  pl.BlockSpec((1, ROW_BLK, GLA_DV), lambda b, t: (b, t, g_off)),
                  pl.BlockSpec((1, GLA_DV_HEAD), lambda b, t: (0, 0))],
        out_specs=blk,
        out_shape=jax.ShapeDtypeStruct((BATCH, S_ROWS, GLA_DV), BF16),
        compiler_params=_cparams(("arbitrary", "arbitrary"), 32),
        name="gla_post",
    )(o_f, o_b, proj, head_norm.reshape(1, GLA_DV_HEAD))


def _gla_layer(xs, h, mod, w_in, w_a1, w_a2, b_a, head_norm, w_out):
    hf = h.reshape(M_ROWS, D_MODEL)
    proj = _mm(hf, w_in.astype(BF16), out_dtype=BF16).reshape(BATCH, S_ROWS, GLA_IN)
    w_low = jnp.zeros((D_MODEL, 128), F32)
    w_low = w_low.at[:, :GLA_GATE_RANK].set(w_a1[0]).at[:, GLA_GATE_RANK:2 * GLA_GATE_RANK].set(w_a1[1])
    low = _mm(hf, w_low.astype(BF16), tn=128)
    la = _gla_la(low, w_a2, b_a).reshape(2, BATCH, S_ROWS, GLA_DK)
    o_f, o_b = _gla_scan(proj, la)
    y = _gla_post(o_f, o_b, proj, head_norm)
    out = _mm(y.reshape(M_ROWS, GLA_DV), w_out.astype(BF16), res=xs.reshape(M_ROWS, D_MODEL), mod=mod, gate_idx=2)
    return out.reshape(BATCH, S_ROWS, D_MODEL)


def _hy_proj_kernel(a_ref, w0, w1, w2, bi0, bi1, bi2, cw0, cw1, cw2, cb0, cb1, cb2, x0_ref, u_ref, *, tm):
    rows = lax.broadcasted_iota(jnp.int32, (tm, 1), 0)
    last_tile = pl.program_id(1) % (S_ROWS // tm) == S_ROWS // tm - 1
    ctx0 = tm - CTX_LEN
    is_ctx = jnp.logical_and(last_tile, rows >= ctx0)
    pos = jnp.where(is_ctx, rows - ctx0, rows % GRID_W)
    last = jnp.where(is_ctx, CTX_LEN - 1, GRID_W - 1)
    has_prev = pos != 0
    has_next = pos != last
    a = a_ref[...]

    def branch(w_ref, bi_ref, cw_ref, cb_ref):
        p = jnp.dot(a, w_ref[...], preferred_element_type=F32) + bi_ref[...]
        prev = jnp.where(has_prev, pltpu.roll(p, 1, 0), 0.0)
        nxt = jnp.where(has_next, pltpu.roll(p, tm - 1, 0), 0.0)
        return prev * cw_ref[0:1, :] + p * cw_ref[1:2, :] + nxt * cw_ref[2:3, :] + cb_ref[...]

    x0_ref[...] = branch(w0, bi0, cw0, cb0).astype(BF16)
    u_ref[...] = (branch(w1, bi1, cw1, cb1) * branch(w2, bi2, cw2, cb2)).astype(BF16)


def _hy_proj(a, w_in, b_in, conv_w, conv_b):
    tm, tn = MM_TM, 512
    nj = D_MODEL // tn
    wspec = lambda k: pl.BlockSpec((D_MODEL, tn), lambda j, i: (0, k * nj + j))
    bspec = lambda k: pl.BlockSpec((1, tn), lambda j, i: (0, k * nj + j))
    cspec = lambda k: pl.BlockSpec((3, tn), lambda j, i: (0, k * nj + j))
    ospec = pl.BlockSpec((tm, tn), lambda j, i: (i, j))
    out_sd = jax.ShapeDtypeStruct((M_ROWS, D_MODEL), BF16)
    bi = b_in.reshape(1, 3 * D_MODEL)
    cb = conv_b.reshape(1, 3 * D_MODEL)
    return pl.pallas_call(
        functools.partial(_hy_proj_kernel, tm=tm),
        grid=(nj, M_ROWS // tm),
        in_specs=[pl.BlockSpec((tm, D_MODEL), lambda j, i: (i, 0)), wspec(0), wspec(1), wspec(2),
                  bspec(0), bspec(1), bspec(2), cspec(0), cspec(1), cspec(2), bspec(0), bspec(1), bspec(2)],
        out_specs=[ospec, ospec],
        out_shape=[out_sd, out_sd],
        compiler_params=_cparams(("arbitrary", "arbitrary"), 52),
        name="hy_proj",
    )(a, w_in, w_in, w_in, bi, bi, bi, conv_w, conv_w, conv_w, cb, cb, cb)


def _hy_filter_kernel(z_ref, w1, b1, w2, b2, w3, b3, w4f, w4b, dl_ref, o_ref):
    def layer(x, w_ref, b_ref):
        y = jnp.dot(x.astype(BF16), w_ref[...].astype(BF16), preferred_element_type=F32) + b_ref[...]
        return jnp.sin(HY_SIN_FREQ * y)

    z = z_ref[...]
    hdn = layer(layer(layer(z, w1, b1), w2, b2), w3, b3).astype(BF16)
    decay = jnp.exp(-z[:, 0:1] * dl_ref[...])
    h_fwd = jnp.dot(hdn, w4f[...].astype(BF16), preferred_element_type=F32) * decay
    h_bwd = jnp.dot(hdn, w4b[...].astype(BF16), preferred_element_type=F32) * decay
    lag0 = lax.broadcasted_iota(jnp.int32, (z.shape[0], 1), 0) == 0
    h_bwd = jnp.where(lag0, 0.0, h_bwd)
    inv = 1.0 / jnp.sum(jnp.abs(h_fwd) + jnp.abs(h_bwd), axis=0, keepdims=True)
    for k, sig in enumerate(((h_fwd + h_bwd) * inv, (h_bwd - h_fwd) * inv)):
        hi = sig.astype(BF16)
        o_ref[2 * k] = hi
        o_ref[2 * k + 1] = (sig - hi.astype(F32)).astype(BF16)


def _hy_filter(L, f_w1, f_b1, f_w2, f_b2, f_w3, f_b3, f_w4):
    tn = 512
    nj = D_MODEL // tn
    t = jnp.linspace(0.0, 1.0, L, dtype=F32)[:, None]
    w = 2 * math.pi * jnp.arange(L, dtype=F32)[:, None] / L
    f = jnp.linspace(1e-4, HY_BANDS - 1, HY_BANDS, dtype=F32)[None, :]
    emb = 1 + 2 * HY_BANDS
    z = jnp.concatenate([t, jnp.cos(f * w), -jnp.sin(f * w), jnp.zeros((L, 128 - emb), F32)], axis=-1)
    w1 = jnp.concatenate([f_w1, jnp.zeros((128 - emb, f_w1.shape[1]), F32)], axis=0)
    max_decay = math.log(HY_DECAY_TARGET) / HY_FAST_DECAY_PCT
    min_decay = math.log(HY_DECAY_TARGET) / HY_SLOW_DECAY_PCT
    deltas = jnp.abs(jnp.linspace(min_decay, max_decay, D_MODEL, dtype=F32))[None, :]
    hid = f_w2.shape[0]
    full = lambda shape: pl.BlockSpec(shape, lambda j: (0, 0))
    return pl.pallas_call(
        _hy_filter_kernel,
        grid=(nj,),
        in_specs=[full((L, 128)), full((128, hid)), full((1, hid)), full((hid, hid)), full((1, hid)),
                  full((hid, hid)), full((1, hid)),
                  pl.BlockSpec((hid, tn), lambda j: (0, j)), pl.BlockSpec((hid, tn), lambda j: (0, nj + j)),
                  pl.BlockSpec((1, tn), lambda j: (0, j))],
        out_specs=pl.BlockSpec((4, L, tn), lambda j: (0, 0, j)),
        out_shape=jax.ShapeDtypeStruct((4, L, D_MODEL), BF16),
        compiler_params=_cparams(("arbitrary",), 48),
        name="hy_filter",
    )(z, w1, f_b1.reshape(1, hid), f_w2, f_b2.reshape(1, hid), f_w3, f_b3.reshape(1, hid), f_w4, f_w4, deltas)


def _dft_tiles(L):
    return (min(2 * L, 1024), min(L, 1024))


def _dft_mats(L):
    n = 2 * L
    tf = _dft_tiles(L)[0] // 2
    f = jnp.arange(L, dtype=jnp.int32)
    ang = ((f[:, None] * f[None, :]) % n).astype(F32) * (2 * math.pi / n)
    cosm = jnp.cos(ang)
    sinm = jnp.sin(ang).at[0].set(jnp.where(f % 2 == 0, 1.0, -1.0))
    fwd = jnp.concatenate([cosm.reshape(L // tf, tf, L), sinm.reshape(L // tf, tf, L)], axis=1).reshape(n, L)
    wgt = jnp.where(f == 0, 1.0 / n, 2.0 / n)[None, :]
    inv = jnp.concatenate([cosm.T * wgt, -sinm.T * wgt], axis=1)
    fwd_hi = fwd.astype(BF16)
    return fwd_hi, (fwd - fwd_hi.astype(F32)).astype(BF16), inv.astype(BF16)


def _dft_fwd_kernel(f_ref, x_ref, *rest, tf):
    acc = jnp.dot(f_ref[...], x_ref[0], preferred_element_type=F32)
    if len(rest) == 1:
        rest[0][0] = acc
    else:
        ka, kb, kap, o_ref = rest
        c, s = acc[:tf], acc[tf:]
        o_ref[0, 0] = (c * ka[...] + s * kb[...]).astype(BF16)
        o_ref[0, 1] = (c * kb[...] - s * kap[...]).astype(BF16)


def _dft_fwd(fwd, x, row_blk, L, spec=None):
    nb = x.shape[0]
    tm = _dft_tiles(L)[0]
    tf, tn = tm // 2, 1024
    in_specs = [pl.BlockSpec((tm, L), lambda b, j, i: (i, 0)),
                pl.BlockSpec((1, L, tn), lambda b, j, i: (b, row_blk, j))]
    args = [fwd, x]
    if spec is None:
        out_spec = pl.BlockSpec((1, tm, tn), lambda b, j, i: (b, i, j))
        out_shape = jax.ShapeDtypeStruct((nb, 2 * L, D_MODEL), F32)
    else:
        in_specs += [pl.BlockSpec((tf, tn), lambda b, j, i: (i, j))] * 3
        args += list(spec)
        out_spec = pl.BlockSpec((1, 2, tf, tn), lambda b, j, i: (b, 0, i, j))
        out_shape = jax.ShapeDtypeStruct((nb, 2, L, D_MODEL), BF16)
    return pl.pallas_call(
        functools.partial(_dft_fwd_kernel, tf=tf),
        grid=(nb, D_MODEL // tn, 2 * L // tm),
        in_specs=in_specs,
        out_specs=out_spec,
        out_shape=out_shape,
        compiler_params=_cparams(("arbitrary", "arbitrary", "arbitrary"), 52),
        name="dft_fwd",
    )(*args)


def _filter_spectrum(sig, fwd_hi, fwd_lo, L):
    tm = _dft_tiles(L)[0]

    def split(s):
        r = s.reshape(s.shape[0], 2 * L // tm, 2, tm // 2, D_MODEL)
        return r[:, :, 0].reshape(-1, L, D_MODEL), r[:, :, 1].reshape(-1, L, D_MODEL)

    c1, s1 = split(_dft_fwd(fwd_hi, sig, 0, L))
    c2, s2 = split(_dft_fwd(fwd_lo, sig[0::2], 0, L))
    ka = c1[0] + c1[1] + c2[0]
    nyq = s1[0, 0] + s1[1, 0] + s2[0, 0]
    kb = (s1[2] + s1[3] + s2[1]).at[0].set(0.0)
    return ka, kb, ka.at[0].set(nyq)


def _dft_inv_kernel(g_ref, p_ref, x0_ref, u_ref, skip_ref, *rest):
    o_ref = rest[-1]
    y = jnp.dot(g_ref[...], p_ref[0], preferred_element_type=F32)
    o_ref[0] = (x0_ref[0].astype(F32) * (y + u_ref[0].astype(F32) * skip_ref[...])).astype(BF16)


def _dft_inv(inv, p, x0, u, skip, L, prev=None):
    tm = _dft_tiles(L)[1]
    tn = 512
    off = 0 if L == SEQ else SEQ // tm
    row = lambda b, j, i: (b, off + i, j)
    in_specs = [pl.BlockSpec((tm, 2 * L), lambda b, j, i: (i, 0)),
                pl.BlockSpec((1, 2 * L, tn), lambda b, j, i: (b, 0, j)),
                pl.BlockSpec((1, tm, tn), row), pl.BlockSpec((1, tm, tn), row),
                pl.BlockSpec((1, tn), lambda b, j, i: (0, j))]
    args = [inv, p, x0, u, skip.reshape(1, D_MODEL)]
    aliases = {}
    if prev is not None:
        in_specs.append(pl.BlockSpec(memory_space=pl.ANY))
        args.append(prev)
        aliases = {len(args) - 1: 0}
    return pl.pallas_call(
        _dft_inv_kernel,
        grid=(BATCH, D_MODEL // tn, L // tm),
        in_specs=in_specs,
        out_specs=pl.BlockSpec((1, tm, tn), row),
        out_shape=jax.ShapeDtypeStruct((BATCH, S_ROWS, D_MODEL), BF16),
        input_output_aliases=aliases,
        compiler_params=_cparams(("arbitrary", "arbitrary", "arbitrary"), 52),
        name="dft_inv",
    )(*args)


def _hyena_layer(xs, h, mod, w_in, b_in, conv_w, conv_b, fw, skip, w_out, b_out):
    hf = h.reshape(M_ROWS, D_MODEL)
    x0, u = _hy_proj(hf, w_in.astype(BF16), b_in, conv_w, conv_b)
    x0 = x0.reshape(BATCH, S_ROWS, D_MODEL)
    u = u.reshape(BATCH, S_ROWS, D_MODEL)
    y = None
    for L in (SEQ, CTX_LEN):
        fwd_hi, fwd_lo, inv = _dft_mats(L)
        spec = _filter_spectrum(_hy_filter(L, *fw), fwd_hi, fwd_lo, L)
        row_blk = 0 if L == SEQ else SEQ // L
        p = _dft_fwd(fwd_hi, u, row_blk, L, spec=spec).reshape(BATCH, 2 * L, D_MODEL)
        y = _dft_inv(inv, p, x0, u, skip, L, prev=y)
    out = _mm(y.reshape(M_ROWS, D_MODEL), w_out.astype(BF16), bias=b_out,
              res=xs.reshape(M_ROWS, D_MODEL), mod=mod, gate_idx=2)
    return out.reshape(BATCH, S_ROWS, D_MODEL)


def _ffn_up_kernel(x_ref, w1_ref, w3_ref, o_ref):
    x = x_ref[0]
    h1 = jnp.dot(x, w1_ref[0].astype(BF16), preferred_element_type=F32)
    h3 = jnp.dot(x, w3_ref[0].astype(BF16), preferred_element_type=F32)
    o_ref[0] = (h1 * jax.nn.sigmoid(h1) * h3).astype(BF16)


def _ffn_down_kernel(a_ref, w2_ref, s_ref, o_ref):
    o_ref[0] = jnp.dot(a_ref[0], w2_ref[0].astype(BF16), preferred_element_type=F32) * s_ref[0]


def _ffn(xe, w1, w3, w2, score):
    tm, tf, tn = FFN_TM, 512, 1024
    act = pl.pallas_call(
        _ffn_up_kernel,
        grid=(N_EXPERTS, EXPERT_FF // tf, EXP_ROWS // tm),
        in_specs=[
            pl.BlockSpec((1, tm, D_MODEL), lambda e, f, i: (e, i, 0)),
            pl.BlockSpec((1, D_MODEL, tf), lambda e, f, i: (e, 0, f)),
            pl.BlockSpec((1, D_MODEL, tf), lambda e, f, i: (e, 0, f)),
        ],
        out_specs=pl.BlockSpec((1, tm, tf), lambda e, f, i: (e, i, f)),
        out_shape=jax.ShapeDtypeStruct((N_EXPERTS, EXP_ROWS, EXPERT_FF), BF16),
        compiler_params=_cparams(("arbitrary", "arbitrary", "arbitrary"), 52),
        name="moe_ffn_up",
    )(xe, w1, w3)
    return pl.pallas_call(
        _ffn_down_kernel,
        grid=(N_EXPERTS, D_MODEL // tn, EXP_ROWS // tm),
        in_specs=[
            pl.BlockSpec((1, tm, EXPERT_FF), lambda e, j, i: (e, i, 0)),
            pl.BlockSpec((1, EXPERT_FF, tn), lambda e, j, i: (e, 0, j)),
            pl.BlockSpec((1, tm, 1), lambda e, j, i: (e, i, 0)),
        ],
        out_specs=pl.BlockSpec((1, tm, tn), lambda e, j, i: (e, i, j)),
        out_shape=jax.ShapeDtypeStruct((N_EXPERTS, EXP_ROWS, D_MODEL), F32),
        compiler_params=_cparams(("arbitrary", "arbitrary", "arbitrary"), 52),
        name="moe_ffn_down",
    )(act, w2, score)


def _moe_layer(xs, h, mod, w_router, w1, w3, w2):
    hf = h.reshape(M_ROWS, D_MODEL)
    wr = jnp.zeros((D_MODEL, 128), F32).at[:, :N_EXPERTS].set(w_router)
    logits = _mm(hf, wr.astype(BF16), tn=128)[:, :N_EXPERTS].reshape(BATCH, S_ROWS, N_EXPERTS)
    s = jax.nn.softmax(logits, axis=-1)
    top_l, idx_l = lax.top_k(jnp.swapaxes(s[:, :SEQ], 1, 2), CAP_LAT)
    top_c, idx_c = lax.top_k(jnp.swapaxes(s[:, SEQ:], 1, 2), CAP_CTX)
    base = (jnp.arange(BATCH, dtype=jnp.int32) * S_ROWS)[:, None, None]

    def per_expert(t):
        return jnp.swapaxes(t, 0, 1).reshape(N_EXPERTS, -1)

    rows = jnp.concatenate([per_expert(idx_l + base), per_expert(idx_c + base + SEQ)], axis=1)
    score = jnp.concatenate([per_expert(top_l), per_expert(top_c)], axis=1)
    xe = jnp.take(hf, rows, axis=0)
    y = _ffn(xe, w1, w3, w2, score[..., None])
    moe = jnp.zeros((M_ROWS, D_MODEL), F32).at[rows.reshape(-1)].add(y.reshape(-1, D_MODEL))
    gate = jnp.concatenate([jnp.broadcast_to(mod[:BATCH, 5][:, None, :], (BATCH, SEQ, D_MODEL)),
                            jnp.broadcast_to(mod[CTX_GROUP, 5], (BATCH, CTX_LEN, D_MODEL))], axis=1)
    return xs + gate * moe.reshape(BATCH, S_ROWS, D_MODEL)


def kernel(x, c, ctx, c_ctx, w_ada, b_ada, norm_mix, norm_ffn, norm_final, gla_w_in, gla_w_a1, gla_w_a2, gla_b_a, gla_head_norm, gla_w_out, hy_w_in, hy_b_in, hy_conv_w, hy_conv_b, hy_f_w1, hy_f_b1, hy_f_w2, hy_f_b2, hy_f_w3, hy_f_b3, hy_f_w4, hy_skip, hy_w_out, hy_b_out, moe_router, moe_w1, moe_w3, moe_w2):
    cond = jnp.zeros((N_GROUPS, D_MODEL), F32).at[:BATCH].set(c).at[CTX_GROUP].set(c_ctx)
    mods = _ada(cond, w_ada, b_ada).reshape(DEPTH, N_GROUPS, N_ADA, D_MODEL)
    xs = jnp.concatenate([x, ctx], axis=1)
    for i in range(DEPTH):
        mod = mods[i]
        j = i // 2
        h = _norm_mod(xs, norm_mix[i], mod, 0, 1)
        if i % 2 == 0:
            xs = _gla_layer(xs, h, mod, gla_w_in[j], gla_w_a1[j], gla_w_a2[j], gla_b_a[j], gla_head_norm[j],
                            gla_w_out[j])
        else:
            fw = (hy_f_w1[j], hy_f_b1[j], hy_f_w2[j], hy_f_b2[j], hy_f_w3[j], hy_f_b3[j], hy_f_w4[j])
            xs = _hyena_layer(xs, h, mod, hy_w_in[j], hy_b_in[j], hy_conv_w[j], hy_conv_b[j], fw, hy_skip[j],
                              hy_w_out[j], hy_b_out[j])
        h = _norm_mod(xs, norm_ffn[i], mod, 3, 4)
        xs = _moe_layer(xs, h, mod, moe_router[i], moe_w1[i], moe_w3[i], moe_w2[i])
    return _final_norm(xs, norm_final)
```

```python
import functools
import math

import jax
import jax.numpy as jnp
import numpy as np
from jax import lax
from jax.experimental import pallas as pl
from jax.experimental.pallas import tpu as pltpu

D_MODEL = 2048
BATCH = 8
SEQ = 2048
DEPTH = 4
CTX_LEN = 256
GRID_W = 64
EPS = 1e-6
N_ADA = 6

GLA_HEADS = 4
GLA_DK = D_MODEL // 2
GLA_DV = D_MODEL
GLA_DK_HEAD = GLA_DK // GLA_HEADS
GLA_DV_HEAD = GLA_DV // GLA_HEADS
GLA_GATE_RANK = 16
GLA_GATE_TEMP = 16.0
GLA_IN = 2 * GLA_DK + 2 * GLA_DV

HY_BANDS = 16
HY_SIN_FREQ = 1.0
HY_FAST_DECAY_PCT = 0.3
HY_SLOW_DECAY_PCT = 1.5
HY_DECAY_TARGET = 1e-2

N_EXPERTS = 16
EXPERT_FF = D_MODEL // 2
CAPACITY_FACTOR = 2

S_ROWS = CTX_LEN + SEQ
M_ROWS = BATCH * S_ROWS
CTX_GROUP = BATCH
N_GROUPS = 16
ROW_BLK = 256
BLKS = S_ROWS // ROW_BLK
MM_TM = S_ROWS // 2
GLA_CHUNK = 64
GLA_LEVELS = 6
CAP_LAT = CAPACITY_FACTOR * SEQ // N_EXPERTS
CAP_CTX = CAPACITY_FACTOR * CTX_LEN // N_EXPERTS
EXP_ROWS = BATCH * (CAP_LAT + CAP_CTX)
FFN_TM = EXP_ROWS // 2

F32 = jnp.float32
BF16 = jnp.bfloat16
MIB = 1024 * 1024


def _cparams(sem, vmem_mib):
    return pltpu.CompilerParams(dimension_semantics=sem, vmem_limit_bytes=vmem_mib * MIB)


def _ada_kernel(c_ref, w_ref, b_ref, o_ref):
    c = c_ref[...]
    s = (c * jax.nn.sigmoid(c)).astype(BF16)
    o_ref[0] = jnp.dot(s, w_ref[0].astype(BF16), preferred_element_type=F32) + b_ref[0]


def _ada(cond, w_ada, b_ada):
    tn = 1024
    n = N_ADA * D_MODEL
    return pl.pallas_call(
        _ada_kernel,
        grid=(DEPTH, n // tn),
        in_specs=[
            pl.BlockSpec((N_GROUPS, D_MODEL), lambda l, j: (0, 0)),
            pl.BlockSpec((1, D_MODEL, tn), lambda l, j: (l, 0, j)),
            pl.BlockSpec((1, 1, tn), lambda l, j: (l, 0, j)),
        ],
        out_specs=pl.BlockSpec((1, N_GROUPS, tn), lambda l, j: (l, 0, j)),
        out_shape=jax.ShapeDtypeStruct((DEPTH, N_GROUPS, n), F32),
        compiler_params=_cparams(("arbitrary", "arbitrary"), 40),
        name="ada",
    )(cond, w_ada, b_ada.reshape(DEPTH, 1, n))


def _norm_mod_kernel(x_ref, g_ref, m_ref, o_ref, *, sh, sc):
    x = x_ref[0]
    y = x * lax.rsqrt(jnp.mean(x * x, axis=-1, keepdims=True) + EPS)
    y = y * g_ref[...]
    o_ref[0] = (y * (1.0 + m_ref[0, sc:sc + 1, :]) + m_ref[0, sh:sh + 1, :]).astype(o_ref.dtype)


def _group_of_block(b, t):
    return jnp.where(t == BLKS - 1, CTX_GROUP, b)


def _norm_mod(xs, g, mod, sh, sc):
    return pl.pallas_call(
        functools.partial(_norm_mod_kernel, sh=sh, sc=sc),
        grid=(BATCH, BLKS),
        in_specs=[
            pl.BlockSpec((1, ROW_BLK, D_MODEL), lambda b, t: (b, t, 0)),
            pl.BlockSpec((1, D_MODEL), lambda b, t: (0, 0)),
            pl.BlockSpec((1, N_ADA, D_MODEL), lambda b, t: (_group_of_block(b, t), 0, 0)),
        ],
        out_specs=pl.BlockSpec((1, ROW_BLK, D_MODEL), lambda b, t: (b, t, 0)),
        out_shape=jax.ShapeDtypeStruct((BATCH, S_ROWS, D_MODEL), BF16),
        compiler_params=_cparams(("arbitrary", "arbitrary"), 32),
        name="norm_mod",
    )(xs, g.reshape(1, D_MODEL), mod)


def _final_norm_kernel(x_ref, g_ref, o_ref):
    x = x_ref[0]
    y = x * lax.rsqrt(jnp.mean(x * x, axis=-1, keepdims=True) + EPS)
    o_ref[0] = y * g_ref[...]


def _final_norm(xs, g):
    return pl.pallas_call(
        _final_norm_kernel,
        grid=(BATCH, SEQ // ROW_BLK),
        in_specs=[
            pl.BlockSpec((1, ROW_BLK, D_MODEL), lambda b, t: (b, t, 0)),
            pl.BlockSpec((1, D_MODEL), lambda b, t: (0, 0)),
        ],
        out_specs=pl.BlockSpec((1, ROW_BLK, D_MODEL), lambda b, t: (b, t, 0)),
        out_shape=jax.ShapeDtypeStruct((BATCH, SEQ, D_MODEL), F32),
        compiler_params=_cparams(("arbitrary", "arbitrary"), 32),
        name="final_norm",
    )(xs, g.reshape(1, D_MODEL))


def _mm_kernel(*refs, has_bias, gate_idx, tm):
    a_ref, w_ref = refs[0], refs[1]
    pos = 2
    acc = jnp.dot(a_ref[...], w_ref[...], preferred_element_type=F32)
    if has_bias:
        acc = acc + refs[pos][...]
        pos += 1
    if gate_idx is None:
        o_ref = refs[pos]
        o_ref[...] = acc.astype(o_ref.dtype)
    else:
        res_ref, mb_ref, mc_ref, o_ref = refs[pos:pos + 4]
        gate_b = mb_ref[0, gate_idx:gate_idx + 1, :]
        gate_c = mc_ref[0, gate_idx:gate_idx + 1, :]
        rows = lax.broadcasted_iota(jnp.int32, (tm, 1), 0)
        last_tile = pl.program_id(1) % (S_ROWS // tm) == S_ROWS // tm - 1
        is_ctx = jnp.logical_and(last_tile, rows >= tm - CTX_LEN)
        o_ref[...] = res_ref[...] + jnp.where(is_ctx, gate_c, gate_b) * acc


def _mm(a, w, bias=None, out_dtype=F32, tn=1024, res=None, mod=None, gate_idx=None):
    m, k = a.shape
    n = w.shape[1]
    tm = MM_TM
    tn = min(tn, n)
    in_specs = [
        pl.BlockSpec((tm, k), lambda j, i: (i, 0)),
        pl.BlockSpec((k, tn), lambda j, i: (0, j)),
    ]
    args = [a, w]
    if bias is not None:
        in_specs.append(pl.BlockSpec((1, tn), lambda j, i: (0, j)))
        args.append(bias.reshape(1, n))
    if gate_idx is not None:
        per = S_ROWS // tm
        in_specs += [
            pl.BlockSpec((tm, tn), lambda j, i: (i, j)),
            pl.BlockSpec((1, N_ADA, tn), lambda j, i: (i // per, 0, j)),
            pl.BlockSpec((1, N_ADA, tn), lambda j, i: (CTX_GROUP, 0, j)),
        ]
        args += [res, mod, mod]
    return pl.pallas_call(
        functools.partial(_mm_kernel, has_bias=bias is not None, gate_idx=gate_idx, tm=tm),
        grid=(n // tn, m // tm),
        in_specs=in_specs,
        out_specs=pl.BlockSpec((tm, tn), lambda j, i: (i, j)),
        out_shape=jax.ShapeDtypeStruct((m, n), out_dtype),
        compiler_params=_cparams(("arbitrary", "arbitrary"), 52),
        name="mm",
    )(*args)


def _gla_la_kernel(low_ref, w2_ref, b_ref, o_ref):
    low = low_ref[...]
    for z in range(2):
        lz = low[:, z * GLA_GATE_RANK:(z + 1) * GLA_GATE_RANK].astype(BF16)
        logit = jnp.dot(lz, w2_ref[z].astype(BF16), preferred_element_type=F32) + b_ref[z]
        log_sig = jnp.minimum(logit, 0.0) - jnp.log1p(jnp.exp(-jnp.abs(logit)))
        o_ref[z] = log_sig / GLA_GATE_TEMP


def _gla_la(low, w_a2, b_a):
    tm = MM_TM
    return pl.pallas_call(
        _gla_la_kernel,
        grid=(M_ROWS // tm,),
        in_specs=[
            pl.BlockSpec((tm, 128), lambda i: (i, 0)),
            pl.BlockSpec((2, GLA_GATE_RANK, GLA_DK), lambda i: (0, 0, 0)),
            pl.BlockSpec((2, 1, GLA_DK), lambda i: (0, 0, 0)),
        ],
        out_specs=pl.BlockSpec((2, tm, GLA_DK), lambda i: (0, i, 0)),
        out_shape=jax.ShapeDtypeStruct((2, M_ROWS, GLA_DK), F32),
        compiler_params=_cparams(("arbitrary",), 48),
        name="gla_la",
    )(low, w_a2, b_a.reshape(2, 1, GLA_DK))


def _gla_tables():
    c = GLA_CHUNK
    t = np.arange(c)
    blocks = [(t[None, :] <= t[:, None]), (t[None, :] > t[:, None])]
    pair = np.full((c, c), GLA_LEVELS + 1, np.int32)
    pair[t, t] = GLA_LEVELS
    for lvl in range(GLA_LEVELS):
        s = c >> (lvl + 1)
        mid = (t // (2 * s)) * 2 * s + s - 1
        second = (t % (2 * s)) >= s
        dq = (t[None, :] > mid[:, None]) & (t[None, :] <= t[:, None])
        dk = (t[None, :] > t[:, None]) & (t[None, :] <= mid[:, None])
        blocks.append(np.where(second[:, None], dq, dk))
        same = (t[:, None] // (2 * s)) == (t[None, :] // (2 * s))
        pair[same & second[:, None] & ~second[None, :]] = lvl
    d_f = np.concatenate(blocks, axis=0).astype(np.float32)
    d_b = np.concatenate([blk[::-1, ::-1] for blk in blocks], axis=0).astype(np.float32)
    return np.stack([d_f, d_b]), np.stack([pair, pair[::-1, ::-1]])


def _gla_chunk(q_ref, k_ref, v_ref, la_ref, d_ref, pair_ref, st_ref, o_ref, r, last_row):
    c = GLA_CHUNK
    rows = pl.ds(r, c)
    la = la_ref[0, 0, rows, :]
    hi = la.astype(BF16)
    lo = (la - hi.astype(F32)).astype(BF16)
    dmat = d_ref[0]
    p = jnp.exp(jnp.dot(dmat, hi, preferred_element_type=F32) + jnp.dot(dmat, lo, preferred_element_type=F32))
    q = q_ref[0, rows, :].astype(F32) * (GLA_DK_HEAD ** -0.5)
    k = k_ref[0, rows, :].astype(F32)
    v = v_ref[0, rows, :]
    nt = (((1,), (1,)), ((), ()))
    st = st_ref[...]
    o = lax.dot_general((q * p[0:c]).astype(BF16), st.astype(BF16), nt, preferred_element_type=F32)
    pair = pair_ref[0]
    a = jnp.where(pair == GLA_LEVELS,
                  lax.dot_general(q.astype(BF16), k.astype(BF16), nt, preferred_element_type=F32), 0.0)
    for lvl in range(GLA_LEVELS):
        pl_ = p[(2 + lvl) * c:(3 + lvl) * c]
        a_l = lax.dot_general((q * pl_).astype(BF16), (k * pl_).astype(BF16), nt, preferred_element_type=F32)
        a = jnp.where(pair == lvl, a_l, a)
    o = o + jnp.dot(a.astype(BF16), v, preferred_element_type=F32)
    o_ref[0, rows, :] = o
    kt = (k * p[c:2 * c]).astype(BF16)
    tn = (((0,), (0,)), ((), ()))
    decay = p[last_row:last_row + 1]
    st_ref[...] = st * decay + lax.dot_general(v, kt, tn, preferred_element_type=F32)


def _gla_scan_kernel(qf, kf, vf, laf, qb, kb, vb, lab, d_ref, pair_ref, of_ref, ob_ref, sf_ref, sb_ref):
    @pl.when(pl.program_id(2) == 0)
    def _():
        sf_ref[...] = jnp.zeros_like(sf_ref)
        sb_ref[...] = jnp.zeros_like(sb_ref)

    n = ROW_BLK // GLA_CHUNK

    for i in range(n):
        rf = i * GLA_CHUNK
        rb = (n - 1 - i) * GLA_CHUNK
        _gla_chunk(qf, kf, vf, laf, d_ref.at[0:1], pair_ref.at[0:1], sf_ref, of_ref, rf, GLA_CHUNK - 1)
        _gla_chunk(qb, kb, vb, lab, d_ref.at[1:2], pair_ref.at[1:2], sb_ref, ob_ref, rb, 0)


def _fwd_block(t):
    return jnp.where(t == 0, BLKS - 1, t - 1)


def _bwd_block(t):
    return jnp.where(t == 0, BLKS - 1, BLKS - 1 - t)


def _gla_scan(proj, la):
    d_np, pair_np = _gla_tables()
    dmat = jnp.asarray(d_np, BF16)
    pair = jnp.asarray(pair_np, jnp.int32)
    kq, kv = GLA_DK_HEAD, GLA_DV_HEAD
    k_off = GLA_DK // kq
    v_off = 2 * GLA_DK // kv
    c = GLA_CHUNK
    spec_f = [
        pl.BlockSpec((1, ROW_BLK, kq), lambda b, h, t: (b, _fwd_block(t), h)),
        pl.BlockSpec((1, ROW_BLK, kq), lambda b, h, t: (b, _fwd_block(t), k_off + h)),
        pl.BlockSpec((1, ROW_BLK, kv), lambda b, h, t: (b, _fwd_block(t), v_off + h)),
        pl.BlockSpec((1, 1, ROW_BLK, kq), lambda b, h, t: (0, b, _fwd_block(t), h)),
    ]
    spec_b = [
        pl.BlockSpec((1, ROW_BLK, kq), lambda b, h, t: (b, _bwd_block(t), h)),
        pl.BlockSpec((1, ROW_BLK, kq), lambda b, h, t: (b, _bwd_block(t), k_off + h)),
        pl.BlockSpec((1, ROW_BLK, kv), lambda b, h, t: (b, _bwd_block(t), v_off + h)),
        pl.BlockSpec((1, 1, ROW_BLK, kq), lambda b, h, t: (1, b, _bwd_block(t), h)),
    ]
    const = [
        pl.BlockSpec((2, (2 + GLA_LEVELS) * c, c), lambda b, h, t: (0, 0, 0)),
        pl.BlockSpec((2, c, c), lambda b, h, t: (0, 0, 0)),
    ]
    out_sd = jax.ShapeDtypeStruct((BATCH, S_ROWS, GLA_DV), F32)
    return pl.pallas_call(
        _gla_scan_kernel,
        grid=(BATCH, GLA_HEADS, BLKS),
        in_specs=spec_f + spec_b + const,
        out_specs=[
            pl.BlockSpec((1, ROW_BLK, kv), lambda b, h, t: (b, _fwd_block(t), h)),
            pl.BlockSpec((1, ROW_BLK, kv), lambda b, h, t: (b, _bwd_block(t), h)),
        ],
        out_shape=[out_sd, out_sd],
        scratch_shapes=[pltpu.VMEM((kv, kq), F32), pltpu.VMEM((kv, kq), F32)],
        compiler_params=_cparams(("arbitrary", "arbitrary", "arbitrary"), 32),
        name="gla_scan",
    )(proj, proj, proj, la, proj, proj, proj, la, dmat, pair)


def _gla_post_kernel(of_ref, ob_ref, g_ref, hn_ref, o_ref):
    hn = hn_ref[...]
    for h in range(GLA_HEADS):
        cols = slice(h * GLA_DV_HEAD, (h + 1) * GLA_DV_HEAD)
        o = of_ref[0, :, cols] + ob_ref[0, :, cols]
        o = o * lax.rsqrt(jnp.mean(o * o, axis=-1, keepdims=True) + EPS) * hn
        g = g_ref[0, :, cols].astype(F32)
        o_ref[0, :, cols] = (o.astype(BF16) * (g * jax.nn.sigmoid(g)).astype(BF16))


def _gla_post(o_f, o_b, proj, head_norm):
    g_off = 2 * GLA_DK // GLA_DV + 1
    blk = pl.BlockSpec((1, ROW_BLK, GLA_DV), lambda b, t: (b, t, 0))
    return pl.pallas_call(
        _gla_post_kernel,
        grid=(BATCH, BLKS),
        in_specs=[blk, blk,
                  pl.BlockSpec((1, ROW_BLK, GLA_DV), lambda b, t: (b, t, g_off)),
                  pl.BlockSpec((1, GLA_DV_HEAD), lambda b, t: (0, 0))],
        out_specs=blk,
        out_shape=jax.ShapeDtypeStruct((BATCH, S_ROWS, GLA_DV), BF16),
        compiler_params=_cparams(("arbitrary", "arbitrary"), 32),
        name="gla_post",
    )(o_f, o_b, proj, head_norm.reshape(1, GLA_DV_HEAD))


def _gla_layer(xs, h, mod, w_in, w_a1, w_a2, b_a, head_norm, w_out):
    hf = h.reshape(M_ROWS, D_MODEL)
    proj = _mm(hf, w_in.astype(BF16), out_dtype=BF16).reshape(BATCH, S_ROWS, GLA_IN)
    w_low = jnp.zeros((D_MODEL, 128), F32)
    w_low = w_low.at[:, :GLA_GATE_RANK].set(w_a1[0]).at[:, GLA_GATE_RANK:2 * GLA_GATE_RANK].set(w_a1[1])
    low = _mm(hf, w_low.astype(BF16), tn=128)
    la = _gla_la(low, w_a2, b_a).reshape(2, BATCH, S_ROWS, GLA_DK)
    o_f, o_b = _gla_scan(proj, la)
    y = _gla_post(o_f, o_b, proj, head_norm)
    out = _mm(y.reshape(M_ROWS, GLA_DV), w_out.astype(BF16), res=xs.reshape(M_ROWS, D_MODEL), mod=mod, gate_idx=2)
    return out.reshape(BATCH, S_ROWS, D_MODEL)


def _hy_proj_kernel(a_ref, w0, w1, w2, bi0, bi1, bi2, cw0, cw1, cw2, cb0, cb1, cb2, x0_ref, u_ref, *, tm):
    rows = lax.broadcasted_iota(jnp.int32, (tm, 1), 0)
    last_tile = pl.program_id(1) % (S_ROWS // tm) == S_ROWS // tm - 1
    ctx0 = tm - CTX_LEN
    is_ctx = jnp.logical_and(last_tile, rows >= ctx0)
    pos = jnp.where(is_ctx, rows - ctx0, rows % GRID_W)
    last = jnp.where(is_ctx, CTX_LEN - 1, GRID_W - 1)
    has_prev = pos != 0
    has_next = pos != last
    a = a_ref[...]

    def branch(w_ref, bi_ref, cw_ref, cb_ref):
        p = jnp.dot(a, w_ref[...], preferred_element_type=F32) + bi_ref[...]
        prev = jnp.where(has_prev, pltpu.roll(p, 1, 0), 0.0)
        nxt = jnp.where(has_next, pltpu.roll(p, tm - 1, 0), 0.0)
        return prev * cw_ref[0:1, :] + p * cw_ref[1:2, :] + nxt * cw_ref[2:3, :] + cb_ref[...]

    x0_ref[...] = branch(w0, bi0, cw0, cb0).astype(BF16)
    u_ref[...] = (branch(w1, bi1, cw1, cb1) * branch(w2, bi2, cw2, cb2)).astype(BF16)


def _hy_proj(a, w_in, b_in, conv_w, conv_b):
    tm, tn = MM_TM, 512
    nj = D_MODEL // tn
    wspec = lambda k: pl.BlockSpec((D_MODEL, tn), lambda j, i: (0, k * nj + j))
    bspec = lambda k: pl.BlockSpec((1, tn), lambda j, i: (0, k * nj + j))
    cspec = lambda k: pl.BlockSpec((3, tn), lambda j, i: (0, k * nj + j))
    ospec = pl.BlockSpec((tm, tn), lambda j, i: (i, j))
    out_sd = jax.ShapeDtypeStruct((M_ROWS, D_MODEL), BF16)
    bi = b_in.reshape(1, 3 * D_MODEL)
    cb = conv_b.reshape(1, 3 * D_MODEL)
    return pl.pallas_call(
        functools.partial(_hy_proj_kernel, tm=tm),
        grid=(nj, M_ROWS // tm),
        in_specs=[pl.BlockSpec((tm, D_MODEL), lambda j, i: (i, 0)), wspec(0), wspec(1), wspec(2),
                  bspec(0), bspec(1), bspec(2), cspec(0), cspec(1), cspec(2), bspec(0), bspec(1), bspec(2)],
        out_specs=[ospec, ospec],
        out_shape=[out_sd, out_sd],
        compiler_params=_cparams(("arbitrary", "arbitrary"), 52),
        name="hy_proj",
    )(a, w_in, w_in, w_in, bi, bi, bi, conv_w, conv_w, conv_w, cb, cb, cb)


def _hy_filter_kernel(z_ref, w1, b1, w2, b2, w3, b3, w4f, w4b, dl_ref, o_ref):
    def layer(x, w_ref, b_ref):
        y = jnp.dot(x.astype(BF16), w_ref[...].astype(BF16), preferred_element_type=F32) + b_ref[...]
        return jnp.sin(HY_SIN_FREQ * y)

    z = z_ref[...]
    hdn = layer(layer(layer(z, w1, b1), w2, b2), w3, b3).astype(BF16)
    decay = jnp.exp(-z[:, 0:1] * dl_ref[...])
    h_fwd = jnp.dot(hdn, w4f[...].astype(BF16), preferred_element_type=F32) * decay
    h_bwd = jnp.dot(hdn, w4b[...].astype(BF16), preferred_element_type=F32) * decay
    lag0 = lax.broadcasted_iota(jnp.int32, (z.shape[0], 1), 0) == 0
    h_bwd = jnp.where(lag0, 0.0, h_bwd)
    inv = 1.0 / jnp.sum(jnp.abs(h_fwd) + jnp.abs(h_bwd), axis=0, keepdims=True)
    for k, sig in enumerate(((h_fwd + h_bwd) * inv, (h_bwd - h_fwd) * inv)):
        hi = sig.astype(BF16)
        o_ref[2 * k] = hi
        o_ref[2 * k + 1] = (sig - hi.astype(F32)).astype(BF16)


def _hy_filter(L, f_w1, f_b1, f_w2, f_b2, f_w3, f_b3, f_w4):
    tn = 512
    nj = D_MODEL // tn
    t = jnp.linspace(0.0, 1.0, L, dtype=F32)[:, None]
    w = 2 * math.pi * jnp.arange(L, dtype=F32)[:, None] / L
    f = jnp.linspace(1e-4, HY_BANDS - 1, HY_BANDS, dtype=F32)[None, :]
    emb = 1 + 2 * HY_BANDS
    z = jnp.concatenate([t, jnp.cos(f * w), -jnp.sin(f * w), jnp.zeros((L, 128 - emb), F32)], axis=-1)
    w1 = jnp.concatenate([f_w1, jnp.zeros((128 - emb, f_w1.shape[1]), F32)], axis=0)
    max_decay = math.log(HY_DECAY_TARGET) / HY_FAST_DECAY_PCT
    min_decay = math.log(HY_DECAY_TARGET) / HY_SLOW_DECAY_PCT
    deltas = jnp.abs(jnp.linspace(min_decay, max_decay, D_MODEL, dtype=F32))[None, :]
    hid = f_w2.shape[0]
    full = lambda shape: pl.BlockSpec(shape, lambda j: (0, 0))
    return pl.pallas_call(
        _hy_filter_kernel,
        grid=(nj,),
        in_specs=[full((L, 128)), full((128, hid)), full((1, hid)), full((hid, hid)), full((1, hid)),
                  full((hid, hid)), full((1, hid)),
                  pl.BlockSpec((hid, tn), lambda j: (0, j)), pl.BlockSpec((hid, tn), lambda j: (0, nj + j)),
                  pl.BlockSpec((1, tn), lambda j: (0, j))],
        out_specs=pl.BlockSpec((4, L, tn), lambda j: (0, 0, j)),
        out_shape=jax.ShapeDtypeStruct((4, L, D_MODEL), BF16),
        compiler_params=_cparams(("arbitrary",), 48),
        name="hy_filter",
    )(z, w1, f_b1.reshape(1, hid), f_w2, f_b2.reshape(1, hid), f_w3, f_b3.reshape(1, hid), f_w4, f_w4, deltas)


def _dft_tiles(L):
    return (min(2 * L, 1024), min(L, 1024))


def _dft_mats(L):
    n = 2 * L
    tf = _dft_tiles(L)[0] // 2
    f = jnp.arange(L, dtype=jnp.int32)
    ang = ((f[:, None] * f[None, :]) % n).astype(F32) * (2 * math.pi / n)
    cosm = jnp.cos(ang)
    sinm = jnp.sin(ang).at[0].set(jnp.where(f % 2 == 0, 1.0, -1.0))
    fwd = jnp.concatenate([cosm.reshape(L // tf, tf, L), sinm.reshape(L // tf, tf, L)], axis=1).reshape(n, L)
    wgt = jnp.where(f == 0, 1.0 / n, 2.0 / n)[None, :]
    inv = jnp.concatenate([cosm.T * wgt, -sinm.T * wgt], axis=1)
    fwd_hi = fwd.astype(BF16)
    return fwd_hi, (fwd - fwd_hi.astype(F32)).astype(BF16), inv.astype(BF16)


def _dft_fwd_kernel(f_ref, x_ref, *rest, tf):
    acc = jnp.dot(f_ref[...], x_ref[0], preferred_element_type=F32)
    if len(rest) == 1:
        rest[0][0] = acc
    else:
        ka, kb, kap, o_ref = rest
        c, s = acc[:tf], acc[tf:]
        o_ref[0, 0] = (c * ka[...] + s * kb[...]).astype(BF16)
        o_ref[0, 1] = (c * kb[...] - s * kap[...]).astype(BF16)


def _dft_fwd(fwd, x, row_blk, L, spec=None):
    nb = x.shape[0]
    tm = _dft_tiles(L)[0]
    tf, tn = tm // 2, 1024
    in_specs = [pl.BlockSpec((tm, L), lambda b, j, i: (i, 0)),
                pl.BlockSpec((1, L, tn), lambda b, j, i: (b, row_blk, j))]
    args = [fwd, x]
    if spec is None:
        out_spec = pl.BlockSpec((1, tm, tn), lambda b, j, i: (b, i, j))
        out_shape = jax.ShapeDtypeStruct((nb, 2 * L, D_MODEL), F32)
    else:
        in_specs += [pl.BlockSpec((tf, tn), lambda b, j, i: (i, j))] * 3
        args += list(spec)
        out_spec = pl.BlockSpec((1, 2, tf, tn), lambda b, j, i: (b, 0, i, j))
        out_shape = jax.ShapeDtypeStruct((nb, 2, L, D_MODEL), BF16)
    return pl.pallas_call(
        functools.partial(_dft_fwd_kernel, tf=tf),
        grid=(nb, D_MODEL // tn, 2 * L // tm),
        in_specs=in_specs,
        out_specs=out_spec,
        out_shape=out_shape,
        compiler_params=_cparams(("arbitrary", "arbitrary", "arbitrary"), 52),
        name="dft_fwd",
    )(*args)


def _filter_spectrum(sig, fwd_hi, fwd_lo, L):
    tm = _dft_tiles(L)[0]

    def split(s):
        r = s.reshape(s.shape[0], 2 * L // tm, 2, tm // 2, D_MODEL)
        return r[:, :, 0].reshape(-1, L, D_MODEL), r[:, :, 1].reshape(-1, L, D_MODEL)

    c1, s1 = split(_dft_fwd(fwd_hi, sig, 0, L))
    c2, s2 = split(_dft_fwd(fwd_lo, sig[0::2], 0, L))
    ka = c1[0] + c1[1] + c2[0]
    nyq = s1[0, 0] + s1[1, 0] + s2[0, 0]
    kb = (s1[2] + s1[3] + s2[1]).at[0].set(0.0)
    return ka, kb, ka.at[0].set(nyq)


def _dft_inv_kernel(g_ref, p_ref, x0_ref, u_ref, skip_ref, *rest):
    o_ref = rest[-1]
    y = jnp.dot(g_ref[...], p_ref[0], preferred_element_type=F32)
    o_ref[0] = (x0_ref[0].astype(F32) * (y + u_ref[0].astype(F32) * skip_ref[...])).astype(BF16)


def _dft_inv(inv, p, x0, u, skip, L, prev=None):
    tm = _dft_tiles(L)[1]
    tn = 512
    off = 0 if L == SEQ else SEQ // tm
    row = lambda b, j, i: (b, off + i, j)
    in_specs = [pl.BlockSpec((tm, 2 * L), lambda b, j, i: (i, 0)),
                pl.BlockSpec((1, 2 * L, tn), lambda b, j, i: (b, 0, j)),
                pl.BlockSpec((1, tm, tn), row), pl.BlockSpec((1, tm, tn), row),
                pl.BlockSpec((1, tn), lambda b, j, i: (0, j))]
    args = [inv, p, x0, u, skip.reshape(1, D_MODEL)]
    aliases = {}
    if prev is not None:
        in_specs.append(pl.BlockSpec(memory_space=pl.ANY))
        args.append(prev)
        aliases = {len(args) - 1: 0}
    return pl.pallas_call(
        _dft_inv_kernel,
        grid=(BATCH, D_MODEL // tn, L // tm),
        in_specs=in_specs,
        out_specs=pl.BlockSpec((1, tm, tn), row),
        out_shape=jax.ShapeDtypeStruct((BATCH, S_ROWS, D_MODEL), BF16),
        input_output_aliases=aliases,
        compiler_params=_cparams(("arbitrary", "arbitrary", "arbitrary"), 52),
        name="dft_inv",
    )(*args)


def _hyena_layer(xs, h, mod, w_in, b_in, conv_w, conv_b, fw, skip, w_out, b_out):
    hf = h.reshape(M_ROWS, D_MODEL)
    x0, u = _hy_proj(hf, w_in.astype(BF16), b_in, conv_w, conv_b)
    x0 = x0.reshape(BATCH, S_ROWS, D_MODEL)
    u = u.reshape(BATCH, S_ROWS, D_MODEL)
    y = None
    for L in (SEQ, CTX_LEN):
        fwd_hi, fwd_lo, inv = _dft_mats(L)
        spec = _filter_spectrum(_hy_filter(L, *fw), fwd_hi, fwd_lo, L)
        row_blk = 0 if L == SEQ else SEQ // L
        p = _dft_fwd(fwd_hi, u, row_blk, L, spec=spec).reshape(BATCH, 2 * L, D_MODEL)
        y = _dft_inv(inv, p, x0, u, skip, L, prev=y)
    out = _mm(y.reshape(M_ROWS, D_MODEL), w_out.astype(BF16), bias=b_out,
              res=xs.reshape(M_ROWS, D_MODEL), mod=mod, gate_idx=2)
    return out.reshape(BATCH, S_ROWS, D_MODEL)


def _ffn_up_kernel(x_ref, w1_ref, w3_ref, o_ref):
    x = x_ref[0]
    h1 = jnp.dot(x, w1_ref[0].astype(BF16), preferred_element_type=F32)
    h3 = jnp.dot(x, w3_ref[0].astype(BF16), preferred_element_type=F32)
    o_ref[0] = (h1 * jax.nn.sigmoid(h1) * h3).astype(BF16)


def _ffn_down_kernel(a_ref, w2_ref, s_ref, o_ref):
    y = jnp.dot(a_ref[0], w2_ref[0].astype(BF16), preferred_element_type=F32) * s_ref[0]
    o_ref[0] = y.astype(o_ref.dtype)


def _ffn(xe, w1, w3, w2, score):
    tm, tf, tn = FFN_TM, 512, 1024
    act = pl.pallas_call(
        _ffn_up_kernel,
        grid=(N_EXPERTS, EXPERT_FF // tf, EXP_ROWS // tm),
        in_specs=[
            pl.BlockSpec((1, tm, D_MODEL), lambda e, f, i: (e, i, 0)),
            pl.BlockSpec((1, D_MODEL, tf), lambda e, f, i: (e, 0, f)),
            pl.BlockSpec((1, D_MODEL, tf), lambda e, f, i: (e, 0, f)),
        ],
        out_specs=pl.BlockSpec((1, tm, tf), lambda e, f, i: (e, i, f)),
        out_shape=jax.ShapeDtypeStruct((N_EXPERTS, EXP_ROWS, EXPERT_FF), BF16),
        compiler_params=_cparams(("arbitrary", "arbitrary", "arbitrary"), 52),
        name="moe_ffn_up",
    )(xe, w1, w3)
    return pl.pallas_call(
        _ffn_down_kernel,
        grid=(N_EXPERTS, D_MODEL // tn, EXP_ROWS // tm),
        in_specs=[
            pl.BlockSpec((1, tm, EXPERT_FF), lambda e, j, i: (e, i, 0)),
            pl.BlockSpec((1, EXPERT_FF, tn), lambda e, j, i: (e, 0, j)),
            pl.BlockSpec((1, tm, 1), lambda e, j, i: (e, i, 0)),
        ],
        out_specs=pl.BlockSpec((1, tm, tn), lambda e, j, i: (e, i, j)),
        out_shape=jax.ShapeDtypeStruct((N_EXPERTS, EXP_ROWS, D_MODEL), BF16),
        compiler_params=_cparams(("arbitrary", "arbitrary", "arbitrary"), 52),
        name="moe_ffn_down",
    )(act, w2, score)


def _moe_combine_kernel(il_ref, ic_ref, yl_ref, yc_ref, x_ref, mb_ref, mc_ref, o_ref, ql_ref, qc_ref):
    @pl.when(pl.program_id(1) == 0)
    def _():
        chunk = 512
        tok = lax.broadcasted_iota(jnp.int32, (SEQ, chunk), 0)
        for kk in range(N_EXPERTS * CAP_LAT // chunk):
            cols = slice(kk * chunk, (kk + 1) * chunk)
            ql_ref[:, cols] = jnp.where(il_ref[0, :, cols] == tok, 1.0, 0.0).astype(BF16)
        tok_c = lax.broadcasted_iota(jnp.int32, (CTX_LEN, N_EXPERTS * CAP_CTX), 0)
        qc_ref[...] = jnp.where(ic_ref[0] == tok_c, 1.0, 0.0).astype(BF16)

    tn = o_ref.shape[-1]
    lat = jnp.dot(ql_ref[...], yl_ref[...].reshape(N_EXPERTS * CAP_LAT, tn), preferred_element_type=F32)
    o_ref[0, :SEQ, :] = x_ref[0, :SEQ, :] + mb_ref[0, 5:6, :] * lat
    cx = jnp.dot(qc_ref[...], yc_ref[...].reshape(N_EXPERTS * CAP_CTX, tn), preferred_element_type=F32)
    o_ref[0, SEQ:, :] = x_ref[0, SEQ:, :] + mc_ref[0, 5:6, :] * cx


def _moe_combine(xs, y, idx_l, idx_c, mod):
    tn = 256
    return pl.pallas_call(
        _moe_combine_kernel,
        grid=(BATCH, D_MODEL // tn),
        in_specs=[
            pl.BlockSpec((1, 1, N_EXPERTS * CAP_LAT), lambda b, j: (b, 0, 0)),
            pl.BlockSpec((1, 1, N_EXPERTS * CAP_CTX), lambda b, j: (b, 0, 0)),
            pl.BlockSpec((N_EXPERTS, CAP_LAT, tn), lambda b, j: (0, b, j)),
            pl.BlockSpec((N_EXPERTS, CAP_CTX, tn), lambda b, j: (0, BATCH * CAP_LAT // CAP_CTX + b, j)),
            pl.BlockSpec((1, S_ROWS, tn), lambda b, j: (b, 0, j)),
            pl.BlockSpec((1, N_ADA, tn), lambda b, j: (b, 0, j)),
            pl.BlockSpec((1, N_ADA, tn), lambda b, j: (CTX_GROUP, 0, j)),
        ],
        out_specs=pl.BlockSpec((1, S_ROWS, tn), lambda b, j: (b, 0, j)),
        out_shape=jax.ShapeDtypeStruct((BATCH, S_ROWS, D_MODEL), F32),
        scratch_shapes=[pltpu.VMEM((SEQ, N_EXPERTS * CAP_LAT), BF16),
                        pltpu.VMEM((CTX_LEN, N_EXPERTS * CAP_CTX), BF16)],
        compiler_params=_cparams(("arbitrary", "arbitrary"), 56),
        name="moe_combine",
    )(idx_l, idx_c, y, y, xs, mod, mod)


def _moe_layer(xs, h, mod, w_router, w1, w3, w2):
    hf = h.reshape(M_ROWS, D_MODEL)
    wr = jnp.zeros((D_MODEL, 128), F32).at[:, :N_EXPERTS].set(w_router)
    logits = _mm(hf, wr.astype(BF16), tn=128)[:, :N_EXPERTS].reshape(BATCH, S_ROWS, N_EXPERTS)
    s = jax.nn.softmax(logits, axis=-1)
    top_l, idx_l = lax.top_k(jnp.swapaxes(s[:, :SEQ], 1, 2), CAP_LAT)
    top_c, idx_c = lax.top_k(jnp.swapaxes(s[:, SEQ:], 1, 2), CAP_CTX)
    base = (jnp.arange(BATCH, dtype=jnp.int32) * S_ROWS)[:, None, None]

    def per_expert(t):
        return jnp.swapaxes(t, 0, 1).reshape(N_EXPERTS, -1)

    rows = jnp.concatenate([per_expert(idx_l + base), per_expert(idx_c + base + SEQ)], axis=1)
    score = jnp.concatenate([per_expert(top_l), per_expert(top_c)], axis=1)
    xe = jnp.take(hf, rows, axis=0)
    y = _ffn(xe, w1, w3, w2, score[..., None])
    return _moe_combine(xs, y, idx_l.reshape(BATCH, 1, -1), idx_c.reshape(BATCH, 1, -1), mod)


def kernel(x, c, ctx, c_ctx, w_ada, b_ada, norm_mix, norm_ffn, norm_final, gla_w_in, gla_w_a1, gla_w_a2, gla_b_a, gla_head_norm, gla_w_out, hy_w_in, hy_b_in, hy_conv_w, hy_conv_b, hy_f_w1, hy_f_b1, hy_f_w2, hy_f_b2, hy_f_w3, hy_f_b3, hy_f_w4, hy_skip, hy_w_out, hy_b_out, moe_router, moe_w1, moe_w3, moe_w2):
    cond = jnp.zeros((N_GROUPS, D_MODEL), F32).at[:BATCH].set(c).at[CTX_GROUP].set(c_ctx)
    mods = _ada(cond, w_ada, b_ada).reshape(DEPTH, N_GROUPS, N_ADA, D_MODEL)
    xs = jnp.concatenate([x, ctx], axis=1)
    for i in range(DEPTH):
        mod = mods[i]
        j = i // 2
        h = _norm_mod(xs, norm_mix[i], mod, 0, 1)
        if i % 2 == 0:
            xs = _gla_layer(xs, h, mod, gla_w_in[j], gla_w_a1[j], gla_w_a2[j], gla_b_a[j], gla_head_norm[j],
                            gla_w_out[j])
        else:
            fw = (hy_f_w1[j], hy_f_b1[j], hy_f_w2[j], hy_f_b2[j], hy_f_w3[j], hy_f_b3[j], hy_f_w4[j])
            xs = _hyena_layer(xs, h, mod, hy_w_in[j], hy_b_in[j], hy_conv_w[j], hy_conv_b[j], fw, hy_skip[j],
                              hy_w_out[j], hy_b_out[j])
        h = _norm_mod(xs, norm_ffn[i], mod, 3, 4)
        xs = _moe_layer(xs, h, mod, moe_router[i], moe_w1[i], moe_w3[i], moe_w2[i])
    return _final_norm(xs, norm_final)
```

```python
import functools
import math

import jax
import jax.numpy as jnp
import numpy as np
from jax import lax
from jax.experimental import pallas as pl
from jax.experimental.pallas import tpu as pltpu

D_MODEL = 2048
BATCH = 8
SEQ = 2048
DEPTH = 4
CTX_LEN = 256
GRID_W = 64
EPS = 1e-6
N_ADA = 6

GLA_HEADS = 4
GLA_DK = D_MODEL // 2
GLA_DV = D_MODEL
GLA_DK_HEAD = GLA_DK // GLA_HEADS
GLA_DV_HEAD = GLA_DV // GLA_HEADS
GLA_GATE_RANK = 16
GLA_GATE_TEMP = 16.0
GLA_IN = 2 * GLA_DK + 2 * GLA_DV

HY_BANDS = 16
HY_SIN_FREQ = 1.0
HY_FAST_DECAY_PCT = 0.3
HY_SLOW_DECAY_PCT = 1.5
HY_DECAY_TARGET = 1e-2

N_EXPERTS = 16
EXPERT_FF = D_MODEL // 2
CAPACITY_FACTOR = 2

S_ROWS = CTX_LEN + SEQ
M_ROWS = BATCH * S_ROWS
CTX_GROUP = BATCH
N_GROUPS = 16
ROW_BLK = 256
BLKS = S_ROWS // ROW_BLK
MM_TM = S_ROWS // 2
GLA_CHUNK = ROW_BLK
GLA_LEVELS = 8
CAP_LAT = CAPACITY_FACTOR * SEQ // N_EXPERTS
CAP_CTX = CAPACITY_FACTOR * CTX_LEN // N_EXPERTS
EXP_ROWS = BATCH * (CAP_LAT + CAP_CTX)
FFN_TM = EXP_ROWS // 2

F32 = jnp.float32
BF16 = jnp.bfloat16
MIB = 1024 * 1024


def _cparams(sem, vmem_mib):
    return pltpu.CompilerParams(dimension_semantics=sem, vmem_limit_bytes=vmem_mib * MIB)


def _ada_kernel(c_ref, w_ref, b_ref, o_ref):
    c = c_ref[...]
    s = (c * jax.nn.sigmoid(c)).astype(BF16)
    o_ref[0] = jnp.dot(s, w_ref[0].astype(BF16), preferred_element_type=F32) + b_ref[0]


def _ada(cond, w_ada, b_ada):
    tn = 1024
    n = N_ADA * D_MODEL
    return pl.pallas_call(
        _ada_kernel,
        grid=(DEPTH, n // tn),
        in_specs=[
            pl.BlockSpec((N_GROUPS, D_MODEL), lambda l, j: (0, 0)),
            pl.BlockSpec((1, D_MODEL, tn), lambda l, j: (l, 0, j)),
            pl.BlockSpec((1, 1, tn), lambda l, j: (l, 0, j)),
        ],
        out_specs=pl.BlockSpec((1, N_GROUPS, tn), lambda l, j: (l, 0, j)),
        out_shape=jax.ShapeDtypeStruct((DEPTH, N_GROUPS, n), F32),
        compiler_params=_cparams(("arbitrary", "arbitrary"), 40),
        name="ada",
    )(cond, w_ada, b_ada.reshape(DEPTH, 1, n))


def _norm_mod_kernel(x_ref, g_ref, m_ref, o_ref, *, sh, sc):
    x = x_ref[0]
    y = x * lax.rsqrt(jnp.mean(x * x, axis=-1, keepdims=True) + EPS)
    y = y * g_ref[...]
    o_ref[0] = (y * (1.0 + m_ref[0, sc:sc + 1, :]) + m_ref[0, sh:sh + 1, :]).astype(o_ref.dtype)


def _group_of_block(b, t):
    return jnp.where(t == BLKS - 1, CTX_GROUP, b)


def _norm_mod(xs, g, mod, sh, sc):
    return pl.pallas_call(
        functools.partial(_norm_mod_kernel, sh=sh, sc=sc),
        grid=(BATCH, BLKS),
        in_specs=[
            pl.BlockSpec((1, ROW_BLK, D_MODEL), lambda b, t: (b, t, 0)),
            pl.BlockSpec((1, D_MODEL), lambda b, t: (0, 0)),
            pl.BlockSpec((1, N_ADA, D_MODEL), lambda b, t: (_group_of_block(b, t), 0, 0)),
        ],
        out_specs=pl.BlockSpec((1, ROW_BLK, D_MODEL), lambda b, t: (b, t, 0)),
        out_shape=jax.ShapeDtypeStruct((BATCH, S_ROWS, D_MODEL), BF16),
        compiler_params=_cparams(("arbitrary", "arbitrary"), 32),
        name="norm_mod",
    )(xs, g.reshape(1, D_MODEL), mod)


def _final_norm_kernel(x_ref, g_ref, o_ref):
    x = x_ref[0]
    y = x * lax.rsqrt(jnp.mean(x * x, axis=-1, keepdims=True) + EPS)
    o_ref[0] = y * g_ref[...]


def _final_norm(xs, g):
    return pl.pallas_call(
        _final_norm_kernel,
        grid=(BATCH, SEQ // ROW_BLK),
        in_specs=[
            pl.BlockSpec((1, ROW_BLK, D_MODEL), lambda b, t: (b, t, 0)),
            pl.BlockSpec((1, D_MODEL), lambda b, t: (0, 0)),
        ],
        out_specs=pl.BlockSpec((1, ROW_BLK, D_MODEL), lambda b, t: (b, t, 0)),
        out_shape=jax.ShapeDtypeStruct((BATCH, SEQ, D_MODEL), F32),
        compiler_params=_cparams(("arbitrary", "arbitrary"), 32),
        name="final_norm",
    )(xs, g.reshape(1, D_MODEL))


def _mm_kernel(*refs, has_bias, gate_idx, tm):
    a_ref, w_ref = refs[0], refs[1]
    pos = 2
    acc = jnp.dot(a_ref[...], w_ref[...], preferred_element_type=F32)
    if has_bias:
        acc = acc + refs[pos][...]
        pos += 1
    if gate_idx is None:
        o_ref = refs[pos]
        o_ref[...] = acc.astype(o_ref.dtype)
    else:
        res_ref, mb_ref, mc_ref, o_ref = refs[pos:pos + 4]
        gate_b = mb_ref[0, gate_idx:gate_idx + 1, :]
        gate_c = mc_ref[0, gate_idx:gate_idx + 1, :]
        rows = lax.broadcasted_iota(jnp.int32, (tm, 1), 0)
        last_tile = pl.program_id(1) % (S_ROWS // tm) == S_ROWS // tm - 1
        is_ctx = jnp.logical_and(last_tile, rows >= tm - CTX_LEN)
        o_ref[...] = res_ref[...] + jnp.where(is_ctx, gate_c, gate_b) * acc


def _mm(a, w, bias=None, out_dtype=F32, tn=1024, res=None, mod=None, gate_idx=None):
    m, k = a.shape
    n = w.shape[1]
    tm = MM_TM
    tn = min(tn, n)
    in_specs = [
        pl.BlockSpec((tm, k), lambda j, i: (i, 0)),
        pl.BlockSpec((k, tn), lambda j, i: (0, j)),
    ]
    args = [a, w]
    if bias is not None:
        in_specs.append(pl.BlockSpec((1, tn), lambda j, i: (0, j)))
        args.append(bias.reshape(1, n))
    if gate_idx is not None:
        per = S_ROWS // tm
        in_specs += [
            pl.BlockSpec((tm, tn), lambda j, i: (i, j)),
            pl.BlockSpec((1, N_ADA, tn), lambda j, i: (i // per, 0, j)),
            pl.BlockSpec((1, N_ADA, tn), lambda j, i: (CTX_GROUP, 0, j)),
        ]
        args += [res, mod, mod]
    return pl.pallas_call(
        functools.partial(_mm_kernel, has_bias=bias is not None, gate_idx=gate_idx, tm=tm),
        grid=(n // tn, m // tm),
        in_specs=in_specs,
        out_specs=pl.BlockSpec((tm, tn), lambda j, i: (i, j)),
        out_shape=jax.ShapeDtypeStruct((m, n), out_dtype),
        compiler_params=_cparams(("arbitrary", "arbitrary"), 52),
        name="mm",
    )(*args)


def _gla_la_kernel(low_ref, w2_ref, b_ref, o_ref):
    low = low_ref[...]
    for z in range(2):
        lz = low[:, z * GLA_GATE_RANK:(z + 1) * GLA_GATE_RANK].astype(BF16)
        logit = jnp.dot(lz, w2_ref[z].astype(BF16), preferred_element_type=F32) + b_ref[z]
        log_sig = jnp.minimum(logit, 0.0) - jnp.log1p(jnp.exp(-jnp.abs(logit)))
        o_ref[z] = log_sig / GLA_GATE_TEMP


def _gla_la(low, w_a2, b_a):
    tm = MM_TM
    return pl.pallas_call(
        _gla_la_kernel,
        grid=(M_ROWS // tm,),
        in_specs=[
            pl.BlockSpec((tm, 128), lambda i: (i, 0)),
            pl.BlockSpec((2, GLA_GATE_RANK, GLA_DK), lambda i: (0, 0, 0)),
            pl.BlockSpec((2, 1, GLA_DK), lambda i: (0, 0, 0)),
        ],
        out_specs=pl.BlockSpec((2, tm, GLA_DK), lambda i: (0, i, 0)),
        out_shape=jax.ShapeDtypeStruct((2, M_ROWS, GLA_DK), F32),
        compiler_params=_cparams(("arbitrary",), 48),
        name="gla_la",
    )(low, w_a2, b_a.reshape(2, 1, GLA_DK))


def _gla_tables():
    c = GLA_CHUNK
    t = np.arange(c)
    pair = np.full((c, c), GLA_LEVELS + 1, np.int32)
    pair[t, t] = GLA_LEVELS
    for lvl in range(GLA_LEVELS):
        s = c >> (lvl + 1)
        second = (t % (2 * s)) >= s
        same = (t[:, None] // (2 * s)) == (t[None, :] // (2 * s))
        pair[same & second[:, None] & ~second[None, :]] = lvl
    lower = (t[None, :] <= t[:, None]).astype(np.float32)
    return np.stack([lower, lower.T]), np.stack([pair, pair.T])


def _gla_dir(q_ref, k_ref, v_ref, la_ref, tri_ref, pair_ref, st_ref, o_ref, reverse):
    c = GLA_CHUNK
    la = la_ref[0, 0]
    hi = la.astype(BF16)
    lo = (la - hi.astype(F32)).astype(BF16)
    tri = tri_ref[0]
    b = jnp.dot(tri, hi, preferred_element_type=F32) + jnp.dot(tri, lo, preferred_element_type=F32)
    q = q_ref[0].astype(F32) * (GLA_DK_HEAD ** -0.5)
    k = k_ref[0].astype(F32)
    v = v_ref[0]
    last = 0 if reverse else c - 1
    b_last = b[last:last + 1]
    nt = (((1,), (1,)), ((), ()))
    st = st_ref[...]
    o = lax.dot_general((q * jnp.exp(b)).astype(BF16), st.astype(BF16), nt, preferred_element_type=F32)
    pair = pair_ref[0]
    a = jnp.where(pair == GLA_LEVELS,
                  lax.dot_general(q.astype(BF16), k.astype(BF16), nt, preferred_element_type=F32), 0.0)
    for lvl in range(GLA_LEVELS):
        s = c >> (lvl + 1)
        n = c // (2 * s)
        mid = s if reverse else s - 1
        b_mid = jnp.broadcast_to(b.reshape(n, 2 * s, -1)[:, mid:mid + 1, :], (n, 2 * s, b.shape[-1]))
        e = jnp.exp(-jnp.abs(b - b_mid.reshape(b.shape)))
        a_l = lax.dot_general((q * e).astype(BF16), (k * e).astype(BF16), nt, preferred_element_type=F32)
        a = jnp.where(pair == lvl, a_l, a)
    o_ref[0] = o + jnp.dot(a.astype(BF16), v, preferred_element_type=F32)
    kt = (k * jnp.exp(b_last - b)).astype(BF16)
    tn = (((0,), (0,)), ((), ()))
    st_ref[...] = st * jnp.exp(b_last) + lax.dot_general(v, kt, tn, preferred_element_type=F32)


def _gla_scan_kernel(qf, kf, vf, laf, qb, kb, vb, lab, tri_ref, pair_ref, of_ref, ob_ref, sf_ref, sb_ref):
    @pl.when(pl.program_id(2) == 0)
    def _():
        sf_ref[...] = jnp.zeros_like(sf_ref)
        sb_ref[...] = jnp.zeros_like(sb_ref)

    _gla_dir(qf, kf, vf, laf, tri_ref.at[0:1], pair_ref.at[0:1], sf_ref, of_ref, False)
    _gla_dir(qb, kb, vb, lab, tri_ref.at[1:2], pair_ref.at[1:2], sb_ref, ob_ref, True)


def _fwd_block(t):
    return jnp.where(t == 0, BLKS - 1, t - 1)


def _bwd_block(t):
    return jnp.where(t == 0, BLKS - 1, BLKS - 1 - t)


def _gla_scan(proj, la):
    tri_np, pair_np = _gla_tables()
    tri = jnp.asarray(tri_np, BF16)
    pair = jnp.asarray(pair_np, jnp.int32)
    kq, kv = GLA_DK_HEAD, GLA_DV_HEAD
    k_off = GLA_DK // kq
    v_off = 2 * GLA_DK // kv
    c = GLA_CHUNK
    spec_f = [
        pl.BlockSpec((1, ROW_BLK, kq), lambda b, h, t: (b, _fwd_block(t), h)),
        pl.BlockSpec((1, ROW_BLK, kq), lambda b, h, t: (b, _fwd_block(t), k_off + h)),
        pl.BlockSpec((1, ROW_BLK, kv), lambda b, h, t: (b, _fwd_block(t), v_off + h)),
        pl.BlockSpec((1, 1, ROW_BLK, kq), lambda b, h, t: (0, b, _fwd_block(t), h)),
    ]
    spec_b = [
        pl.BlockSpec((1, ROW_BLK, kq), lambda b, h, t: (b, _bwd_block(t), h)),
        pl.BlockSpec((1, ROW_BLK, kq), lambda b, h, t: (b, _bwd_block(t), k_off + h)),
        pl.BlockSpec((1, ROW_BLK, kv), lambda b, h, t: (b, _bwd_block(t), v_off + h)),
        pl.BlockSpec((1, 1, ROW_BLK, kq), lambda b, h, t: (1, b, _bwd_block(t), h)),
    ]
    const = [
        pl.BlockSpec((2, c, c), lambda b, h, t: (0, 0, 0)),
        pl.BlockSpec((2, c, c), lambda b, h, t: (0, 0, 0)),
    ]
    out_sd = jax.ShapeDtypeStruct((BATCH, S_ROWS, GLA_DV), F32)
    return pl.pallas_call(
        _gla_scan_kernel,
        grid=(BATCH, GLA_HEADS, BLKS),
        in_specs=spec_f + spec_b + const,
        out_specs=[
            pl.BlockSpec((1, ROW_BLK, kv), lambda b, h, t: (b, _fwd_block(t), h)),
            pl.BlockSpec((1, ROW_BLK, kv), lambda b, h, t: (b, _bwd_block(t), h)),
        ],
        out_shape=[out_sd, out_sd],
        scratch_shapes=[pltpu.VMEM((kv, kq), F32), pltpu.VMEM((kv, kq), F32)],
        compiler_params=_cparams(("arbitrary", "arbitrary", "arbitrary"), 32),
        name="gla_scan",
    )(proj, proj, proj, la, proj, proj, proj, la, tri, pair)


def _gla_post_kernel(of_ref, ob_ref, g_ref, hn_ref, o_ref):
    hn = hn_ref[...]
    for h in range(GLA_HEADS):
        cols = slice(h * GLA_DV_HEAD, (h + 1) * GLA_DV_HEAD)
        o = of_ref[0, :, cols] + ob_ref[0, :, cols]
        o = o * lax.rsqrt(jnp.mean(o * o, axis=-1, keepdims=True) + EPS) * hn
        g = g_ref[0, :, cols].astype(F32)
        o_ref[0, :, cols] = (o.astype(BF16) * (g * jax.nn.sigmoid(g)).astype(BF16))


def _gla_post(o_f, o_b, proj, head_norm):
    g_off = 2 * GLA_DK // GLA_DV + 1
    blk = pl.BlockSpec((1, ROW_BLK, GLA_DV), lambda b, t: (b, t, 0))
    return pl.pallas_call(
        _gla_post_kernel,
        grid=(BATCH, BLKS),
        in_specs=[blk, blk,
                  pl.BlockSpec((1, ROW_BLK, GLA_DV), lambda b, t: (b, t, g_off)),
                  pl.BlockSpec((1, GLA_DV_HEAD), lambda b, t: (0, 0))],
        out_specs=blk,
        out_shape=jax.ShapeDtypeStruct((BATCH, S_ROWS, GLA_DV), BF16),
        compiler_params=_cparams(("arbitrary", "arbitrary"), 32),
        name="gla_post",
    )(o_f, o_b, proj, head_norm.reshape(1, GLA_DV_HEAD))


def _gla_layer(xs, h, mod, w_in, w_a1, w_a2, b_a, head_norm, w_out):
    hf = h.reshape(M_ROWS, D_MODEL)
    proj = _mm(hf, w_in.astype(BF16), out_dtype=BF16).reshape(BATCH, S_ROWS, GLA_IN)
    w_low = jnp.zeros((D_MODEL, 128), F32)
    w_low = w_low.at[:, :GLA_GATE_RANK].set(w_a1[0]).at[:, GLA_GATE_RANK:2 * GLA_GATE_RANK].set(w_a1[1])
    low = _mm(hf, w_low.astype(BF16), tn=128)
    la = _gla_la(low, w_a2, b_a).reshape(2, BATCH, S_ROWS, GLA_DK)
    o_f, o_b = _gla_scan(proj, la)
    y = _gla_post(o_f, o_b, proj, head_norm)
    out = _mm(y.reshape(M_ROWS, GLA_DV), w_out.astype(BF16), res=xs.reshape(M_ROWS, D_MODEL), mod=mod, gate_idx=2)
    return out.reshape(BATCH, S_ROWS, D_MODEL)


def _hy_proj_kernel(a_ref, w0, w1, w2, bi0, bi1, bi2, cw0, cw1, cw2, cb0, cb1, cb2, x0_ref, u_ref, *, tm):
    rows = lax.broadcasted_iota(jnp.int32, (tm, 1), 0)
    last_tile = pl.program_id(1) % (S_ROWS // tm) == S_ROWS // tm - 1
    ctx0 = tm - CTX_LEN
    is_ctx = jnp.logical_and(last_tile, rows >= ctx0)
    pos = jnp.where(is_ctx, rows - ctx0, rows % GRID_W)
    last = jnp.where(is_ctx, CTX_LEN - 1, GRID_W - 1)
    has_prev = pos != 0
    has_next = pos != last
    a = a_ref[...]

    def branch(w_ref, bi_ref, cw_ref, cb_ref):
        p = jnp.dot(a, w_ref[...], preferred_element_type=F32) + bi_ref[...]
        prev = jnp.where(has_prev, pltpu.roll(p, 1, 0), 0.0)
        nxt = jnp.where(has_next, pltpu.roll(p, tm - 1, 0), 0.0)
        return prev * cw_ref[0:1, :] + p * cw_ref[1:2, :] + nxt * cw_ref[2:3, :] + cb_ref[...]

    x0_ref[...] = branch(w0, bi0, cw0, cb0).astype(BF16)
    u_ref[...] = (branch(w1, bi1, cw1, cb1) * branch(w2, bi2, cw2, cb2)).astype(BF16)


def _hy_proj(a, w_in, b_in, conv_w, conv_b):
    tm, tn = MM_TM, 512
    nj = D_MODEL // tn
    wspec = lambda k: pl.BlockSpec((D_MODEL, tn), lambda j, i: (0, k * nj + j))
    bspec = lambda k: pl.BlockSpec((1, tn), lambda j, i: (0, k * nj + j))
    cspec = lambda k: pl.BlockSpec((3, tn), lambda j, i: (0, k * nj + j))
    ospec = pl.BlockSpec((tm, tn), lambda j, i: (i, j))
    out_sd = jax.ShapeDtypeStruct((M_ROWS, D_MODEL), BF16)
    bi = b_in.reshape(1, 3 * D_MODEL)
    cb = conv_b.reshape(1, 3 * D_MODEL)
    return pl.pallas_call(
        functools.partial(_hy_proj_kernel, tm=tm),
        grid=(nj, M_ROWS // tm),
        in_specs=[pl.BlockSpec((tm, D_MODEL), lambda j, i: (i, 0)), wspec(0), wspec(1), wspec(2),
                  bspec(0), bspec(1), bspec(2), cspec(0), cspec(1), cspec(2), bspec(0), bspec(1), bspec(2)],
        out_specs=[ospec, ospec],
        out_shape=[out_sd, out_sd],
        compiler_params=_cparams(("arbitrary", "arbitrary"), 52),
        name="hy_proj",
    )(a, w_in, w_in, w_in, bi, bi, bi, conv_w, conv_w, conv_w, cb, cb, cb)


def _hy_filter_kernel(z_ref, w1, b1, w2, b2, w3, b3, w4f, w4b, dl_ref, o_ref):
    def layer(x, w_ref, b_ref):
        y = jnp.dot(x.astype(BF16), w_ref[...].astype(BF16), preferred_element_type=F32) + b_ref[...]
        return jnp.sin(HY_SIN_FREQ * y)

    z = z_ref[...]
    hdn = layer(layer(layer(z, w1, b1), w2, b2), w3, b3).astype(BF16)
    decay = jnp.exp(-z[:, 0:1] * dl_ref[...])
    h_fwd = jnp.dot(hdn, w4f[...].astype(BF16), preferred_element_type=F32) * decay
    h_bwd = jnp.dot(hdn, w4b[...].astype(BF16), preferred_element_type=F32) * decay
    lag0 = lax.broadcasted_iota(jnp.int32, (z.shape[0], 1), 0) == 0
    h_bwd = jnp.where(lag0, 0.0, h_bwd)
    inv = 1.0 / jnp.sum(jnp.abs(h_fwd) + jnp.abs(h_bwd), axis=0, keepdims=True)
    for k, sig in enumerate(((h_fwd + h_bwd) * inv, (h_bwd - h_fwd) * inv)):
        hi = sig.astype(BF16)
        o_ref[2 * k] = hi
        o_ref[2 * k + 1] = (sig - hi.astype(F32)).astype(BF16)


def _hy_filter(L, f_w1, f_b1, f_w2, f_b2, f_w3, f_b3, f_w4):
    tn = 512
    nj = D_MODEL // tn
    t = jnp.linspace(0.0, 1.0, L, dtype=F32)[:, None]
    w = 2 * math.pi * jnp.arange(L, dtype=F32)[:, None] / L
    f = jnp.linspace(1e-4, HY_BANDS - 1, HY_BANDS, dtype=F32)[None, :]
    emb = 1 + 2 * HY_BANDS
    z = jnp.concatenate([t, jnp.cos(f * w), -jnp.sin(f * w), jnp.zeros((L, 128 - emb), F32)], axis=-1)
    w1 = jnp.concatenate([f_w1, jnp.zeros((128 - emb, f_w1.shape[1]), F32)], axis=0)
    max_decay = math.log(HY_DECAY_TARGET) / HY_FAST_DECAY_PCT
    min_decay = math.log(HY_DECAY_TARGET) / HY_SLOW_DECAY_PCT
    deltas = jnp.abs(jnp.linspace(min_decay, max_decay, D_MODEL, dtype=F32))[None, :]
    hid = f_w2.shape[0]
    full = lambda shape: pl.BlockSpec(shape, lambda j: (0, 0))
    return pl.pallas_call(
        _hy_filter_kernel,
        grid=(nj,),
        in_specs=[full((L, 128)), full((128, hid)), full((1, hid)), full((hid, hid)), full((1, hid)),
                  full((hid, hid)), full((1, hid)),
                  pl.BlockSpec((hid, tn), lambda j: (0, j)), pl.BlockSpec((hid, tn), lambda j: (0, nj + j)),
                  pl.BlockSpec((1, tn), lambda j: (0, j))],
        out_specs=pl.BlockSpec((4, L, tn), lambda j: (0, 0, j)),
        out_shape=jax.ShapeDtypeStruct((4, L, D_MODEL), BF16),
        compiler_params=_cparams(("arbitrary",), 48),
        name="hy_filter",
    )(z, w1, f_b1.reshape(1, hid), f_w2, f_b2.reshape(1, hid), f_w3, f_b3.reshape(1, hid), f_w4, f_w4, deltas)


def _dft_tiles(L):
    return (min(2 * L, 1024), min(L, 1024))


def _dft_mats(L):
    n = 2 * L
    tf = _dft_tiles(L)[0] // 2
    f = jnp.arange(L, dtype=jnp.int32)
    ang = ((f[:, None] * f[None, :]) % n).astype(F32) * (2 * math.pi / n)
    cosm = jnp.cos(ang)
    sinm = jnp.sin(ang).at[0].set(jnp.where(f % 2 == 0, 1.0, -1.0))
    fwd = jnp.concatenate([cosm.reshape(L // tf, tf, L), sinm.reshape(L // tf, tf, L)], axis=1).reshape(n, L)
    wgt = jnp.where(f == 0, 1.0 / n, 2.0 / n)[None, :]
    inv = jnp.concatenate([cosm.T * wgt, -sinm.T * wgt], axis=1)
    fwd_hi = fwd.astype(BF16)
    return fwd_hi, (fwd - fwd_hi.astype(F32)).astype(BF16), inv.astype(BF16)


def _dft_fwd_kernel(f_ref, x_ref, ka, kb, kap, o_ref, *, tf):
    acc = jnp.dot(f_ref[...], x_ref[0], preferred_element_type=F32)
    c, s = acc[:tf], acc[tf:]
    o_ref[0, 0] = (c * ka[...] + s * kb[...]).astype(BF16)
    o_ref[0, 1] = (c * kb[...] - s * kap[...]).astype(BF16)


def _dft_fwd(fwd, x, row_blk, L, spec):
    nb = x.shape[0]
    tm = _dft_tiles(L)[0]
    tf, tn = tm // 2, 1024
    kspec = pl.BlockSpec((tf, tn), lambda b, j, i: (i, j))
    return pl.pallas_call(
        functools.partial(_dft_fwd_kernel, tf=tf),
        grid=(nb, D_MODEL // tn, 2 * L // tm),
        in_specs=[pl.BlockSpec((tm, L), lambda b, j, i: (i, 0)),
                  pl.BlockSpec((1, L, tn), lambda b, j, i: (b, row_blk, j)), kspec, kspec, kspec],
        out_specs=pl.BlockSpec((1, 2, tf, tn), lambda b, j, i: (b, 0, i, j)),
        out_shape=jax.ShapeDtypeStruct((nb, 2, L, D_MODEL), BF16),
        compiler_params=_cparams(("arbitrary", "arbitrary", "arbitrary"), 52),
        name="dft_fwd",
    )(fwd, x, *spec)


def _filter_spectrum_kernel(fh_ref, fl_ref, sig_ref, ka_ref, kb_ref, kap_ref, *, tf):
    fh, fl = fh_ref[...], fl_ref[...]

    def dft(rows, hi, lo):
        return (jnp.dot(fh[rows], hi, preferred_element_type=F32) + jnp.dot(fh[rows], lo, preferred_element_type=F32)
                + jnp.dot(fl[rows], hi, preferred_element_type=F32))

    ka = dft(slice(0, tf), sig_ref[0], sig_ref[1])
    kb = dft(slice(tf, 2 * tf), sig_ref[2], sig_ref[3])
    nyq = dft(slice(tf, tf + 16), sig_ref[0], sig_ref[1])[0:1]
    freq = lax.broadcasted_iota(jnp.int32, (tf, 1), 0) + pl.program_id(1) * tf
    ka_ref[...] = ka
    kb_ref[...] = jnp.where(freq == 0, 0.0, kb)
    kap_ref[...] = jnp.where(freq == 0, nyq, ka)


def _filter_spectrum(sig, fwd_hi, fwd_lo, L):
    tm = _dft_tiles(L)[0]
    tf, tn = tm // 2, 256
    fspec = pl.BlockSpec((tm, L), lambda j, i: (i, 0))
    ospec = pl.BlockSpec((tf, tn), lambda j, i: (i, j))
    out_sd = jax.ShapeDtypeStruct((L, D_MODEL), F32)
    return pl.pallas_call(
        functools.partial(_filter_spectrum_kernel, tf=tf),
        grid=(D_MODEL // tn, 2 * L // tm),
        in_specs=[fspec, fspec, pl.BlockSpec((4, L, tn), lambda j, i: (0, 0, j))],
        out_specs=[ospec, ospec, ospec],
        out_shape=[out_sd, out_sd, out_sd],
        compiler_params=_cparams(("arbitrary", "arbitrary"), 52),
        name="filter_spectrum",
    )(fwd_hi, fwd_lo, sig)


def _dft_inv_kernel(g_ref, p_ref, x0_ref, u_ref, skip_ref, *rest):
    o_ref = rest[-1]
    y = jnp.dot(g_ref[...], p_ref[0], preferred_element_type=F32)
    o_ref[0] = (x0_ref[0].astype(F32) * (y + u_ref[0].astype(F32) * skip_ref[...])).astype(BF16)


def _dft_inv(inv, p, x0, u, skip, L, prev=None):
    tm = _dft_tiles(L)[1]
    tn = 512
    off = 0 if L == SEQ else SEQ // tm
    row = lambda b, j, i: (b, off + i, j)
    in_specs = [pl.BlockSpec((tm, 2 * L), lambda b, j, i: (i, 0)),
                pl.BlockSpec((1, 2 * L, tn), lambda b, j, i: (b, 0, j)),
                pl.BlockSpec((1, tm, tn), row), pl.BlockSpec((1, tm, tn), row),
                pl.BlockSpec((1, tn), lambda b, j, i: (0, j))]
    args = [inv, p, x0, u, skip.reshape(1, D_MODEL)]
    aliases = {}
    if prev is not None:
        in_specs.append(pl.BlockSpec(memory_space=pl.ANY))
        args.append(prev)
        aliases = {len(args) - 1: 0}
    return pl.pallas_call(
        _dft_inv_kernel,
        grid=(BATCH, D_MODEL // tn, L // tm),
        in_specs=in_specs,
        out_specs=pl.BlockSpec((1, tm, tn), row),
        out_shape=jax.ShapeDtypeStruct((BATCH, S_ROWS, D_MODEL), BF16),
        input_output_aliases=aliases,
        compiler_params=_cparams(("arbitrary", "arbitrary", "arbitrary"), 52),
        name="dft_inv",
    )(*args)


def _hyena_layer(xs, h, mod, w_in, b_in, conv_w, conv_b, fw, skip, w_out, b_out):
    hf = h.reshape(M_ROWS, D_MODEL)
    x0, u = _hy_proj(hf, w_in.astype(BF16), b_in, conv_w, conv_b)
    x0 = x0.reshape(BATCH, S_ROWS, D_MODEL)
    u = u.reshape(BATCH, S_ROWS, D_MODEL)
    y = None
    for L in (SEQ, CTX_LEN):
        fwd_hi, fwd_lo, inv = _dft_mats(L)
        spec = _filter_spectrum(_hy_filter(L, *fw), fwd_hi, fwd_lo, L)
        row_blk = 0 if L == SEQ else SEQ // L
        p = _dft_fwd(fwd_hi, u, row_blk, L, spec=spec).reshape(BATCH, 2 * L, D_MODEL)
        y = _dft_inv(inv, p, x0, u, skip, L, prev=y)
    out = _mm(y.reshape(M_ROWS, D_MODEL), w_out.astype(BF16), bias=b_out,
              res=xs.reshape(M_ROWS, D_MODEL), mod=mod, gate_idx=2)
    return out.reshape(BATCH, S_ROWS, D_MODEL)


def _ffn_up_kernel(x_ref, w1_ref, w3_ref, o_ref):
    x = x_ref[0]
    h1 = jnp.dot(x, w1_ref[0].astype(BF16), preferred_element_type=F32)
    h3 = jnp.dot(x, w3_ref[0].astype(BF16), preferred_element_type=F32)
    o_ref[0] = (h1 * jax.nn.sigmoid(h1) * h3).astype(BF16)


def _ffn_down_kernel(a_ref, w2_ref, s_ref, o_ref):
    y = jnp.dot(a_ref[0], w2_ref[0].astype(BF16), preferred_element_type=F32) * s_ref[0]
    o_ref[0] = y.astype(o_ref.dtype)


def _ffn(xe, w1, w3, w2, score):
    tm, tf, tn = FFN_TM, 512, 1024
    act = pl.pallas_call(
        _ffn_up_kernel,
        grid=(N_EXPERTS, EXPERT_FF // tf, EXP_ROWS // tm),
        in_specs=[
            pl.BlockSpec((1, tm, D_MODEL), lambda e, f, i: (e, i, 0)),
            pl.BlockSpec((1, D_MODEL, tf), lambda e, f, i: (e, 0, f)),
            pl.BlockSpec((1, D_MODEL, tf), lambda e, f, i: (e, 0, f)),
        ],
        out_specs=pl.BlockSpec((1, tm, tf), lambda e, f, i: (e, i, f)),
        out_shape=jax.ShapeDtypeStruct((N_EXPERTS, EXP_ROWS, EXPERT_FF), BF16),
        compiler_params=_cparams(("arbitrary", "arbitrary", "arbitrary"), 52),
        name="moe_ffn_up",
    )(xe, w1, w3)
    return pl.pallas_call(
        _ffn_down_kernel,
        grid=(N_EXPERTS, D_MODEL // tn, EXP_ROWS // tm),
        in_specs=[
            pl.BlockSpec((1, tm, EXPERT_FF), lambda e, j, i: (e, i, 0)),
            pl.BlockSpec((1, EXPERT_FF, tn), lambda e, j, i: (e, 0, j)),
            pl.BlockSpec((1, tm, 1), lambda e, j, i: (e, i, 0)),
        ],
        out_specs=pl.BlockSpec((1, tm, tn), lambda e, j, i: (e, i, j)),
        out_shape=jax.ShapeDtypeStruct((N_EXPERTS, EXP_ROWS, D_MODEL), BF16),
        compiler_params=_cparams(("arbitrary", "arbitrary", "arbitrary"), 52),
        name="moe_ffn_down",
    )(act, w2, score)


def _moe_combine_kernel(il_ref, ic_ref, yl_ref, yc_ref, x_ref, mb_ref, mc_ref, o_ref, ql_ref, qc_ref):
    @pl.when(pl.program_id(1) == 0)
    def _():
        chunk = 512
        tok = lax.broadcasted_iota(jnp.int32, (SEQ, chunk), 0)
        for kk in range(N_EXPERTS * CAP_LAT // chunk):
            cols = slice(kk * chunk, (kk + 1) * chunk)
            ql_ref[:, cols] = jnp.where(il_ref[0, :, cols] == tok, 1.0, 0.0).astype(BF16)
        tok_c = lax.broadcasted_iota(jnp.int32, (CTX_LEN, N_EXPERTS * CAP_CTX), 0)
        qc_ref[...] = jnp.where(ic_ref[0] == tok_c, 1.0, 0.0).astype(BF16)

    tn = o_ref.shape[-1]
    lat = jnp.dot(ql_ref[...], yl_ref[...].reshape(N_EXPERTS * CAP_LAT, tn), preferred_element_type=F32)
    o_ref[0, :SEQ, :] = x_ref[0, :SEQ, :] + mb_ref[0, 5:6, :] * lat
    cx = jnp.dot(qc_ref[...], yc_ref[...].reshape(N_EXPERTS * CAP_CTX, tn), preferred_element_type=F32)
    o_ref[0, SEQ:, :] = x_ref[0, SEQ:, :] + mc_ref[0, 5:6, :] * cx


def _moe_combine(xs, y, idx_l, idx_c, mod):
    tn = 256
    return pl.pallas_call(
        _moe_combine_kernel,
        grid=(BATCH, D_MODEL // tn),
        in_specs=[
            pl.BlockSpec((1, 1, N_EXPERTS * CAP_LAT), lambda b, j: (b, 0, 0)),
            pl.BlockSpec((1, 1, N_EXPERTS * CAP_CTX), lambda b, j: (b, 0, 0)),
            pl.BlockSpec((N_EXPERTS, CAP_LAT, tn), lambda b, j: (0, b, j)),
            pl.BlockSpec((N_EXPERTS, CAP_CTX, tn), lambda b, j: (0, BATCH * CAP_LAT // CAP_CTX + b, j)),
            pl.BlockSpec((1, S_ROWS, tn), lambda b, j: (b, 0, j)),
            pl.BlockSpec((1, N_ADA, tn), lambda b, j: (b, 0, j)),
            pl.BlockSpec((1, N_ADA, tn), lambda b, j: (CTX_GROUP, 0, j)),
        ],
        out_specs=pl.BlockSpec((1, S_ROWS, tn), lambda b, j: (b, 0, j)),
        out_shape=jax.ShapeDtypeStruct((BATCH, S_ROWS, D_MODEL), F32),
        scratch_shapes=[pltpu.VMEM((SEQ, N_EXPERTS * CAP_LAT), BF16),
                        pltpu.VMEM((CTX_LEN, N_EXPERTS * CAP_CTX), BF16)],
        compiler_params=_cparams(("arbitrary", "arbitrary"), 56),
        name="moe_combine",
    )(idx_l, idx_c, y, y, xs, mod, mod)


def _moe_layer(xs, h, mod, w_router, w1, w3, w2):
    hf = h.reshape(M_ROWS, D_MODEL)
    wr = jnp.zeros((D_MODEL, 128), F32).at[:, :N_EXPERTS].set(w_router)
    logits = _mm(hf, wr.astype(BF16), tn=128)[:, :N_EXPERTS].reshape(BATCH, S_ROWS, N_EXPERTS)
    s = jax.nn.softmax(logits, axis=-1)
    top_l, idx_l = lax.top_k(jnp.swapaxes(s[:, :SEQ], 1, 2), CAP_LAT)
    top_c, idx_c = lax.top_k(jnp.swapaxes(s[:, SEQ:], 1, 2), CAP_CTX)
    base = (jnp.arange(BATCH, dtype=jnp.int32) * S_ROWS)[:, None, None]

    def per_expert(t):
        return jnp.swapaxes(t, 0, 1).reshape(N_EXPERTS, -1)

    rows = jnp.concatenate([per_expert(idx_l + base), per_expert(idx_c + base + SEQ)], axis=1)
    score = jnp.concatenate([per_expert(top_l), per_expert(top_c)], axis=1)
    xe = jnp.take(hf, rows, axis=0)
    y = _ffn(xe, w1, w3, w2, score[..., None])
    return _moe_combine(xs, y, idx_l.reshape(BATCH, 1, -1), idx_c.reshape(BATCH, 1, -1), mod)


def kernel(x, c, ctx, c_ctx, w_ada, b_ada, norm_mix, norm_ffn, norm_final, gla_w_in, gla_w_a1, gla_w_a2, gla_b_a, gla_head_norm, gla_w_out, hy_w_in, hy_b_in, hy_conv_w, hy_conv_b, hy_f_w1, hy_f_b1, hy_f_w2, hy_f_b2, hy_f_w3, hy_f_b3, hy_f_w4, hy_skip, hy_w_out, hy_b_out, moe_router, moe_w1, moe_w3, moe_w2):
    cond = jnp.zeros((N_GROUPS, D_MODEL), F32).at[:BATCH].set(c).at[CTX_GROUP].set(c_ctx)
    mods = _ada(cond, w_ada, b_ada).reshape(DEPTH, N_GROUPS, N_ADA, D_MODEL)
    xs = jnp.concatenate([x, ctx], axis=1)
    for i in range(DEPTH):
        mod = mods[i]
        j = i // 2
        h = _norm_mod(xs, norm_mix[i], mod, 0, 1)
        if i % 2 == 0:
            xs = _gla_layer(xs, h, mod, gla_w_in[j], gla_w_a1[j], gla_w_a2[j], gla_b_a[j], gla_head_norm[j],
                            gla_w_out[j])
        else:
            fw = (hy_f_w1[j], hy_f_b1[j], hy_f_w2[j], hy_f_b2[j], hy_f_w3[j], hy_f_b3[j], hy_f_w4[j])
            xs = _hyena_layer(xs, h, mod, hy_w_in[j], hy_b_in[j], hy_conv_w[j], hy_conv_b[j], fw, hy_skip[j],
                              hy_w_out[j], hy_b_out[j])
        h = _norm_mod(xs, norm_ffn[i], mod, 3, 4)
        xs = _moe_layer(xs, h, mod, moe_router[i], moe_w1[i], moe_w3[i], moe_w2[i])
    return _final_norm(xs, norm_final)
```

```python
import functools
import math

import jax
import jax.numpy as jnp
import numpy as np
from jax import lax
from jax.experimental import pallas as pl
from jax.experimental.pallas import tpu as pltpu

D_MODEL = 2048
BATCH = 8
SEQ = 2048
DEPTH = 4
CTX_LEN = 256
GRID_W = 64
EPS = 1e-6
N_ADA = 6

GLA_HEADS = 4
GLA_DK = D_MODEL // 2
GLA_DV = D_MODEL
GLA_DK_HEAD = GLA_DK // GLA_HEADS
GLA_DV_HEAD = GLA_DV // GLA_HEADS
GLA_GATE_RANK = 16
GLA_GATE_TEMP = 16.0
GLA_IN = 2 * GLA_DK + 2 * GLA_DV

HY_BANDS = 16
HY_SIN_FREQ = 1.0
HY_FAST_DECAY_PCT = 0.3
HY_SLOW_DECAY_PCT = 1.5
HY_DECAY_TARGET = 1e-2

N_EXPERTS = 16
EXPERT_FF = D_MODEL // 2
CAPACITY_FACTOR = 2

S_ROWS = CTX_LEN + SEQ
M_ROWS = BATCH * S_ROWS
CTX_GROUP = BATCH
N_GROUPS = 16
ROW_BLK = 256
BLKS = S_ROWS // ROW_BLK
MM_TM = S_ROWS // 2
GLA_CHUNK = ROW_BLK
GLA_LEVELS = 8
CAP_LAT = CAPACITY_FACTOR * SEQ // N_EXPERTS
CAP_CTX = CAPACITY_FACTOR * CTX_LEN // N_EXPERTS
EXP_ROWS = BATCH * (CAP_LAT + CAP_CTX)
FFN_TM = EXP_ROWS // 2

F32 = jnp.float32
BF16 = jnp.bfloat16
MIB = 1024 * 1024


def _cparams(sem, vmem_mib):
    return pltpu.CompilerParams(dimension_semantics=sem, vmem_limit_bytes=vmem_mib * MIB)


def _ada_kernel(c_ref, w_ref, b_ref, o_ref):
    c = c_ref[...]
    s = (c * jax.nn.sigmoid(c)).astype(BF16)
    o_ref[0] = jnp.dot(s, w_ref[0].astype(BF16), preferred_element_type=F32) + b_ref[0]


def _ada(cond, w_ada, b_ada):
    tn = 1024
    n = N_ADA * D_MODEL
    return pl.pallas_call(
        _ada_kernel,
        grid=(DEPTH, n // tn),
        in_specs=[
            pl.BlockSpec((N_GROUPS, D_MODEL), lambda l, j: (0, 0)),
            pl.BlockSpec((1, D_MODEL, tn), lambda l, j: (l, 0, j)),
            pl.BlockSpec((1, 1, tn), lambda l, j: (l, 0, j)),
        ],
        out_specs=pl.BlockSpec((1, N_GROUPS, tn), lambda l, j: (l, 0, j)),
        out_shape=jax.ShapeDtypeStruct((DEPTH, N_GROUPS, n), F32),
        compiler_params=_cparams(("arbitrary", "arbitrary"), 40),
        name="ada",
    )(cond, w_ada, b_ada.reshape(DEPTH, 1, n))


def _norm_mod_kernel(x_ref, g_ref, m_ref, o_ref, *, sh, sc):
    x = x_ref[0]
    y = x * lax.rsqrt(jnp.mean(x * x, axis=-1, keepdims=True) + EPS)
    y = y * g_ref[...]
    o_ref[0] = (y * (1.0 + m_ref[0, sc:sc + 1, :]) + m_ref[0, sh:sh + 1, :]).astype(o_ref.dtype)


def _group_of_block(b, t):
    return jnp.where(t == BLKS - 1, CTX_GROUP, b)


def _norm_mod(xs, g, mod, sh, sc):
    return pl.pallas_call(
        functools.partial(_norm_mod_kernel, sh=sh, sc=sc),
        grid=(BATCH, BLKS),
        in_specs=[
            pl.BlockSpec((1, ROW_BLK, D_MODEL), lambda b, t: (b, t, 0)),
            pl.BlockSpec((1, D_MODEL), lambda b, t: (0, 0)),
            pl.BlockSpec((1, N_ADA, D_MODEL), lambda b, t: (_group_of_block(b, t), 0, 0)),
        ],
        out_specs=pl.BlockSpec((1, ROW_BLK, D_MODEL), lambda b, t: (b, t, 0)),
        out_shape=jax.ShapeDtypeStruct((BATCH, S_ROWS, D_MODEL), BF16),
        compiler_params=_cparams(("arbitrary", "arbitrary"), 32),
        name="norm_mod",
    )(xs, g.reshape(1, D_MODEL), mod)


def _final_norm_kernel(x_ref, g_ref, o_ref):
    x = x_ref[0]
    y = x * lax.rsqrt(jnp.mean(x * x, axis=-1, keepdims=True) + EPS)
    o_ref[0] = y * g_ref[...]


def _final_norm(xs, g):
    return pl.pallas_call(
        _final_norm_kernel,
        grid=(BATCH, SEQ // ROW_BLK),
        in_specs=[
            pl.BlockSpec((1, ROW_BLK, D_MODEL), lambda b, t: (b, t, 0)),
            pl.BlockSpec((1, D_MODEL), lambda b, t: (0, 0)),
        ],
        out_specs=pl.BlockSpec((1, ROW_BLK, D_MODEL), lambda b, t: (b, t, 0)),
        out_shape=jax.ShapeDtypeStruct((BATCH, SEQ, D_MODEL), F32),
        compiler_params=_cparams(("arbitrary", "arbitrary"), 32),
        name="final_norm",
    )(xs, g.reshape(1, D_MODEL))


def _mm_kernel(*refs, has_bias, gate_idx, tm):
    a_ref, w_ref = refs[0], refs[1]
    pos = 2
    acc = jnp.dot(a_ref[...], w_ref[...], preferred_element_type=F32)
    if has_bias:
        acc = acc + refs[pos][...]
        pos += 1
    if gate_idx is None:
        o_ref = refs[pos]
        o_ref[...] = acc.astype(o_ref.dtype)
    else:
        res_ref, mb_ref, mc_ref, o_ref = refs[pos:pos + 4]
        gate_b = mb_ref[0, gate_idx:gate_idx + 1, :]
        gate_c = mc_ref[0, gate_idx:gate_idx + 1, :]
        rows = lax.broadcasted_iota(jnp.int32, (tm, 1), 0)
        last_tile = pl.program_id(1) % (S_ROWS // tm) == S_ROWS // tm - 1
        is_ctx = jnp.logical_and(last_tile, rows >= tm - CTX_LEN)
        o_ref[...] = res_ref[...] + jnp.where(is_ctx, gate_c, gate_b) * acc


def _mm(a, w, bias=None, out_dtype=F32, tn=1024, res=None, mod=None, gate_idx=None):
    m, k = a.shape
    n = w.shape[1]
    tm = MM_TM
    tn = min(tn, n)
    in_specs = [
        pl.BlockSpec((tm, k), lambda j, i: (i, 0)),
        pl.BlockSpec((k, tn), lambda j, i: (0, j)),
    ]
    args = [a, w]
    if bias is not None:
        in_specs.append(pl.BlockSpec((1, tn), lambda j, i: (0, j)))
        args.append(bias.reshape(1, n))
    if gate_idx is not None:
        per = S_ROWS // tm
        in_specs += [
            pl.BlockSpec((tm, tn), lambda j, i: (i, j)),
            pl.BlockSpec((1, N_ADA, tn), lambda j, i: (i // per, 0, j)),
            pl.BlockSpec((1, N_ADA, tn), lambda j, i: (CTX_GROUP, 0, j)),
        ]
        args += [res, mod, mod]
    return pl.pallas_call(
        functools.partial(_mm_kernel, has_bias=bias is not None, gate_idx=gate_idx, tm=tm),
        grid=(n // tn, m // tm),
        in_specs=in_specs,
        out_specs=pl.BlockSpec((tm, tn), lambda j, i: (i, j)),
        out_shape=jax.ShapeDtypeStruct((m, n), out_dtype),
        compiler_params=_cparams(("arbitrary", "arbitrary"), 52),
        name="mm",
    )(*args)


def _gla_la_kernel(low_ref, w2_ref, b_ref, o_ref):
    low = low_ref[...]
    for z in range(2):
        lz = low[:, z * GLA_GATE_RANK:(z + 1) * GLA_GATE_RANK].astype(BF16)
        logit = jnp.dot(lz, w2_ref[z].astype(BF16), preferred_element_type=F32) + b_ref[z]
        log_sig = jnp.minimum(logit, 0.0) - jnp.log1p(jnp.exp(-jnp.abs(logit)))
        o_ref[z] = log_sig / GLA_GATE_TEMP


def _gla_la(low, w_a2, b_a):
    tm = MM_TM
    return pl.pallas_call(
        _gla_la_kernel,
        grid=(M_ROWS // tm,),
        in_specs=[
            pl.BlockSpec((tm, 128), lambda i: (i, 0)),
            pl.BlockSpec((2, GLA_GATE_RANK, GLA_DK), lambda i: (0, 0, 0)),
            pl.BlockSpec((2, 1, GLA_DK), lambda i: (0, 0, 0)),
        ],
        out_specs=pl.BlockSpec((2, tm, GLA_DK), lambda i: (0, i, 0)),
        out_shape=jax.ShapeDtypeStruct((2, M_ROWS, GLA_DK), F32),
        compiler_params=_cparams(("arbitrary",), 48),
        name="gla_la",
    )(low, w_a2, b_a.reshape(2, 1, GLA_DK))


def _gla_tables():
    c = GLA_CHUNK
    t = np.arange(c)
    pair = np.full((c, c), GLA_LEVELS + 1, np.int32)
    pair[t, t] = GLA_LEVELS
    for lvl in range(GLA_LEVELS):
        s = c >> (lvl + 1)
        second = (t % (2 * s)) >= s
        same = (t[:, None] // (2 * s)) == (t[None, :] // (2 * s))
        pair[same & second[:, None] & ~second[None, :]] = lvl
    lower = (t[None, :] <= t[:, None]).astype(np.float32)
    return np.stack([lower, lower.T]), np.stack([pair, pair.T])


def _gla_dir(q_ref, k_ref, v_ref, la_ref, tri_ref, pair_ref, st_ref, o_ref, reverse):
    c = GLA_CHUNK
    la = la_ref[0, 0]
    hi = la.astype(BF16)
    lo = (la - hi.astype(F32)).astype(BF16)
    tri = tri_ref[0]
    b = jnp.dot(tri, hi, preferred_element_type=F32) + jnp.dot(tri, lo, preferred_element_type=F32)
    q = q_ref[0].astype(F32) * (GLA_DK_HEAD ** -0.5)
    k = k_ref[0].astype(F32)
    v = v_ref[0]
    last = 0 if reverse else c - 1
    b_last = b[last:last + 1]
    nt = (((1,), (1,)), ((), ()))
    st = st_ref[...]
    o = lax.dot_general((q * jnp.exp(b)).astype(BF16), st.astype(BF16), nt, preferred_element_type=F32)
    pair = pair_ref[0]
    a = jnp.where(pair == GLA_LEVELS,
                  lax.dot_general(q.astype(BF16), k.astype(BF16), nt, preferred_element_type=F32), 0.0)
    for lvl in range(GLA_LEVELS):
        s = c >> (lvl + 1)
        n = c // (2 * s)
        mid = s if reverse else s - 1
        b_mid = jnp.broadcast_to(b.reshape(n, 2 * s, -1)[:, mid:mid + 1, :], (n, 2 * s, b.shape[-1]))
        e = jnp.exp(-jnp.abs(b - b_mid.reshape(b.shape)))
        a_l = lax.dot_general((q * e).astype(BF16), (k * e).astype(BF16), nt, preferred_element_type=F32)
        a = jnp.where(pair == lvl, a_l, a)
    o_ref[0] = o + jnp.dot(a.astype(BF16), v, preferred_element_type=F32)
    kt = (k * jnp.exp(b_last - b)).astype(BF16)
    tn = (((0,), (0,)), ((), ()))
    st_ref[...] = st * jnp.exp(b_last) + lax.dot_general(v, kt, tn, preferred_element_type=F32)


def _gla_scan_kernel(qf, kf, vf, laf, qb, kb, vb, lab, tri_ref, pair_ref, of_ref, ob_ref, sf_ref, sb_ref):
    @pl.when(pl.program_id(2) == 0)
    def _():
        sf_ref[...] = jnp.zeros_like(sf_ref)
        sb_ref[...] = jnp.zeros_like(sb_ref)

    _gla_dir(qf, kf, vf, laf, tri_ref.at[0:1], pair_ref.at[0:1], sf_ref, of_ref, False)
    _gla_dir(qb, kb, vb, lab, tri_ref.at[1:2], pair_ref.at[1:2], sb_ref, ob_ref, True)


def _fwd_block(t):
    return jnp.where(t == 0, BLKS - 1, t - 1)


def _bwd_block(t):
    return jnp.where(t == 0, BLKS - 1, BLKS - 1 - t)


def _gla_scan(proj, la):
    tri_np, pair_np = _gla_tables()
    tri = jnp.asarray(tri_np, BF16)
    pair = jnp.asarray(pair_np, jnp.int32)
    kq, kv = GLA_DK_HEAD, GLA_DV_HEAD
    k_off = GLA_DK // kq
    v_off = 2 * GLA_DK // kv
    c = GLA_CHUNK
    spec_f = [
        pl.BlockSpec((1, ROW_BLK, kq), lambda b, h, t: (b, _fwd_block(t), h)),
        pl.BlockSpec((1, ROW_BLK, kq), lambda b, h, t: (b, _fwd_block(t), k_off + h)),
        pl.BlockSpec((1, ROW_BLK, kv), lambda b, h, t: (b, _fwd_block(t), v_off + h)),
        pl.BlockSpec((1, 1, ROW_BLK, kq), lambda b, h, t: (0, b, _fwd_block(t), h)),
    ]
    spec_b = [
        pl.BlockSpec((1, ROW_BLK, kq), lambda b, h, t: (b, _bwd_block(t), h)),
        pl.BlockSpec((1, ROW_BLK, kq), lambda b, h, t: (b, _bwd_block(t), k_off + h)),
        pl.BlockSpec((1, ROW_BLK, kv), lambda b, h, t: (b, _bwd_block(t), v_off + h)),
        pl.BlockSpec((1, 1, ROW_BLK, kq), lambda b, h, t: (1, b, _bwd_block(t), h)),
    ]
    const = [
        pl.BlockSpec((2, c, c), lambda b, h, t: (0, 0, 0)),
        pl.BlockSpec((2, c, c), lambda b, h, t: (0, 0, 0)),
    ]
    out_sd = jax.ShapeDtypeStruct((BATCH, S_ROWS, GLA_DV), F32)
    return pl.pallas_call(
        _gla_scan_kernel,
        grid=(BATCH, GLA_HEADS, BLKS),
        in_specs=spec_f + spec_b + const,
        out_specs=[
            pl.BlockSpec((1, ROW_BLK, kv), lambda b, h, t: (b, _fwd_block(t), h)),
            pl.BlockSpec((1, ROW_BLK, kv), lambda b, h, t: (b, _bwd_block(t), h)),
        ],
        out_shape=[out_sd, out_sd],
        scratch_shapes=[pltpu.VMEM((kv, kq), F32), pltpu.VMEM((kv, kq), F32)],
        compiler_params=_cparams(("arbitrary", "arbitrary", "arbitrary"), 32),
        name="gla_scan",
    )(proj, proj, proj, la, proj, proj, proj, la, tri, pair)


def _gla_post_kernel(of_ref, ob_ref, g_ref, hn_ref, o_ref):
    hn = hn_ref[...]
    for h in range(GLA_HEADS):
        cols = slice(h * GLA_DV_HEAD, (h + 1) * GLA_DV_HEAD)
        o = of_ref[0, :, cols] + ob_ref[0, :, cols]
        o = o * lax.rsqrt(jnp.mean(o * o, axis=-1, keepdims=True) + EPS) * hn
        g = g_ref[0, :, cols].astype(F32)
        o_ref[0, :, cols] = (o.astype(BF16) * (g * jax.nn.sigmoid(g)).astype(BF16))


def _gla_post(o_f, o_b, proj, head_norm):
    g_off = 2 * GLA_DK // GLA_DV + 1
    blk = pl.BlockSpec((1, ROW_BLK, GLA_DV), lambda b, t: (b, t, 0))
    return pl.pallas_call(
        _gla_post_kernel,
        grid=(BATCH, BLKS),
        in_specs=[blk, blk,
                  pl.BlockSpec((1, ROW_BLK, GLA_DV), lambda b, t: (b, t, g_off)),
                  pl.BlockSpec((1, GLA_DV_HEAD), lambda b, t: (0, 0))],
        out_specs=blk,
        out_shape=jax.ShapeDtypeStruct((BATCH, S_ROWS, GLA_DV), BF16),
        compiler_params=_cparams(("arbitrary", "arbitrary"), 32),
        name="gla_post",
    )(o_f, o_b, proj, head_norm.reshape(1, GLA_DV_HEAD))


def _gla_layer(xs, h, mod, w_in, w_a1, w_a2, b_a, head_norm, w_out):
    hf = h.reshape(M_ROWS, D_MODEL)
    proj = _mm(hf, w_in.astype(BF16), out_dtype=BF16).reshape(BATCH, S_ROWS, GLA_IN)
    w_low = jnp.zeros((D_MODEL, 128), F32)
    w_low = w_low.at[:, :GLA_GATE_RANK].set(w_a1[0]).at[:, GLA_GATE_RANK:2 * GLA_GATE_RANK].set(w_a1[1])
    low = _mm(hf, w_low.astype(BF16), tn=128)
    la = _gla_la(low, w_a2, b_a).reshape(2, BATCH, S_ROWS, GLA_DK)
    o_f, o_b = _gla_scan(proj, la)
    y = _gla_post(o_f, o_b, proj, head_norm)
    out = _mm(y.reshape(M_ROWS, GLA_DV), w_out.astype(BF16), res=xs.reshape(M_ROWS, D_MODEL), mod=mod, gate_idx=2)
    return out.reshape(BATCH, S_ROWS, D_MODEL)


def _hy_proj_kernel(a_ref, w0, w1, w2, bi0, bi1, bi2, cw0, cw1, cw2, cb0, cb1, cb2, x0_ref, u_ref, *, tm):
    rows = lax.broadcasted_iota(jnp.int32, (tm, 1), 0)
    last_tile = pl.program_id(1) % (S_ROWS // tm) == S_ROWS // tm - 1
    ctx0 = tm - CTX_LEN
    is_ctx = jnp.logical_and(last_tile, rows >= ctx0)
    pos = jnp.where(is_ctx, rows - ctx0, rows % GRID_W)
    last = jnp.where(is_ctx, CTX_LEN - 1, GRID_W - 1)
    has_prev = pos != 0
    has_next = pos != last
    a = a_ref[...]

    def branch(w_ref, bi_ref, cw_ref, cb_ref):
        p = jnp.dot(a, w_ref[...], preferred_element_type=F32) + bi_ref[...]
        prev = jnp.where(has_prev, pltpu.roll(p, 1, 0), 0.0)
        nxt = jnp.where(has_next, pltpu.roll(p, tm - 1, 0), 0.0)
        return prev * cw_ref[0:1, :] + p * cw_ref[1:2, :] + nxt * cw_ref[2:3, :] + cb_ref[...]

    x0_ref[...] = branch(w0, bi0, cw0, cb0).astype(BF16)
    u_ref[...] = (branch(w1, bi1, cw1, cb1) * branch(w2, bi2, cw2, cb2)).astype(BF16)


def _hy_proj(a, w_in, b_in, conv_w, conv_b):
    tm, tn = MM_TM, 512
    nj = D_MODEL // tn
    wspec = lambda k: pl.BlockSpec((D_MODEL, tn), lambda j, i: (0, k * nj + j))
    bspec = lambda k: pl.BlockSpec((1, tn), lambda j, i: (0, k * nj + j))
    cspec = lambda k: pl.BlockSpec((3, tn), lambda j, i: (0, k * nj + j))
    ospec = pl.BlockSpec((tm, tn), lambda j, i: (i, j))
    out_sd = jax.ShapeDtypeStruct((M_ROWS, D_MODEL), BF16)
    bi = b_in.reshape(1, 3 * D_MODEL)
    cb = conv_b.reshape(1, 3 * D_MODEL)
    return pl.pallas_call(
        functools.partial(_hy_proj_kernel, tm=tm),
        grid=(nj, M_ROWS // tm),
        in_specs=[pl.BlockSpec((tm, D_MODEL), lambda j, i: (i, 0)), wspec(0), wspec(1), wspec(2),
                  bspec(0), bspec(1), bspec(2), cspec(0), cspec(1), cspec(2), bspec(0), bspec(1), bspec(2)],
        out_specs=[ospec, ospec],
        out_shape=[out_sd, out_sd],
        compiler_params=_cparams(("arbitrary", "arbitrary"), 52),
        name="hy_proj",
    )(a, w_in, w_in, w_in, bi, bi, bi, conv_w, conv_w, conv_w, cb, cb, cb)


def _hy_filter_kernel(z_ref, w1, b1, w2, b2, w3, b3, w4f, w4b, dl_ref, o_ref):
    def layer(x, w_ref, b_ref):
        y = jnp.dot(x.astype(BF16), w_ref[...].astype(BF16), preferred_element_type=F32) + b_ref[...]
        return jnp.sin(HY_SIN_FREQ * y)

    z = z_ref[...]
    hdn = layer(layer(layer(z, w1, b1), w2, b2), w3, b3).astype(BF16)
    decay = jnp.exp(-z[:, 0:1] * dl_ref[...])
    h_fwd = jnp.dot(hdn, w4f[...].astype(BF16), preferred_element_type=F32) * decay
    h_bwd = jnp.dot(hdn, w4b[...].astype(BF16), preferred_element_type=F32) * decay
    lag0 = lax.broadcasted_iota(jnp.int32, (z.shape[0], 1), 0) == 0
    h_bwd = jnp.where(lag0, 0.0, h_bwd)
    inv = 1.0 / jnp.sum(jnp.abs(h_fwd) + jnp.abs(h_bwd), axis=0, keepdims=True)
    for k, sig in enumerate(((h_fwd + h_bwd) * inv, (h_bwd - h_fwd) * inv)):
        hi = sig.astype(BF16)
        o_ref[2 * k] = hi
        o_ref[2 * k + 1] = (sig - hi.astype(F32)).astype(BF16)


def _hy_filter(L, f_w1, f_b1, f_w2, f_b2, f_w3, f_b3, f_w4):
    tn = 512
    nj = D_MODEL // tn
    t = jnp.linspace(0.0, 1.0, L, dtype=F32)[:, None]
    w = 2 * math.pi * jnp.arange(L, dtype=F32)[:, None] / L
    f = jnp.linspace(1e-4, HY_BANDS - 1, HY_BANDS, dtype=F32)[None, :]
    emb = 1 + 2 * HY_BANDS
    z = jnp.concatenate([t, jnp.cos(f * w), -jnp.sin(f * w), jnp.zeros((L, 128 - emb), F32)], axis=-1)
    w1 = jnp.concatenate([f_w1, jnp.zeros((128 - emb, f_w1.shape[1]), F32)], axis=0)
    max_decay = math.log(HY_DECAY_TARGET) / HY_FAST_DECAY_PCT
    min_decay = math.log(HY_DECAY_TARGET) / HY_SLOW_DECAY_PCT
    deltas = jnp.abs(jnp.linspace(min_decay, max_decay, D_MODEL, dtype=F32))[None, :]
    hid = f_w2.shape[0]
    full = lambda shape: pl.BlockSpec(shape, lambda j: (0, 0))
    return pl.pallas_call(
        _hy_filter_kernel,
        grid=(nj,),
        in_specs=[full((L, 128)), full((128, hid)), full((1, hid)), full((hid, hid)), full((1, hid)),
                  full((hid, hid)), full((1, hid)),
                  pl.BlockSpec((hid, tn), lambda j: (0, j)), pl.BlockSpec((hid, tn), lambda j: (0, nj + j)),
                  pl.BlockSpec((1, tn), lambda j: (0, j))],
        out_specs=pl.BlockSpec((4, L, tn), lambda j: (0, 0, j)),
        out_shape=jax.ShapeDtypeStruct((4, L, D_MODEL), BF16),
        compiler_params=_cparams(("arbitrary",), 48),
        name="hy_filter",
    )(z, w1, f_b1.reshape(1, hid), f_w2, f_b2.reshape(1, hid), f_w3, f_b3.reshape(1, hid), f_w4, f_w4, deltas)


def _dft_tiles(L):
    return (min(2 * L, 1024), min(L, 1024))


def _dft_mats(L):
    n = 2 * L
    tf = _dft_tiles(L)[0] // 2
    f = jnp.arange(L, dtype=jnp.int32)
    ang = ((f[:, None] * f[None, :]) % n).astype(F32) * (2 * math.pi / n)
    cosm = jnp.cos(ang)
    sinm = jnp.sin(ang).at[0].set(jnp.where(f % 2 == 0, 1.0, -1.0))
    fwd = jnp.concatenate([cosm.reshape(L // tf, tf, L), sinm.reshape(L // tf, tf, L)], axis=1).reshape(n, L)
    wgt = jnp.where(f == 0, 1.0 / n, 2.0 / n)[None, :]
    inv = jnp.concatenate([cosm.T * wgt, -sinm.T * wgt], axis=1)
    fwd_hi = fwd.astype(BF16)
    return fwd_hi, (fwd - fwd_hi.astype(F32)).astype(BF16), inv.astype(BF16)


def _dft_fwd_kernel(f_ref, x_ref, ka, kb, kap, o_ref, *, tf):
    acc = jnp.dot(f_ref[...], x_ref[0], preferred_element_type=F32)
    c, s = acc[:tf], acc[tf:]
    o_ref[0, 0] = (c * ka[...] + s * kb[...]).astype(BF16)
    o_ref[0, 1] = (c * kb[...] - s * kap[...]).astype(BF16)


def _dft_fwd(fwd, x, row_blk, L, spec):
    nb = x.shape[0]
    tm = _dft_tiles(L)[0]
    tf, tn = tm // 2, 1024
    kspec = pl.BlockSpec((tf, tn), lambda b, j, i: (i, j))
    return pl.pallas_call(
        functools.partial(_dft_fwd_kernel, tf=tf),
        grid=(nb, D_MODEL // tn, 2 * L // tm),
        in_specs=[pl.BlockSpec((tm, L), lambda b, j, i: (i, 0)),
                  pl.BlockSpec((1, L, tn), lambda b, j, i: (b, row_blk, j)), kspec, kspec, kspec],
        out_specs=pl.BlockSpec((1, 2, tf, tn), lambda b, j, i: (b, 0, i, j)),
        out_shape=jax.ShapeDtypeStruct((nb, 2, L, D_MODEL), BF16),
        compiler_params=_cparams(("arbitrary", "arbitrary", "arbitrary"), 52),
        name="dft_fwd",
    )(fwd, x, *spec)


def _filter_spectrum_kernel(fh_ref, fl_ref, sig_ref, ka_ref, kb_ref, kap_ref, *, tf):
    fh, fl = fh_ref[...], fl_ref[...]

    def dft(rows, hi, lo):
        return (jnp.dot(fh[rows], hi, preferred_element_type=F32) + jnp.dot(fh[rows], lo, preferred_element_type=F32)
                + jnp.dot(fl[rows], hi, preferred_element_type=F32))

    ka = dft(slice(0, tf), sig_ref[0], sig_ref[1])
    kb = dft(slice(tf, 2 * tf), sig_ref[2], sig_ref[3])
    nyq = dft(slice(tf, tf + 16), sig_ref[0], sig_ref[1])[0:1]
    freq = lax.broadcasted_iota(jnp.int32, (tf, 1), 0) + pl.program_id(1) * tf
    ka_ref[...] = ka
    kb_ref[...] = jnp.where(freq == 0, 0.0, kb)
    kap_ref[...] = jnp.where(freq == 0, nyq, ka)


def _filter_spectrum(sig, fwd_hi, fwd_lo, L):
    tm = _dft_tiles(L)[0]
    tf, tn = tm // 2, 256
    fspec = pl.BlockSpec((tm, L), lambda j, i: (i, 0))
    ospec = pl.BlockSpec((tf, tn), lambda j, i: (i, j))
    out_sd = jax.ShapeDtypeStruct((L, D_MODEL), F32)
    return pl.pallas_call(
        functools.partial(_filter_spectrum_kernel, tf=tf),
        grid=(D_MODEL // tn, 2 * L // tm),
        in_specs=[fspec, fspec, pl.BlockSpec((4, L, tn), lambda j, i: (0, 0, j))],
        out_specs=[ospec, ospec, ospec],
        out_shape=[out_sd, out_sd, out_sd],
        compiler_params=_cparams(("arbitrary", "arbitrary"), 52),
        name="filter_spectrum",
    )(fwd_hi, fwd_lo, sig)


def _dft_inv_kernel(g_ref, p_ref, x0_ref, u_ref, skip_ref, *rest):
    o_ref = rest[-1]
    y = jnp.dot(g_ref[...], p_ref[0], preferred_element_type=F32)
    o_ref[0] = (x0_ref[0].astype(F32) * (y + u_ref[0].astype(F32) * skip_ref[...])).astype(BF16)


def _dft_inv(inv, p, x0, u, skip, L, prev=None):
    tm = _dft_tiles(L)[1]
    tn = 512
    off = 0 if L == SEQ else SEQ // tm
    row = lambda b, j, i: (b, off + i, j)
    in_specs = [pl.BlockSpec((tm, 2 * L), lambda b, j, i: (i, 0)),
                pl.BlockSpec((1, 2 * L, tn), lambda b, j, i: (b, 0, j)),
                pl.BlockSpec((1, tm, tn), row), pl.BlockSpec((1, tm, tn), row),
                pl.BlockSpec((1, tn), lambda b, j, i: (0, j))]
    args = [inv, p, x0, u, skip.reshape(1, D_MODEL)]
    aliases = {}
    if prev is not None:
        in_specs.append(pl.BlockSpec(memory_space=pl.ANY))
        args.append(prev)
        aliases = {len(args) - 1: 0}
    return pl.pallas_call(
        _dft_inv_kernel,
        grid=(BATCH, D_MODEL // tn, L // tm),
        in_specs=in_specs,
        out_specs=pl.BlockSpec((1, tm, tn), row),
        out_shape=jax.ShapeDtypeStruct((BATCH, S_ROWS, D_MODEL), BF16),
        input_output_aliases=aliases,
        compiler_params=_cparams(("arbitrary", "arbitrary", "arbitrary"), 52),
        name="dft_inv",
    )(*args)


def _hyena_layer(xs, h, mod, w_in, b_in, conv_w, conv_b, fw, skip, w_out, b_out):
    hf = h.reshape(M_ROWS, D_MODEL)
    x0, u = _hy_proj(hf, w_in.astype(BF16), b_in, conv_w, conv_b)
    x0 = x0.reshape(BATCH, S_ROWS, D_MODEL)
    u = u.reshape(BATCH, S_ROWS, D_MODEL)
    y = None
    for L in (SEQ, CTX_LEN):
        fwd_hi, fwd_lo, inv = _dft_mats(L)
        spec = _filter_spectrum(_hy_filter(L, *fw), fwd_hi, fwd_lo, L)
        row_blk = 0 if L == SEQ else SEQ // L
        p = _dft_fwd(fwd_hi, u, row_blk, L, spec=spec).reshape(BATCH, 2 * L, D_MODEL)
        y = _dft_inv(inv, p, x0, u, skip, L, prev=y)
    out = _mm(y.reshape(M_ROWS, D_MODEL), w_out.astype(BF16), bias=b_out,
              res=xs.reshape(M_ROWS, D_MODEL), mod=mod, gate_idx=2)
    return out.reshape(BATCH, S_ROWS, D_MODEL)


def _ffn_up_kernel(x_ref, w1_ref, w3_ref, o_ref):
    x = x_ref[0]
    h1 = jnp.dot(x, w1_ref[0, 0].astype(BF16), preferred_element_type=F32)
    h3 = jnp.dot(x, w3_ref[0, 0].astype(BF16), preferred_element_type=F32)
    o_ref[0] = (h1 * jax.nn.sigmoid(h1) * h3).astype(BF16)


def _ffn_down_kernel(a_ref, w2_ref, s_ref, o_ref):
    y = jnp.dot(a_ref[0], w2_ref[0, 0].astype(BF16), preferred_element_type=F32) * s_ref[0]
    o_ref[0] = y.astype(o_ref.dtype)


def _ffn(xe, w1, w3, w2, layer, score):
    tm, tf, tn = FFN_TM, 512, 1024
    act = pl.pallas_call(
        _ffn_up_kernel,
        grid=(N_EXPERTS, EXPERT_FF // tf, EXP_ROWS // tm),
        in_specs=[
            pl.BlockSpec((1, tm, D_MODEL), lambda e, f, i: (e, i, 0)),
            pl.BlockSpec((1, 1, D_MODEL, tf), lambda e, f, i: (layer, e, 0, f)),
            pl.BlockSpec((1, 1, D_MODEL, tf), lambda e, f, i: (layer, e, 0, f)),
        ],
        out_specs=pl.BlockSpec((1, tm, tf), lambda e, f, i: (e, i, f)),
        out_shape=jax.ShapeDtypeStruct((N_EXPERTS, EXP_ROWS, EXPERT_FF), BF16),
        compiler_params=_cparams(("arbitrary", "arbitrary", "arbitrary"), 52),
        name="moe_ffn_up",
    )(xe, w1, w3)
    return pl.pallas_call(
        _ffn_down_kernel,
        grid=(N_EXPERTS, D_MODEL // tn, EXP_ROWS // tm),
        in_specs=[
            pl.BlockSpec((1, tm, EXPERT_FF), lambda e, j, i: (e, i, 0)),
            pl.BlockSpec((1, 1, EXPERT_FF, tn), lambda e, j, i: (layer, e, 0, j)),
            pl.BlockSpec((1, tm, 1), lambda e, j, i: (e, i, 0)),
        ],
        out_specs=pl.BlockSpec((1, tm, tn), lambda e, j, i: (e, i, j)),
        out_shape=jax.ShapeDtypeStruct((N_EXPERTS, EXP_ROWS, D_MODEL), BF16),
        compiler_params=_cparams(("arbitrary", "arbitrary", "arbitrary"), 52),
        name="moe_ffn_down",
    )(act, w2, score)


def _moe_combine_kernel(il_ref, ic_ref, yl_ref, yc_ref, x_ref, mb_ref, mc_ref, o_ref, ql_ref, qc_ref):
    @pl.when(pl.program_id(1) == 0)
    def _():
        chunk = 512
        tok = lax.broadcasted_iota(jnp.int32, (SEQ, chunk), 0)
        for kk in range(N_EXPERTS * CAP_LAT // chunk):
            cols = slice(kk * chunk, (kk + 1) * chunk)
            ql_ref[:, cols] = jnp.where(il_ref[0, :, cols] == tok, 1.0, 0.0).astype(BF16)
        tok_c = lax.broadcasted_iota(jnp.int32, (CTX_LEN, N_EXPERTS * CAP_CTX), 0)
        qc_ref[...] = jnp.where(ic_ref[0] == tok_c, 1.0, 0.0).astype(BF16)

    tn = o_ref.shape[-1]
    lat = jnp.dot(ql_ref[...], yl_ref[...].reshape(N_EXPERTS * CAP_LAT, tn), preferred_element_type=F32)
    o_ref[0, :SEQ, :] = x_ref[0, :SEQ, :] + mb_ref[0, 5:6, :] * lat
    cx = jnp.dot(qc_ref[...], yc_ref[...].reshape(N_EXPERTS * CAP_CTX, tn), preferred_element_type=F32)
    o_ref[0, SEQ:, :] = x_ref[0, SEQ:, :] + mc_ref[0, 5:6, :] * cx


def _moe_combine(xs, y, idx_l, idx_c, mod):
    tn = 256
    return pl.pallas_call(
        _moe_combine_kernel,
        grid=(BATCH, D_MODEL // tn),
        in_specs=[
            pl.BlockSpec((1, 1, N_EXPERTS * CAP_LAT), lambda b, j: (b, 0, 0)),
            pl.BlockSpec((1, 1, N_EXPERTS * CAP_CTX), lambda b, j: (b, 0, 0)),
            pl.BlockSpec((N_EXPERTS, CAP_LAT, tn), lambda b, j: (0, b, j)),
            pl.BlockSpec((N_EXPERTS, CAP_CTX, tn), lambda b, j: (0, BATCH * CAP_LAT // CAP_CTX + b, j)),
            pl.BlockSpec((1, S_ROWS, tn), lambda b, j: (b, 0, j)),
            pl.BlockSpec((1, N_ADA, tn), lambda b, j: (b, 0, j)),
            pl.BlockSpec((1, N_ADA, tn), lambda b, j: (CTX_GROUP, 0, j)),
        ],
        out_specs=pl.BlockSpec((1, S_ROWS, tn), lambda b, j: (b, 0, j)),
        out_shape=jax.ShapeDtypeStruct((BATCH, S_ROWS, D_MODEL), F32),
        scratch_shapes=[pltpu.VMEM((SEQ, N_EXPERTS * CAP_LAT), BF16),
                        pltpu.VMEM((CTX_LEN, N_EXPERTS * CAP_CTX), BF16)],
        compiler_params=_cparams(("arbitrary", "arbitrary"), 56),
        name="moe_combine",
    )(idx_l, idx_c, y, y, xs, mod, mod)


def _route_stream(s_tok, s_exp, u_ref, cap, idx_ref, score_ref):
    n_tok = s_exp.shape[1]
    bits = lax.bitcast_convert_type(s_exp, jnp.int32)
    thr = jnp.zeros((N_EXPERTS, 1), jnp.int32)
    for bit in range(30, -1, -1):
        cand = thr | (1 << bit)
        cnt = jnp.sum(jnp.where(bits >= cand, 1.0, 0.0), axis=1, keepdims=True)
        thr = jnp.where(cnt >= cap, cand, thr)
    above = bits > thr
    tied = bits == thr
    n_above = jnp.sum(jnp.where(above, 1.0, 0.0), axis=1, keepdims=True)
    before = u_ref[:n_tok, :n_tok]
    tie_rank = jnp.dot(jnp.where(tied, 1.0, 0.0).astype(BF16), before, preferred_element_type=F32)
    chosen = jnp.logical_or(above, jnp.logical_and(tied, tie_rank < cap - n_above))
    pos = jnp.dot(jnp.where(chosen, 1.0, 0.0).astype(BF16), before, preferred_element_type=F32)
    slot = jnp.where(chosen, pos, -1.0)
    slot_tok = jnp.concatenate([slot, jnp.full((128 - N_EXPERTS, n_tok), -1.0, F32)], axis=0).T
    slot_ids = lax.broadcasted_iota(jnp.int32, (1, cap), 1).astype(F32)
    tok_ids = lax.broadcasted_iota(jnp.int32, (n_tok, 1), 0).astype(F32)
    for e in range(N_EXPERTS):
        hit = slot_tok[:, e:e + 1] == slot_ids
        idx_ref[0, e:e + 1, :] = jnp.sum(jnp.where(hit, tok_ids, 0.0), axis=0, keepdims=True).astype(jnp.int32)
        score_ref[0, e:e + 1, :] = jnp.sum(jnp.where(hit, s_tok[:, e:e + 1], 0.0), axis=0, keepdims=True)


def _router_kernel(lg_ref, u_ref, il_ref, sl_ref, ic_ref, sc_ref):
    lg = lg_ref[0]
    valid = lax.broadcasted_iota(jnp.int32, (1, lg.shape[1]), 1) < N_EXPERTS
    x = jnp.where(valid, lg, -jnp.inf)
    e = jnp.where(valid, jnp.exp(x - jnp.max(x, axis=-1, keepdims=True)), 0.0)
    s = e / jnp.sum(e, axis=-1, keepdims=True)
    s_exp = s.T[:N_EXPERTS]
    _route_stream(s[:SEQ], s_exp[:, :SEQ], u_ref, CAP_LAT, il_ref, sl_ref)
    _route_stream(s[SEQ:], s_exp[:, SEQ:], u_ref, CAP_CTX, ic_ref, sc_ref)


def _router(logits, before):
    out = lambda cap: pl.BlockSpec((1, N_EXPERTS, cap), lambda b: (b, 0, 0))
    sd = lambda cap, dt: jax.ShapeDtypeStruct((BATCH, N_EXPERTS, cap), dt)
    return pl.pallas_call(
        _router_kernel,
        grid=(BATCH,),
        in_specs=[pl.BlockSpec((1, S_ROWS, 128), lambda b: (b, 0, 0)),
                  pl.BlockSpec((SEQ, SEQ), lambda b: (0, 0))],
        out_specs=[out(CAP_LAT), out(CAP_LAT), out(CAP_CTX), out(CAP_CTX)],
        out_shape=[sd(CAP_LAT, jnp.int32), sd(CAP_LAT, F32), sd(CAP_CTX, jnp.int32), sd(CAP_CTX, F32)],
        compiler_params=_cparams(("arbitrary",), 52),
        name="router",
    )(logits, before)


def _moe_layer(xs, h, mod, w_router, w1, w3, w2, layer, before):
    hf = h.reshape(M_ROWS, D_MODEL)
    wr = jnp.zeros((D_MODEL, 128), F32).at[:, :N_EXPERTS].set(w_router)
    logits = _mm(hf, wr.astype(BF16), tn=128).reshape(BATCH, S_ROWS, 128)
    idx_l, top_l, idx_c, top_c = _router(logits, before)
    base = (jnp.arange(BATCH, dtype=jnp.int32) * S_ROWS)[:, None, None]

    def per_expert(t):
        return jnp.swapaxes(t, 0, 1).reshape(N_EXPERTS, -1)

    rows = jnp.concatenate([per_expert(idx_l + base), per_expert(idx_c + base + SEQ)], axis=1)
    score = jnp.concatenate([per_expert(top_l), per_expert(top_c)], axis=1)
    xe = hf.at[rows].get(mode="promise_in_bounds")
    y = _ffn(xe, w1, w3, w2, layer, score[..., None])
    return _moe_combine(xs, y, idx_l.reshape(BATCH, 1, -1), idx_c.reshape(BATCH, 1, -1), mod)


def kernel(x, c, ctx, c_ctx, w_ada, b_ada, norm_mix, norm_ffn, norm_final, gla_w_in, gla_w_a1, gla_w_a2, gla_b_a, gla_head_norm, gla_w_out, hy_w_in, hy_b_in, hy_conv_w, hy_conv_b, hy_f_w1, hy_f_b1, hy_f_w2, hy_f_b2, hy_f_w3, hy_f_b3, hy_f_w4, hy_skip, hy_w_out, hy_b_out, moe_router, moe_w1, moe_w3, moe_w2):
    cond = jnp.zeros((N_GROUPS, D_MODEL), F32).at[:BATCH].set(c).at[CTX_GROUP].set(c_ctx)
    mods = _ada(cond, w_ada, b_ada).reshape(DEPTH, N_GROUPS, N_ADA, D_MODEL)
    xs = jnp.concatenate([x, ctx], axis=1)
    tok = jnp.arange(SEQ, dtype=jnp.int32)
    before = (tok[:, None] < tok[None, :]).astype(BF16)
    for i in range(DEPTH):
        mod = mods[i]
        j = i // 2
        h = _norm_mod(xs, norm_mix[i], mod, 0, 1)
        if i % 2 == 0:
            xs = _gla_layer(xs, h, mod, gla_w_in[j], gla_w_a1[j], gla_w_a2[j], gla_b_a[j], gla_head_norm[j],
                            gla_w_out[j])
        else:
            fw = (hy_f_w1[j], hy_f_b1[j], hy_f_w2[j], hy_f_b2[j], hy_f_w3[j], hy_f_b3[j], hy_f_w4[j])
            xs = _hyena_layer(xs, h, mod, hy_w_in[j], hy_b_in[j], hy_conv_w[j], hy_conv_b[j], fw, hy_skip[j],
                              hy_w_out[j], hy_b_out[j])
        h = _norm_mod(xs, norm_ffn[i], mod, 3, 4)
        xs = _moe_layer(xs, h, mod, moe_router[i], moe_w1, moe_w3, moe_w2, i, before)
    return _final_norm(xs, norm_final)
```

```python
import functools
import math

import jax
import jax.numpy as jnp
import numpy as np
from jax import lax
from jax.experimental import pallas as pl
from jax.experimental.pallas import tpu as pltpu

D_MODEL = 2048
BATCH = 8
SEQ = 2048
DEPTH = 4
CTX_LEN = 256
GRID_W = 64
EPS = 1e-6
N_ADA = 6

GLA_HEADS = 4
GLA_DK = D_MODEL // 2
GLA_DV = D_MODEL
GLA_DK_HEAD = GLA_DK // GLA_HEADS
GLA_DV_HEAD = GLA_DV // GLA_HEADS
GLA_GATE_RANK = 16
GLA_GATE_TEMP = 16.0
GLA_IN = 2 * GLA_DK + 2 * GLA_DV

HY_BANDS = 16
HY_SIN_FREQ = 1.0
HY_FAST_DECAY_PCT = 0.3
HY_SLOW_DECAY_PCT = 1.5
HY_DECAY_TARGET = 1e-2

N_EXPERTS = 16
EXPERT_FF = D_MODEL // 2
CAPACITY_FACTOR = 2

S_ROWS = CTX_LEN + SEQ
M_ROWS = BATCH * S_ROWS
CTX_GROUP = BATCH
N_GROUPS = 16
ROW_BLK = 256
BLKS = S_ROWS // ROW_BLK
MM_TM = S_ROWS // 2
GLA_CHUNK = ROW_BLK
GLA_LEVELS = 8
GLA_SUBLANES = 8
CAP_LAT = CAPACITY_FACTOR * SEQ // N_EXPERTS
CAP_CTX = CAPACITY_FACTOR * CTX_LEN // N_EXPERTS
EXP_ROWS = BATCH * (CAP_LAT + CAP_CTX)
FFN_TM = EXP_ROWS // 2

F32 = jnp.float32
BF16 = jnp.bfloat16
MIB = 1024 * 1024


def _cparams(sem, vmem_mib):
    return pltpu.CompilerParams(dimension_semantics=sem, vmem_limit_bytes=vmem_mib * MIB)


def _ada_kernel(c_ref, w_ref, b_ref, o_ref):
    c = c_ref[...]
    s = (c * jax.nn.sigmoid(c)).astype(BF16)
    o_ref[0] = jnp.dot(s, w_ref[0].astype(BF16), preferred_element_type=F32) + b_ref[0]


def _ada(cond, w_ada, b_ada):
    tn = 1024
    n = N_ADA * D_MODEL
    return pl.pallas_call(
        _ada_kernel,
        grid=(DEPTH, n // tn),
        in_specs=[
            pl.BlockSpec((N_GROUPS, D_MODEL), lambda l, j: (0, 0)),
            pl.BlockSpec((1, D_MODEL, tn), lambda l, j: (l, 0, j)),
            pl.BlockSpec((1, 1, tn), lambda l, j: (l, 0, j)),
        ],
        out_specs=pl.BlockSpec((1, N_GROUPS, tn), lambda l, j: (l, 0, j)),
        out_shape=jax.ShapeDtypeStruct((DEPTH, N_GROUPS, n), F32),
        compiler_params=_cparams(("arbitrary", "arbitrary"), 40),
        name="ada",
    )(cond, w_ada, b_ada.reshape(DEPTH, 1, n))


def _norm_mod_kernel(x_ref, g_ref, m_ref, o_ref, *, sh, sc):
    x = x_ref[0]
    y = x * lax.rsqrt(jnp.mean(x * x, axis=-1, keepdims=True) + EPS)
    y = y * g_ref[...]
    o_ref[0] = (y * (1.0 + m_ref[0, sc:sc + 1, :]) + m_ref[0, sh:sh + 1, :]).astype(o_ref.dtype)


def _group_of_block(b, t):
    return jnp.where(t == BLKS - 1, CTX_GROUP, b)


def _norm_mod(xs, g, mod, sh, sc):
    return pl.pallas_call(
        functools.partial(_norm_mod_kernel, sh=sh, sc=sc),
        grid=(BATCH, BLKS),
        in_specs=[
            pl.BlockSpec((1, ROW_BLK, D_MODEL), lambda b, t: (b, t, 0)),
            pl.BlockSpec((1, D_MODEL), lambda b, t: (0, 0)),
            pl.BlockSpec((1, N_ADA, D_MODEL), lambda b, t: (_group_of_block(b, t), 0, 0)),
        ],
        out_specs=pl.BlockSpec((1, ROW_BLK, D_MODEL), lambda b, t: (b, t, 0)),
        out_shape=jax.ShapeDtypeStruct((BATCH, S_ROWS, D_MODEL), BF16),
        compiler_params=_cparams(("arbitrary", "arbitrary"), 32),
        name="norm_mod",
    )(xs, g.reshape(1, D_MODEL), mod)


def _final_norm_kernel(x_ref, g_ref, o_ref):
    x = x_ref[0]
    y = x * lax.rsqrt(jnp.mean(x * x, axis=-1, keepdims=True) + EPS)
    o_ref[0] = y * g_ref[...]


def _final_norm(xs, g):
    return pl.pallas_call(
        _final_norm_kernel,
        grid=(BATCH, SEQ // ROW_BLK),
        in_specs=[
            pl.BlockSpec((1, ROW_BLK, D_MODEL), lambda b, t: (b, t, 0)),
            pl.BlockSpec((1, D_MODEL), lambda b, t: (0, 0)),
        ],
        out_specs=pl.BlockSpec((1, ROW_BLK, D_MODEL), lambda b, t: (b, t, 0)),
        out_shape=jax.ShapeDtypeStruct((BATCH, SEQ, D_MODEL), F32),
        compiler_params=_cparams(("arbitrary", "arbitrary"), 32),
        name="final_norm",
    )(xs, g.reshape(1, D_MODEL))


def _mm_kernel(*refs, has_bias, gate_idx, tm):
    a_ref, w_ref = refs[0], refs[1]
    pos = 2
    acc = jnp.dot(a_ref[...], w_ref[...], preferred_element_type=F32)
    if has_bias:
        acc = acc + refs[pos][...]
        pos += 1
    if gate_idx is None:
        o_ref = refs[pos]
        o_ref[...] = acc.astype(o_ref.dtype)
    else:
        res_ref, mb_ref, mc_ref, o_ref = refs[pos:pos + 4]
        gate_b = mb_ref[0, gate_idx:gate_idx + 1, :]
        gate_c = mc_ref[0, gate_idx:gate_idx + 1, :]
        rows = lax.broadcasted_iota(jnp.int32, (tm, 1), 0)
        last_tile = pl.program_id(1) % (S_ROWS // tm) == S_ROWS // tm - 1
        is_ctx = jnp.logical_and(last_tile, rows >= tm - CTX_LEN)
        o_ref[...] = res_ref[...] + jnp.where(is_ctx, gate_c, gate_b) * acc


def _mm(a, w, bias=None, out_dtype=F32, tn=1024, res=None, mod=None, gate_idx=None):
    m, k = a.shape
    n = w.shape[1]
    tm = MM_TM
    tn = min(tn, n)
    in_specs = [
        pl.BlockSpec((tm, k), lambda j, i: (i, 0)),
        pl.BlockSpec((k, tn), lambda j, i: (0, j)),
    ]
    args = [a, w]
    if bias is not None:
        in_specs.append(pl.BlockSpec((1, tn), lambda j, i: (0, j)))
        args.append(bias.reshape(1, n))
    if gate_idx is not None:
        per = S_ROWS // tm
        in_specs += [
            pl.BlockSpec((tm, tn), lambda j, i: (i, j)),
            pl.BlockSpec((1, N_ADA, tn), lambda j, i: (i // per, 0, j)),
            pl.BlockSpec((1, N_ADA, tn), lambda j, i: (CTX_GROUP, 0, j)),
        ]
        args += [res, mod, mod]
    return pl.pallas_call(
        functools.partial(_mm_kernel, has_bias=bias is not None, gate_idx=gate_idx, tm=tm),
        grid=(n // tn, m // tm),
        in_specs=in_specs,
        out_specs=pl.BlockSpec((tm, tn), lambda j, i: (i, j)),
        out_shape=jax.ShapeDtypeStruct((m, n), out_dtype),
        compiler_params=_cparams(("arbitrary", "arbitrary"), 52),
        name="mm",
    )(*args)


def _norm_mod_tile(x_ref, g_ref, mb_ref, mc_ref, hn_ref, tile, sh, sc):
    tm = x_ref.shape[0]
    per = S_ROWS // tm
    last_tile = tile % per == per - 1
    step = tm // 4
    for r in range(0, tm, step):
        x = x_ref[r:r + step, :]
        y = x * lax.rsqrt(jnp.mean(x * x, axis=-1, keepdims=True) + EPS) * g_ref[...]
        rows = lax.broadcasted_iota(jnp.int32, (step, 1), 0) + r
        is_ctx = jnp.logical_and(last_tile, rows >= tm - CTX_LEN)
        scale = jnp.where(is_ctx, mc_ref[0, sc:sc + 1, :], mb_ref[0, sc:sc + 1, :])
        shift = jnp.where(is_ctx, mc_ref[0, sh:sh + 1, :], mb_ref[0, sh:sh + 1, :])
        hn_ref[r:r + step, :] = (y * (1.0 + scale) + shift).astype(hn_ref.dtype)


def _norm_mod_specs(index_i):
    per = S_ROWS // MM_TM
    return [
        pl.BlockSpec((MM_TM, D_MODEL), lambda *ids: (index_i(*ids), 0)),
        pl.BlockSpec((1, D_MODEL), lambda *ids: (0, 0)),
        pl.BlockSpec((1, N_ADA, D_MODEL), lambda *ids: (index_i(*ids) // per, 0, 0)),
        pl.BlockSpec((1, N_ADA, D_MODEL), lambda *ids: (CTX_GROUP, 0, 0)),
    ]


def _nmm_kernel(x_ref, g_ref, mb_ref, mc_ref, w_ref, o_ref, hn_ref, *, sh, sc):
    @pl.when(pl.program_id(1) == 0)
    def _():
        _norm_mod_tile(x_ref, g_ref, mb_ref, mc_ref, hn_ref, pl.program_id(0), sh, sc)

    o_ref[...] = jnp.dot(hn_ref[...], w_ref[...], preferred_element_type=F32).astype(o_ref.dtype)


def _nmm(x, g, mod, sh, sc, w, tn):
    m, n = x.shape[0], w.shape[1]
    return pl.pallas_call(
        functools.partial(_nmm_kernel, sh=sh, sc=sc),
        grid=(m // MM_TM, n // tn),
        in_specs=_norm_mod_specs(lambda i, j: i) + [pl.BlockSpec((D_MODEL, tn), lambda i, j: (0, j))],
        out_specs=pl.BlockSpec((MM_TM, tn), lambda i, j: (i, j)),
        out_shape=jax.ShapeDtypeStruct((m, n), BF16),
        scratch_shapes=[pltpu.VMEM((MM_TM, D_MODEL), BF16)],
        compiler_params=_cparams(("arbitrary", "arbitrary"), 52),
        name="norm_mm",
    )(x, g.reshape(1, D_MODEL), mod, mod, w)


def _gla_la_kernel(low_ref, w2_ref, b_ref, o_ref):
    low = low_ref[...]
    for z in range(2):
        lz = low[:, z * GLA_GATE_RANK:(z + 1) * GLA_GATE_RANK].astype(BF16)
        logit = jnp.dot(lz, w2_ref[z].astype(BF16), preferred_element_type=F32) + b_ref[z]
        log_sig = jnp.minimum(logit, 0.0) - jnp.log1p(jnp.exp(-jnp.abs(logit)))
        o_ref[z] = log_sig / GLA_GATE_TEMP


def _gla_la(proj, w_a2, b_a):
    tm = MM_TM
    return pl.pallas_call(
        _gla_la_kernel,
        grid=(M_ROWS // tm,),
        in_specs=[
            pl.BlockSpec((tm, 128), lambda i: (i, GLA_IN // 128)),
            pl.BlockSpec((2, GLA_GATE_RANK, GLA_DK), lambda i: (0, 0, 0)),
            pl.BlockSpec((2, 1, GLA_DK), lambda i: (0, 0, 0)),
        ],
        out_specs=pl.BlockSpec((2, tm, GLA_DK), lambda i: (0, i, 0)),
        out_shape=jax.ShapeDtypeStruct((2, M_ROWS, GLA_DK), F32),
        compiler_params=_cparams(("arbitrary",), 48),
        name="gla_la",
    )(proj, w_a2, b_a.reshape(2, 1, GLA_DK))


def _gla_tables():
    c = GLA_CHUNK
    t = np.arange(c)
    pair = np.full((c, c), GLA_LEVELS + 1, np.int32)
    pair[t, t] = GLA_LEVELS
    sums = [t[None, :] <= t[:, None]]
    for lvl in range(GLA_LEVELS):
        s = c >> (lvl + 1)
        second = (t % (2 * s)) >= s
        same = (t[:, None] // (2 * s)) == (t[None, :] // (2 * s))
        pair[same & second[:, None] & ~second[None, :]] = lvl
        if s < GLA_SUBLANES:
            mid = (t // (2 * s)) * 2 * s + s - 1
            after = (t[None, :] > mid[:, None]) & (t[None, :] <= t[:, None])
            upto = (t[None, :] > t[:, None]) & (t[None, :] <= mid[:, None])
            sums.append(np.where(second[:, None], after, upto))
    fwd = np.concatenate(sums, axis=0).astype(np.float32)
    bwd = np.concatenate([m[::-1, ::-1] for m in sums], axis=0).astype(np.float32)
    tri = np.stack([np.concatenate([fwd, fwd], axis=1), np.concatenate([bwd, bwd], axis=1)])
    return tri, np.stack([pair, pair.T])


def _gla_dir(q_ref, k_ref, v_ref, la_ref, tri_ref, pair_ref, st_ref, o_ref, reverse):
    c = GLA_CHUNK
    la = la_ref[0, 0]
    hi = la.astype(BF16)
    lo = (la - hi.astype(F32)).astype(BF16)
    sums = jnp.dot(tri_ref[0], jnp.concatenate([hi, lo], axis=0), preferred_element_type=F32)
    b = sums[:c]
    q = q_ref[0].astype(F32) * (GLA_DK_HEAD ** -0.5)
    k = k_ref[0].astype(F32)
    v = v_ref[0]
    last = 0 if reverse else c - 1
    b_last = b[last:last + 1]
    nt = (((1,), (1,)), ((), ()))
    st = st_ref[...]
    o = lax.dot_general((q * jnp.exp(b)).astype(BF16), st.astype(BF16), nt, preferred_element_type=F32)
    pair = pair_ref[0]
    a = jnp.where(pair == GLA_LEVELS,
                  lax.dot_general(q.astype(BF16), k.astype(BF16), nt, preferred_element_type=F32), 0.0)
    n_small = 0
    for lvl in range(GLA_LEVELS):
        s = c >> (lvl + 1)
        if s < GLA_SUBLANES:
            n_small += 1
            e = jnp.exp(sums[n_small * c:(n_small + 1) * c])
        else:
            n = c // (2 * s)
            mid = s if reverse else s - 1
            b_mid = jnp.broadcast_to(b.reshape(n, 2 * s, -1)[:, mid:mid + 1, :], (n, 2 * s, b.shape[-1]))
            e = jnp.exp(-jnp.abs(b - b_mid.reshape(b.shape)))
        a_l = lax.dot_general((q * e).astype(BF16), (k * e).astype(BF16), nt, preferred_element_type=F32)
        a = jnp.where(pair == lvl, a_l, a)
    o_ref[0] = o + jnp.dot(a.astype(BF16), v, preferred_element_type=F32)
    kt = (k * jnp.exp(b_last - b)).astype(BF16)
    tn = (((0,), (0,)), ((), ()))
    st_ref[...] = st * jnp.exp(b_last) + lax.dot_general(v, kt, tn, preferred_element_type=F32)


def _gla_scan_kernel(qf, kf, vf, laf, qb, kb, vb, lab, tri_ref, pair_ref, of_ref, ob_ref, sf_ref, sb_ref):
    @pl.when(pl.program_id(2) == 0)
    def _():
        sf_ref[...] = jnp.zeros_like(sf_ref)
        sb_ref[...] = jnp.zeros_like(sb_ref)

    _gla_dir(qf, kf, vf, laf, tri_ref.at[0:1], pair_ref.at[0:1], sf_ref, of_ref, False)
    _gla_dir(qb, kb, vb, lab, tri_ref.at[1:2], pair_ref.at[1:2], sb_ref, ob_ref, True)


def _fwd_block(t):
    return jnp.where(t == 0, BLKS - 1, t - 1)


def _bwd_block(t):
    return jnp.where(t == 0, BLKS - 1, BLKS - 1 - t)


def _gla_scan(proj, la):
    tri_np, pair_np = _gla_tables()
    tri = jnp.asarray(tri_np, BF16)
    pair = jnp.asarray(pair_np, jnp.int32)
    kq, kv = GLA_DK_HEAD, GLA_DV_HEAD
    k_off = GLA_DK // kq
    v_off = 2 * GLA_DK // kv
    c = GLA_CHUNK
    spec_f = [
        pl.BlockSpec((1, ROW_BLK, kq), lambda b, h, t: (b, _fwd_block(t), h)),
        pl.BlockSpec((1, ROW_BLK, kq), lambda b, h, t: (b, _fwd_block(t), k_off + h)),
        pl.BlockSpec((1, ROW_BLK, kv), lambda b, h, t: (b, _fwd_block(t), v_off + h)),
        pl.BlockSpec((1, 1, ROW_BLK, kq), lambda b, h, t: (0, b, _fwd_block(t), h)),
    ]
    spec_b = [
        pl.BlockSpec((1, ROW_BLK, kq), lambda b, h, t: (b, _bwd_block(t), h)),
        pl.BlockSpec((1, ROW_BLK, kq), lambda b, h, t: (b, _bwd_block(t), k_off + h)),
        pl.BlockSpec((1, ROW_BLK, kv), lambda b, h, t: (b, _bwd_block(t), v_off + h)),
        pl.BlockSpec((1, 1, ROW_BLK, kq), lambda b, h, t: (1, b, _bwd_block(t), h)),
    ]
    const = [
        pl.BlockSpec((2,) + tri_np.shape[1:], lambda b, h, t: (0, 0, 0)),
        pl.BlockSpec((2, c, c), lambda b, h, t: (0, 0, 0)),
    ]
    out_sd = jax.ShapeDtypeStruct((BATCH, S_ROWS, GLA_DV), F32)
    return pl.pallas_call(
        _gla_scan_kernel,
        grid=(BATCH, GLA_HEADS, BLKS),
        in_specs=spec_f + spec_b + const,
        out_specs=[
            pl.BlockSpec((1, ROW_BLK, kv), lambda b, h, t: (b, _fwd_block(t), h)),
            pl.BlockSpec((1, ROW_BLK, kv), lambda b, h, t: (b, _bwd_block(t), h)),
        ],
        out_shape=[out_sd, out_sd],
        scratch_shapes=[pltpu.VMEM((kv, kq), F32), pltpu.VMEM((kv, kq), F32)],
        compiler_params=_cparams(("arbitrary", "arbitrary", "arbitrary"), 32),
        name="gla_scan",
    )(proj, proj, proj, la, proj, proj, proj, la, tri, pair)


def _gla_post_kernel(of_ref, ob_ref, g_ref, hn_ref, o_ref):
    hn = hn_ref[...]
    for h in range(GLA_HEADS):
        cols = slice(h * GLA_DV_HEAD, (h + 1) * GLA_DV_HEAD)
        o = of_ref[0, :, cols] + ob_ref[0, :, cols]
        o = o * lax.rsqrt(jnp.mean(o * o, axis=-1, keepdims=True) + EPS) * hn
        g = g_ref[0, :, cols].astype(F32)
        o_ref[0, :, cols] = (o.astype(BF16) * (g * jax.nn.sigmoid(g)).astype(BF16))


def _gla_post(o_f, o_b, proj, head_norm):
    g_off = 2 * GLA_DK // GLA_DV + 1
    blk = pl.BlockSpec((1, ROW_BLK, GLA_DV), lambda b, t: (b, t, 0))
    return pl.pallas_call(
        _gla_post_kernel,
        grid=(BATCH, BLKS),
        in_specs=[blk, blk,
                  pl.BlockSpec((1, ROW_BLK, GLA_DV), lambda b, t: (b, t, g_off)),
                  pl.BlockSpec((1, GLA_DV_HEAD), lambda b, t: (0, 0))],
        out_specs=blk,
        out_shape=jax.ShapeDtypeStruct((BATCH, S_ROWS, GLA_DV), BF16),
        compiler_params=_cparams(("arbitrary", "arbitrary"), 32),
        name="gla_post",
    )(o_f, o_b, proj, head_norm.reshape(1, GLA_DV_HEAD))


def _gla_layer(xs, g_norm, mod, w_in, w_a1, w_a2, b_a, head_norm, w_out):
    w_cat = jnp.concatenate([w_in, w_a1[0], w_a1[1], jnp.zeros((D_MODEL, 128 - 2 * GLA_GATE_RANK), F32)], axis=1)
    proj = _nmm(xs.reshape(M_ROWS, D_MODEL), g_norm, mod, 0, 1, w_cat.astype(BF16), tn=896)
    la = _gla_la(proj, w_a2, b_a).reshape(2, BATCH, S_ROWS, GLA_DK)
    proj = proj.reshape(BATCH, S_ROWS, GLA_IN + 128)
    o_f, o_b = _gla_scan(proj, la)
    y = _gla_post(o_f, o_b, proj, head_norm)
    out = _mm(y.reshape(M_ROWS, GLA_DV), w_out.astype(BF16), res=xs.reshape(M_ROWS, D_MODEL), mod=mod, gate_idx=2)
    return out.reshape(BATCH, S_ROWS, D_MODEL)


def _hy_proj_kernel(x_ref, g_ref, mb_ref, mc_ref, w0, w1, w2, bi0, bi1, bi2, cw0, cw1, cw2, cb0, cb1, cb2,
                    x0_ref, u_ref, a_ref, *, tm):
    @pl.when(pl.program_id(1) == 0)
    def _():
        _norm_mod_tile(x_ref, g_ref, mb_ref, mc_ref, a_ref, pl.program_id(0), 0, 1)

    rows = lax.broadcasted_iota(jnp.int32, (tm, 1), 0)
    last_tile = pl.program_id(0) % (S_ROWS // tm) == S_ROWS // tm - 1
    ctx0 = tm - CTX_LEN
    is_ctx = jnp.logical_and(last_tile, rows >= ctx0)
    pos = jnp.where(is_ctx, rows - ctx0, rows % GRID_W)
    last = jnp.where(is_ctx, CTX_LEN - 1, GRID_W - 1)
    has_prev = pos != 0
    has_next = pos != last
    a = a_ref[...]

    def branch(w_ref, bi_ref, cw_ref, cb_ref):
        p = jnp.dot(a, w_ref[...], preferred_element_type=F32) + bi_ref[...]
        prev = jnp.where(has_prev, pltpu.roll(p, 1, 0), 0.0)
        nxt = jnp.where(has_next, pltpu.roll(p, tm - 1, 0), 0.0)
        return prev * cw_ref[0:1, :] + p * cw_ref[1:2, :] + nxt * cw_ref[2:3, :] + cb_ref[...]

    x0_ref[...] = branch(w0, bi0, cw0, cb0).astype(BF16)
    u_ref[...] = (branch(w1, bi1, cw1, cb1) * branch(w2, bi2, cw2, cb2)).astype(BF16)


def _hy_proj(x, g_norm, mod, w_in, b_in, conv_w, conv_b):
    tm, tn = MM_TM, 512
    nj = D_MODEL // tn
    wspec = lambda k: pl.BlockSpec((D_MODEL, tn), lambda i, j: (0, k * nj + j))
    bspec = lambda k: pl.BlockSpec((1, tn), lambda i, j: (0, k * nj + j))
    cspec = lambda k: pl.BlockSpec((3, tn), lambda i, j: (0, k * nj + j))
    ospec = pl.BlockSpec((tm, tn), lambda i, j: (i, j))
    out_sd = jax.ShapeDtypeStruct((M_ROWS, D_MODEL), BF16)
    bi = b_in.reshape(1, 3 * D_MODEL)
    cb = conv_b.reshape(1, 3 * D_MODEL)
    return pl.pallas_call(
        functools.partial(_hy_proj_kernel, tm=tm),
        grid=(M_ROWS // tm, nj),
        in_specs=_norm_mod_specs(lambda i, j: i) + [wspec(0), wspec(1), wspec(2),
                  bspec(0), bspec(1), bspec(2), cspec(0), cspec(1), cspec(2), bspec(0), bspec(1), bspec(2)],
        out_specs=[ospec, ospec],
        out_shape=[out_sd, out_sd],
        scratch_shapes=[pltpu.VMEM((tm, D_MODEL), BF16)],
        compiler_params=_cparams(("arbitrary", "arbitrary"), 56),
        name="hy_proj",
    )(x, g_norm.reshape(1, D_MODEL), mod, mod, w_in, w_in, w_in, bi, bi, bi, conv_w, conv_w, conv_w, cb, cb, cb)


def _hy_filter_kernel(z_ref, w1, b1, w2, b2, w3, b3, w4f, w4b, dl_ref, o_ref):
    def layer(x, w_ref, b_ref):
        y = jnp.dot(x.astype(BF16), w_ref[...].astype(BF16), preferred_element_type=F32) + b_ref[...]
        return jnp.sin(HY_SIN_FREQ * y)

    z = z_ref[...]
    hdn = layer(layer(layer(z, w1, b1), w2, b2), w3, b3).astype(BF16)
    decay = jnp.exp(-z[:, 0:1] * dl_ref[...])
    h_fwd = jnp.dot(hdn, w4f[...].astype(BF16), preferred_element_type=F32) * decay
    h_bwd = jnp.dot(hdn, w4b[...].astype(BF16), preferred_element_type=F32) * decay
    lag0 = lax.broadcasted_iota(jnp.int32, (z.shape[0], 1), 0) == 0
    h_bwd = jnp.where(lag0, 0.0, h_bwd)
    inv = 1.0 / jnp.sum(jnp.abs(h_fwd) + jnp.abs(h_bwd), axis=0, keepdims=True)
    for k, sig in enumerate(((h_fwd + h_bwd) * inv, (h_bwd - h_fwd) * inv)):
        hi = sig.astype(BF16)
        o_ref[2 * k] = hi
        o_ref[2 * k + 1] = (sig - hi.astype(F32)).astype(BF16)


def _hy_filter(L, f_w1, f_b1, f_w2, f_b2, f_w3, f_b3, f_w4):
    tn = 512
    nj = D_MODEL // tn
    t = jnp.linspace(0.0, 1.0, L, dtype=F32)[:, None]
    w = 2 * math.pi * jnp.arange(L, dtype=F32)[:, None] / L
    f = jnp.linspace(1e-4, HY_BANDS - 1, HY_BANDS, dtype=F32)[None, :]
    emb = 1 + 2 * HY_BANDS
    z = jnp.concatenate([t, jnp.cos(f * w), -jnp.sin(f * w), jnp.zeros((L, 128 - emb), F32)], axis=-1)
    w1 = jnp.concatenate([f_w1, jnp.zeros((128 - emb, f_w1.shape[1]), F32)], axis=0)
    max_decay = math.log(HY_DECAY_TARGET) / HY_FAST_DECAY_PCT
    min_decay = math.log(HY_DECAY_TARGET) / HY_SLOW_DECAY_PCT
    deltas = jnp.abs(jnp.linspace(min_decay, max_decay, D_MODEL, dtype=F32))[None, :]
    hid = f_w2.shape[0]
    full = lambda shape: pl.BlockSpec(shape, lambda j: (0, 0))
    return pl.pallas_call(
        _hy_filter_kernel,
        grid=(nj,),
        in_specs=[full((L, 128)), full((128, hid)), full((1, hid)), full((hid, hid)), full((1, hid)),
                  full((hid, hid)), full((1, hid)),
                  pl.BlockSpec((hid, tn), lambda j: (0, j)), pl.BlockSpec((hid, tn), lambda j: (0, nj + j)),
                  pl.BlockSpec((1, tn), lambda j: (0, j))],
        out_specs=pl.BlockSpec((4, L, tn), lambda j: (0, 0, j)),
        out_shape=jax.ShapeDtypeStruct((4, L, D_MODEL), BF16),
        compiler_params=_cparams(("arbitrary",), 48),
        name="hy_filter",
    )(z, w1, f_b1.reshape(1, hid), f_w2, f_b2.reshape(1, hid), f_w3, f_b3.reshape(1, hid), f_w4, f_w4, deltas)


def _dft_tiles(L):
    return (min(2 * L, 1024), min(L, 1024))


def _dft_mats(L):
    n = 2 * L
    tf = _dft_tiles(L)[0] // 2
    f = jnp.arange(L, dtype=jnp.int32)
    ang = ((f[:, None] * f[None, :]) % n).astype(F32) * (2 * math.pi / n)
    cosm = jnp.cos(ang)
    sinm = jnp.sin(ang).at[0].set(jnp.where(f % 2 == 0, 1.0, -1.0))
    fwd = jnp.concatenate([cosm.reshape(L // tf, tf, L), sinm.reshape(L // tf, tf, L)], axis=1).reshape(n, L)
    wgt = jnp.where(f == 0, 1.0 / n, 2.0 / n)[None, :]
    inv = jnp.concatenate([cosm.T * wgt, -sinm.T * wgt], axis=1)
    fwd_hi = fwd.astype(BF16)
    return fwd_hi, (fwd - fwd_hi.astype(F32)).astype(BF16), inv.astype(BF16)


def _dft_fwd_kernel(f_ref, x_ref, ka, kb, kap, o_ref, *, tf):
    acc = jnp.dot(f_ref[...], x_ref[0], preferred_element_type=F32)
    c, s = acc[:tf], acc[tf:]
    o_ref[0, 0] = (c * ka[...] + s * kb[...]).astype(BF16)
    o_ref[0, 1] = (c * kb[...] - s * kap[...]).astype(BF16)


def _dft_fwd(fwd, x, row_blk, L, spec):
    nb = x.shape[0]
    tm = _dft_tiles(L)[0]
    tf, tn = tm // 2, 1024
    kspec = pl.BlockSpec((tf, tn), lambda b, j, i: (i, j))
    return pl.pallas_call(
        functools.partial(_dft_fwd_kernel, tf=tf),
        grid=(nb, D_MODEL // tn, 2 * L // tm),
        in_specs=[pl.BlockSpec((tm, L), lambda b, j, i: (i, 0)),
                  pl.BlockSpec((1, L, tn), lambda b, j, i: (b, row_blk, j)), kspec, kspec, kspec],
        out_specs=pl.BlockSpec((1, 2, tf, tn), lambda b, j, i: (b, 0, i, j)),
        out_shape=jax.ShapeDtypeStruct((nb, 2, L, D_MODEL), BF16),
        compiler_params=_cparams(("arbitrary", "arbitrary", "arbitrary"), 52),
        name="dft_fwd",
    )(fwd, x, *spec)


def _filter_spectrum_kernel(fh_ref, fl_ref, sig_ref, ka_ref, kb_ref, kap_ref, *, tf):
    fh, fl = fh_ref[...], fl_ref[...]

    def dft(rows, hi, lo):
        return (jnp.dot(fh[rows], hi, preferred_element_type=F32) + jnp.dot(fh[rows], lo, preferred_element_type=F32)
                + jnp.dot(fl[rows], hi, preferred_element_type=F32))

    ka = dft(slice(0, tf), sig_ref[0], sig_ref[1])
    kb = dft(slice(tf, 2 * tf), sig_ref[2], sig_ref[3])
    nyq = dft(slice(tf, tf + 16), sig_ref[0], sig_ref[1])[0:1]
    freq = lax.broadcasted_iota(jnp.int32, (tf, 1), 0) + pl.program_id(1) * tf
    ka_ref[...] = ka
    kb_ref[...] = jnp.where(freq == 0, 0.0, kb)
    kap_ref[...] = jnp.where(freq == 0, nyq, ka)


def _filter_spectrum(sig, fwd_hi, fwd_lo, L):
    tm = _dft_tiles(L)[0]
    tf, tn = tm // 2, 256
    fspec = pl.BlockSpec((tm, L), lambda j, i: (i, 0))
    ospec = pl.BlockSpec((tf, tn), lambda j, i: (i, j))
    out_sd = jax.ShapeDtypeStruct((L, D_MODEL), F32)
    return pl.pallas_call(
        functools.partial(_filter_spectrum_kernel, tf=tf),
        grid=(D_MODEL // tn, 2 * L // tm),
        in_specs=[fspec, fspec, pl.BlockSpec((4, L, tn), lambda j, i: (0, 0, j))],
        out_specs=[ospec, ospec, ospec],
        out_shape=[out_sd, out_sd, out_sd],
        compiler_params=_cparams(("arbitrary", "arbitrary"), 52),
        name="filter_spectrum",
    )(fwd_hi, fwd_lo, sig)


def _dft_inv_kernel(g_ref, p_ref, x0_ref, u_ref, skip_ref, *rest):
    o_ref = rest[-1]
    y = jnp.dot(g_ref[...], p_ref[0], preferred_element_type=F32)
    o_ref[0] = (x0_ref[0].astype(F32) * (y + u_ref[0].astype(F32) * skip_ref[...])).astype(BF16)


def _dft_inv(inv, p, x0, u, skip, L, prev=None):
    tm = _dft_tiles(L)[1]
    tn = 512
    off = 0 if L == SEQ else SEQ // tm
    row = lambda b, j, i: (b, off + i, j)
    in_specs = [pl.BlockSpec((tm, 2 * L), lambda b, j, i: (i, 0)),
                pl.BlockSpec((1, 2 * L, tn), lambda b, j, i: (b, 0, j)),
                pl.BlockSpec((1, tm, tn), row), pl.BlockSpec((1, tm, tn), row),
                pl.BlockSpec((1, tn), lambda b, j, i: (0, j))]
    args = [inv, p, x0, u, skip.reshape(1, D_MODEL)]
    aliases = {}
    if prev is not None:
        in_specs.append(pl.BlockSpec(memory_space=pl.ANY))
        args.append(prev)
        aliases = {len(args) - 1: 0}
    return pl.pallas_call(
        _dft_inv_kernel,
        grid=(BATCH, D_MODEL // tn, L // tm),
        in_specs=in_specs,
        out_specs=pl.BlockSpec((1, tm, tn), row),
        out_shape=jax.ShapeDtypeStruct((BATCH, S_ROWS, D_MODEL), BF16),
        input_output_aliases=aliases,
        compiler_params=_cparams(("arbitrary", "arbitrary", "arbitrary"), 52),
        name="dft_inv",
    )(*args)


def _hyena_layer(xs, g_norm, mod, w_in, b_in, conv_w, conv_b, fw, skip, w_out, b_out):
    x0, u = _hy_proj(xs.reshape(M_ROWS, D_MODEL), g_norm, mod, w_in.astype(BF16), b_in, conv_w, conv_b)
    x0 = x0.reshape(BATCH, S_ROWS, D_MODEL)
    u = u.reshape(BATCH, S_ROWS, D_MODEL)
    y = None
    for L in (SEQ, CTX_LEN):
        fwd_hi, fwd_lo, inv = _dft_mats(L)
        spec = _filter_spectrum(_hy_filter(L, *fw), fwd_hi, fwd_lo, L)
        row_blk = 0 if L == SEQ else SEQ // L
        p = _dft_fwd(fwd_hi, u, row_blk, L, spec=spec).reshape(BATCH, 2 * L, D_MODEL)
        y = _dft_inv(inv, p, x0, u, skip, L, prev=y)
    out = _mm(y.reshape(M_ROWS, D_MODEL), w_out.astype(BF16), bias=b_out,
              res=xs.reshape(M_ROWS, D_MODEL), mod=mod, gate_idx=2)
    return out.reshape(BATCH, S_ROWS, D_MODEL)


def _ffn_up_kernel(x_ref, w1_ref, w3_ref, o_ref):
    x = x_ref[0]
    h1 = jnp.dot(x, w1_ref[0, 0].astype(BF16), preferred_element_type=F32)
    h3 = jnp.dot(x, w3_ref[0, 0].astype(BF16), preferred_element_type=F32)
    o_ref[0] = (h1 * jax.nn.sigmoid(h1) * h3).astype(BF16)


def _ffn_down_kernel(a_ref, w2_ref, s_ref, o_ref):
    y = jnp.dot(a_ref[0], w2_ref[0, 0].astype(BF16), preferred_element_type=F32) * s_ref[0]
    o_ref[0] = y.astype(o_ref.dtype)


def _ffn(xe, w1, w3, w2, layer, score):
    tm, tf, tn = FFN_TM, 512, 1024
    act = pl.pallas_call(
        _ffn_up_kernel,
        grid=(N_EXPERTS, EXPERT_FF // tf, EXP_ROWS // tm),
        in_specs=[
            pl.BlockSpec((1, tm, D_MODEL), lambda e, f, i: (e, i, 0)),
            pl.BlockSpec((1, 1, D_MODEL, tf), lambda e, f, i: (layer, e, 0, f)),
            pl.BlockSpec((1, 1, D_MODEL, tf), lambda e, f, i: (layer, e, 0, f)),
        ],
        out_specs=pl.BlockSpec((1, tm, tf), lambda e, f, i: (e, i, f)),
        out_shape=jax.ShapeDtypeStruct((N_EXPERTS, EXP_ROWS, EXPERT_FF), BF16),
        compiler_params=_cparams(("arbitrary", "arbitrary", "arbitrary"), 52),
        name="moe_ffn_up",
    )(xe, w1, w3)
    return pl.pallas_call(
        _ffn_down_kernel,
        grid=(N_EXPERTS, D_MODEL // tn, EXP_ROWS // tm),
        in_specs=[
            pl.BlockSpec((1, tm, EXPERT_FF), lambda e, j, i: (e, i, 0)),
            pl.BlockSpec((1, 1, EXPERT_FF, tn), lambda e, j, i: (layer, e, 0, j)),
            pl.BlockSpec((1, tm, 1), lambda e, j, i: (e, i, 0)),
        ],
        out_specs=pl.BlockSpec((1, tm, tn), lambda e, j, i: (e, i, j)),
        out_shape=jax.ShapeDtypeStruct((N_EXPERTS, EXP_ROWS, D_MODEL), BF16),
        compiler_params=_cparams(("arbitrary", "arbitrary", "arbitrary"), 52),
        name="moe_ffn_down",
    )(act, w2, score)


def _moe_combine_kernel(il_ref, ic_ref, yl_ref, yc_ref, x_ref, mb_ref, mc_ref, o_ref, ql_ref, qc_ref):
    @pl.when(pl.program_id(1) == 0)
    def _():
        chunk = 512
        tok = lax.broadcasted_iota(jnp.int32, (SEQ, chunk), 0)
        for kk in range(N_EXPERTS * CAP_LAT // chunk):
            cols = slice(kk * chunk, (kk + 1) * chunk)
            ql_ref[:, cols] = jnp.where(il_ref[0, :, cols] == tok, 1.0, 0.0).astype(BF16)
        tok_c = lax.broadcasted_iota(jnp.int32, (CTX_LEN, N_EXPERTS * CAP_CTX), 0)
        qc_ref[...] = jnp.where(ic_ref[0] == tok_c, 1.0, 0.0).astype(BF16)

    tn = o_ref.shape[-1]
    lat = jnp.dot(ql_ref[...], yl_ref[...].reshape(N_EXPERTS * CAP_LAT, tn), preferred_element_type=F32)
    o_ref[0, :SEQ, :] = x_ref[0, :SEQ, :] + mb_ref[0, 5:6, :] * lat
    cx = jnp.dot(qc_ref[...], yc_ref[...].reshape(N_EXPERTS * CAP_CTX, tn), preferred_element_type=F32)
    o_ref[0, SEQ:, :] = x_ref[0, SEQ:, :] + mc_ref[0, 5:6, :] * cx


def _moe_combine(xs, y, idx_l, idx_c, mod):
    tn = 256
    return pl.pallas_call(
        _moe_combine_kernel,
        grid=(BATCH, D_MODEL // tn),
        in_specs=[
            pl.BlockSpec((1, 1, N_EXPERTS * CAP_LAT), lambda b, j: (b, 0, 0)),
            pl.BlockSpec((1, 1, N_EXPERTS * CAP_CTX), lambda b, j: (b, 0, 0)),
            pl.BlockSpec((N_EXPERTS, CAP_LAT, tn), lambda b, j: (0, b, j)),
            pl.BlockSpec((N_EXPERTS, CAP_CTX, tn), lambda b, j: (0, BATCH * CAP_LAT // CAP_CTX + b, j)),
            pl.BlockSpec((1, S_ROWS, tn), lambda b, j: (b, 0, j)),
            pl.BlockSpec((1, N_ADA, tn), lambda b, j: (b, 0, j)),
            pl.BlockSpec((1, N_ADA, tn), lambda b, j: (CTX_GROUP, 0, j)),
        ],
        out_specs=pl.BlockSpec((1, S_ROWS, tn), lambda b, j: (b, 0, j)),
        out_shape=jax.ShapeDtypeStruct((BATCH, S_ROWS, D_MODEL), F32),
        scratch_shapes=[pltpu.VMEM((SEQ, N_EXPERTS * CAP_LAT), BF16),
                        pltpu.VMEM((CTX_LEN, N_EXPERTS * CAP_CTX), BF16)],
        compiler_params=_cparams(("arbitrary", "arbitrary"), 56),
        name="moe_combine",
    )(idx_l, idx_c, y, y, xs, mod, mod)


def _route_stream(s_tok, s_exp, u_ref, cap, idx_ref, score_ref):
    n_tok = s_exp.shape[1]
    bits = lax.bitcast_convert_type(s_exp, jnp.int32)
    thr = jnp.zeros((N_EXPERTS, 1), jnp.int32)
    for bit in range(30, -1, -1):
        cand = thr | (1 << bit)
        cnt = jnp.sum(jnp.where(bits >= cand, 1.0, 0.0), axis=1, keepdims=True)
        thr = jnp.where(cnt >= cap, cand, thr)
    above = bits > thr
    tied = bits == thr
    n_above = jnp.sum(jnp.where(above, 1.0, 0.0), axis=1, keepdims=True)
    before = u_ref[:n_tok, :n_tok]
    tie_rank = jnp.dot(jnp.where(tied, 1.0, 0.0).astype(BF16), before, preferred_element_type=F32)
    chosen = jnp.logical_or(above, jnp.logical_and(tied, tie_rank < cap - n_above))
    pos = jnp.dot(jnp.where(chosen, 1.0, 0.0).astype(BF16), before, preferred_element_type=F32)
    slot = jnp.where(chosen, pos, -1.0)
    slot_tok = jnp.concatenate([slot, jnp.full((128 - N_EXPERTS, n_tok), -1.0, F32)], axis=0).T
    slot_ids = lax.broadcasted_iota(jnp.int32, (1, cap), 1).astype(F32)
    tok_ids = lax.broadcasted_iota(jnp.int32, (n_tok, 1), 0).astype(F32)
    for e in range(N_EXPERTS):
        hit = slot_tok[:, e:e + 1] == slot_ids
        idx_ref[0, e:e + 1, :] = jnp.sum(jnp.where(hit, tok_ids, 0.0), axis=0, keepdims=True).astype(jnp.int32)
        score_ref[0, e:e + 1, :] = jnp.sum(jnp.where(hit, s_tok[:, e:e + 1], 0.0), axis=0, keepdims=True)


def _router_kernel(lg_ref, u_ref, il_ref, sl_ref, ic_ref, sc_ref):
    lg = lg_ref[0]
    valid = lax.broadcasted_iota(jnp.int32, (1, lg.shape[1]), 1) < N_EXPERTS
    x = jnp.where(valid, lg, -jnp.inf)
    e = jnp.where(valid, jnp.exp(x - jnp.max(x, axis=-1, keepdims=True)), 0.0)
    s = e / jnp.sum(e, axis=-1, keepdims=True)
    s_exp = s.T[:N_EXPERTS]
    _route_stream(s[:SEQ], s_exp[:, :SEQ], u_ref, CAP_LAT, il_ref, sl_ref)
    _route_stream(s[SEQ:], s_exp[:, SEQ:], u_ref, CAP_CTX, ic_ref, sc_ref)


def _router(logits, before):
    out = lambda cap: pl.BlockSpec((1, N_EXPERTS, cap), lambda b: (b, 0, 0))
    sd = lambda cap, dt: jax.ShapeDtypeStruct((BATCH, N_EXPERTS, cap), dt)
    return pl.pallas_call(
        _router_kernel,
        grid=(BATCH,),
        in_specs=[pl.BlockSpec((1, S_ROWS, 128), lambda b: (b, 0, 0)),
                  pl.BlockSpec((SEQ, SEQ), lambda b: (0, 0))],
        out_specs=[out(CAP_LAT), out(CAP_LAT), out(CAP_CTX), out(CAP_CTX)],
        out_shape=[sd(CAP_LAT, jnp.int32), sd(CAP_LAT, F32), sd(CAP_CTX, jnp.int32), sd(CAP_CTX, F32)],
        compiler_params=_cparams(("arbitrary",), 52),
        name="router",
    )(logits, before)


def _moe_layer(xs, h, mod, w_router, w1, w3, w2, layer, before):
    hf = h.reshape(M_ROWS, D_MODEL)
    wr = jnp.zeros((D_MODEL, 128), F32).at[:, :N_EXPERTS].set(w_router)
    logits = _mm(hf, wr.astype(BF16), tn=128).reshape(BATCH, S_ROWS, 128)
    idx_l, top_l, idx_c, top_c = _router(logits, before)
    base = (jnp.arange(BATCH, dtype=jnp.int32) * S_ROWS)[:, None, None]

    def per_expert(t):
        return jnp.swapaxes(t, 0, 1).reshape(N_EXPERTS, -1)

    rows = jnp.concatenate([per_expert(idx_l + base), per_expert(idx_c + base + SEQ)], axis=1)
    score = jnp.concatenate([per_expert(top_l), per_expert(top_c)], axis=1)
    xe = hf.at[rows].get(mode="promise_in_bounds")
    y = _ffn(xe, w1, w3, w2, layer, score[..., None])
    return _moe_combine(xs, y, idx_l.reshape(BATCH, 1, -1), idx_c.reshape(BATCH, 1, -1), mod)


def kernel(x, c, ctx, c_ctx, w_ada, b_ada, norm_mix, norm_ffn, norm_final, gla_w_in, gla_w_a1, gla_w_a2, gla_b_a, gla_head_norm, gla_w_out, hy_w_in, hy_b_in, hy_conv_w, hy_conv_b, hy_f_w1, hy_f_b1, hy_f_w2, hy_f_b2, hy_f_w3, hy_f_b3, hy_f_w4, hy_skip, hy_w_out, hy_b_out, moe_router, moe_w1, moe_w3, moe_w2):
    cond = jnp.zeros((N_GROUPS, D_MODEL), F32).at[:BATCH].set(c).at[CTX_GROUP].set(c_ctx)
    mods = _ada(cond, w_ada, b_ada).reshape(DEPTH, N_GROUPS, N_ADA, D_MODEL)
    xs = jnp.concatenate([x, ctx], axis=1)
    tok = jnp.arange(SEQ, dtype=jnp.int32)
    before = (tok[:, None] < tok[None, :]).astype(BF16)
    for i in range(DEPTH):
        mod = mods[i]
        j = i // 2
        if i % 2 == 0:
            xs = _gla_layer(xs, norm_mix[i], mod, gla_w_in[j], gla_w_a1[j], gla_w_a2[j], gla_b_a[j], gla_head_norm[j],
                            gla_w_out[j])
        else:
            fw = (hy_f_w1[j], hy_f_b1[j], hy_f_w2[j], hy_f_b2[j], hy_f_w3[j], hy_f_b3[j], hy_f_w4[j])
            xs = _hyena_layer(xs, norm_mix[i], mod, hy_w_in[j], hy_b_in[j], hy_conv_w[j], hy_conv_b[j], fw, hy_skip[j],
                              hy_w_out[j], hy_b_out[j])
        h = _norm_mod(xs, norm_ffn[i], mod, 3, 4)
        xs = _moe_layer(xs, h, mod, moe_router[i], moe_w1, moe_w3, moe_w2, i, before)
    return _final_norm(xs, norm_final)
```

```python
import functools
import math

import jax
import jax.numpy as jnp
import numpy as np
from jax import lax
from jax.experimental import pallas as pl
from jax.experimental.pallas import tpu as pltpu

D_MODEL = 2048
BATCH = 8
SEQ = 2048
DEPTH = 4
CTX_LEN = 256
GRID_W = 64
EPS = 1e-6
N_ADA = 6

GLA_HEADS = 4
GLA_DK = D_MODEL // 2
GLA_DV = D_MODEL
GLA_DK_HEAD = GLA_DK // GLA_HEADS
GLA_DV_HEAD = GLA_DV // GLA_HEADS
GLA_GATE_RANK = 16
GLA_GATE_TEMP = 16.0
GLA_IN = 2 * GLA_DK + 2 * GLA_DV

HY_BANDS = 16
HY_SIN_FREQ = 1.0
HY_FAST_DECAY_PCT = 0.3
HY_SLOW_DECAY_PCT = 1.5
HY_DECAY_TARGET = 1e-2

N_EXPERTS = 16
EXPERT_FF = D_MODEL // 2
CAPACITY_FACTOR = 2

S_ROWS = CTX_LEN + SEQ
M_ROWS = BATCH * S_ROWS
CTX_GROUP = BATCH
N_GROUPS = 16
ROW_BLK = 256
BLKS = S_ROWS // ROW_BLK
MM_TM = S_ROWS // 2
GLA_CHUNK = ROW_BLK
GLA_LEVELS = 8
GLA_SUBLANES = 8
CAP_LAT = CAPACITY_FACTOR * SEQ // N_EXPERTS
CAP_CTX = CAPACITY_FACTOR * CTX_LEN // N_EXPERTS
EXP_ROWS = BATCH * (CAP_LAT + CAP_CTX)
FFN_TM = EXP_ROWS // 2

F32 = jnp.float32
BF16 = jnp.bfloat16
MIB = 1024 * 1024


def _cparams(sem, vmem_mib):
    return pltpu.CompilerParams(dimension_semantics=sem, vmem_limit_bytes=vmem_mib * MIB)


def _ada_kernel(c_ref, w_ref, b_ref, o_ref):
    c = c_ref[...]
    s = (c * jax.nn.sigmoid(c)).astype(BF16)
    o_ref[0] = jnp.dot(s, w_ref[0].astype(BF16), preferred_element_type=F32) + b_ref[0]


def _ada(cond, w_ada, b_ada):
    tn = 1024
    n = N_ADA * D_MODEL
    return pl.pallas_call(
        _ada_kernel,
        grid=(DEPTH, n // tn),
        in_specs=[
            pl.BlockSpec((N_GROUPS, D_MODEL), lambda l, j: (0, 0)),
            pl.BlockSpec((1, D_MODEL, tn), lambda l, j: (l, 0, j)),
            pl.BlockSpec((1, 1, tn), lambda l, j: (l, 0, j)),
        ],
        out_specs=pl.BlockSpec((1, N_GROUPS, tn), lambda l, j: (l, 0, j)),
        out_shape=jax.ShapeDtypeStruct((DEPTH, N_GROUPS, n), F32),
        compiler_params=_cparams(("arbitrary", "arbitrary"), 40),
        name="ada",
    )(cond, w_ada, b_ada.reshape(DEPTH, 1, n))


def _final_norm_kernel(x_ref, g_ref, o_ref):
    x = x_ref[0]
    y = x * lax.rsqrt(jnp.mean(x * x, axis=-1, keepdims=True) + EPS)
    o_ref[0] = y * g_ref[...]


def _final_norm(xs, g):
    return pl.pallas_call(
        _final_norm_kernel,
        grid=(BATCH, SEQ // ROW_BLK),
        in_specs=[
            pl.BlockSpec((1, ROW_BLK, D_MODEL), lambda b, t: (b, t, 0)),
            pl.BlockSpec((1, D_MODEL), lambda b, t: (0, 0)),
        ],
        out_specs=pl.BlockSpec((1, ROW_BLK, D_MODEL), lambda b, t: (b, t, 0)),
        out_shape=jax.ShapeDtypeStruct((BATCH, SEQ, D_MODEL), F32),
        compiler_params=_cparams(("arbitrary", "arbitrary"), 32),
        name="final_norm",
    )(xs, g.reshape(1, D_MODEL))


def _mm_kernel(*refs, has_bias, gate_idx, tm):
    a_ref, w_ref = refs[0], refs[1]
    pos = 2
    acc = jnp.dot(a_ref[...], w_ref[...], preferred_element_type=F32)
    if has_bias:
        acc = acc + refs[pos][...]
        pos += 1
    if gate_idx is None:
        o_ref = refs[pos]
        o_ref[...] = acc.astype(o_ref.dtype)
    else:
        res_ref, mb_ref, mc_ref, o_ref = refs[pos:pos + 4]
        gate_b = mb_ref[0, gate_idx:gate_idx + 1, :]
        gate_c = mc_ref[0, gate_idx:gate_idx + 1, :]
        rows = lax.broadcasted_iota(jnp.int32, (tm, 1), 0)
        last_tile = pl.program_id(1) % (S_ROWS // tm) == S_ROWS // tm - 1
        is_ctx = jnp.logical_and(last_tile, rows >= tm - CTX_LEN)
        o_ref[...] = res_ref[...] + jnp.where(is_ctx, gate_c, gate_b) * acc


def _mm(a, w, bias=None, out_dtype=F32, tn=1024, res=None, mod=None, gate_idx=None):
    m, k = a.shape
    n = w.shape[1]
    tm = MM_TM
    tn = min(tn, n)
    in_specs = [
        pl.BlockSpec((tm, k), lambda j, i: (i, 0)),
        pl.BlockSpec((k, tn), lambda j, i: (0, j)),
    ]
    args = [a, w]
    if bias is not None:
        in_specs.append(pl.BlockSpec((1, tn), lambda j, i: (0, j)))
        args.append(bias.reshape(1, n))
    if gate_idx is not None:
        per = S_ROWS // tm
        in_specs += [
            pl.BlockSpec((tm, tn), lambda j, i: (i, j)),
            pl.BlockSpec((1, N_ADA, tn), lambda j, i: (i // per, 0, j)),
            pl.BlockSpec((1, N_ADA, tn), lambda j, i: (CTX_GROUP, 0, j)),
        ]
        args += [res, mod, mod]
    return pl.pallas_call(
        functools.partial(_mm_kernel, has_bias=bias is not None, gate_idx=gate_idx, tm=tm),
        grid=(n // tn, m // tm),
        in_specs=in_specs,
        out_specs=pl.BlockSpec((tm, tn), lambda j, i: (i, j)),
        out_shape=jax.ShapeDtypeStruct((m, n), out_dtype),
        compiler_params=_cparams(("arbitrary", "arbitrary"), 52),
        name="mm",
    )(*args)


def _norm_mod_tile(x_ref, g_ref, mb_ref, mc_ref, hn_ref, tile, sh, sc):
    tm = x_ref.shape[0]
    per = S_ROWS // tm
    last_tile = tile % per == per - 1
    step = tm // 4
    for r in range(0, tm, step):
        x = x_ref[r:r + step, :]
        y = x * lax.rsqrt(jnp.mean(x * x, axis=-1, keepdims=True) + EPS) * g_ref[...]
        rows = lax.broadcasted_iota(jnp.int32, (step, 1), 0) + r
        is_ctx = jnp.logical_and(last_tile, rows >= tm - CTX_LEN)
        scale = jnp.where(is_ctx, mc_ref[0, sc:sc + 1, :], mb_ref[0, sc:sc + 1, :])
        shift = jnp.where(is_ctx, mc_ref[0, sh:sh + 1, :], mb_ref[0, sh:sh + 1, :])
        hn_ref[r:r + step, :] = (y * (1.0 + scale) + shift).astype(hn_ref.dtype)


def _norm_mod_specs(index_i):
    per = S_ROWS // MM_TM
    return [
        pl.BlockSpec((MM_TM, D_MODEL), lambda *ids: (index_i(*ids), 0)),
        pl.BlockSpec((1, D_MODEL), lambda *ids: (0, 0)),
        pl.BlockSpec((1, N_ADA, D_MODEL), lambda *ids: (index_i(*ids) // per, 0, 0)),
        pl.BlockSpec((1, N_ADA, D_MODEL), lambda *ids: (CTX_GROUP, 0, 0)),
    ]


def _nmm_kernel(x_ref, g_ref, mb_ref, mc_ref, w_ref, o_ref, hn_ref, *, sh, sc):
    @pl.when(pl.program_id(1) == 0)
    def _():
        _norm_mod_tile(x_ref, g_ref, mb_ref, mc_ref, hn_ref, pl.program_id(0), sh, sc)

    o_ref[...] = jnp.dot(hn_ref[...], w_ref[...], preferred_element_type=F32).astype(o_ref.dtype)


def _nmm(x, g, mod, sh, sc, w, tn):
    m, n = x.shape[0], w.shape[1]
    return pl.pallas_call(
        functools.partial(_nmm_kernel, sh=sh, sc=sc),
        grid=(m // MM_TM, n // tn),
        in_specs=_norm_mod_specs(lambda i, j: i) + [pl.BlockSpec((D_MODEL, tn), lambda i, j: (0, j))],
        out_specs=pl.BlockSpec((MM_TM, tn), lambda i, j: (i, j)),
        out_shape=jax.ShapeDtypeStruct((m, n), BF16),
        scratch_shapes=[pltpu.VMEM((MM_TM, D_MODEL), BF16)],
        compiler_params=_cparams(("arbitrary", "arbitrary"), 52),
        name="norm_mm",
    )(x, g.reshape(1, D_MODEL), mod, mod, w)


def _norm_route_kernel(x_ref, g_ref, mb_ref, mc_ref, wr_ref, h_ref, lg_ref):
    _norm_mod_tile(x_ref, g_ref, mb_ref, mc_ref, h_ref, pl.program_id(0), 3, 4)
    lg_ref[...] = jnp.dot(h_ref[...], wr_ref[...], preferred_element_type=F32)


def _norm_route(x, g, mod, wr):
    return pl.pallas_call(
        _norm_route_kernel,
        grid=(M_ROWS // MM_TM,),
        in_specs=_norm_mod_specs(lambda i: i) + [pl.BlockSpec((D_MODEL, 128), lambda i: (0, 0))],
        out_specs=[pl.BlockSpec((MM_TM, D_MODEL), lambda i: (i, 0)), pl.BlockSpec((MM_TM, 128), lambda i: (i, 0))],
        out_shape=[jax.ShapeDtypeStruct((M_ROWS, D_MODEL), BF16), jax.ShapeDtypeStruct((M_ROWS, 128), F32)],
        compiler_params=_cparams(("arbitrary",), 48),
        name="norm_route",
    )(x, g.reshape(1, D_MODEL), mod, mod, wr)


def _gla_la_kernel(low_ref, w2_ref, b_ref, o_ref):
    low = low_ref[...]
    for z in range(2):
        lz = low[:, z * GLA_GATE_RANK:(z + 1) * GLA_GATE_RANK].astype(BF16)
        logit = jnp.dot(lz, w2_ref[z].astype(BF16), preferred_element_type=F32) + b_ref[z]
        log_sig = jnp.minimum(logit, 0.0) - jnp.log1p(jnp.exp(-jnp.abs(logit)))
        o_ref[z] = log_sig / GLA_GATE_TEMP


def _gla_la(proj, w_a2, b_a):
    tm = MM_TM
    return pl.pallas_call(
        _gla_la_kernel,
        grid=(M_ROWS // tm,),
        in_specs=[
            pl.BlockSpec((tm, 128), lambda i: (i, GLA_IN // 128)),
            pl.BlockSpec((2, GLA_GATE_RANK, GLA_DK), lambda i: (0, 0, 0)),
            pl.BlockSpec((2, 1, GLA_DK), lambda i: (0, 0, 0)),
        ],
        out_specs=pl.BlockSpec((2, tm, GLA_DK), lambda i: (0, i, 0)),
        out_shape=jax.ShapeDtypeStruct((2, M_ROWS, GLA_DK), F32),
        compiler_params=_cparams(("arbitrary",), 48),
        name="gla_la",
    )(proj, w_a2, b_a.reshape(2, 1, GLA_DK))


def _gla_tables():
    c = GLA_CHUNK
    t = np.arange(c)
    pair = np.full((c, c), GLA_LEVELS + 1, np.int32)
    pair[t, t] = GLA_LEVELS
    for lvl in range(GLA_LEVELS):
        s = c >> (lvl + 1)
        second = (t % (2 * s)) >= s
        same = (t[:, None] // (2 * s)) == (t[None, :] // (2 * s))
        pair[same & second[:, None] & ~second[None, :]] = lvl
    lower = (t[None, :] <= t[:, None]).astype(np.float32)
    tri = np.stack([np.concatenate([lower, lower], axis=1), np.concatenate([lower.T, lower.T], axis=1)])
    return tri, np.stack([pair, pair.T])


def _gla_dir(q_ref, k_ref, v_ref, la_ref, tri_ref, pair_ref, st_ref, o_ref, reverse):
    c = GLA_CHUNK
    la = la_ref[0, 0] * math.log2(math.e)
    hi = la.astype(BF16)
    lo = (la - hi.astype(F32)).astype(BF16)
    b = jnp.dot(tri_ref[0], jnp.concatenate([hi, lo], axis=0), preferred_element_type=F32)
    q = q_ref[0].astype(F32) * (GLA_DK_HEAD ** -0.5)
    k = k_ref[0].astype(F32)
    v = v_ref[0]
    d = q.shape[-1]
    last = 0 if reverse else c - 1
    b_last = b[last:last + 1]
    nt = (((1,), (1,)), ((), ()))
    st = st_ref[...]
    o = lax.dot_general((q * jnp.exp2(b)).astype(BF16), st.astype(BF16), nt, preferred_element_type=F32)
    pair = pair_ref[0]
    a = jnp.where(pair == GLA_LEVELS,
                  lax.dot_general(q.astype(BF16), k.astype(BF16), nt, preferred_element_type=F32), 0.0)
    row = lax.broadcasted_iota(jnp.int32, (c, 1), 0)
    for lvl in range(GLA_LEVELS):
        s = c >> (lvl + 1)
        n = c // (2 * s)
        mid = s if reverse else s - 1
        b_mid = jnp.broadcast_to(b.reshape(n, 2 * s, d)[:, mid:mid + 1, :], (n, 2 * s, d)).reshape(c, d)
        first, second = (q, k) if reverse else (k, q)
        if s >= GLA_SUBLANES:
            mix = jnp.concatenate([first.reshape(n, 2 * s, d)[:, :s], second.reshape(n, 2 * s, d)[:, s:]],
                                  axis=1).reshape(c, d)
        else:
            mix = jnp.where(row % (2 * s) >= s, second, first)
        m = (mix * jnp.exp2(-jnp.abs(b - b_mid))).astype(BF16)
        a = jnp.where(pair == lvl, lax.dot_general(m, m, nt, preferred_element_type=F32), a)
    o_ref[0] = (o + jnp.dot(a.astype(BF16), v, preferred_element_type=F32)).astype(o_ref.dtype)
    kt = (k * jnp.exp2(b_last - b)).astype(BF16)
    tn = (((0,), (0,)), ((), ()))
    st_ref[...] = st * jnp.exp2(b_last) + lax.dot_general(v, kt, tn, preferred_element_type=F32)


def _gla_scan_kernel(qf, kf, vf, laf, qb, kb, vb, lab, tri_ref, pair_ref, of_ref, ob_ref, sf_ref, sb_ref):
    @pl.when(pl.program_id(2) == 0)
    def _():
        sf_ref[...] = jnp.zeros_like(sf_ref)
        sb_ref[...] = jnp.zeros_like(sb_ref)

    _gla_dir(qf, kf, vf, laf, tri_ref.at[0:1], pair_ref.at[0:1], sf_ref, of_ref, False)
    _gla_dir(qb, kb, vb, lab, tri_ref.at[1:2], pair_ref.at[1:2], sb_ref, ob_ref, True)


def _fwd_block(t):
    return jnp.where(t == 0, BLKS - 1, t - 1)


def _bwd_block(t):
    return jnp.where(t == 0, BLKS - 1, BLKS - 1 - t)


def _gla_scan(proj, la):
    tri_np, pair_np = _gla_tables()
    tri = jnp.asarray(tri_np, BF16)
    pair = jnp.asarray(pair_np, jnp.int32)
    kq, kv = GLA_DK_HEAD, GLA_DV_HEAD
    k_off = GLA_DK // kq
    v_off = 2 * GLA_DK // kv
    c = GLA_CHUNK
    spec_f = [
        pl.BlockSpec((1, ROW_BLK, kq), lambda b, h, t: (b, _fwd_block(t), h)),
        pl.BlockSpec((1, ROW_BLK, kq), lambda b, h, t: (b, _fwd_block(t), k_off + h)),
        pl.BlockSpec((1, ROW_BLK, kv), lambda b, h, t: (b, _fwd_block(t), v_off + h)),
        pl.BlockSpec((1, 1, ROW_BLK, kq), lambda b, h, t: (0, b, _fwd_block(t), h)),
    ]
    spec_b = [
        pl.BlockSpec((1, ROW_BLK, kq), lambda b, h, t: (b, _bwd_block(t), h)),
        pl.BlockSpec((1, ROW_BLK, kq), lambda b, h, t: (b, _bwd_block(t), k_off + h)),
        pl.BlockSpec((1, ROW_BLK, kv), lambda b, h, t: (b, _bwd_block(t), v_off + h)),
        pl.BlockSpec((1, 1, ROW_BLK, kq), lambda b, h, t: (1, b, _bwd_block(t), h)),
    ]
    const = [
        pl.BlockSpec((2,) + tri_np.shape[1:], lambda b, h, t: (0, 0, 0)),
        pl.BlockSpec((2, c, c), lambda b, h, t: (0, 0, 0)),
    ]
    out_sd = jax.ShapeDtypeStruct((BATCH, S_ROWS, GLA_DV), BF16)
    return pl.pallas_call(
        _gla_scan_kernel,
        grid=(BATCH, GLA_HEADS, BLKS),
        in_specs=spec_f + spec_b + const,
        out_specs=[
            pl.BlockSpec((1, ROW_BLK, kv), lambda b, h, t: (b, _fwd_block(t), h)),
            pl.BlockSpec((1, ROW_BLK, kv), lambda b, h, t: (b, _bwd_block(t), h)),
        ],
        out_shape=[out_sd, out_sd],
        scratch_shapes=[pltpu.VMEM((kv, kq), F32), pltpu.VMEM((kv, kq), F32)],
        compiler_params=_cparams(("arbitrary", "arbitrary", "arbitrary"), 32),
        name="gla_scan",
    )(proj, proj, proj, la, proj, proj, proj, la, tri, pair)


def _gla_post_kernel(of_ref, ob_ref, g_ref, hn_ref, o_ref):
    hn = hn_ref[...]
    for h in range(GLA_HEADS):
        cols = slice(h * GLA_DV_HEAD, (h + 1) * GLA_DV_HEAD)
        o = of_ref[0, :, cols].astype(F32) + ob_ref[0, :, cols].astype(F32)
        o = o * lax.rsqrt(jnp.mean(o * o, axis=-1, keepdims=True) + EPS) * hn
        g = g_ref[0, :, cols].astype(F32)
        o_ref[0, :, cols] = (o.astype(BF16) * (g * jax.nn.sigmoid(g)).astype(BF16))


def _gla_post(o_f, o_b, proj, head_norm):
    g_off = 2 * GLA_DK // GLA_DV + 1
    blk = pl.BlockSpec((1, ROW_BLK, GLA_DV), lambda b, t: (b, t, 0))
    return pl.pallas_call(
        _gla_post_kernel,
        grid=(BATCH, BLKS),
        in_specs=[blk, blk,
                  pl.BlockSpec((1, ROW_BLK, GLA_DV), lambda b, t: (b, t, g_off)),
                  pl.BlockSpec((1, GLA_DV_HEAD), lambda b, t: (0, 0))],
        out_specs=blk,
        out_shape=jax.ShapeDtypeStruct((BATCH, S_ROWS, GLA_DV), BF16),
        compiler_params=_cparams(("arbitrary", "arbitrary"), 32),
        name="gla_post",
    )(o_f, o_b, proj, head_norm.reshape(1, GLA_DV_HEAD))


def _gla_layer(xs, g_norm, mod, w_in, w_a1, w_a2, b_a, head_norm, w_out):
    w_cat = jnp.concatenate([w_in, w_a1[0], w_a1[1], jnp.zeros((D_MODEL, 128 - 2 * GLA_GATE_RANK), F32)], axis=1)
    proj = _nmm(xs.reshape(M_ROWS, D_MODEL), g_norm, mod, 0, 1, w_cat.astype(BF16), tn=896)
    la = _gla_la(proj, w_a2, b_a).reshape(2, BATCH, S_ROWS, GLA_DK)
    proj = proj.reshape(BATCH, S_ROWS, GLA_IN + 128)
    o_f, o_b = _gla_scan(proj, la)
    y = _gla_post(o_f, o_b, proj, head_norm)
    out = _mm(y.reshape(M_ROWS, GLA_DV), w_out.astype(BF16), res=xs.reshape(M_ROWS, D_MODEL), mod=mod, gate_idx=2)
    return out.reshape(BATCH, S_ROWS, D_MODEL)


def _hy_proj_kernel(x_ref, g_ref, mb_ref, mc_ref, w0, w1, w2, bi0, bi1, bi2, cw0, cw1, cw2, cb0, cb1, cb2,
                    x0_ref, u_ref, a_ref, *, tm):
    @pl.when(pl.program_id(1) == 0)
    def _():
        _norm_mod_tile(x_ref, g_ref, mb_ref, mc_ref, a_ref, pl.program_id(0), 0, 1)

    rows = lax.broadcasted_iota(jnp.int32, (tm, 1), 0)
    last_tile = pl.program_id(0) % (S_ROWS // tm) == S_ROWS // tm - 1
    ctx0 = tm - CTX_LEN
    is_ctx = jnp.logical_and(last_tile, rows >= ctx0)
    pos = jnp.where(is_ctx, rows - ctx0, rows % GRID_W)
    last = jnp.where(is_ctx, CTX_LEN - 1, GRID_W - 1)
    has_prev = pos != 0
    has_next = pos != last
    a = a_ref[...]

    def branch(w_ref, bi_ref, cw_ref, cb_ref):
        p = jnp.dot(a, w_ref[...], preferred_element_type=F32) + bi_ref[...]
        prev = jnp.where(has_prev, pltpu.roll(p, 1, 0), 0.0)
        nxt = jnp.where(has_next, pltpu.roll(p, tm - 1, 0), 0.0)
        return prev * cw_ref[0:1, :] + p * cw_ref[1:2, :] + nxt * cw_ref[2:3, :] + cb_ref[...]

    x0_ref[...] = branch(w0, bi0, cw0, cb0).astype(BF16)
    u_ref[...] = (branch(w1, bi1, cw1, cb1) * branch(w2, bi2, cw2, cb2)).astype(BF16)


def _hy_proj(x, g_norm, mod, w_in, b_in, conv_w, conv_b):
    tm, tn = MM_TM, 512
    nj = D_MODEL // tn
    wspec = lambda k: pl.BlockSpec((D_MODEL, tn), lambda i, j: (0, k * nj + j))
    bspec = lambda k: pl.BlockSpec((1, tn), lambda i, j: (0, k * nj + j))
    cspec = lambda k: pl.BlockSpec((3, tn), lambda i, j: (0, k * nj + j))
    ospec = pl.BlockSpec((tm, tn), lambda i, j: (i, j))
    out_sd = jax.ShapeDtypeStruct((M_ROWS, D_MODEL), BF16)
    bi = b_in.reshape(1, 3 * D_MODEL)
    cb = conv_b.reshape(1, 3 * D_MODEL)
    return pl.pallas_call(
        functools.partial(_hy_proj_kernel, tm=tm),
        grid=(M_ROWS // tm, nj),
        in_specs=_norm_mod_specs(lambda i, j: i) + [wspec(0), wspec(1), wspec(2),
                  bspec(0), bspec(1), bspec(2), cspec(0), cspec(1), cspec(2), bspec(0), bspec(1), bspec(2)],
        out_specs=[ospec, ospec],
        out_shape=[out_sd, out_sd],
        scratch_shapes=[pltpu.VMEM((tm, D_MODEL), BF16)],
        compiler_params=_cparams(("arbitrary", "arbitrary"), 56),
        name="hy_proj",
    )(x, g_norm.reshape(1, D_MODEL), mod, mod, w_in, w_in, w_in, bi, bi, bi, conv_w, conv_w, conv_w, cb, cb, cb)


def _hy_filter_kernel(z_ref, w1, b1, w2, b2, w3, b3, w4f, w4b, dl_ref, o_ref):
    def layer(x, w_ref, b_ref):
        y = jnp.dot(x.astype(BF16), w_ref[...].astype(BF16), preferred_element_type=F32) + b_ref[...]
        return jnp.sin(HY_SIN_FREQ * y)

    z = z_ref[...]
    hdn = layer(layer(layer(z, w1, b1), w2, b2), w3, b3).astype(BF16)
    decay = jnp.exp(-z[:, 0:1] * dl_ref[...])
    h_fwd = jnp.dot(hdn, w4f[...].astype(BF16), preferred_element_type=F32) * decay
    h_bwd = jnp.dot(hdn, w4b[...].astype(BF16), preferred_element_type=F32) * decay
    lag0 = lax.broadcasted_iota(jnp.int32, (z.shape[0], 1), 0) == 0
    h_bwd = jnp.where(lag0, 0.0, h_bwd)
    inv = 1.0 / jnp.sum(jnp.abs(h_fwd) + jnp.abs(h_bwd), axis=0, keepdims=True)
    for k, sig in enumerate(((h_fwd + h_bwd) * inv, (h_bwd - h_fwd) * inv)):
        hi = sig.astype(BF16)
        o_ref[2 * k] = hi
        o_ref[2 * k + 1] = (sig - hi.astype(F32)).astype(BF16)


def _hy_filter(L, f_w1, f_b1, f_w2, f_b2, f_w3, f_b3, f_w4):
    tn = 512
    nj = D_MODEL // tn
    t = jnp.linspace(0.0, 1.0, L, dtype=F32)[:, None]
    w = 2 * math.pi * jnp.arange(L, dtype=F32)[:, None] / L
    f = jnp.linspace(1e-4, HY_BANDS - 1, HY_BANDS, dtype=F32)[None, :]
    emb = 1 + 2 * HY_BANDS
    z = jnp.concatenate([t, jnp.cos(f * w), -jnp.sin(f * w), jnp.zeros((L, 128 - emb), F32)], axis=-1)
    w1 = jnp.concatenate([f_w1, jnp.zeros((128 - emb, f_w1.shape[1]), F32)], axis=0)
    max_decay = math.log(HY_DECAY_TARGET) / HY_FAST_DECAY_PCT
    min_decay = math.log(HY_DECAY_TARGET) / HY_SLOW_DECAY_PCT
    deltas = jnp.abs(jnp.linspace(min_decay, max_decay, D_MODEL, dtype=F32))[None, :]
    hid = f_w2.shape[0]
    full = lambda shape: pl.BlockSpec(shape, lambda j: (0, 0))
    return pl.pallas_call(
        _hy_filter_kernel,
        grid=(nj,),
        in_specs=[full((L, 128)), full((128, hid)), full((1, hid)), full((hid, hid)), full((1, hid)),
                  full((hid, hid)), full((1, hid)),
                  pl.BlockSpec((hid, tn), lambda j: (0, j)), pl.BlockSpec((hid, tn), lambda j: (0, nj + j)),
                  pl.BlockSpec((1, tn), lambda j: (0, j))],
        out_specs=pl.BlockSpec((4, L, tn), lambda j: (0, 0, j)),
        out_shape=jax.ShapeDtypeStruct((4, L, D_MODEL), BF16),
        compiler_params=_cparams(("arbitrary",), 48),
        name="hy_filter",
    )(z, w1, f_b1.reshape(1, hid), f_w2, f_b2.reshape(1, hid), f_w3, f_b3.reshape(1, hid), f_w4, f_w4, deltas)


def _dft_tiles(L):
    return (min(2 * L, 1024), min(L, 1024))


def _dft_mats(L):
    n = 2 * L
    tf = _dft_tiles(L)[0] // 2
    f = jnp.arange(L, dtype=jnp.int32)
    ang = ((f[:, None] * f[None, :]) % n).astype(F32) * (2 * math.pi / n)
    cosm = jnp.cos(ang)
    sinm = jnp.sin(ang).at[0].set(jnp.where(f % 2 == 0, 1.0, -1.0))
    fwd = jnp.concatenate([cosm.reshape(L // tf, tf, L), sinm.reshape(L // tf, tf, L)], axis=1).reshape(n, L)
    wgt = jnp.where(f == 0, 1.0 / n, 2.0 / n)[None, :]
    inv = jnp.concatenate([cosm.T * wgt, -sinm.T * wgt], axis=1)
    fwd_hi = fwd.astype(BF16)
    return fwd_hi, (fwd - fwd_hi.astype(F32)).astype(BF16), inv.astype(BF16)


def _dft_fwd_kernel(f_ref, x_ref, ka, kb, kap, o_ref, *, tf):
    acc = jnp.dot(f_ref[...], x_ref[0], preferred_element_type=F32)
    c, s = acc[:tf], acc[tf:]
    o_ref[0, 0] = (c * ka[...] + s * kb[...]).astype(BF16)
    o_ref[0, 1] = (c * kb[...] - s * kap[...]).astype(BF16)


def _dft_fwd(fwd, x, row_blk, L, spec):
    nb = x.shape[0]
    tm = _dft_tiles(L)[0]
    tf, tn = tm // 2, 1024
    kspec = pl.BlockSpec((tf, tn), lambda b, j, i: (i, j))
    return pl.pallas_call(
        functools.partial(_dft_fwd_kernel, tf=tf),
        grid=(nb, D_MODEL // tn, 2 * L // tm),
        in_specs=[pl.BlockSpec((tm, L), lambda b, j, i: (i, 0)),
                  pl.BlockSpec((1, L, tn), lambda b, j, i: (b, row_blk, j)), kspec, kspec, kspec],
        out_specs=pl.BlockSpec((1, 2, tf, tn), lambda b, j, i: (b, 0, i, j)),
        out_shape=jax.ShapeDtypeStruct((nb, 2, L, D_MODEL), BF16),
        compiler_params=_cparams(("arbitrary", "arbitrary", "arbitrary"), 52),
        name="dft_fwd",
    )(fwd, x, *spec)


def _filter_spectrum_kernel(fh_ref, fl_ref, sig_ref, ka_ref, kb_ref, kap_ref, *, tf):
    fh, fl = fh_ref[...], fl_ref[...]

    def dft(rows, hi, lo):
        return (jnp.dot(fh[rows], hi, preferred_element_type=F32) + jnp.dot(fh[rows], lo, preferred_element_type=F32)
                + jnp.dot(fl[rows], hi, preferred_element_type=F32))

    ka = dft(slice(0, tf), sig_ref[0], sig_ref[1])
    kb = dft(slice(tf, 2 * tf), sig_ref[2], sig_ref[3])
    nyq = dft(slice(tf, tf + 16), sig_ref[0], sig_ref[1])[0:1]
    freq = lax.broadcasted_iota(jnp.int32, (tf, 1), 0) + pl.program_id(1) * tf
    ka_ref[...] = ka
    kb_ref[...] = jnp.where(freq == 0, 0.0, kb)
    kap_ref[...] = jnp.where(freq == 0, nyq, ka)


def _filter_spectrum(sig, fwd_hi, fwd_lo, L):
    tm = _dft_tiles(L)[0]
    tf, tn = tm // 2, 256
    fspec = pl.BlockSpec((tm, L), lambda j, i: (i, 0))
    ospec = pl.BlockSpec((tf, tn), lambda j, i: (i, j))
    out_sd = jax.ShapeDtypeStruct((L, D_MODEL), F32)
    return pl.pallas_call(
        functools.partial(_filter_spectrum_kernel, tf=tf),
        grid=(D_MODEL // tn, 2 * L // tm),
        in_specs=[fspec, fspec, pl.BlockSpec((4, L, tn), lambda j, i: (0, 0, j))],
        out_specs=[ospec, ospec, ospec],
        out_shape=[out_sd, out_sd, out_sd],
        compiler_params=_cparams(("arbitrary", "arbitrary"), 52),
        name="filter_spectrum",
    )(fwd_hi, fwd_lo, sig)


def _dft_inv_kernel(g_ref, p_ref, x0_ref, u_ref, skip_ref, *rest):
    o_ref = rest[-1]
    y = jnp.dot(g_ref[...], p_ref[0], preferred_element_type=F32)
    o_ref[0] = (x0_ref[0].astype(F32) * (y + u_ref[0].astype(F32) * skip_ref[...])).astype(BF16)


def _dft_inv(inv, p, x0, u, skip, L, prev=None):
    tm = _dft_tiles(L)[1]
    tn = 512
    off = 0 if L == SEQ else SEQ // tm
    row = lambda b, j, i: (b, off + i, j)
    in_specs = [pl.BlockSpec((tm, 2 * L), lambda b, j, i: (i, 0)),
                pl.BlockSpec((1, 2 * L, tn), lambda b, j, i: (b, 0, j)),
                pl.BlockSpec((1, tm, tn), row), pl.BlockSpec((1, tm, tn), row),
                pl.BlockSpec((1, tn), lambda b, j, i: (0, j))]
    args = [inv, p, x0, u, skip.reshape(1, D_MODEL)]
    aliases = {}
    if prev is not None:
        in_specs.append(pl.BlockSpec(memory_space=pl.ANY))
        args.append(prev)
        aliases = {len(args) - 1: 0}
    return pl.pallas_call(
        _dft_inv_kernel,
        grid=(BATCH, D_MODEL // tn, L // tm),
        in_specs=in_specs,
        out_specs=pl.BlockSpec((1, tm, tn), row),
        out_shape=jax.ShapeDtypeStruct((BATCH, S_ROWS, D_MODEL), BF16),
        input_output_aliases=aliases,
        compiler_params=_cparams(("arbitrary", "arbitrary", "arbitrary"), 52),
        name="dft_inv",
    )(*args)


def _hyena_layer(xs, g_norm, mod, w_in, b_in, conv_w, conv_b, fw, skip, w_out, b_out):
    x0, u = _hy_proj(xs.reshape(M_ROWS, D_MODEL), g_norm, mod, w_in.astype(BF16), b_in, conv_w, conv_b)
    x0 = x0.reshape(BATCH, S_ROWS, D_MODEL)
    u = u.reshape(BATCH, S_ROWS, D_MODEL)
    y = None
    for L in (SEQ, CTX_LEN):
        fwd_hi, fwd_lo, inv = _dft_mats(L)
        spec = _filter_spectrum(_hy_filter(L, *fw), fwd_hi, fwd_lo, L)
        row_blk = 0 if L == SEQ else SEQ // L
        p = _dft_fwd(fwd_hi, u, row_blk, L, spec=spec).reshape(BATCH, 2 * L, D_MODEL)
        y = _dft_inv(inv, p, x0, u, skip, L, prev=y)
    out = _mm(y.reshape(M_ROWS, D_MODEL), w_out.astype(BF16), bias=b_out,
              res=xs.reshape(M_ROWS, D_MODEL), mod=mod, gate_idx=2)
    return out.reshape(BATCH, S_ROWS, D_MODEL)


def _ffn_up_kernel(x_ref, w1_ref, w3_ref, o_ref):
    x = x_ref[0]
    h1 = jnp.dot(x, w1_ref[0, 0].astype(BF16), preferred_element_type=F32)
    h3 = jnp.dot(x, w3_ref[0, 0].astype(BF16), preferred_element_type=F32)
    o_ref[0] = (h1 * jax.nn.sigmoid(h1) * h3).astype(BF16)


def _ffn_down_kernel(a_ref, w2_ref, s_ref, o_ref):
    y = jnp.dot(a_ref[0], w2_ref[0, 0].astype(BF16), preferred_element_type=F32) * s_ref[0]
    o_ref[0] = y.astype(o_ref.dtype)


def _ffn(xe, w1, w3, w2, layer, score):
    tm, tf, tn = FFN_TM, 512, 1024
    act = pl.pallas_call(
        _ffn_up_kernel,
        grid=(N_EXPERTS, EXPERT_FF // tf, EXP_ROWS // tm),
        in_specs=[
            pl.BlockSpec((1, tm, D_MODEL), lambda e, f, i: (e, i, 0)),
            pl.BlockSpec((1, 1, D_MODEL, tf), lambda e, f, i: (layer, e, 0, f)),
            pl.BlockSpec((1, 1, D_MODEL, tf), lambda e, f, i: (layer, e, 0, f)),
        ],
        out_specs=pl.BlockSpec((1, tm, tf), lambda e, f, i: (e, i, f)),
        out_shape=jax.ShapeDtypeStruct((N_EXPERTS, EXP_ROWS, EXPERT_FF), BF16),
        compiler_params=_cparams(("arbitrary", "arbitrary", "arbitrary"), 52),
        name="moe_ffn_up",
    )(xe, w1, w3)
    return pl.pallas_call(
        _ffn_down_kernel,
        grid=(N_EXPERTS, D_MODEL // tn, EXP_ROWS // tm),
        in_specs=[
            pl.BlockSpec((1, tm, EXPERT_FF), lambda e, j, i: (e, i, 0)),
            pl.BlockSpec((1, 1, EXPERT_FF, tn), lambda e, j, i: (layer, e, 0, j)),
            pl.BlockSpec((1, tm, 1), lambda e, j, i: (e, i, 0)),
        ],
        out_specs=pl.BlockSpec((1, tm, tn), lambda e, j, i: (e, i, j)),
        out_shape=jax.ShapeDtypeStruct((N_EXPERTS, EXP_ROWS, D_MODEL), BF16),
        compiler_params=_cparams(("arbitrary", "arbitrary", "arbitrary"), 52),
        name="moe_ffn_down",
    )(act, w2, score)


def _moe_combine_kernel(il_ref, ic_ref, yl_ref, yc_ref, x_ref, mb_ref, mc_ref, o_ref, ql_ref, qc_ref):
    @pl.when(pl.program_id(1) == 0)
    def _():
        chunk = 512
        tok = lax.broadcasted_iota(jnp.int32, (SEQ, chunk), 0)
        for kk in range(N_EXPERTS * CAP_LAT // chunk):
            cols = slice(kk * chunk, (kk + 1) * chunk)
            ql_ref[:, cols] = jnp.where(il_ref[0, :, cols] == tok, 1.0, 0.0).astype(BF16)
        tok_c = lax.broadcasted_iota(jnp.int32, (CTX_LEN, N_EXPERTS * CAP_CTX), 0)
        qc_ref[...] = jnp.where(ic_ref[0] == tok_c, 1.0, 0.0).astype(BF16)

    tn = o_ref.shape[-1]
    lat = jnp.dot(ql_ref[...], yl_ref[...].reshape(N_EXPERTS * CAP_LAT, tn), preferred_element_type=F32)
    o_ref[0, :SEQ, :] = x_ref[0, :SEQ, :] + mb_ref[0, 5:6, :] * lat
    cx = jnp.dot(qc_ref[...], yc_ref[...].reshape(N_EXPERTS * CAP_CTX, tn), preferred_element_type=F32)
    o_ref[0, SEQ:, :] = x_ref[0, SEQ:, :] + mc_ref[0, 5:6, :] * cx


def _moe_combine(xs, y, idx_l, idx_c, mod):
    tn = 256
    return pl.pallas_call(
        _moe_combine_kernel,
        grid=(BATCH, D_MODEL // tn),
        in_specs=[
            pl.BlockSpec((1, 1, N_EXPERTS * CAP_LAT), lambda b, j: (b, 0, 0)),
            pl.BlockSpec((1, 1, N_EXPERTS * CAP_CTX), lambda b, j: (b, 0, 0)),
            pl.BlockSpec((N_EXPERTS, CAP_LAT, tn), lambda b, j: (0, b, j)),
            pl.BlockSpec((N_EXPERTS, CAP_CTX, tn), lambda b, j: (0, BATCH * CAP_LAT // CAP_CTX + b, j)),
            pl.BlockSpec((1, S_ROWS, tn), lambda b, j: (b, 0, j)),
            pl.BlockSpec((1, N_ADA, tn), lambda b, j: (b, 0, j)),
            pl.BlockSpec((1, N_ADA, tn), lambda b, j: (CTX_GROUP, 0, j)),
        ],
        out_specs=pl.BlockSpec((1, S_ROWS, tn), lambda b, j: (b, 0, j)),
        out_shape=jax.ShapeDtypeStruct((BATCH, S_ROWS, D_MODEL), F32),
        scratch_shapes=[pltpu.VMEM((SEQ, N_EXPERTS * CAP_LAT), BF16),
                        pltpu.VMEM((CTX_LEN, N_EXPERTS * CAP_CTX), BF16)],
        compiler_params=_cparams(("arbitrary", "arbitrary"), 56),
        name="moe_combine",
    )(idx_l, idx_c, y, y, xs, mod, mod)


def _route_stream(s_tok, s_exp, u_ref, cap, idx_ref, score_ref):
    n_tok = s_exp.shape[1]
    bits = lax.bitcast_convert_type(s_exp, jnp.int32)
    thr = jnp.zeros((N_EXPERTS, 1), jnp.int32)
    for bit in range(30, -1, -1):
        cand = thr | (1 << bit)
        cnt = jnp.sum(jnp.where(bits >= cand, 1.0, 0.0), axis=1, keepdims=True)
        thr = jnp.where(cnt >= cap, cand, thr)
    above = bits > thr
    tied = bits == thr
    n_above = jnp.sum(jnp.where(above, 1.0, 0.0), axis=1, keepdims=True)
    before = u_ref[:n_tok, :n_tok]
    tie_rank = jnp.dot(jnp.where(tied, 1.0, 0.0).astype(BF16), before, preferred_element_type=F32)
    chosen = jnp.logical_or(above, jnp.logical_and(tied, tie_rank < cap - n_above))
    pos = jnp.dot(jnp.where(chosen, 1.0, 0.0).astype(BF16), before, preferred_element_type=F32)
    slot = jnp.where(chosen, pos, -1.0)
    slot_tok = jnp.concatenate([slot, jnp.full((128 - N_EXPERTS, n_tok), -1.0, F32)], axis=0).T
    slot_ids = lax.broadcasted_iota(jnp.int32, (1, cap), 1).astype(F32)
    tok_ids = lax.broadcasted_iota(jnp.int32, (n_tok, 1), 0).astype(F32)
    for e in range(N_EXPERTS):
        hit = slot_tok[:, e:e + 1] == slot_ids
        idx_ref[0, e:e + 1, :] = jnp.sum(jnp.where(hit, tok_ids, 0.0), axis=0, keepdims=True).astype(jnp.int32)
        score_ref[0, e:e + 1, :] = jnp.sum(jnp.where(hit, s_tok[:, e:e + 1], 0.0), axis=0, keepdims=True)


def _router_kernel(lg_ref, u_ref, il_ref, sl_ref, ic_ref, sc_ref):
    lg = lg_ref[0]
    valid = lax.broadcasted_iota(jnp.int32, (1, lg.shape[1]), 1) < N_EXPERTS
    x = jnp.where(valid, lg, -jnp.inf)
    e = jnp.where(valid, jnp.exp(x - jnp.max(x, axis=-1, keepdims=True)), 0.0)
    s = e / jnp.sum(e, axis=-1, keepdims=True)
    s_exp = s.T[:N_EXPERTS]
    _route_stream(s[:SEQ], s_exp[:, :SEQ], u_ref, CAP_LAT, il_ref, sl_ref)
    _route_stream(s[SEQ:], s_exp[:, SEQ:], u_ref, CAP_CTX, ic_ref, sc_ref)


def _router(logits, before):
    out = lambda cap: pl.BlockSpec((1, N_EXPERTS, cap), lambda b: (b, 0, 0))
    sd = lambda cap, dt: jax.ShapeDtypeStruct((BATCH, N_EXPERTS, cap), dt)
    return pl.pallas_call(
        _router_kernel,
        grid=(BATCH,),
        in_specs=[pl.BlockSpec((1, S_ROWS, 128), lambda b: (b, 0, 0)),
                  pl.BlockSpec((SEQ, SEQ), lambda b: (0, 0))],
        out_specs=[out(CAP_LAT), out(CAP_LAT), out(CAP_CTX), out(CAP_CTX)],
        out_shape=[sd(CAP_LAT, jnp.int32), sd(CAP_LAT, F32), sd(CAP_CTX, jnp.int32), sd(CAP_CTX, F32)],
        compiler_params=_cparams(("arbitrary",), 52),
        name="router",
    )(logits, before)


def _moe_layer(xs, g_norm, mod, w_router, w1, w3, w2, layer, before):
    wr = jnp.zeros((D_MODEL, 128), F32).at[:, :N_EXPERTS].set(w_router)
    hf, logits = _norm_route(xs.reshape(M_ROWS, D_MODEL), g_norm, mod, wr.astype(BF16))
    idx_l, top_l, idx_c, top_c = _router(logits.reshape(BATCH, S_ROWS, 128), before)
    base = (jnp.arange(BATCH, dtype=jnp.int32) * S_ROWS)[:, None, None]

    def per_expert(t):
        return jnp.swapaxes(t, 0, 1).reshape(N_EXPERTS, -1)

    rows = jnp.concatenate([per_expert(idx_l + base), per_expert(idx_c + base + SEQ)], axis=1)
    score = jnp.concatenate([per_expert(top_l), per_expert(top_c)], axis=1)
    xe = hf.at[rows].get(mode="promise_in_bounds")
    y = _ffn(xe, w1, w3, w2, layer, score[..., None])
    return _moe_combine(xs, y, idx_l.reshape(BATCH, 1, -1), idx_c.reshape(BATCH, 1, -1), mod)


def kernel(x, c, ctx, c_ctx, w_ada, b_ada, norm_mix, norm_ffn, norm_final, gla_w_in, gla_w_a1, gla_w_a2, gla_b_a, gla_head_norm, gla_w_out, hy_w_in, hy_b_in, hy_conv_w, hy_conv_b, hy_f_w1, hy_f_b1, hy_f_w2, hy_f_b2, hy_f_w3, hy_f_b3, hy_f_w4, hy_skip, hy_w_out, hy_b_out, moe_router, moe_w1, moe_w3, moe_w2):
    cond = jnp.zeros((N_GROUPS, D_MODEL), F32).at[:BATCH].set(c).at[CTX_GROUP].set(c_ctx)
    mods = _ada(cond, w_ada, b_ada).reshape(DEPTH, N_GROUPS, N_ADA, D_MODEL)
    xs = jnp.concatenate([x, ctx], axis=1)
    tok = jnp.arange(SEQ, dtype=jnp.int32)
    before = (tok[:, None] < tok[None, :]).astype(BF16)
    for i in range(DEPTH):
        mod = mods[i]
        j = i // 2
        if i % 2 == 0:
            xs = _gla_layer(xs, norm_mix[i], mod, gla_w_in[j], gla_w_a1[j], gla_w_a2[j], gla_b_a[j], gla_head_norm[j],
                            gla_w_out[j])
        else:
            fw = (hy_f_w1[j], hy_f_b1[j], hy_f_w2[j], hy_f_b2[j], hy_f_w3[j], hy_f_b3[j], hy_f_w4[j])
            xs = _hyena_layer(xs, norm_mix[i], mod, hy_w_in[j], hy_b_in[j], hy_conv_w[j], hy_conv_b[j], fw, hy_skip[j],
                              hy_w_out[j], hy_b_out[j])
        xs = _moe_layer(xs, norm_ffn[i], mod, moe_router[i], moe_w1, moe_w3, moe_w2, i, before)
    return _final_norm(xs, norm_final)
```

```python
import functools
import math

import jax
import jax.numpy as jnp
import numpy as np
from jax import lax
from jax.experimental import pallas as pl
from jax.experimental.pallas import tpu as pltpu

D_MODEL = 2048
BATCH = 8
SEQ = 2048
DEPTH = 4
CTX_LEN = 256
GRID_W = 64
EPS = 1e-6
N_ADA = 6

GLA_HEADS = 4
GLA_DK = D_MODEL // 2
GLA_DV = D_MODEL
GLA_DK_HEAD = GLA_DK // GLA_HEADS
GLA_DV_HEAD = GLA_DV // GLA_HEADS
GLA_GATE_RANK = 16
GLA_GATE_TEMP = 16.0
GLA_IN = 2 * GLA_DK + 2 * GLA_DV

HY_BANDS = 16
HY_SIN_FREQ = 1.0
HY_FAST_DECAY_PCT = 0.3
HY_SLOW_DECAY_PCT = 1.5
HY_DECAY_TARGET = 1e-2

N_EXPERTS = 16
EXPERT_FF = D_MODEL // 2
CAPACITY_FACTOR = 2

S_ROWS = CTX_LEN + SEQ
M_ROWS = BATCH * S_ROWS
CTX_GROUP = BATCH
N_GROUPS = 16
ROW_BLK = 256
BLKS = S_ROWS // ROW_BLK
MM_TM = S_ROWS // 2
GLA_CHUNK = ROW_BLK
GLA_LEVELS = 8
GLA_SUBLANES = 8
CAP_LAT = CAPACITY_FACTOR * SEQ // N_EXPERTS
CAP_CTX = CAPACITY_FACTOR * CTX_LEN // N_EXPERTS
EXP_ROWS = BATCH * (CAP_LAT + CAP_CTX)
FFN_TM = EXP_ROWS // 2

F32 = jnp.float32
BF16 = jnp.bfloat16
MIB = 1024 * 1024


def _cparams(sem, vmem_mib):
    return pltpu.CompilerParams(dimension_semantics=sem, vmem_limit_bytes=vmem_mib * MIB)


def _ada_kernel(c_ref, w_ref, b_ref, o_ref):
    c = c_ref[...]
    s = (c * jax.nn.sigmoid(c)).astype(BF16)
    o_ref[0] = jnp.dot(s, w_ref[0].astype(BF16), preferred_element_type=F32) + b_ref[0]


def _ada(cond, w_ada, b_ada):
    tn = 1024
    n = N_ADA * D_MODEL
    return pl.pallas_call(
        _ada_kernel,
        grid=(DEPTH, n // tn),
        in_specs=[
            pl.BlockSpec((N_GROUPS, D_MODEL), lambda l, j: (0, 0)),
            pl.BlockSpec((1, D_MODEL, tn), lambda l, j: (l, 0, j)),
            pl.BlockSpec((1, 1, tn), lambda l, j: (l, 0, j)),
        ],
        out_specs=pl.BlockSpec((1, N_GROUPS, tn), lambda l, j: (l, 0, j)),
        out_shape=jax.ShapeDtypeStruct((DEPTH, N_GROUPS, n), F32),
        compiler_params=_cparams(("arbitrary", "arbitrary"), 40),
        name="ada",
    )(cond, w_ada, b_ada.reshape(DEPTH, 1, n))


def _final_norm_kernel(x_ref, g_ref, o_ref):
    x = x_ref[0]
    y = x * lax.rsqrt(jnp.mean(x * x, axis=-1, keepdims=True) + EPS)
    o_ref[0] = y * g_ref[...]


def _final_norm(xs, g):
    return pl.pallas_call(
        _final_norm_kernel,
        grid=(BATCH, SEQ // ROW_BLK),
        in_specs=[
            pl.BlockSpec((1, ROW_BLK, D_MODEL), lambda b, t: (b, t, 0)),
            pl.BlockSpec((1, D_MODEL), lambda b, t: (0, 0)),
        ],
        out_specs=pl.BlockSpec((1, ROW_BLK, D_MODEL), lambda b, t: (b, t, 0)),
        out_shape=jax.ShapeDtypeStruct((BATCH, SEQ, D_MODEL), F32),
        compiler_params=_cparams(("arbitrary", "arbitrary"), 32),
        name="final_norm",
    )(xs, g.reshape(1, D_MODEL))


def _mm_kernel(*refs, has_bias, gate_idx, tm):
    a_ref, w_ref = refs[0], refs[1]
    pos = 2
    acc = jnp.dot(a_ref[...], w_ref[...], preferred_element_type=F32)
    if has_bias:
        acc = acc + refs[pos][...]
        pos += 1
    if gate_idx is None:
        o_ref = refs[pos]
        o_ref[...] = acc.astype(o_ref.dtype)
    else:
        res_ref, mb_ref, mc_ref, o_ref = refs[pos:pos + 4]
        gate_b = mb_ref[0, gate_idx:gate_idx + 1, :]
        gate_c = mc_ref[0, gate_idx:gate_idx + 1, :]
        rows = lax.broadcasted_iota(jnp.int32, (tm, 1), 0)
        last_tile = pl.program_id(1) % (S_ROWS // tm) == S_ROWS // tm - 1
        is_ctx = jnp.logical_and(last_tile, rows >= tm - CTX_LEN)
        o_ref[...] = res_ref[...] + jnp.where(is_ctx, gate_c, gate_b) * acc


def _mm(a, w, bias=None, out_dtype=F32, tn=1024, res=None, mod=None, gate_idx=None):
    m, k = a.shape
    n = w.shape[1]
    tm = MM_TM
    tn = min(tn, n)
    in_specs = [
        pl.BlockSpec((tm, k), lambda j, i: (i, 0)),
        pl.BlockSpec((k, tn), lambda j, i: (0, j)),
    ]
    args = [a, w]
    if bias is not None:
        in_specs.append(pl.BlockSpec((1, tn), lambda j, i: (0, j)))
        args.append(bias.reshape(1, n))
    if gate_idx is not None:
        per = S_ROWS // tm
        in_specs += [
            pl.BlockSpec((tm, tn), lambda j, i: (i, j)),
            pl.BlockSpec((1, N_ADA, tn), lambda j, i: (i // per, 0, j)),
            pl.BlockSpec((1, N_ADA, tn), lambda j, i: (CTX_GROUP, 0, j)),
        ]
        args += [res, mod, mod]
    return pl.pallas_call(
        functools.partial(_mm_kernel, has_bias=bias is not None, gate_idx=gate_idx, tm=tm),
        grid=(n // tn, m // tm),
        in_specs=in_specs,
        out_specs=pl.BlockSpec((tm, tn), lambda j, i: (i, j)),
        out_shape=jax.ShapeDtypeStruct((m, n), out_dtype),
        compiler_params=_cparams(("arbitrary", "arbitrary"), 52),
        name="mm",
    )(*args)


def _norm_mod_tile(x_ref, g_ref, mb_ref, mc_ref, hn_ref, tile, sh, sc):
    tm = x_ref.shape[0]
    per = S_ROWS // tm
    last_tile = tile % per == per - 1
    step = tm // 4
    for r in range(0, tm, step):
        x = x_ref[r:r + step, :]
        y = x * lax.rsqrt(jnp.mean(x * x, axis=-1, keepdims=True) + EPS) * g_ref[...]
        rows = lax.broadcasted_iota(jnp.int32, (step, 1), 0) + r
        is_ctx = jnp.logical_and(last_tile, rows >= tm - CTX_LEN)
        scale = jnp.where(is_ctx, mc_ref[0, sc:sc + 1, :], mb_ref[0, sc:sc + 1, :])
        shift = jnp.where(is_ctx, mc_ref[0, sh:sh + 1, :], mb_ref[0, sh:sh + 1, :])
        hn_ref[r:r + step, :] = (y * (1.0 + scale) + shift).astype(hn_ref.dtype)


def _norm_mod_specs(index_i):
    per = S_ROWS // MM_TM
    return [
        pl.BlockSpec((MM_TM, D_MODEL), lambda *ids: (index_i(*ids), 0)),
        pl.BlockSpec((1, D_MODEL), lambda *ids: (0, 0)),
        pl.BlockSpec((1, N_ADA, D_MODEL), lambda *ids: (index_i(*ids) // per, 0, 0)),
        pl.BlockSpec((1, N_ADA, D_MODEL), lambda *ids: (CTX_GROUP, 0, 0)),
    ]


def _nmm_kernel(x_ref, g_ref, mb_ref, mc_ref, w_ref, o_ref, hn_ref, *, sh, sc):
    @pl.when(pl.program_id(1) == 0)
    def _():
        _norm_mod_tile(x_ref, g_ref, mb_ref, mc_ref, hn_ref, pl.program_id(0), sh, sc)

    o_ref[...] = jnp.dot(hn_ref[...], w_ref[...], preferred_element_type=F32).astype(o_ref.dtype)


def _nmm(x, g, mod, sh, sc, w, tn):
    m, n = x.shape[0], w.shape[1]
    return pl.pallas_call(
        functools.partial(_nmm_kernel, sh=sh, sc=sc),
        grid=(m // MM_TM, n // tn),
        in_specs=_norm_mod_specs(lambda i, j: i) + [pl.BlockSpec((D_MODEL, tn), lambda i, j: (0, j))],
        out_specs=pl.BlockSpec((MM_TM, tn), lambda i, j: (i, j)),
        out_shape=jax.ShapeDtypeStruct((m, n), BF16),
        scratch_shapes=[pltpu.VMEM((MM_TM, D_MODEL), BF16)],
        compiler_params=_cparams(("arbitrary", "arbitrary"), 52),
        name="norm_mm",
    )(x, g.reshape(1, D_MODEL), mod, mod, w)


def _norm_route_kernel(x_ref, g_ref, mb_ref, mc_ref, wr_ref, h_ref, lg_ref):
    _norm_mod_tile(x_ref, g_ref, mb_ref, mc_ref, h_ref, pl.program_id(0), 3, 4)
    lg_ref[...] = jnp.dot(h_ref[...], wr_ref[...], preferred_element_type=F32)


def _norm_route(x, g, mod, wr):
    return pl.pallas_call(
        _norm_route_kernel,
        grid=(M_ROWS // MM_TM,),
        in_specs=_norm_mod_specs(lambda i: i) + [pl.BlockSpec((D_MODEL, 128), lambda i: (0, 0))],
        out_specs=[pl.BlockSpec((MM_TM, D_MODEL), lambda i: (i, 0)), pl.BlockSpec((MM_TM, 128), lambda i: (i, 0))],
        out_shape=[jax.ShapeDtypeStruct((M_ROWS, D_MODEL), BF16), jax.ShapeDtypeStruct((M_ROWS, 128), F32)],
        compiler_params=_cparams(("arbitrary",), 48),
        name="norm_route",
    )(x, g.reshape(1, D_MODEL), mod, mod, wr)


def _gla_la_kernel(low_ref, w2_ref, b_ref, o_ref):
    low = low_ref[...]
    for z in range(2):
        lz = low[:, z * GLA_GATE_RANK:(z + 1) * GLA_GATE_RANK].astype(BF16)
        logit = jnp.dot(lz, w2_ref[z].astype(BF16), preferred_element_type=F32) + b_ref[z]
        log_sig = jnp.minimum(logit, 0.0) - jnp.log1p(jnp.exp(-jnp.abs(logit)))
        o_ref[z] = log_sig / GLA_GATE_TEMP


def _gla_la(proj, w_a2, b_a):
    tm = MM_TM
    return pl.pallas_call(
        _gla_la_kernel,
        grid=(M_ROWS // tm,),
        in_specs=[
            pl.BlockSpec((tm, 128), lambda i: (i, GLA_IN // 128)),
            pl.BlockSpec((2, GLA_GATE_RANK, GLA_DK), lambda i: (0, 0, 0)),
            pl.BlockSpec((2, 1, GLA_DK), lambda i: (0, 0, 0)),
        ],
        out_specs=pl.BlockSpec((2, tm, GLA_DK), lambda i: (0, i, 0)),
        out_shape=jax.ShapeDtypeStruct((2, M_ROWS, GLA_DK), F32),
        compiler_params=_cparams(("arbitrary",), 48),
        name="gla_la",
    )(proj, w_a2, b_a.reshape(2, 1, GLA_DK))


def _gla_tables():
    c = GLA_CHUNK
    t = np.arange(c)
    pair = np.full((c, c), GLA_LEVELS + 1, np.int32)
    pair[t, t] = GLA_LEVELS
    for lvl in range(GLA_LEVELS):
        s = c >> (lvl + 1)
        second = (t % (2 * s)) >= s
        same = (t[:, None] // (2 * s)) == (t[None, :] // (2 * s))
        pair[same & second[:, None] & ~second[None, :]] = lvl
    lower = (t[None, :] <= t[:, None]).astype(np.float32)
    tri = np.stack([np.concatenate([lower, lower], axis=1), np.concatenate([lower.T, lower.T], axis=1)])
    return tri, np.stack([pair, pair.T])


def _gla_dir(q_ref, k_ref, v_ref, la_ref, tri_ref, pair_ref, st_ref, o_ref, reverse):
    c = GLA_CHUNK
    la = la_ref[0, 0] * math.log2(math.e)
    hi = la.astype(BF16)
    lo = (la - hi.astype(F32)).astype(BF16)
    b = jnp.dot(tri_ref[0], jnp.concatenate([hi, lo], axis=0), preferred_element_type=F32)
    q = q_ref[0].astype(F32) * (GLA_DK_HEAD ** -0.5)
    k = k_ref[0].astype(F32)
    v = v_ref[0]
    d = q.shape[-1]
    last = 0 if reverse else c - 1
    b_last = b[last:last + 1]
    nt = (((1,), (1,)), ((), ()))
    st = st_ref[...]
    o = lax.dot_general((q * jnp.exp2(b)).astype(BF16), st.astype(BF16), nt, preferred_element_type=F32)
    row = lax.broadcasted_iota(jnp.int32, (c, 1), 0)
    ms = []
    for lvl in range(GLA_LEVELS):
        s = c >> (lvl + 1)
        n = c // (2 * s)
        mid = s if reverse else s - 1
        b_mid = jnp.broadcast_to(b.reshape(n, 2 * s, d)[:, mid:mid + 1, :], (n, 2 * s, d)).reshape(c, d)
        first, second = (q, k) if reverse else (k, q)
        if s >= GLA_SUBLANES:
            mix = jnp.concatenate([first.reshape(n, 2 * s, d)[:, :s], second.reshape(n, 2 * s, d)[:, s:]],
                                  axis=1).reshape(c, d)
        else:
            mix = jnp.where(row % (2 * s) >= s, second, first)
        ms.append((mix * jnp.exp2(-jnp.abs(b - b_mid))).astype(BF16))
    h = c // 2
    lo_rows, hi_rows = slice(0, h), slice(h, c)
    q_rows, k_rows = (lo_rows, hi_rows) if reverse else (hi_rows, lo_rows)
    a_off = lax.dot_general(ms[0][q_rows], ms[0][k_rows], nt, preferred_element_type=F32).astype(BF16)
    pair = pair_ref[0, :h, :h]
    qb, kb = q.astype(BF16), k.astype(BF16)
    diag = []
    for rows in (lo_rows, hi_rows):
        acc = jnp.where(pair == GLA_LEVELS, lax.dot_general(qb[rows], kb[rows], nt, preferred_element_type=F32), 0.0)
        for lvl in range(1, GLA_LEVELS):
            m = ms[lvl][rows]
            acc = jnp.where(pair == lvl, lax.dot_general(m, m, nt, preferred_element_type=F32), acc)
        diag.append(acc.astype(BF16))
    o_lo = jnp.dot(diag[0], v[lo_rows], preferred_element_type=F32)
    o_hi = jnp.dot(diag[1], v[hi_rows], preferred_element_type=F32)
    if reverse:
        o_lo = o_lo + jnp.dot(a_off, v[hi_rows], preferred_element_type=F32)
    else:
        o_hi = o_hi + jnp.dot(a_off, v[lo_rows], preferred_element_type=F32)
    o_ref[0] = (o + jnp.concatenate([o_lo, o_hi], axis=0)).astype(o_ref.dtype)
    kt = (k * jnp.exp2(b_last - b)).astype(BF16)
    tn = (((0,), (0,)), ((), ()))
    st_ref[...] = st * jnp.exp2(b_last) + lax.dot_general(v, kt, tn, preferred_element_type=F32)


def _gla_scan_kernel(qf, kf, vf, laf, qb, kb, vb, lab, tri_ref, pair_ref, of_ref, ob_ref, sf_ref, sb_ref):
    @pl.when(pl.program_id(2) == 0)
    def _():
        sf_ref[...] = jnp.zeros_like(sf_ref)
        sb_ref[...] = jnp.zeros_like(sb_ref)

    _gla_dir(qf, kf, vf, laf, tri_ref.at[0:1], pair_ref.at[0:1], sf_ref, of_ref, False)
    _gla_dir(qb, kb, vb, lab, tri_ref.at[1:2], pair_ref.at[1:2], sb_ref, ob_ref, True)


def _fwd_block(t):
    return jnp.where(t == 0, BLKS - 1, t - 1)


def _bwd_block(t):
    return jnp.where(t == 0, BLKS - 1, BLKS - 1 - t)


def _gla_scan(proj, la):
    tri_np, pair_np = _gla_tables()
    tri = jnp.asarray(tri_np, BF16)
    pair = jnp.asarray(pair_np, jnp.int32)
    kq, kv = GLA_DK_HEAD, GLA_DV_HEAD
    k_off = GLA_DK // kq
    v_off = 2 * GLA_DK // kv
    c = GLA_CHUNK
    spec_f = [
        pl.BlockSpec((1, ROW_BLK, kq), lambda b, h, t: (b, _fwd_block(t), h)),
        pl.BlockSpec((1, ROW_BLK, kq), lambda b, h, t: (b, _fwd_block(t), k_off + h)),
        pl.BlockSpec((1, ROW_BLK, kv), lambda b, h, t: (b, _fwd_block(t), v_off + h)),
        pl.BlockSpec((1, 1, ROW_BLK, kq), lambda b, h, t: (0, b, _fwd_block(t), h)),
    ]
    spec_b = [
        pl.BlockSpec((1, ROW_BLK, kq), lambda b, h, t: (b, _bwd_block(t), h)),
        pl.BlockSpec((1, ROW_BLK, kq), lambda b, h, t: (b, _bwd_block(t), k_off + h)),
        pl.BlockSpec((1, ROW_BLK, kv), lambda b, h, t: (b, _bwd_block(t), v_off + h)),
        pl.BlockSpec((1, 1, ROW_BLK, kq), lambda b, h, t: (1, b, _bwd_block(t), h)),
    ]
    const = [
        pl.BlockSpec((2,) + tri_np.shape[1:], lambda b, h, t: (0, 0, 0)),
        pl.BlockSpec((2, c, c), lambda b, h, t: (0, 0, 0)),
    ]
    out_sd = jax.ShapeDtypeStruct((BATCH, S_ROWS, GLA_DV), BF16)
    return pl.pallas_call(
        _gla_scan_kernel,
        grid=(BATCH, GLA_HEADS, BLKS),
        in_specs=spec_f + spec_b + const,
        out_specs=[
            pl.BlockSpec((1, ROW_BLK, kv), lambda b, h, t: (b, _fwd_block(t), h)),
            pl.BlockSpec((1, ROW_BLK, kv), lambda b, h, t: (b, _bwd_block(t), h)),
        ],
        out_shape=[out_sd, out_sd],
        scratch_shapes=[pltpu.VMEM((kv, kq), F32), pltpu.VMEM((kv, kq), F32)],
        compiler_params=_cparams(("arbitrary", "arbitrary", "arbitrary"), 32),
        name="gla_scan",
    )(proj, proj, proj, la, proj, proj, proj, la, tri, pair)


def _gla_post_kernel(of_ref, ob_ref, g_ref, hn_ref, o_ref):
    hn = hn_ref[...]
    for h in range(GLA_HEADS):
        cols = slice(h * GLA_DV_HEAD, (h + 1) * GLA_DV_HEAD)
        o = of_ref[0, :, cols].astype(F32) + ob_ref[0, :, cols].astype(F32)
        o = o * lax.rsqrt(jnp.mean(o * o, axis=-1, keepdims=True) + EPS) * hn
        g = g_ref[0, :, cols].astype(F32)
        o_ref[0, :, cols] = (o.astype(BF16) * (g * jax.nn.sigmoid(g)).astype(BF16))


def _gla_post(o_f, o_b, proj, head_norm):
    g_off = 2 * GLA_DK // GLA_DV + 1
    blk = pl.BlockSpec((1, ROW_BLK, GLA_DV), lambda b, t: (b, t, 0))
    return pl.pallas_call(
        _gla_post_kernel,
        grid=(BATCH, BLKS),
        in_specs=[blk, blk,
                  pl.BlockSpec((1, ROW_BLK, GLA_DV), lambda b, t: (b, t, g_off)),
                  pl.BlockSpec((1, GLA_DV_HEAD), lambda b, t: (0, 0))],
        out_specs=blk,
        out_shape=jax.ShapeDtypeStruct((BATCH, S_ROWS, GLA_DV), BF16),
        compiler_params=_cparams(("arbitrary", "arbitrary"), 32),
        name="gla_post",
    )(o_f, o_b, proj, head_norm.reshape(1, GLA_DV_HEAD))


def _gla_layer(xs, g_norm, mod, w_in, w_a1, w_a2, b_a, head_norm, w_out):
    w_cat = jnp.concatenate([w_in, w_a1[0], w_a1[1], jnp.zeros((D_MODEL, 128 - 2 * GLA_GATE_RANK), F32)], axis=1)
    proj = _nmm(xs.reshape(M_ROWS, D_MODEL), g_norm, mod, 0, 1, w_cat.astype(BF16), tn=896)
    la = _gla_la(proj, w_a2, b_a).reshape(2, BATCH, S_ROWS, GLA_DK)
    proj = proj.reshape(BATCH, S_ROWS, GLA_IN + 128)
    o_f, o_b = _gla_scan(proj, la)
    y = _gla_post(o_f, o_b, proj, head_norm)
    out = _mm(y.reshape(M_ROWS, GLA_DV), w_out.astype(BF16), res=xs.reshape(M_ROWS, D_MODEL), mod=mod, gate_idx=2)
    return out.reshape(BATCH, S_ROWS, D_MODEL)


def _hy_proj_kernel(x_ref, g_ref, mb_ref, mc_ref, w0, w1, w2, bi0, bi1, bi2, cw0, cw1, cw2, cb0, cb1, cb2,
                    x0_ref, u_ref, a_ref, *, tm):
    @pl.when(pl.program_id(1) == 0)
    def _():
        _norm_mod_tile(x_ref, g_ref, mb_ref, mc_ref, a_ref, pl.program_id(0), 0, 1)

    rows = lax.broadcasted_iota(jnp.int32, (tm, 1), 0)
    last_tile = pl.program_id(0) % (S_ROWS // tm) == S_ROWS // tm - 1
    ctx0 = tm - CTX_LEN
    is_ctx = jnp.logical_and(last_tile, rows >= ctx0)
    pos = jnp.where(is_ctx, rows - ctx0, rows % GRID_W)
    last = jnp.where(is_ctx, CTX_LEN - 1, GRID_W - 1)
    has_prev = pos != 0
    has_next = pos != last
    a = a_ref[...]

    def branch(w_ref, bi_ref, cw_ref, cb_ref):
        p = jnp.dot(a, w_ref[...], preferred_element_type=F32) + bi_ref[...]
        prev = jnp.where(has_prev, pltpu.roll(p, 1, 0), 0.0)
        nxt = jnp.where(has_next, pltpu.roll(p, tm - 1, 0), 0.0)
        return prev * cw_ref[0:1, :] + p * cw_ref[1:2, :] + nxt * cw_ref[2:3, :] + cb_ref[...]

    x0_ref[...] = branch(w0, bi0, cw0, cb0).astype(BF16)
    u_ref[...] = (branch(w1, bi1, cw1, cb1) * branch(w2, bi2, cw2, cb2)).astype(BF16)


def _hy_proj(x, g_norm, mod, w_in, b_in, conv_w, conv_b):
    tm, tn = MM_TM, 512
    nj = D_MODEL // tn
    wspec = lambda k: pl.BlockSpec((D_MODEL, tn), lambda i, j: (0, k * nj + j))
    bspec = lambda k: pl.BlockSpec((1, tn), lambda i, j: (0, k * nj + j))
    cspec = lambda k: pl.BlockSpec((3, tn), lambda i, j: (0, k * nj + j))
    ospec = pl.BlockSpec((tm, tn), lambda i, j: (i, j))
    out_sd = jax.ShapeDtypeStruct((M_ROWS, D_MODEL), BF16)
    bi = b_in.reshape(1, 3 * D_MODEL)
    cb = conv_b.reshape(1, 3 * D_MODEL)
    return pl.pallas_call(
        functools.partial(_hy_proj_kernel, tm=tm),
        grid=(M_ROWS // tm, nj),
        in_specs=_norm_mod_specs(lambda i, j: i) + [wspec(0), wspec(1), wspec(2),
                  bspec(0), bspec(1), bspec(2), cspec(0), cspec(1), cspec(2), bspec(0), bspec(1), bspec(2)],
        out_specs=[ospec, ospec],
        out_shape=[out_sd, out_sd],
        scratch_shapes=[pltpu.VMEM((tm, D_MODEL), BF16)],
        compiler_params=_cparams(("arbitrary", "arbitrary"), 56),
        name="hy_proj",
    )(x, g_norm.reshape(1, D_MODEL), mod, mod, w_in, w_in, w_in, bi, bi, bi, conv_w, conv_w, conv_w, cb, cb, cb)


def _hy_filter_kernel(z_ref, w1, b1, w2, b2, w3, b3, w4f, w4b, dl_ref, o_ref):
    def layer(x, w_ref, b_ref):
        y = jnp.dot(x.astype(BF16), w_ref[...].astype(BF16), preferred_element_type=F32) + b_ref[...]
        return jnp.sin(HY_SIN_FREQ * y)

    z = z_ref[...]
    hdn = layer(layer(layer(z, w1, b1), w2, b2), w3, b3).astype(BF16)
    decay = jnp.exp(-z[:, 0:1] * dl_ref[...])
    h_fwd = jnp.dot(hdn, w4f[...].astype(BF16), preferred_element_type=F32) * decay
    h_bwd = jnp.dot(hdn, w4b[...].astype(BF16), preferred_element_type=F32) * decay
    lag0 = lax.broadcasted_iota(jnp.int32, (z.shape[0], 1), 0) == 0
    h_bwd = jnp.where(lag0, 0.0, h_bwd)
    inv = 1.0 / jnp.sum(jnp.abs(h_fwd) + jnp.abs(h_bwd), axis=0, keepdims=True)
    for k, sig in enumerate(((h_fwd + h_bwd) * inv, (h_bwd - h_fwd) * inv)):
        hi = sig.astype(BF16)
        o_ref[2 * k] = hi
        o_ref[2 * k + 1] = (sig - hi.astype(F32)).astype(BF16)


def _hy_filter(L, f_w1, f_b1, f_w2, f_b2, f_w3, f_b3, f_w4):
    tn = 512
    nj = D_MODEL // tn
    t = jnp.linspace(0.0, 1.0, L, dtype=F32)[:, None]
    w = 2 * math.pi * jnp.arange(L, dtype=F32)[:, None] / L
    f = jnp.linspace(1e-4, HY_BANDS - 1, HY_BANDS, dtype=F32)[None, :]
    emb = 1 + 2 * HY_BANDS
    z = jnp.concatenate([t, jnp.cos(f * w), -jnp.sin(f * w), jnp.zeros((L, 128 - emb), F32)], axis=-1)
    w1 = jnp.concatenate([f_w1, jnp.zeros((128 - emb, f_w1.shape[1]), F32)], axis=0)
    max_decay = math.log(HY_DECAY_TARGET) / HY_FAST_DECAY_PCT
    min_decay = math.log(HY_DECAY_TARGET) / HY_SLOW_DECAY_PCT
    deltas = jnp.abs(jnp.linspace(min_decay, max_decay, D_MODEL, dtype=F32))[None, :]
    hid = f_w2.shape[0]
    full = lambda shape: pl.BlockSpec(shape, lambda j: (0, 0))
    return pl.pallas_call(
        _hy_filter_kernel,
        grid=(nj,),
        in_specs=[full((L, 128)), full((128, hid)), full((1, hid)), full((hid, hid)), full((1, hid)),
                  full((hid, hid)), full((1, hid)),
                  pl.BlockSpec((hid, tn), lambda j: (0, j)), pl.BlockSpec((hid, tn), lambda j: (0, nj + j)),
                  pl.BlockSpec((1, tn), lambda j: (0, j))],
        out_specs=pl.BlockSpec((4, L, tn), lambda j: (0, 0, j)),
        out_shape=jax.ShapeDtypeStruct((4, L, D_MODEL), BF16),
        compiler_params=_cparams(("arbitrary",), 48),
        name="hy_filter",
    )(z, w1, f_b1.reshape(1, hid), f_w2, f_b2.reshape(1, hid), f_w3, f_b3.reshape(1, hid), f_w4, f_w4, deltas)


def _dft_tiles(L):
    return (min(2 * L, 1024), min(L, 1024))


def _dft_mats(L):
    n = 2 * L
    tf = _dft_tiles(L)[0] // 2
    f = jnp.arange(L, dtype=jnp.int32)
    ang = ((f[:, None] * f[None, :]) % n).astype(F32) * (2 * math.pi / n)
    cosm = jnp.cos(ang)
    sinm = jnp.sin(ang).at[0].set(jnp.where(f % 2 == 0, 1.0, -1.0))
    fwd = jnp.concatenate([cosm.reshape(L // tf, tf, L), sinm.reshape(L // tf, tf, L)], axis=1).reshape(n, L)
    wgt = jnp.where(f == 0, 1.0 / n, 2.0 / n)[None, :]
    inv = jnp.concatenate([cosm.T * wgt, -sinm.T * wgt], axis=1)
    fwd_hi = fwd.astype(BF16)
    return fwd_hi, (fwd - fwd_hi.astype(F32)).astype(BF16), inv.astype(BF16)


def _dft_fwd_kernel(f_ref, x_ref, ka, kb, kap, o_ref, *, tf):
    acc = jnp.dot(f_ref[...], x_ref[0], preferred_element_type=F32)
    c, s = acc[:tf], acc[tf:]
    o_ref[0, 0] = (c * ka[...] + s * kb[...]).astype(BF16)
    o_ref[0, 1] = (c * kb[...] - s * kap[...]).astype(BF16)


def _dft_fwd(fwd, x, row_blk, L, spec):
    nb = x.shape[0]
    tm = _dft_tiles(L)[0]
    tf, tn = tm // 2, 1024
    kspec = pl.BlockSpec((tf, tn), lambda b, j, i: (i, j))
    return pl.pallas_call(
        functools.partial(_dft_fwd_kernel, tf=tf),
        grid=(nb, D_MODEL // tn, 2 * L // tm),
        in_specs=[pl.BlockSpec((tm, L), lambda b, j, i: (i, 0)),
                  pl.BlockSpec((1, L, tn), lambda b, j, i: (b, row_blk, j)), kspec, kspec, kspec],
        out_specs=pl.BlockSpec((1, 2, tf, tn), lambda b, j, i: (b, 0, i, j)),
        out_shape=jax.ShapeDtypeStruct((nb, 2, L, D_MODEL), BF16),
        compiler_params=_cparams(("arbitrary", "arbitrary", "arbitrary"), 52),
        name="dft_fwd",
    )(fwd, x, *spec)


def _filter_spectrum_kernel(fh_ref, fl_ref, sig_ref, ka_ref, kb_ref, kap_ref, *, tf):
    fh, fl = fh_ref[...], fl_ref[...]

    def dft(rows, hi, lo):
        return (jnp.dot(fh[rows], hi, preferred_element_type=F32) + jnp.dot(fh[rows], lo, preferred_element_type=F32)
                + jnp.dot(fl[rows], hi, preferred_element_type=F32))

    ka = dft(slice(0, tf), sig_ref[0], sig_ref[1])
    kb = dft(slice(tf, 2 * tf), sig_ref[2], sig_ref[3])
    nyq = dft(slice(tf, tf + 16), sig_ref[0], sig_ref[1])[0:1]
    freq = lax.broadcasted_iota(jnp.int32, (tf, 1), 0) + pl.program_id(1) * tf
    ka_ref[...] = ka
    kb_ref[...] = jnp.where(freq == 0, 0.0, kb)
    kap_ref[...] = jnp.where(freq == 0, nyq, ka)


def _filter_spectrum(sig, fwd_hi, fwd_lo, L):
    tm = _dft_tiles(L)[0]
    tf, tn = tm // 2, 256
    fspec = pl.BlockSpec((tm, L), lambda j, i: (i, 0))
    ospec = pl.BlockSpec((tf, tn), lambda j, i: (i, j))
    out_sd = jax.ShapeDtypeStruct((L, D_MODEL), F32)
    return pl.pallas_call(
        functools.partial(_filter_spectrum_kernel, tf=tf),
        grid=(D_MODEL // tn, 2 * L // tm),
        in_specs=[fspec, fspec, pl.BlockSpec((4, L, tn), lambda j, i: (0, 0, j))],
        out_specs=[ospec, ospec, ospec],
        out_shape=[out_sd, out_sd, out_sd],
        compiler_params=_cparams(("arbitrary", "arbitrary"), 52),
        name="filter_spectrum",
    )(fwd_hi, fwd_lo, sig)


def _dft_inv_kernel(g_ref, p_ref, x0_ref, u_ref, skip_ref, *rest):
    o_ref = rest[-1]
    y = jnp.dot(g_ref[...], p_ref[0], preferred_element_type=F32)
    o_ref[0] = (x0_ref[0].astype(F32) * (y + u_ref[0].astype(F32) * skip_ref[...])).astype(BF16)


def _dft_inv(inv, p, x0, u, skip, L, prev=None):
    tm = _dft_tiles(L)[1]
    tn = 512
    off = 0 if L == SEQ else SEQ // tm
    row = lambda b, j, i: (b, off + i, j)
    in_specs = [pl.BlockSpec((tm, 2 * L), lambda b, j, i: (i, 0)),
                pl.BlockSpec((1, 2 * L, tn), lambda b, j, i: (b, 0, j)),
                pl.BlockSpec((1, tm, tn), row), pl.BlockSpec((1, tm, tn), row),
                pl.BlockSpec((1, tn), lambda b, j, i: (0, j))]
    args = [inv, p, x0, u, skip.reshape(1, D_MODEL)]
    aliases = {}
    if prev is not None:
        in_specs.append(pl.BlockSpec(memory_space=pl.ANY))
        args.append(prev)
        aliases = {len(args) - 1: 0}
    return pl.pallas_call(
        _dft_inv_kernel,
        grid=(BATCH, D_MODEL // tn, L // tm),
        in_specs=in_specs,
        out_specs=pl.BlockSpec((1, tm, tn), row),
        out_shape=jax.ShapeDtypeStruct((BATCH, S_ROWS, D_MODEL), BF16),
        input_output_aliases=aliases,
        compiler_params=_cparams(("arbitrary", "arbitrary", "arbitrary"), 52),
        name="dft_inv",
    )(*args)


def _hyena_layer(xs, g_norm, mod, w_in, b_in, conv_w, conv_b, fw, skip, w_out, b_out):
    x0, u = _hy_proj(xs.reshape(M_ROWS, D_MODEL), g_norm, mod, w_in.astype(BF16), b_in, conv_w, conv_b)
    x0 = x0.reshape(BATCH, S_ROWS, D_MODEL)
    u = u.reshape(BATCH, S_ROWS, D_MODEL)
    y = None
    for L in (SEQ, CTX_LEN):
        fwd_hi, fwd_lo, inv = _dft_mats(L)
        spec = _filter_spectrum(_hy_filter(L, *fw), fwd_hi, fwd_lo, L)
        row_blk = 0 if L == SEQ else SEQ // L
        p = _dft_fwd(fwd_hi, u, row_blk, L, spec=spec).reshape(BATCH, 2 * L, D_MODEL)
        y = _dft_inv(inv, p, x0, u, skip, L, prev=y)
    out = _mm(y.reshape(M_ROWS, D_MODEL), w_out.astype(BF16), bias=b_out,
              res=xs.reshape(M_ROWS, D_MODEL), mod=mod, gate_idx=2)
    return out.reshape(BATCH, S_ROWS, D_MODEL)


def _ffn_up_kernel(x_ref, w1_ref, w3_ref, o_ref):
    x = x_ref[0]
    h1 = jnp.dot(x, w1_ref[0, 0].astype(BF16), preferred_element_type=F32)
    h3 = jnp.dot(x, w3_ref[0, 0].astype(BF16), preferred_element_type=F32)
    o_ref[0] = (h1 * jax.nn.sigmoid(h1) * h3).astype(BF16)


def _ffn_down_kernel(a_ref, w2_ref, s_ref, o_ref):
    y = jnp.dot(a_ref[0], w2_ref[0, 0].astype(BF16), preferred_element_type=F32) * s_ref[0]
    o_ref[0] = y.astype(o_ref.dtype)


def _ffn(xe, w1, w3, w2, layer, score):
    tm, tf, tn = FFN_TM, 512, 1024
    act = pl.pallas_call(
        _ffn_up_kernel,
        grid=(N_EXPERTS, EXPERT_FF // tf, EXP_ROWS // tm),
        in_specs=[
            pl.BlockSpec((1, tm, D_MODEL), lambda e, f, i: (e, i, 0)),
            pl.BlockSpec((1, 1, D_MODEL, tf), lambda e, f, i: (layer, e, 0, f)),
            pl.BlockSpec((1, 1, D_MODEL, tf), lambda e, f, i: (layer, e, 0, f)),
        ],
        out_specs=pl.BlockSpec((1, tm, tf), lambda e, f, i: (e, i, f)),
        out_shape=jax.ShapeDtypeStruct((N_EXPERTS, EXP_ROWS, EXPERT_FF), BF16),
        compiler_params=_cparams(("arbitrary", "arbitrary", "arbitrary"), 52),
        name="moe_ffn_up",
    )(xe, w1, w3)
    return pl.pallas_call(
        _ffn_down_kernel,
        grid=(N_EXPERTS, D_MODEL // tn, EXP_ROWS // tm),
        in_specs=[
            pl.BlockSpec((1, tm, EXPERT_FF), lambda e, j, i: (e, i, 0)),
            pl.BlockSpec((1, 1, EXPERT_FF, tn), lambda e, j, i: (layer, e, 0, j)),
            pl.BlockSpec((1, tm, 1), lambda e, j, i: (e, i, 0)),
        ],
        out_specs=pl.BlockSpec((1, tm, tn), lambda e, j, i: (e, i, j)),
        out_shape=jax.ShapeDtypeStruct((N_EXPERTS, EXP_ROWS, D_MODEL), BF16),
        compiler_params=_cparams(("arbitrary", "arbitrary", "arbitrary"), 52),
        name="moe_ffn_down",
    )(act, w2, score)


def _moe_combine_kernel(il_ref, ic_ref, yl_ref, yc_ref, x_ref, mb_ref, mc_ref, o_ref, ql_ref, qc_ref):
    @pl.when(pl.program_id(1) == 0)
    def _():
        chunk = 512
        tok = lax.broadcasted_iota(jnp.int32, (SEQ, chunk), 0)
        for kk in range(N_EXPERTS * CAP_LAT // chunk):
            cols = slice(kk * chunk, (kk + 1) * chunk)
            ql_ref[:, cols] = jnp.where(il_ref[0, :, cols] == tok, 1.0, 0.0).astype(BF16)
        tok_c = lax.broadcasted_iota(jnp.int32, (CTX_LEN, N_EXPERTS * CAP_CTX), 0)
        qc_ref[...] = jnp.where(ic_ref[0] == tok_c, 1.0, 0.0).astype(BF16)

    tn = o_ref.shape[-1]
    lat = jnp.dot(ql_ref[...], yl_ref[...].reshape(N_EXPERTS * CAP_LAT, tn), preferred_element_type=F32)
    o_ref[0, :SEQ, :] = x_ref[0, :SEQ, :] + mb_ref[0, 5:6, :] * lat
    cx = jnp.dot(qc_ref[...], yc_ref[...].reshape(N_EXPERTS * CAP_CTX, tn), preferred_element_type=F32)
    o_ref[0, SEQ:, :] = x_ref[0, SEQ:, :] + mc_ref[0, 5:6, :] * cx


def _moe_combine(xs, y, idx_l, idx_c, mod):
    tn = 256
    return pl.pallas_call(
        _moe_combine_kernel,
        grid=(BATCH, D_MODEL // tn),
        in_specs=[
            pl.BlockSpec((1, 1, N_EXPERTS * CAP_LAT), lambda b, j: (b, 0, 0)),
            pl.BlockSpec((1, 1, N_EXPERTS * CAP_CTX), lambda b, j: (b, 0, 0)),
            pl.BlockSpec((N_EXPERTS, CAP_LAT, tn), lambda b, j: (0, b, j)),
            pl.BlockSpec((N_EXPERTS, CAP_CTX, tn), lambda b, j: (0, BATCH * CAP_LAT // CAP_CTX + b, j)),
            pl.BlockSpec((1, S_ROWS, tn), lambda b, j: (b, 0, j)),
            pl.BlockSpec((1, N_ADA, tn), lambda b, j: (b, 0, j)),
            pl.BlockSpec((1, N_ADA, tn), lambda b, j: (CTX_GROUP, 0, j)),
        ],
        out_specs=pl.BlockSpec((1, S_ROWS, tn), lambda b, j: (b, 0, j)),
        out_shape=jax.ShapeDtypeStruct((BATCH, S_ROWS, D_MODEL), F32),
        scratch_shapes=[pltpu.VMEM((SEQ, N_EXPERTS * CAP_LAT), BF16),
                        pltpu.VMEM((CTX_LEN, N_EXPERTS * CAP_CTX), BF16)],
        compiler_params=_cparams(("arbitrary", "arbitrary"), 56),
        name="moe_combine",
    )(idx_l, idx_c, y, y, xs, mod, mod)


def _route_stream(s_tok, s_exp, u_ref, cap, idx_ref, score_ref):
    n_tok = s_exp.shape[1]
    bits = lax.bitcast_convert_type(s_exp, jnp.int32)
    thr = jnp.zeros((N_EXPERTS, 1), jnp.int32)
    for bit in range(30, -1, -1):
        cand = thr | (1 << bit)
        cnt = jnp.sum(jnp.where(bits >= cand, 1.0, 0.0), axis=1, keepdims=True)
        thr = jnp.where(cnt >= cap, cand, thr)
    above = bits > thr
    tied = bits == thr
    n_above = jnp.sum(jnp.where(above, 1.0, 0.0), axis=1, keepdims=True)
    before = u_ref[:n_tok, :n_tok]
    tie_rank = jnp.dot(jnp.where(tied, 1.0, 0.0).astype(BF16), before, preferred_element_type=F32)
    chosen = jnp.logical_or(above, jnp.logical_and(tied, tie_rank < cap - n_above))
    pos = jnp.dot(jnp.where(chosen, 1.0, 0.0).astype(BF16), before, preferred_element_type=F32)
    slot = jnp.where(chosen, pos, -1.0)
    slot_tok = jnp.concatenate([slot, jnp.full((128 - N_EXPERTS, n_tok), -1.0, F32)], axis=0).T
    slot_ids = lax.broadcasted_iota(jnp.int32, (1, cap), 1).astype(F32)
    tok_ids = lax.broadcasted_iota(jnp.int32, (n_tok, 1), 0).astype(F32)
    for e in range(N_EXPERTS):
        hit = slot_tok[:, e:e + 1] == slot_ids
        idx_ref[0, e:e + 1, :] = jnp.sum(jnp.where(hit, tok_ids, 0.0), axis=0, keepdims=True).astype(jnp.int32)
        score_ref[0, e:e + 1, :] = jnp.sum(jnp.where(hit, s_tok[:, e:e + 1], 0.0), axis=0, keepdims=True)


def _router_kernel(lg_ref, u_ref, il_ref, sl_ref, ic_ref, sc_ref):
    lg = lg_ref[0]
    valid = lax.broadcasted_iota(jnp.int32, (1, lg.shape[1]), 1) < N_EXPERTS
    x = jnp.where(valid, lg, -jnp.inf)
    e = jnp.where(valid, jnp.exp(x - jnp.max(x, axis=-1, keepdims=True)), 0.0)
    s = e / jnp.sum(e, axis=-1, keepdims=True)
    s_exp = s.T[:N_EXPERTS]
    _route_stream(s[:SEQ], s_exp[:, :SEQ], u_ref, CAP_LAT, il_ref, sl_ref)
    _route_stream(s[SEQ:], s_exp[:, SEQ:], u_ref, CAP_CTX, ic_ref, sc_ref)


def _router(logits, before):
    out = lambda cap: pl.BlockSpec((1, N_EXPERTS, cap), lambda b: (b, 0, 0))
    sd = lambda cap, dt: jax.ShapeDtypeStruct((BATCH, N_EXPERTS, cap), dt)
    return pl.pallas_call(
        _router_kernel,
        grid=(BATCH,),
        in_specs=[pl.BlockSpec((1, S_ROWS, 128), lambda b: (b, 0, 0)),
                  pl.BlockSpec((SEQ, SEQ), lambda b: (0, 0))],
        out_specs=[out(CAP_LAT), out(CAP_LAT), out(CAP_CTX), out(CAP_CTX)],
        out_shape=[sd(CAP_LAT, jnp.int32), sd(CAP_LAT, F32), sd(CAP_CTX, jnp.int32), sd(CAP_CTX, F32)],
        compiler_params=_cparams(("arbitrary",), 52),
        name="router",
    )(logits, before)


def _moe_layer(xs, g_norm, mod, w_router, w1, w3, w2, layer, before):
    wr = jnp.zeros((D_MODEL, 128), F32).at[:, :N_EXPERTS].set(w_router)
    hf, logits = _norm_route(xs.reshape(M_ROWS, D_MODEL), g_norm, mod, wr.astype(BF16))
    idx_l, top_l, idx_c, top_c = _router(logits.reshape(BATCH, S_ROWS, 128), before)
    base = (jnp.arange(BATCH, dtype=jnp.int32) * S_ROWS)[:, None, None]

    def per_expert(t):
        return jnp.swapaxes(t, 0, 1).reshape(N_EXPERTS, -1)

    rows = jnp.concatenate([per_expert(idx_l + base), per_expert(idx_c + base + SEQ)], axis=1)
    score = jnp.concatenate([per_expert(top_l), per_expert(top_c)], axis=1)
    xe = hf.at[rows].get(mode="promise_in_bounds")
    y = _ffn(xe, w1, w3, w2, layer, score[..., None])
    return _moe_combine(xs, y, idx_l.reshape(BATCH, 1, -1), idx_c.reshape(BATCH, 1, -1), mod)


def kernel(x, c, ctx, c_ctx, w_ada, b_ada, norm_mix, norm_ffn, norm_final, gla_w_in, gla_w_a1, gla_w_a2, gla_b_a, gla_head_norm, gla_w_out, hy_w_in, hy_b_in, hy_conv_w, hy_conv_b, hy_f_w1, hy_f_b1, hy_f_w2, hy_f_b2, hy_f_w3, hy_f_b3, hy_f_w4, hy_skip, hy_w_out, hy_b_out, moe_router, moe_w1, moe_w3, moe_w2):
    cond = jnp.zeros((N_GROUPS, D_MODEL), F32).at[:BATCH].set(c).at[CTX_GROUP].set(c_ctx)
    mods = _ada(cond, w_ada, b_ada).reshape(DEPTH, N_GROUPS, N_ADA, D_MODEL)
    xs = jnp.concatenate([x, ctx], axis=1)
    tok = jnp.arange(SEQ, dtype=jnp.int32)
    before = (tok[:, None] < tok[None, :]).astype(BF16)
    for i in range(DEPTH):
        mod = mods[i]
        j = i // 2
        if i % 2 == 0:
            xs = _gla_layer(xs, norm_mix[i], mod, gla_w_in[j], gla_w_a1[j], gla_w_a2[j], gla_b_a[j], gla_head_norm[j],
                            gla_w_out[j])
        else:
            fw = (hy_f_w1[j], hy_f_b1[j], hy_f_w2[j], hy_f_b2[j], hy_f_w3[j], hy_f_b3[j], hy_f_w4[j])
            xs = _hyena_layer(xs, norm_mix[i], mod, hy_w_in[j], hy_b_in[j], hy_conv_w[j], hy_conv_b[j], fw, hy_skip[j],
                              hy_w_out[j], hy_b_out[j])
        xs = _moe_layer(xs, norm_ffn[i], mod, moe_router[i], moe_w1, moe_w3, moe_w2, i, before)
    return _final_norm(xs, norm_final)
```

```python
import functools
import math

import jax
import jax.numpy as jnp
import numpy as np
from jax import lax
from jax.experimental import pallas as pl
from jax.experimental.pallas import tpu as pltpu
from jax.experimental.pallas import tpu_sc as plsc

D_MODEL = 2048
BATCH = 8
SEQ = 2048
DEPTH = 4
CTX_LEN = 256
GRID_W = 64
EPS = 1e-6
N_ADA = 6

GLA_HEADS = 4
GLA_DK = D_MODEL // 2
GLA_DV = D_MODEL
GLA_DK_HEAD = GLA_DK // GLA_HEADS
GLA_DV_HEAD = GLA_DV // GLA_HEADS
GLA_GATE_RANK = 16
GLA_GATE_TEMP = 16.0
GLA_IN = 2 * GLA_DK + 2 * GLA_DV

HY_BANDS = 16
HY_SIN_FREQ = 1.0
HY_FAST_DECAY_PCT = 0.3
HY_SLOW_DECAY_PCT = 1.5
HY_DECAY_TARGET = 1e-2

N_EXPERTS = 16
EXPERT_FF = D_MODEL // 2
CAPACITY_FACTOR = 2

S_ROWS = CTX_LEN + SEQ
M_ROWS = BATCH * S_ROWS
CTX_GROUP = BATCH
N_GROUPS = 16
ROW_BLK = 256
BLKS = S_ROWS // ROW_BLK
MM_TM = S_ROWS // 2
GLA_CHUNK = ROW_BLK
GLA_LEVELS = 8
GLA_SUBLANES = 8
CAP_LAT = CAPACITY_FACTOR * SEQ // N_EXPERTS
CAP_CTX = CAPACITY_FACTOR * CTX_LEN // N_EXPERTS
EXP_ROWS = BATCH * (CAP_LAT + CAP_CTX)
FFN_TM = EXP_ROWS // 2

F32 = jnp.float32
BF16 = jnp.bfloat16
MIB = 1024 * 1024


def _cparams(sem, vmem_mib):
    return pltpu.CompilerParams(dimension_semantics=sem, vmem_limit_bytes=vmem_mib * MIB)


def _ada_kernel(c_ref, w_ref, b_ref, o_ref):
    c = c_ref[...]
    s = (c * jax.nn.sigmoid(c)).astype(BF16)
    o_ref[0] = jnp.dot(s, w_ref[0].astype(BF16), preferred_element_type=F32) + b_ref[0]


def _ada(cond, w_ada, b_ada):
    tn = 1024
    n = N_ADA * D_MODEL
    return pl.pallas_call(
        _ada_kernel,
        grid=(DEPTH, n // tn),
        in_specs=[
            pl.BlockSpec((N_GROUPS, D_MODEL), lambda l, j: (0, 0)),
            pl.BlockSpec((1, D_MODEL, tn), lambda l, j: (l, 0, j)),
            pl.BlockSpec((1, 1, tn), lambda l, j: (l, 0, j)),
        ],
        out_specs=pl.BlockSpec((1, N_GROUPS, tn), lambda l, j: (l, 0, j)),
        out_shape=jax.ShapeDtypeStruct((DEPTH, N_GROUPS, n), F32),
        compiler_params=_cparams(("arbitrary", "arbitrary"), 40),
        name="ada",
    )(cond, w_ada, b_ada.reshape(DEPTH, 1, n))


def _final_norm_kernel(x_ref, g_ref, o_ref):
    x = x_ref[0]
    y = x * lax.rsqrt(jnp.mean(x * x, axis=-1, keepdims=True) + EPS)
    o_ref[0] = y * g_ref[...]


def _final_norm(xs, g):
    return pl.pallas_call(
        _final_norm_kernel,
        grid=(BATCH, SEQ // ROW_BLK),
        in_specs=[
            pl.BlockSpec((1, ROW_BLK, D_MODEL), lambda b, t: (b, t, 0)),
            pl.BlockSpec((1, D_MODEL), lambda b, t: (0, 0)),
        ],
        out_specs=pl.BlockSpec((1, ROW_BLK, D_MODEL), lambda b, t: (b, t, 0)),
        out_shape=jax.ShapeDtypeStruct((BATCH, SEQ, D_MODEL), F32),
        compiler_params=_cparams(("arbitrary", "arbitrary"), 32),
        name="final_norm",
    )(xs, g.reshape(1, D_MODEL))


def _mm_kernel(*refs, has_bias, gate_idx, tm):
    a_ref, w_ref = refs[0], refs[1]
    pos = 2
    acc = jnp.dot(a_ref[...], w_ref[...], preferred_element_type=F32)
    if has_bias:
        acc = acc + refs[pos][...]
        pos += 1
    if gate_idx is None:
        o_ref = refs[pos]
        o_ref[...] = acc.astype(o_ref.dtype)
    else:
        res_ref, mb_ref, mc_ref, o_ref = refs[pos:pos + 4]
        gate_b = mb_ref[0, gate_idx:gate_idx + 1, :]
        gate_c = mc_ref[0, gate_idx:gate_idx + 1, :]
        rows = lax.broadcasted_iota(jnp.int32, (tm, 1), 0)
        last_tile = pl.program_id(1) % (S_ROWS // tm) == S_ROWS // tm - 1
        is_ctx = jnp.logical_and(last_tile, rows >= tm - CTX_LEN)
        o_ref[...] = res_ref[...] + jnp.where(is_ctx, gate_c, gate_b) * acc


def _mm(a, w, bias=None, out_dtype=F32, tn=1024, res=None, mod=None, gate_idx=None):
    m, k = a.shape
    n = w.shape[1]
    tm = MM_TM
    tn = min(tn, n)
    in_specs = [
        pl.BlockSpec((tm, k), lambda j, i: (i, 0)),
        pl.BlockSpec((k, tn), lambda j, i: (0, j)),
    ]
    args = [a, w]
    if bias is not None:
        in_specs.append(pl.BlockSpec((1, tn), lambda j, i: (0, j)))
        args.append(bias.reshape(1, n))
    if gate_idx is not None:
        per = S_ROWS // tm
        in_specs += [
            pl.BlockSpec((tm, tn), lambda j, i: (i, j)),
            pl.BlockSpec((1, N_ADA, tn), lambda j, i: (i // per, 0, j)),
            pl.BlockSpec((1, N_ADA, tn), lambda j, i: (CTX_GROUP, 0, j)),
        ]
        args += [res, mod, mod]
    return pl.pallas_call(
        functools.partial(_mm_kernel, has_bias=bias is not None, gate_idx=gate_idx, tm=tm),
        grid=(n // tn, m // tm),
        in_specs=in_specs,
        out_specs=pl.BlockSpec((tm, tn), lambda j, i: (i, j)),
        out_shape=jax.ShapeDtypeStruct((m, n), out_dtype),
        compiler_params=_cparams(("arbitrary", "arbitrary"), 52),
        name="mm",
    )(*args)


def _norm_mod_tile(x_ref, g_ref, mb_ref, mc_ref, hn_ref, tile, sh, sc):
    tm = x_ref.shape[0]
    per = S_ROWS // tm
    last_tile = tile % per == per - 1
    step = tm // 4
    for r in range(0, tm, step):
        x = x_ref[r:r + step, :]
        y = x * lax.rsqrt(jnp.mean(x * x, axis=-1, keepdims=True) + EPS) * g_ref[...]
        rows = lax.broadcasted_iota(jnp.int32, (step, 1), 0) + r
        is_ctx = jnp.logical_and(last_tile, rows >= tm - CTX_LEN)
        scale = jnp.where(is_ctx, mc_ref[0, sc:sc + 1, :], mb_ref[0, sc:sc + 1, :])
        shift = jnp.where(is_ctx, mc_ref[0, sh:sh + 1, :], mb_ref[0, sh:sh + 1, :])
        hn_ref[r:r + step, :] = (y * (1.0 + scale) + shift).astype(hn_ref.dtype)


def _norm_mod_specs(index_i):
    per = S_ROWS // MM_TM
    return [
        pl.BlockSpec((MM_TM, D_MODEL), lambda *ids: (index_i(*ids), 0)),
        pl.BlockSpec((1, D_MODEL), lambda *ids: (0, 0)),
        pl.BlockSpec((1, N_ADA, D_MODEL), lambda *ids: (index_i(*ids) // per, 0, 0)),
        pl.BlockSpec((1, N_ADA, D_MODEL), lambda *ids: (CTX_GROUP, 0, 0)),
    ]


def _nmm_kernel(x_ref, g_ref, mb_ref, mc_ref, w_ref, o_ref, hn_ref, *, sh, sc):
    @pl.when(pl.program_id(1) == 0)
    def _():
        _norm_mod_tile(x_ref, g_ref, mb_ref, mc_ref, hn_ref, pl.program_id(0), sh, sc)

    o_ref[...] = jnp.dot(hn_ref[...], w_ref[...], preferred_element_type=F32).astype(o_ref.dtype)


def _nmm(x, g, mod, sh, sc, w, tn):
    m, n = x.shape[0], w.shape[1]
    return pl.pallas_call(
        functools.partial(_nmm_kernel, sh=sh, sc=sc),
        grid=(m // MM_TM, n // tn),
        in_specs=_norm_mod_specs(lambda i, j: i) + [pl.BlockSpec((D_MODEL, tn), lambda i, j: (0, j))],
        out_specs=pl.BlockSpec((MM_TM, tn), lambda i, j: (i, j)),
        out_shape=jax.ShapeDtypeStruct((m, n), BF16),
        scratch_shapes=[pltpu.VMEM((MM_TM, D_MODEL), BF16)],
        compiler_params=_cparams(("arbitrary", "arbitrary"), 52),
        name="norm_mm",
    )(x, g.reshape(1, D_MODEL), mod, mod, w)


def _pack_bf16_pairs(h):
    bits = lax.bitcast_convert_type(h.astype(F32), jnp.uint32)
    half = h.shape[1] // 2
    return (bits[:, :half] >> 16) | bits[:, half:]


def _unpack_bf16_pairs(p):
    lo = lax.bitcast_convert_type(p << 16, F32)
    hi = lax.bitcast_convert_type(p & jnp.uint32(0xFFFF0000), F32)
    return jnp.concatenate([lo, hi], axis=1).astype(BF16)


def _norm_route_kernel(x_ref, g_ref, mb_ref, mc_ref, wr_ref, hp_ref, lg_ref, h_ref):
    _norm_mod_tile(x_ref, g_ref, mb_ref, mc_ref, h_ref, pl.program_id(0), 3, 4)
    h = h_ref[...]
    lg_ref[...] = jnp.dot(h, wr_ref[...], preferred_element_type=F32)
    hp_ref[...] = _pack_bf16_pairs(h)


def _norm_route(x, g, mod, wr):
    return pl.pallas_call(
        _norm_route_kernel,
        grid=(M_ROWS // MM_TM,),
        in_specs=_norm_mod_specs(lambda i: i) + [pl.BlockSpec((D_MODEL, 128), lambda i: (0, 0))],
        out_specs=[pl.BlockSpec((MM_TM, D_MODEL // 2), lambda i: (i, 0)), pl.BlockSpec((MM_TM, 128), lambda i: (i, 0))],
        out_shape=[jax.ShapeDtypeStruct((M_ROWS, D_MODEL // 2), jnp.uint32), jax.ShapeDtypeStruct((M_ROWS, 128), F32)],
        scratch_shapes=[pltpu.VMEM((MM_TM, D_MODEL), BF16)],
        compiler_params=_cparams(("arbitrary",), 52),
        name="norm_route",
    )(x, g.reshape(1, D_MODEL), mod, mod, wr)


SC_GATHER_CHUNK = 96


def _row_gather(table, idx):
    info = plsc.get_sparse_core_info()
    nc, ns = info.num_cores, info.num_subcores
    n_rows = idx.shape[0]
    per_worker = n_rows // (nc * ns)
    n_chunks = per_worker // SC_GATHER_CHUNK
    assert per_worker * nc * ns == n_rows and n_chunks * SC_GATHER_CHUNK == per_worker
    mesh = plsc.VectorSubcoreMesh(core_axis_name="c", subcore_axis_name="s")

    @functools.partial(
        pl.kernel, mesh=mesh,
        out_type=jax.ShapeDtypeStruct((n_rows,) + table.shape[1:], table.dtype),
        scratch_types=[pltpu.VMEM((SC_GATHER_CHUNK,), jnp.int32),
                       pltpu.VMEM((SC_GATHER_CHUNK,) + table.shape[1:], table.dtype),
                       pltpu.SemaphoreType.DMA],
    )
    def gather(table_hbm, idx_hbm, out_hbm, idx_v, rows_v, sem):
        worker = lax.axis_index("s") * nc + lax.axis_index("c")

        @pl.loop(0, n_chunks)
        def _(j):
            base = pl.multiple_of(worker * per_worker + j * SC_GATHER_CHUNK, 8)
            pltpu.sync_copy(idx_hbm.at[pl.ds(base, SC_GATHER_CHUNK)], idx_v)
            pltpu.async_copy(table_hbm.at[idx_v], rows_v, sem).wait()
            pltpu.sync_copy(rows_v, out_hbm.at[pl.ds(base, SC_GATHER_CHUNK)])

    return gather(table, idx)


def _gla_la_kernel(low_ref, w2_ref, b_ref, o_ref):
    low = low_ref[...]
    for z in range(2):
        lz = low[:, z * GLA_GATE_RANK:(z + 1) * GLA_GATE_RANK].astype(BF16)
        logit = jnp.dot(lz, w2_ref[z].astype(BF16), preferred_element_type=F32) + b_ref[z]
        log_sig = jnp.minimum(logit, 0.0) - jnp.log1p(jnp.exp(-jnp.abs(logit)))
        o_ref[z] = log_sig / GLA_GATE_TEMP


def _gla_la(proj, w_a2, b_a):
    tm = MM_TM
    return pl.pallas_call(
        _gla_la_kernel,
        grid=(M_ROWS // tm,),
        in_specs=[
            pl.BlockSpec((tm, 128), lambda i: (i, GLA_IN // 128)),
            pl.BlockSpec((2, GLA_GATE_RANK, GLA_DK), lambda i: (0, 0, 0)),
            pl.BlockSpec((2, 1, GLA_DK), lambda i: (0, 0, 0)),
        ],
        out_specs=pl.BlockSpec((2, tm, GLA_DK), lambda i: (0, i, 0)),
        out_shape=jax.ShapeDtypeStruct((2, M_ROWS, GLA_DK), F32),
        compiler_params=_cparams(("arbitrary",), 48),
        name="gla_la",
    )(proj, w_a2, b_a.reshape(2, 1, GLA_DK))


def _gla_tables():
    c = GLA_CHUNK
    t = np.arange(c)
    pair = np.full((c, c), GLA_LEVELS + 1, np.int32)
    pair[t, t] = GLA_LEVELS
    for lvl in range(GLA_LEVELS):
        s = c >> (lvl + 1)
        second = (t % (2 * s)) >= s
        same = (t[:, None] // (2 * s)) == (t[None, :] // (2 * s))
        pair[same & second[:, None] & ~second[None, :]] = lvl
    lower = (t[None, :] <= t[:, None]).astype(np.float32)
    tri = np.stack([np.concatenate([lower, lower], axis=1), np.concatenate([lower.T, lower.T], axis=1)])
    return tri, np.stack([pair, pair.T])


def _gla_dir(q_ref, k_ref, v_ref, la_ref, tri_ref, pair_ref, st_ref, o_ref, reverse):
    c = GLA_CHUNK
    la = la_ref[0, 0] * math.log2(math.e)
    hi = la.astype(BF16)
    lo = (la - hi.astype(F32)).astype(BF16)
    b = jnp.dot(tri_ref[0], jnp.concatenate([hi, lo], axis=0), preferred_element_type=F32)
    q = q_ref[0].astype(F32) * (GLA_DK_HEAD ** -0.5)
    k = k_ref[0].astype(F32)
    v = v_ref[0]
    d = q.shape[-1]
    last = 0 if reverse else c - 1
    b_last = b[last:last + 1]
    nt = (((1,), (1,)), ((), ()))
    st = st_ref[...]
    o = lax.dot_general((q * jnp.exp2(b)).astype(BF16), st.astype(BF16), nt, preferred_element_type=F32)
    row = lax.broadcasted_iota(jnp.int32, (c, 1), 0)
    ms = []
    for lvl in range(GLA_LEVELS):
        s = c >> (lvl + 1)
        n = c // (2 * s)
        mid = s if reverse else s - 1
        b_mid = jnp.broadcast_to(b.reshape(n, 2 * s, d)[:, mid:mid + 1, :], (n, 2 * s, d)).reshape(c, d)
        first, second = (q, k) if reverse else (k, q)
        if s >= GLA_SUBLANES:
            mix = jnp.concatenate([first.reshape(n, 2 * s, d)[:, :s], second.reshape(n, 2 * s, d)[:, s:]],
                                  axis=1).reshape(c, d)
        else:
            mix = jnp.where(row % (2 * s) >= s, second, first)
        ms.append((mix * jnp.exp2(-jnp.abs(b - b_mid))).astype(BF16))
    h = c // 2
    lo_rows, hi_rows = slice(0, h), slice(h, c)
    q_rows, k_rows = (lo_rows, hi_rows) if reverse else (hi_rows, lo_rows)
    a_off = lax.dot_general(ms[0][q_rows], ms[0][k_rows], nt, preferred_element_type=F32).astype(BF16)
    pair = pair_ref[0, :h, :h]
    qb, kb = q.astype(BF16), k.astype(BF16)
    diag = []
    for rows in (lo_rows, hi_rows):
        acc = jnp.where(pair == GLA_LEVELS, lax.dot_general(qb[rows], kb[rows], nt, preferred_element_type=F32), 0.0)
        for lvl in range(1, GLA_LEVELS):
            m = ms[lvl][rows]
            acc = jnp.where(pair == lvl, lax.dot_general(m, m, nt, preferred_element_type=F32), acc)
        diag.append(acc.astype(BF16))
    o_lo = jnp.dot(diag[0], v[lo_rows], preferred_element_type=F32)
    o_hi = jnp.dot(diag[1], v[hi_rows], preferred_element_type=F32)
    if reverse:
        o_lo = o_lo + jnp.dot(a_off, v[hi_rows], preferred_element_type=F32)
    else:
        o_hi = o_hi + jnp.dot(a_off, v[lo_rows], preferred_element_type=F32)
    o_ref[0] = (o + jnp.concatenate([o_lo, o_hi], axis=0)).astype(o_ref.dtype)
    kt = (k * jnp.exp2(b_last - b)).astype(BF16)
    tn = (((0,), (0,)), ((), ()))
    st_ref[...] = st * jnp.exp2(b_last) + lax.dot_general(v, kt, tn, preferred_element_type=F32)


def _gla_scan_kernel(qf, kf, vf, laf, qb, kb, vb, lab, tri_ref, pair_ref, of_ref, ob_ref, sf_ref, sb_ref):
    @pl.when(pl.program_id(2) == 0)
    def _():
        sf_ref[...] = jnp.zeros_like(sf_ref)
        sb_ref[...] = jnp.zeros_like(sb_ref)

    _gla_dir(qf, kf, vf, laf, tri_ref.at[0:1], pair_ref.at[0:1], sf_ref, of_ref, False)
    _gla_dir(qb, kb, vb, lab, tri_ref.at[1:2], pair_ref.at[1:2], sb_ref, ob_ref, True)


def _fwd_block(t):
    return jnp.where(t == 0, BLKS - 1, t - 1)


def _bwd_block(t):
    return jnp.where(t == 0, BLKS - 1, BLKS - 1 - t)


def _gla_scan(proj, la):
    tri_np, pair_np = _gla_tables()
    tri = jnp.asarray(tri_np, BF16)
    pair = jnp.asarray(pair_np, jnp.int32)
    kq, kv = GLA_DK_HEAD, GLA_DV_HEAD
    k_off = GLA_DK // kq
    v_off = 2 * GLA_DK // kv
    c = GLA_CHUNK
    spec_f = [
        pl.BlockSpec((1, ROW_BLK, kq), lambda b, h, t: (b, _fwd_block(t), h)),
        pl.BlockSpec((1, ROW_BLK, kq), lambda b, h, t: (b, _fwd_block(t), k_off + h)),
        pl.BlockSpec((1, ROW_BLK, kv), lambda b, h, t: (b, _fwd_block(t), v_off + h)),
        pl.BlockSpec((1, 1, ROW_BLK, kq), lambda b, h, t: (0, b, _fwd_block(t), h)),
    ]
    spec_b = [
        pl.BlockSpec((1, ROW_BLK, kq), lambda b, h, t: (b, _bwd_block(t), h)),
        pl.BlockSpec((1, ROW_BLK, kq), lambda b, h, t: (b, _bwd_block(t), k_off + h)),
        pl.BlockSpec((1, ROW_BLK, kv), lambda b, h, t: (b, _bwd_block(t), v_off + h)),
        pl.BlockSpec((1, 1, ROW_BLK, kq), lambda b, h, t: (1, b, _bwd_block(t), h)),
    ]
    const = [
        pl.BlockSpec((2,) + tri_np.shape[1:], lambda b, h, t: (0, 0, 0)),
        pl.BlockSpec((2, c, c), lambda b, h, t: (0, 0, 0)),
    ]
    out_sd = jax.ShapeDtypeStruct((BATCH, S_ROWS, GLA_DV), BF16)
    return pl.pallas_call(
        _gla_scan_kernel,
        grid=(BATCH, GLA_HEADS, BLKS),
        in_specs=spec_f + spec_b + const,
        out_specs=[
            pl.BlockSpec((1, ROW_BLK, kv), lambda b, h, t: (b, _fwd_block(t), h)),
            pl.BlockSpec((1, ROW_BLK, kv), lambda b, h, t: (b, _bwd_block(t), h)),
        ],
        out_shape=[out_sd, out_sd],
        scratch_shapes=[pltpu.VMEM((kv, kq), F32), pltpu.VMEM((kv, kq), F32)],
        compiler_params=_cparams(("arbitrary", "arbitrary", "arbitrary"), 32),
        name="gla_scan",
    )(proj, proj, proj, la, proj, proj, proj, la, tri, pair)


def _gla_post_kernel(of_ref, ob_ref, g_ref, hn_ref, o_ref):
    hn = hn_ref[...]
    for h in range(GLA_HEADS):
        cols = slice(h * GLA_DV_HEAD, (h + 1) * GLA_DV_HEAD)
        o = of_ref[0, :, cols].astype(F32) + ob_ref[0, :, cols].astype(F32)
        o = o * lax.rsqrt(jnp.mean(o * o, axis=-1, keepdims=True) + EPS) * hn
        g = g_ref[0, :, cols].astype(F32)
        o_ref[0, :, cols] = (o.astype(BF16) * (g * jax.nn.sigmoid(g)).astype(BF16))


def _gla_post(o_f, o_b, proj, head_norm):
    g_off = 2 * GLA_DK // GLA_DV + 1
    blk = pl.BlockSpec((1, ROW_BLK, GLA_DV), lambda b, t: (b, t, 0))
    return pl.pallas_call(
        _gla_post_kernel,
        grid=(BATCH, BLKS),
        in_specs=[blk, blk,
                  pl.BlockSpec((1, ROW_BLK, GLA_DV), lambda b, t: (b, t, g_off)),
                  pl.BlockSpec((1, GLA_DV_HEAD), lambda b, t: (0, 0))],
        out_specs=blk,
        out_shape=jax.ShapeDtypeStruct((BATCH, S_ROWS, GLA_DV), BF16),
        compiler_params=_cparams(("arbitrary", "arbitrary"), 32),
        name="gla_post",
    )(o_f, o_b, proj, head_norm.reshape(1, GLA_DV_HEAD))


def _gla_layer(xs, g_norm, mod, w_in, w_a1, w_a2, b_a, head_norm, w_out):
    w_cat = jnp.concatenate([w_in, w_a1[0], w_a1[1], jnp.zeros((D_MODEL, 128 - 2 * GLA_GATE_RANK), F32)], axis=1)
    proj = _nmm(xs.reshape(M_ROWS, D_MODEL), g_norm, mod, 0, 1, w_cat.astype(BF16), tn=896)
    la = _gla_la(proj, w_a2, b_a).reshape(2, BATCH, S_ROWS, GLA_DK)
    proj = proj.reshape(BATCH, S_ROWS, GLA_IN + 128)
    o_f, o_b = _gla_scan(proj, la)
    y = _gla_post(o_f, o_b, proj, head_norm)
    out = _mm(y.reshape(M_ROWS, GLA_DV), w_out.astype(BF16), res=xs.reshape(M_ROWS, D_MODEL), mod=mod, gate_idx=2)
    return out.reshape(BATCH, S_ROWS, D_MODEL)


def _hy_proj_kernel(x_ref, g_ref, mb_ref, mc_ref, w0, w1, w2, bi0, bi1, bi2, cw0, cw1, cw2, cb0, cb1, cb2,
                    x0_ref, u_ref, a_ref, *, tm):
    @pl.when(pl.program_id(1) == 0)
    def _():
        _norm_mod_tile(x_ref, g_ref, mb_ref, mc_ref, a_ref, pl.program_id(0), 0, 1)

    rows = lax.broadcasted_iota(jnp.int32, (tm, 1), 0)
    last_tile = pl.program_id(0) % (S_ROWS // tm) == S_ROWS // tm - 1
    ctx0 = tm - CTX_LEN
    is_ctx = jnp.logical_and(last_tile, rows >= ctx0)
    pos = jnp.where(is_ctx, rows - ctx0, rows % GRID_W)
    last = jnp.where(is_ctx, CTX_LEN - 1, GRID_W - 1)
    has_prev = pos != 0
    has_next = pos != last
    a = a_ref[...]

    def branch(w_ref, bi_ref, cw_ref, cb_ref):
        p = jnp.dot(a, w_ref[...], preferred_element_type=F32) + bi_ref[...]
        prev = jnp.where(has_prev, pltpu.roll(p, 1, 0), 0.0)
        nxt = jnp.where(has_next, pltpu.roll(p, tm - 1, 0), 0.0)
        return prev * cw_ref[0:1, :] + p * cw_ref[1:2, :] + nxt * cw_ref[2:3, :] + cb_ref[...]

    x0_ref[...] = branch(w0, bi0, cw0, cb0).astype(BF16)
    u_ref[...] = (branch(w1, bi1, cw1, cb1) * branch(w2, bi2, cw2, cb2)).astype(BF16)


def _hy_proj(x, g_norm, mod, w_in, b_in, conv_w, conv_b):
    tm, tn = MM_TM, 512
    nj = D_MODEL // tn
    wspec = lambda k: pl.BlockSpec((D_MODEL, tn), lambda i, j: (0, k * nj + j))
    bspec = lambda k: pl.BlockSpec((1, tn), lambda i, j: (0, k * nj + j))
    cspec = lambda k: pl.BlockSpec((3, tn), lambda i, j: (0, k * nj + j))
    ospec = pl.BlockSpec((tm, tn), lambda i, j: (i, j))
    out_sd = jax.ShapeDtypeStruct((M_ROWS, D_MODEL), BF16)
    bi = b_in.reshape(1, 3 * D_MODEL)
    cb = conv_b.reshape(1, 3 * D_MODEL)
    return pl.pallas_call(
        functools.partial(_hy_proj_kernel, tm=tm),
        grid=(M_ROWS // tm, nj),
        in_specs=_norm_mod_specs(lambda i, j: i) + [wspec(0), wspec(1), wspec(2),
                  bspec(0), bspec(1), bspec(2), cspec(0), cspec(1), cspec(2), bspec(0), bspec(1), bspec(2)],
        out_specs=[ospec, ospec],
        out_shape=[out_sd, out_sd],
        scratch_shapes=[pltpu.VMEM((tm, D_MODEL), BF16)],
        compiler_params=_cparams(("arbitrary", "arbitrary"), 56),
        name="hy_proj",
    )(x, g_norm.reshape(1, D_MODEL), mod, mod, w_in, w_in, w_in, bi, bi, bi, conv_w, conv_w, conv_w, cb, cb, cb)


def _hy_filter_kernel(z_ref, w1, b1, w2, b2, w3, b3, w4f, w4b, dl_ref, o_ref):
    def layer(x, w_ref, b_ref):
        y = jnp.dot(x.astype(BF16), w_ref[...].astype(BF16), preferred_element_type=F32) + b_ref[...]
        return jnp.sin(HY_SIN_FREQ * y)

    z = z_ref[...]
    hdn = layer(layer(layer(z, w1, b1), w2, b2), w3, b3).astype(BF16)
    decay = jnp.exp(-z[:, 0:1] * dl_ref[...])
    h_fwd = jnp.dot(hdn, w4f[...].astype(BF16), preferred_element_type=F32) * decay
    h_bwd = jnp.dot(hdn, w4b[...].astype(BF16), preferred_element_type=F32) * decay
    lag0 = lax.broadcasted_iota(jnp.int32, (z.shape[0], 1), 0) == 0
    h_bwd = jnp.where(lag0, 0.0, h_bwd)
    inv = 1.0 / jnp.sum(jnp.abs(h_fwd) + jnp.abs(h_bwd), axis=0, keepdims=True)
    for k, sig in enumerate(((h_fwd + h_bwd) * inv, (h_bwd - h_fwd) * inv)):
        hi = sig.astype(BF16)
        o_ref[2 * k] = hi
        o_ref[2 * k + 1] = (sig - hi.astype(F32)).astype(BF16)


def _hy_filter(L, f_w1, f_b1, f_w2, f_b2, f_w3, f_b3, f_w4):
    tn = 512
    nj = D_MODEL // tn
    t = jnp.linspace(0.0, 1.0, L, dtype=F32)[:, None]
    w = 2 * math.pi * jnp.arange(L, dtype=F32)[:, None] / L
    f = jnp.linspace(1e-4, HY_BANDS - 1, HY_BANDS, dtype=F32)[None, :]
    emb = 1 + 2 * HY_BANDS
    z = jnp.concatenate([t, jnp.cos(f * w), -jnp.sin(f * w), jnp.zeros((L, 128 - emb), F32)], axis=-1)
    w1 = jnp.concatenate([f_w1, jnp.zeros((128 - emb, f_w1.shape[1]), F32)], axis=0)
    max_decay = math.log(HY_DECAY_TARGET) / HY_FAST_DECAY_PCT
    min_decay = math.log(HY_DECAY_TARGET) / HY_SLOW_DECAY_PCT
    deltas = jnp.abs(jnp.linspace(min_decay, max_decay, D_MODEL, dtype=F32))[None, :]
    hid = f_w2.shape[0]
    full = lambda shape: pl.BlockSpec(shape, lambda j: (0, 0))
    return pl.pallas_call(
        _hy_filter_kernel,
        grid=(nj,),
        in_specs=[full((L, 128)), full((128, hid)), full((1, hid)), full((hid, hid)), full((1, hid)),
                  full((hid, hid)), full((1, hid)),
                  pl.BlockSpec((hid, tn), lambda j: (0, j)), pl.BlockSpec((hid, tn), lambda j: (0, nj + j)),
                  pl.BlockSpec((1, tn), lambda j: (0, j))],
        out_specs=pl.BlockSpec((4, L, tn), lambda j: (0, 0, j)),
        out_shape=jax.ShapeDtypeStruct((4, L, D_MODEL), BF16),
        compiler_params=_cparams(("arbitrary",), 48),
        name="hy_filter",
    )(z, w1, f_b1.reshape(1, hid), f_w2, f_b2.reshape(1, hid), f_w3, f_b3.reshape(1, hid), f_w4, f_w4, deltas)


def _dft_tiles(L):
    return (min(2 * L, 1024), min(L, 1024))


def _dft_mats(L):
    n = 2 * L
    tf = _dft_tiles(L)[0] // 2
    f = jnp.arange(L, dtype=jnp.int32)
    ang = ((f[:, None] * f[None, :]) % n).astype(F32) * (2 * math.pi / n)
    cosm = jnp.cos(ang)
    sinm = jnp.sin(ang).at[0].set(jnp.where(f % 2 == 0, 1.0, -1.0))
    fwd = jnp.concatenate([cosm.reshape(L // tf, tf, L), sinm.reshape(L // tf, tf, L)], axis=1).reshape(n, L)
    wgt = jnp.where(f == 0, 1.0 / n, 2.0 / n)[None, :]
    inv = jnp.concatenate([cosm.T * wgt, -sinm.T * wgt], axis=1)
    fwd_hi = fwd.astype(BF16)
    return fwd_hi, (fwd - fwd_hi.astype(F32)).astype(BF16), inv.astype(BF16)


def _dft_fwd_kernel(f_ref, x_ref, ka, kb, kap, o_ref, *, tf):
    acc = jnp.dot(f_ref[...], x_ref[0], preferred_element_type=F32)
    c, s = acc[:tf], acc[tf:]
    o_ref[0, 0] = (c * ka[...] + s * kb[...]).astype(BF16)
    o_ref[0, 1] = (c * kb[...] - s * kap[...]).astype(BF16)


def _dft_fwd(fwd, x, row_blk, L, spec):
    nb = x.shape[0]
    tm = _dft_tiles(L)[0]
    tf, tn = tm // 2, 1024
    kspec = pl.BlockSpec((tf, tn), lambda b, j, i: (i, j))
    return pl.pallas_call(
        functools.partial(_dft_fwd_kernel, tf=tf),
        grid=(nb, D_MODEL // tn, 2 * L // tm),
        in_specs=[pl.BlockSpec((tm, L), lambda b, j, i: (i, 0)),
                  pl.BlockSpec((1, L, tn), lambda b, j, i: (b, row_blk, j)), kspec, kspec, kspec],
        out_specs=pl.BlockSpec((1, 2, tf, tn), lambda b, j, i: (b, 0, i, j)),
        out_shape=jax.ShapeDtypeStruct((nb, 2, L, D_MODEL), BF16),
        compiler_params=_cparams(("arbitrary", "arbitrary", "arbitrary"), 52),
        name="dft_fwd",
    )(fwd, x, *spec)


def _filter_spectrum_kernel(fh_ref, fl_ref, sig_ref, ka_ref, kb_ref, kap_ref, *, tf):
    fh, fl = fh_ref[...], fl_ref[...]

    def dft(rows, hi, lo):
        return (jnp.dot(fh[rows], hi, preferred_element_type=F32) + jnp.dot(fh[rows], lo, preferred_element_type=F32)
                + jnp.dot(fl[rows], hi, preferred_element_type=F32))

    ka = dft(slice(0, tf), sig_ref[0], sig_ref[1])
    kb = dft(slice(tf, 2 * tf), sig_ref[2], sig_ref[3])
    nyq = dft(slice(tf, tf + 16), sig_ref[0], sig_ref[1])[0:1]
    freq = lax.broadcasted_iota(jnp.int32, (tf, 1), 0) + pl.program_id(1) * tf
    ka_ref[...] = ka
    kb_ref[...] = jnp.where(freq == 0, 0.0, kb)
    kap_ref[...] = jnp.where(freq == 0, nyq, ka)


def _filter_spectrum(sig, fwd_hi, fwd_lo, L):
    tm = _dft_tiles(L)[0]
    tf, tn = tm // 2, 256
    fspec = pl.BlockSpec((tm, L), lambda j, i: (i, 0))
    ospec = pl.BlockSpec((tf, tn), lambda j, i: (i, j))
    out_sd = jax.ShapeDtypeStruct((L, D_MODEL), F32)
    return pl.pallas_call(
        functools.partial(_filter_spectrum_kernel, tf=tf),
        grid=(D_MODEL // tn, 2 * L // tm),
        in_specs=[fspec, fspec, pl.BlockSpec((4, L, tn), lambda j, i: (0, 0, j))],
        out_specs=[ospec, ospec, ospec],
        out_shape=[out_sd, out_sd, out_sd],
        compiler_params=_cparams(("arbitrary", "arbitrary"), 52),
        name="filter_spectrum",
    )(fwd_hi, fwd_lo, sig)


def _dft_inv_kernel(g_ref, p_ref, x0_ref, u_ref, skip_ref, *rest):
    o_ref = rest[-1]
    y = jnp.dot(g_ref[...], p_ref[0], preferred_element_type=F32)
    o_ref[0] = (x0_ref[0].astype(F32) * (y + u_ref[0].astype(F32) * skip_ref[...])).astype(BF16)


def _dft_inv(inv, p, x0, u, skip, L, prev=None):
    tm = _dft_tiles(L)[1]
    tn = 512
    off = 0 if L == SEQ else SEQ // tm
    row = lambda b, j, i: (b, off + i, j)
    in_specs = [pl.BlockSpec((tm, 2 * L), lambda b, j, i: (i, 0)),
                pl.BlockSpec((1, 2 * L, tn), lambda b, j, i: (b, 0, j)),
                pl.BlockSpec((1, tm, tn), row), pl.BlockSpec((1, tm, tn), row),
                pl.BlockSpec((1, tn), lambda b, j, i: (0, j))]
    args = [inv, p, x0, u, skip.reshape(1, D_MODEL)]
    aliases = {}
    if prev is not None:
        in_specs.append(pl.BlockSpec(memory_space=pl.ANY))
        args.append(prev)
        aliases = {len(args) - 1: 0}
    return pl.pallas_call(
        _dft_inv_kernel,
        grid=(BATCH, D_MODEL // tn, L // tm),
        in_specs=in_specs,
        out_specs=pl.BlockSpec((1, tm, tn), row),
        out_shape=jax.ShapeDtypeStruct((BATCH, S_ROWS, D_MODEL), BF16),
        input_output_aliases=aliases,
        compiler_params=_cparams(("arbitrary", "arbitrary", "arbitrary"), 52),
        name="dft_inv",
    )(*args)


def _hyena_layer(xs, g_norm, mod, w_in, b_in, conv_w, conv_b, fw, skip, w_out, b_out):
    x0, u = _hy_proj(xs.reshape(M_ROWS, D_MODEL), g_norm, mod, w_in.astype(BF16), b_in, conv_w, conv_b)
    x0 = x0.reshape(BATCH, S_ROWS, D_MODEL)
    u = u.reshape(BATCH, S_ROWS, D_MODEL)
    y = None
    for L in (SEQ, CTX_LEN):
        fwd_hi, fwd_lo, inv = _dft_mats(L)
        spec = _filter_spectrum(_hy_filter(L, *fw), fwd_hi, fwd_lo, L)
        row_blk = 0 if L == SEQ else SEQ // L
        p = _dft_fwd(fwd_hi, u, row_blk, L, spec=spec).reshape(BATCH, 2 * L, D_MODEL)
        y = _dft_inv(inv, p, x0, u, skip, L, prev=y)
    out = _mm(y.reshape(M_ROWS, D_MODEL), w_out.astype(BF16), bias=b_out,
              res=xs.reshape(M_ROWS, D_MODEL), mod=mod, gate_idx=2)
    return out.reshape(BATCH, S_ROWS, D_MODEL)


def _ffn_up_kernel(x_ref, w1_ref, w3_ref, o_ref):
    x = _unpack_bf16_pairs(x_ref[0])
    h1 = jnp.dot(x, w1_ref[0, 0].astype(BF16), preferred_element_type=F32)
    h3 = jnp.dot(x, w3_ref[0, 0].astype(BF16), preferred_element_type=F32)
    o_ref[0] = (h1 * jax.nn.sigmoid(h1) * h3).astype(BF16)


def _ffn_down_kernel(a_ref, w2_ref, s_ref, o_ref):
    y = jnp.dot(a_ref[0], w2_ref[0, 0].astype(BF16), preferred_element_type=F32) * s_ref[0]
    o_ref[0] = y.astype(o_ref.dtype)


def _ffn(xe, w1, w3, w2, layer, score):
    tm, tf, tn = FFN_TM, 512, 1024
    act = pl.pallas_call(
        _ffn_up_kernel,
        grid=(N_EXPERTS, EXPERT_FF // tf, EXP_ROWS // tm),
        in_specs=[
            pl.BlockSpec((1, tm, D_MODEL // 2), lambda e, f, i: (e, i, 0)),
            pl.BlockSpec((1, 1, D_MODEL, tf), lambda e, f, i: (layer, e, 0, f)),
            pl.BlockSpec((1, 1, D_MODEL, tf), lambda e, f, i: (layer, e, 0, f)),
        ],
        out_specs=pl.BlockSpec((1, tm, tf), lambda e, f, i: (e, i, f)),
        out_shape=jax.ShapeDtypeStruct((N_EXPERTS, EXP_ROWS, EXPERT_FF), BF16),
        compiler_params=_cparams(("arbitrary", "arbitrary", "arbitrary"), 52),
        name="moe_ffn_up",
    )(xe, w1, w3)
    return pl.pallas_call(
        _ffn_down_kernel,
        grid=(N_EXPERTS, D_MODEL // tn, EXP_ROWS // tm),
        in_specs=[
            pl.BlockSpec((1, tm, EXPERT_FF), lambda e, j, i: (e, i, 0)),
            pl.BlockSpec((1, 1, EXPERT_FF, tn), lambda e, j, i: (layer, e, 0, j)),
            pl.BlockSpec((1, tm, 1), lambda e, j, i: (e, i, 0)),
        ],
        out_specs=pl.BlockSpec((1, tm, tn), lambda e, j, i: (e, i, j)),
        out_shape=jax.ShapeDtypeStruct((N_EXPERTS, EXP_ROWS, D_MODEL), BF16),
        compiler_params=_cparams(("arbitrary", "arbitrary", "arbitrary"), 52),
        name="moe_ffn_down",
    )(act, w2, score)


def _moe_combine_kernel(il_ref, ic_ref, yl_ref, yc_ref, x_ref, mb_ref, mc_ref, o_ref, ql_ref, qc_ref):
    @pl.when(pl.program_id(1) == 0)
    def _():
        chunk = 512
        tok = lax.broadcasted_iota(jnp.int32, (SEQ, chunk), 0)
        for kk in range(N_EXPERTS * CAP_LAT // chunk):
            cols = slice(kk * chunk, (kk + 1) * chunk)
            ql_ref[:, cols] = jnp.where(il_ref[0, :, cols] == tok, 1.0, 0.0).astype(BF16)
        tok_c = lax.broadcasted_iota(jnp.int32, (CTX_LEN, N_EXPERTS * CAP_CTX), 0)
        qc_ref[...] = jnp.where(ic_ref[0] == tok_c, 1.0, 0.0).astype(BF16)

    tn = o_ref.shape[-1]
    lat = jnp.dot(ql_ref[...], yl_ref[...].reshape(N_EXPERTS * CAP_LAT, tn), preferred_element_type=F32)
    o_ref[0, :SEQ, :] = x_ref[0, :SEQ, :] + mb_ref[0, 5:6, :] * lat
    cx = jnp.dot(qc_ref[...], yc_ref[...].reshape(N_EXPERTS * CAP_CTX, tn), preferred_element_type=F32)
    o_ref[0, SEQ:, :] = x_ref[0, SEQ:, :] + mc_ref[0, 5:6, :] * cx


def _moe_combine(xs, y, idx_l, idx_c, mod):
    tn = 256
    return pl.pallas_call(
        _moe_combine_kernel,
        grid=(BATCH, D_MODEL // tn),
        in_specs=[
            pl.BlockSpec((1, 1, N_EXPERTS * CAP_LAT), lambda b, j: (b, 0, 0)),
            pl.BlockSpec((1, 1, N_EXPERTS * CAP_CTX), lambda b, j: (b, 0, 0)),
            pl.BlockSpec((N_EXPERTS, CAP_LAT, tn), lambda b, j: (0, b, j)),
            pl.BlockSpec((N_EXPERTS, CAP_CTX, tn), lambda b, j: (0, BATCH * CAP_LAT // CAP_CTX + b, j)),
            pl.BlockSpec((1, S_ROWS, tn), lambda b, j: (b, 0, j)),
            pl.BlockSpec((1, N_ADA, tn), lambda b, j: (b, 0, j)),
            pl.BlockSpec((1, N_ADA, tn), lambda b, j: (CTX_GROUP, 0, j)),
        ],
        out_specs=pl.BlockSpec((1, S_ROWS, tn), lambda b, j: (b, 0, j)),
        out_shape=jax.ShapeDtypeStruct((BATCH, S_ROWS, D_MODEL), F32),
        scratch_shapes=[pltpu.VMEM((SEQ, N_EXPERTS * CAP_LAT), BF16),
                        pltpu.VMEM((CTX_LEN, N_EXPERTS * CAP_CTX), BF16)],
        compiler_params=_cparams(("arbitrary", "arbitrary"), 56),
        name="moe_combine",
    )(idx_l, idx_c, y, y, xs, mod, mod)


def _route_stream(s_tok, s_exp, u_ref, cap, idx_ref, score_ref):
    n_tok = s_exp.shape[1]
    bits = lax.bitcast_convert_type(s_exp, jnp.int32)
    thr = jnp.zeros((N_EXPERTS, 1), jnp.int32)
    for bit in range(30, -1, -1):
        cand = thr | (1 << bit)
        cnt = jnp.sum(jnp.where(bits >= cand, 1.0, 0.0), axis=1, keepdims=True)
        thr = jnp.where(cnt >= cap, cand, thr)
    above = bits > thr
    tied = bits == thr
    n_above = jnp.sum(jnp.where(above, 1.0, 0.0), axis=1, keepdims=True)
    before = u_ref[:n_tok, :n_tok]
    tie_rank = jnp.dot(jnp.where(tied, 1.0, 0.0).astype(BF16), before, preferred_element_type=F32)
    chosen = jnp.logical_or(above, jnp.logical_and(tied, tie_rank < cap - n_above))
    pos = jnp.dot(jnp.where(chosen, 1.0, 0.0).astype(BF16), before, preferred_element_type=F32)
    slot = jnp.where(chosen, pos, -1.0)
    slot_tok = jnp.concatenate([slot, jnp.full((128 - N_EXPERTS, n_tok), -1.0, F32)], axis=0).T
    slot_ids = lax.broadcasted_iota(jnp.int32, (1, cap), 1).astype(F32)
    tok_ids = lax.broadcasted_iota(jnp.int32, (n_tok, 1), 0).astype(F32)
    for e in range(N_EXPERTS):
        hit = slot_tok[:, e:e + 1] == slot_ids
        idx_ref[0, e:e + 1, :] = jnp.sum(jnp.where(hit, tok_ids, 0.0), axis=0, keepdims=True).astype(jnp.int32)
        score_ref[0, e:e + 1, :] = jnp.sum(jnp.where(hit, s_tok[:, e:e + 1], 0.0), axis=0, keepdims=True)


def _router_kernel(lg_ref, u_ref, il_ref, sl_ref, ic_ref, sc_ref):
    lg = lg_ref[0]
    valid = lax.broadcasted_iota(jnp.int32, (1, lg.shape[1]), 1) < N_EXPERTS
    x = jnp.where(valid, lg, -jnp.inf)
    e = jnp.where(valid, jnp.exp(x - jnp.max(x, axis=-1, keepdims=True)), 0.0)
    s = e / jnp.sum(e, axis=-1, keepdims=True)
    s_exp = s.T[:N_EXPERTS]
    _route_stream(s[:SEQ], s_exp[:, :SEQ], u_ref, CAP_LAT, il_ref, sl_ref)
    _route_stream(s[SEQ:], s_exp[:, SEQ:], u_ref, CAP_CTX, ic_ref, sc_ref)


def _router(logits, before):
    out = lambda cap: pl.BlockSpec((1, N_EXPERTS, cap), lambda b: (b, 0, 0))
    sd = lambda cap, dt: jax.ShapeDtypeStruct((BATCH, N_EXPERTS, cap), dt)
    return pl.pallas_call(
        _router_kernel,
        grid=(BATCH,),
        in_specs=[pl.BlockSpec((1, S_ROWS, 128), lambda b: (b, 0, 0)),
                  pl.BlockSpec((SEQ, SEQ), lambda b: (0, 0))],
        out_specs=[out(CAP_LAT), out(CAP_LAT), out(CAP_CTX), out(CAP_CTX)],
        out_shape=[sd(CAP_LAT, jnp.int32), sd(CAP_LAT, F32), sd(CAP_CTX, jnp.int32), sd(CAP_CTX, F32)],
        compiler_params=_cparams(("arbitrary",), 52),
        name="router",
    )(logits, before)


def _moe_layer(xs, g_norm, mod, w_router, w1, w3, w2, layer, before):
    wr = jnp.zeros((D_MODEL, 128), F32).at[:, :N_EXPERTS].set(w_router)
    hp, logits = _norm_route(xs.reshape(M_ROWS, D_MODEL), g_norm, mod, wr.astype(BF16))
    idx_l, top_l, idx_c, top_c = _router(logits.reshape(BATCH, S_ROWS, 128), before)
    base = (jnp.arange(BATCH, dtype=jnp.int32) * S_ROWS)[:, None, None]

    def per_expert(t):
        return jnp.swapaxes(t, 0, 1).reshape(N_EXPERTS, -1)

    rows = jnp.concatenate([per_expert(idx_l + base), per_expert(idx_c + base + SEQ)], axis=1)
    score = jnp.concatenate([per_expert(top_l), per_expert(top_c)], axis=1)
    xe = _row_gather(hp.reshape(M_ROWS, D_MODEL // 2 // 128, 128), rows.reshape(-1))
    y = _ffn(xe.reshape(N_EXPERTS, EXP_ROWS, D_MODEL // 2), w1, w3, w2, layer, score[..., None])
    return _moe_combine(xs, y, idx_l.reshape(BATCH, 1, -1), idx_c.reshape(BATCH, 1, -1), mod)


def kernel(x, c, ctx, c_ctx, w_ada, b_ada, norm_mix, norm_ffn, norm_final, gla_w_in, gla_w_a1, gla_w_a2, gla_b_a, gla_head_norm, gla_w_out, hy_w_in, hy_b_in, hy_conv_w, hy_conv_b, hy_f_w1, hy_f_b1, hy_f_w2, hy_f_b2, hy_f_w3, hy_f_b3, hy_f_w4, hy_skip, hy_w_out, hy_b_out, moe_router, moe_w1, moe_w3, moe_w2):
    cond = jnp.zeros((N_GROUPS, D_MODEL), F32).at[:BATCH].set(c).at[CTX_GROUP].set(c_ctx)
    mods = _ada(cond, w_ada, b_ada).reshape(DEPTH, N_GROUPS, N_ADA, D_MODEL)
    xs = jnp.concatenate([x, ctx], axis=1)
    tok = jnp.arange(SEQ, dtype=jnp.int32)
    before = (tok[:, None] < tok[None, :]).astype(BF16)
    for i in range(DEPTH):
        mod = mods[i]
        j = i // 2
        if i % 2 == 0:
            xs = _gla_layer(xs, norm_mix[i], mod, gla_w_in[j], gla_w_a1[j], gla_w_a2[j], gla_b_a[j], gla_head_norm[j],
                            gla_w_out[j])
        else:
            fw = (hy_f_w1[j], hy_f_b1[j], hy_f_w2[j], hy_f_b2[j], hy_f_w3[j], hy_f_b3[j], hy_f_w4[j])
            xs = _hyena_layer(xs, norm_mix[i], mod, hy_w_in[j], hy_b_in[j], hy_conv_w[j], hy_conv_b[j], fw, hy_skip[j],
                              hy_w_out[j], hy_b_out[j])
        xs = _moe_layer(xs, norm_ffn[i], mod, moe_router[i], moe_w1, moe_w3, moe_w2, i, before)
    return _final_norm(xs, norm_final)
```

```python
import functools
import math

import jax
import jax.numpy as jnp
import numpy as np
from jax import lax
from jax.experimental import pallas as pl
from jax.experimental.pallas import tpu as pltpu
from jax.experimental.pallas import tpu_sc as plsc

D_MODEL = 2048
BATCH = 8
SEQ = 2048
DEPTH = 4
CTX_LEN = 256
GRID_W = 64
EPS = 1e-6
N_ADA = 6

GLA_HEADS = 4
GLA_DK = D_MODEL // 2
GLA_DV = D_MODEL
GLA_DK_HEAD = GLA_DK // GLA_HEADS
GLA_DV_HEAD = GLA_DV // GLA_HEADS
GLA_GATE_RANK = 16
GLA_GATE_TEMP = 16.0
GLA_IN = 2 * GLA_DK + 2 * GLA_DV

HY_BANDS = 16
HY_SIN_FREQ = 1.0
HY_FAST_DECAY_PCT = 0.3
HY_SLOW_DECAY_PCT = 1.5
HY_DECAY_TARGET = 1e-2

N_EXPERTS = 16
EXPERT_FF = D_MODEL // 2
CAPACITY_FACTOR = 2

S_ROWS = CTX_LEN + SEQ
M_ROWS = BATCH * S_ROWS
CTX_GROUP = BATCH
N_GROUPS = 16
ROW_BLK = 256
BLKS = S_ROWS // ROW_BLK
MM_TM = S_ROWS // 2
GLA_CHUNK = ROW_BLK
GLA_LEVELS = 8
GLA_SUBLANES = 8
CAP_LAT = CAPACITY_FACTOR * SEQ // N_EXPERTS
CAP_CTX = CAPACITY_FACTOR * CTX_LEN // N_EXPERTS
EXP_ROWS = BATCH * (CAP_LAT + CAP_CTX)
FFN_TM = EXP_ROWS // 2

F32 = jnp.float32
BF16 = jnp.bfloat16
MIB = 1024 * 1024


def _cparams(sem, vmem_mib):
    return pltpu.CompilerParams(dimension_semantics=sem, vmem_limit_bytes=vmem_mib * MIB)


def _ada_kernel(c_ref, w_ref, b_ref, o_ref):
    c = c_ref[...]
    s = (c * jax.nn.sigmoid(c)).astype(BF16)
    o_ref[0] = jnp.dot(s, w_ref[0].astype(BF16), preferred_element_type=F32) + b_ref[0]


def _ada(cond, w_ada, b_ada):
    tn = 1024
    n = N_ADA * D_MODEL
    return pl.pallas_call(
        _ada_kernel,
        grid=(DEPTH, n // tn),
        in_specs=[
            pl.BlockSpec((N_GROUPS, D_MODEL), lambda l, j: (0, 0)),
            pl.BlockSpec((1, D_MODEL, tn), lambda l, j: (l, 0, j)),
            pl.BlockSpec((1, 1, tn), lambda l, j: (l, 0, j)),
        ],
        out_specs=pl.BlockSpec((1, N_GROUPS, tn), lambda l, j: (l, 0, j)),
        out_shape=jax.ShapeDtypeStruct((DEPTH, N_GROUPS, n), F32),
        compiler_params=_cparams(("arbitrary", "arbitrary"), 40),
        name="ada",
    )(cond, w_ada, b_ada.reshape(DEPTH, 1, n))


def _final_norm_kernel(x_ref, g_ref, o_ref):
    x = x_ref[0]
    y = x * lax.rsqrt(jnp.mean(x * x, axis=-1, keepdims=True) + EPS)
    o_ref[0] = y * g_ref[...]


def _final_norm(xs, g):
    return pl.pallas_call(
        _final_norm_kernel,
        grid=(BATCH, SEQ // ROW_BLK),
        in_specs=[
            pl.BlockSpec((1, ROW_BLK, D_MODEL), lambda b, t: (b, t, 0)),
            pl.BlockSpec((1, D_MODEL), lambda b, t: (0, 0)),
        ],
        out_specs=pl.BlockSpec((1, ROW_BLK, D_MODEL), lambda b, t: (b, t, 0)),
        out_shape=jax.ShapeDtypeStruct((BATCH, SEQ, D_MODEL), F32),
        compiler_params=_cparams(("arbitrary", "arbitrary"), 32),
        name="final_norm",
    )(xs, g.reshape(1, D_MODEL))


def _mm_kernel(*refs, has_bias, gate_idx, tm):
    a_ref, w_ref = refs[0], refs[1]
    pos = 2
    acc = jnp.dot(a_ref[...], w_ref[...], preferred_element_type=F32)
    if has_bias:
        acc = acc + refs[pos][...]
        pos += 1
    if gate_idx is None:
        o_ref = refs[pos]
        o_ref[...] = acc.astype(o_ref.dtype)
    else:
        res_ref, mb_ref, mc_ref, o_ref = refs[pos:pos + 4]
        gate_b = mb_ref[0, gate_idx:gate_idx + 1, :]
        gate_c = mc_ref[0, gate_idx:gate_idx + 1, :]
        rows = lax.broadcasted_iota(jnp.int32, (tm, 1), 0)
        last_tile = pl.program_id(1) % (S_ROWS // tm) == S_ROWS // tm - 1
        is_ctx = jnp.logical_and(last_tile, rows >= tm - CTX_LEN)
        o_ref[...] = res_ref[...] + jnp.where(is_ctx, gate_c, gate_b) * acc


def _mm(a, w, bias=None, out_dtype=F32, tn=1024, res=None, mod=None, gate_idx=None):
    m, k = a.shape
    n = w.shape[1]
    tm = MM_TM
    tn = min(tn, n)
    in_specs = [
        pl.BlockSpec((tm, k), lambda j, i: (i, 0)),
        pl.BlockSpec((k, tn), lambda j, i: (0, j)),
    ]
    args = [a, w]
    if bias is not None:
        in_specs.append(pl.BlockSpec((1, tn), lambda j, i: (0, j)))
        args.append(bias.reshape(1, n))
    if gate_idx is not None:
        per = S_ROWS // tm
        in_specs += [
            pl.BlockSpec((tm, tn), lambda j, i: (i, j)),
            pl.BlockSpec((1, N_ADA, tn), lambda j, i: (i // per, 0, j)),
            pl.BlockSpec((1, N_ADA, tn), lambda j, i: (CTX_GROUP, 0, j)),
        ]
        args += [res, mod, mod]
    return pl.pallas_call(
        functools.partial(_mm_kernel, has_bias=bias is not None, gate_idx=gate_idx, tm=tm),
        grid=(n // tn, m // tm),
        in_specs=in_specs,
        out_specs=pl.BlockSpec((tm, tn), lambda j, i: (i, j)),
        out_shape=jax.ShapeDtypeStruct((m, n), out_dtype),
        compiler_params=_cparams(("arbitrary", "arbitrary"), 52),
        name="mm",
    )(*args)


def _norm_mod_tile(x_ref, g_ref, mb_ref, mc_ref, hn_ref, tile, sh, sc):
    tm = x_ref.shape[0]
    per = S_ROWS // tm
    last_tile = tile % per == per - 1
    step = tm // 4
    for r in range(0, tm, step):
        x = x_ref[r:r + step, :]
        y = x * lax.rsqrt(jnp.mean(x * x, axis=-1, keepdims=True) + EPS) * g_ref[...]
        rows = lax.broadcasted_iota(jnp.int32, (step, 1), 0) + r
        is_ctx = jnp.logical_and(last_tile, rows >= tm - CTX_LEN)
        scale = jnp.where(is_ctx, mc_ref[0, sc:sc + 1, :], mb_ref[0, sc:sc + 1, :])
        shift = jnp.where(is_ctx, mc_ref[0, sh:sh + 1, :], mb_ref[0, sh:sh + 1, :])
        hn_ref[r:r + step, :] = (y * (1.0 + scale) + shift).astype(hn_ref.dtype)


def _norm_mod_specs(index_i):
    per = S_ROWS // MM_TM
    return [
        pl.BlockSpec((MM_TM, D_MODEL), lambda *ids: (index_i(*ids), 0)),
        pl.BlockSpec((1, D_MODEL), lambda *ids: (0, 0)),
        pl.BlockSpec((1, N_ADA, D_MODEL), lambda *ids: (index_i(*ids) // per, 0, 0)),
        pl.BlockSpec((1, N_ADA, D_MODEL), lambda *ids: (CTX_GROUP, 0, 0)),
    ]


def _nmm_kernel(x_ref, g_ref, mb_ref, mc_ref, w_ref, o_ref, hn_ref, *, sh, sc):
    @pl.when(pl.program_id(1) == 0)
    def _():
        _norm_mod_tile(x_ref, g_ref, mb_ref, mc_ref, hn_ref, pl.program_id(0), sh, sc)

    o_ref[...] = jnp.dot(hn_ref[...], w_ref[...], preferred_element_type=F32).astype(o_ref.dtype)


def _nmm(x, g, mod, sh, sc, w, tn):
    m, n = x.shape[0], w.shape[1]
    return pl.pallas_call(
        functools.partial(_nmm_kernel, sh=sh, sc=sc),
        grid=(m // MM_TM, n // tn),
        in_specs=_norm_mod_specs(lambda i, j: i) + [pl.BlockSpec((D_MODEL, tn), lambda i, j: (0, j))],
        out_specs=pl.BlockSpec((MM_TM, tn), lambda i, j: (i, j)),
        out_shape=jax.ShapeDtypeStruct((m, n), BF16),
        scratch_shapes=[pltpu.VMEM((MM_TM, D_MODEL), BF16)],
        compiler_params=_cparams(("arbitrary", "arbitrary"), 52),
        name="norm_mm",
    )(x, g.reshape(1, D_MODEL), mod, mod, w)


def _pack_bf16_pairs(h):
    bits = lax.bitcast_convert_type(h.astype(F32), jnp.uint32)
    half = h.shape[1] // 2
    return (bits[:, :half] >> 16) | bits[:, half:]


def _unpack_bf16_pairs(p):
    lo = lax.bitcast_convert_type(p << 16, F32)
    hi = lax.bitcast_convert_type(p & jnp.uint32(0xFFFF0000), F32)
    return jnp.concatenate([lo, hi], axis=1).astype(BF16)


def _norm_route_kernel(x_ref, g_ref, mb_ref, mc_ref, wr_ref, hp_ref, lg_ref, h_ref):
    _norm_mod_tile(x_ref, g_ref, mb_ref, mc_ref, h_ref, pl.program_id(0), 3, 4)
    h = h_ref[...]
    lg_ref[...] = jnp.dot(h, wr_ref[...], preferred_element_type=F32)
    hp_ref[...] = _pack_bf16_pairs(h)


def _norm_route(x, g, mod, wr):
    return pl.pallas_call(
        _norm_route_kernel,
        grid=(M_ROWS // MM_TM,),
        in_specs=_norm_mod_specs(lambda i: i) + [pl.BlockSpec((D_MODEL, 128), lambda i: (0, 0))],
        out_specs=[pl.BlockSpec((MM_TM, D_MODEL // 2), lambda i: (i, 0)), pl.BlockSpec((MM_TM, 128), lambda i: (i, 0))],
        out_shape=[jax.ShapeDtypeStruct((M_ROWS, D_MODEL // 2), jnp.uint32), jax.ShapeDtypeStruct((M_ROWS, 128), F32)],
        scratch_shapes=[pltpu.VMEM((MM_TM, D_MODEL), BF16)],
        compiler_params=_cparams(("arbitrary",), 52),
        name="norm_route",
    )(x, g.reshape(1, D_MODEL), mod, mod, wr)


SC_GATHER_CHUNK = 96


def _row_gather(table, idx):
    info = plsc.get_sparse_core_info()
    nc, ns = info.num_cores, info.num_subcores
    n_rows = idx.shape[0]
    per_worker = n_rows // (nc * ns)
    n_chunks = per_worker // SC_GATHER_CHUNK
    assert per_worker * nc * ns == n_rows and n_chunks * SC_GATHER_CHUNK == per_worker
    mesh = plsc.VectorSubcoreMesh(core_axis_name="c", subcore_axis_name="s")

    @functools.partial(
        pl.kernel, mesh=mesh,
        out_type=jax.ShapeDtypeStruct((n_rows,) + table.shape[1:], table.dtype),
        scratch_types=[pltpu.VMEM((SC_GATHER_CHUNK,), jnp.int32),
                       pltpu.VMEM((SC_GATHER_CHUNK,) + table.shape[1:], table.dtype),
                       pltpu.SemaphoreType.DMA],
    )
    def gather(table_hbm, idx_hbm, out_hbm, idx_v, rows_v, sem):
        worker = lax.axis_index("s") * nc + lax.axis_index("c")

        @pl.loop(0, n_chunks)
        def _(j):
            base = pl.multiple_of(worker * per_worker + j * SC_GATHER_CHUNK, 8)
            pltpu.sync_copy(idx_hbm.at[pl.ds(base, SC_GATHER_CHUNK)], idx_v)
            pltpu.async_copy(table_hbm.at[idx_v], rows_v, sem).wait()
            pltpu.sync_copy(rows_v, out_hbm.at[pl.ds(base, SC_GATHER_CHUNK)])

    return gather(table, idx)


def _gla_la_kernel(low_ref, w2_ref, b_ref, o_ref):
    low = low_ref[...]
    for z in range(2):
        lz = low[:, z * GLA_GATE_RANK:(z + 1) * GLA_GATE_RANK].astype(BF16)
        logit = jnp.dot(lz, w2_ref[z].astype(BF16), preferred_element_type=F32) + b_ref[z]
        log_sig = jnp.minimum(logit, 0.0) - jnp.log1p(jnp.exp(-jnp.abs(logit)))
        o_ref[z] = log_sig / GLA_GATE_TEMP


def _gla_la(proj, w_a2, b_a):
    tm = MM_TM
    return pl.pallas_call(
        _gla_la_kernel,
        grid=(M_ROWS // tm,),
        in_specs=[
            pl.BlockSpec((tm, 128), lambda i: (i, GLA_IN // 128)),
            pl.BlockSpec((2, GLA_GATE_RANK, GLA_DK), lambda i: (0, 0, 0)),
            pl.BlockSpec((2, 1, GLA_DK), lambda i: (0, 0, 0)),
        ],
        out_specs=pl.BlockSpec((2, tm, GLA_DK), lambda i: (0, i, 0)),
        out_shape=jax.ShapeDtypeStruct((2, M_ROWS, GLA_DK), F32),
        compiler_params=_cparams(("arbitrary",), 48),
        name="gla_la",
    )(proj, w_a2, b_a.reshape(2, 1, GLA_DK))


def _gla_tables():
    c = GLA_CHUNK
    t = np.arange(c)
    pair = np.full((c, c), GLA_LEVELS + 1, np.int32)
    pair[t, t] = GLA_LEVELS
    for lvl in range(GLA_LEVELS):
        s = c >> (lvl + 1)
        second = (t % (2 * s)) >= s
        same = (t[:, None] // (2 * s)) == (t[None, :] // (2 * s))
        pair[same & second[:, None] & ~second[None, :]] = lvl
    lower = (t[None, :] <= t[:, None]).astype(np.float32)
    tri = np.stack([np.concatenate([lower, lower], axis=1), np.concatenate([lower.T, lower.T], axis=1)])
    return tri, np.stack([pair, pair.T])


def _gla_dir(q_ref, k_ref, v_ref, la_ref, tri_ref, pair_ref, st_ref, o_ref, reverse):
    c = GLA_CHUNK
    la = la_ref[0, 0] * math.log2(math.e)
    hi = la.astype(BF16)
    lo = (la - hi.astype(F32)).astype(BF16)
    b = jnp.dot(tri_ref[0], jnp.concatenate([hi, lo], axis=0), preferred_element_type=F32)
    q = q_ref[0].astype(F32) * (GLA_DK_HEAD ** -0.5)
    k = k_ref[0].astype(F32)
    v = v_ref[0]
    d = q.shape[-1]
    last = 0 if reverse else c - 1
    b_last = b[last:last + 1]
    nt = (((1,), (1,)), ((), ()))
    st = st_ref[...]
    o = lax.dot_general((q * jnp.exp2(b)).astype(BF16), st.astype(BF16), nt, preferred_element_type=F32)
    row = lax.broadcasted_iota(jnp.int32, (c, 1), 0)
    ms = []
    for lvl in range(GLA_LEVELS):
        s = c >> (lvl + 1)
        n = c // (2 * s)
        mid = s if reverse else s - 1
        b_mid = jnp.broadcast_to(b.reshape(n, 2 * s, d)[:, mid:mid + 1, :], (n, 2 * s, d)).reshape(c, d)
        first, second = (q, k) if reverse else (k, q)
        if s >= GLA_SUBLANES:
            mix = jnp.concatenate([first.reshape(n, 2 * s, d)[:, :s], second.reshape(n, 2 * s, d)[:, s:]],
                                  axis=1).reshape(c, d)
        else:
            mix = jnp.where(row % (2 * s) >= s, second, first)
        ms.append((mix * jnp.exp2(-jnp.abs(b - b_mid))).astype(BF16))
    h = c // 2
    lo_rows, hi_rows = slice(0, h), slice(h, c)
    q_rows, k_rows = (lo_rows, hi_rows) if reverse else (hi_rows, lo_rows)
    a_off = lax.dot_general(ms[0][q_rows], ms[0][k_rows], nt, preferred_element_type=F32).astype(BF16)
    pair = pair_ref[0, :h, :h]
    qb, kb = q.astype(BF16), k.astype(BF16)
    diag = []
    for rows in (lo_rows, hi_rows):
        acc = jnp.where(pair == GLA_LEVELS, lax.dot_general(qb[rows], kb[rows], nt, preferred_element_type=F32), 0.0)
        for lvl in range(1, GLA_LEVELS):
            m = ms[lvl][rows]
            acc = jnp.where(pair == lvl, lax.dot_general(m, m, nt, preferred_element_type=F32), acc)
        diag.append(acc.astype(BF16))
    o_lo = jnp.dot(diag[0], v[lo_rows], preferred_element_type=F32)
    o_hi = jnp.dot(diag[1], v[hi_rows], preferred_element_type=F32)
    if reverse:
        o_lo = o_lo + jnp.dot(a_off, v[hi_rows], preferred_element_type=F32)
    else:
        o_hi = o_hi + jnp.dot(a_off, v[lo_rows], preferred_element_type=F32)
    o_ref[0] = (o + jnp.concatenate([o_lo, o_hi], axis=0)).astype(o_ref.dtype)
    kt = (k * jnp.exp2(b_last - b)).astype(BF16)
    tn = (((0,), (0,)), ((), ()))
    st_ref[...] = st * jnp.exp2(b_last) + lax.dot_general(v, kt, tn, preferred_element_type=F32)


def _gla_scan_kernel(qf, kf, vf, laf, qb, kb, vb, lab, tri_ref, pair_ref, of_ref, ob_ref, sf_ref, sb_ref):
    @pl.when(pl.program_id(2) == 0)
    def _():
        sf_ref[...] = jnp.zeros_like(sf_ref)
        sb_ref[...] = jnp.zeros_like(sb_ref)

    _gla_dir(qf, kf, vf, laf, tri_ref.at[0:1], pair_ref.at[0:1], sf_ref, of_ref, False)
    _gla_dir(qb, kb, vb, lab, tri_ref.at[1:2], pair_ref.at[1:2], sb_ref, ob_ref, True)


def _fwd_block(t):
    return jnp.where(t == 0, BLKS - 1, t - 1)


def _bwd_block(t):
    return jnp.where(t == 0, BLKS - 1, BLKS - 1 - t)


def _gla_scan(proj, la):
    tri_np, pair_np = _gla_tables()
    tri = jnp.asarray(tri_np, BF16)
    pair = jnp.asarray(pair_np, jnp.int32)
    kq, kv = GLA_DK_HEAD, GLA_DV_HEAD
    k_off = GLA_DK // kq
    v_off = 2 * GLA_DK // kv
    c = GLA_CHUNK
    spec_f = [
        pl.BlockSpec((1, ROW_BLK, kq), lambda b, h, t: (b, _fwd_block(t), h)),
        pl.BlockSpec((1, ROW_BLK, kq), lambda b, h, t: (b, _fwd_block(t), k_off + h)),
        pl.BlockSpec((1, ROW_BLK, kv), lambda b, h, t: (b, _fwd_block(t), v_off + h)),
        pl.BlockSpec((1, 1, ROW_BLK, kq), lambda b, h, t: (0, b, _fwd_block(t), h)),
    ]
    spec_b = [
        pl.BlockSpec((1, ROW_BLK, kq), lambda b, h, t: (b, _bwd_block(t), h)),
        pl.BlockSpec((1, ROW_BLK, kq), lambda b, h, t: (b, _bwd_block(t), k_off + h)),
        pl.BlockSpec((1, ROW_BLK, kv), lambda b, h, t: (b, _bwd_block(t), v_off + h)),
        pl.BlockSpec((1, 1, ROW_BLK, kq), lambda b, h, t: (1, b, _bwd_block(t), h)),
    ]
    const = [
        pl.BlockSpec((2,) + tri_np.shape[1:], lambda b, h, t: (0, 0, 0)),
        pl.BlockSpec((2, c, c), lambda b, h, t: (0, 0, 0)),
    ]
    out_sd = jax.ShapeDtypeStruct((BATCH, S_ROWS, GLA_DV), BF16)
    return pl.pallas_call(
        _gla_scan_kernel,
        grid=(BATCH, GLA_HEADS, BLKS),
        in_specs=spec_f + spec_b + const,
        out_specs=[
            pl.BlockSpec((1, ROW_BLK, kv), lambda b, h, t: (b, _fwd_block(t), h)),
            pl.BlockSpec((1, ROW_BLK, kv), lambda b, h, t: (b, _bwd_block(t), h)),
        ],
        out_shape=[out_sd, out_sd],
        scratch_shapes=[pltpu.VMEM((kv, kq), F32), pltpu.VMEM((kv, kq), F32)],
        compiler_params=_cparams(("arbitrary", "arbitrary", "arbitrary"), 32),
        name="gla_scan",
    )(proj, proj, proj, la, proj, proj, proj, la, tri, pair)


def _gla_post_kernel(of_ref, ob_ref, g_ref, hn_ref, o_ref):
    hn = hn_ref[...]
    for h in range(GLA_HEADS):
        cols = slice(h * GLA_DV_HEAD, (h + 1) * GLA_DV_HEAD)
        o = of_ref[0, :, cols].astype(F32) + ob_ref[0, :, cols].astype(F32)
        o = o * lax.rsqrt(jnp.mean(o * o, axis=-1, keepdims=True) + EPS) * hn
        g = g_ref[0, :, cols].astype(F32)
        o_ref[0, :, cols] = (o.astype(BF16) * (g * jax.nn.sigmoid(g)).astype(BF16))


def _gla_post(o_f, o_b, proj, head_norm):
    g_off = 2 * GLA_DK // GLA_DV + 1
    blk = pl.BlockSpec((1, ROW_BLK, GLA_DV), lambda b, t: (b, t, 0))
    return pl.pallas_call(
        _gla_post_kernel,
        grid=(BATCH, BLKS),
        in_specs=[blk, blk,
                  pl.BlockSpec((1, ROW_BLK, GLA_DV), lambda b, t: (b, t, g_off)),
                  pl.BlockSpec((1, GLA_DV_HEAD), lambda b, t: (0, 0))],
        out_specs=blk,
        out_shape=jax.ShapeDtypeStruct((BATCH, S_ROWS, GLA_DV), BF16),
        compiler_params=_cparams(("arbitrary", "arbitrary"), 32),
        name="gla_post",
    )(o_f, o_b, proj, head_norm.reshape(1, GLA_DV_HEAD))


def _gla_layer(xs, g_norm, mod, w_in, w_a1, w_a2, b_a, head_norm, w_out):
    w_cat = jnp.concatenate([w_in, w_a1[0], w_a1[1], jnp.zeros((D_MODEL, 128 - 2 * GLA_GATE_RANK), F32)], axis=1)
    proj = _nmm(xs.reshape(M_ROWS, D_MODEL), g_norm, mod, 0, 1, w_cat.astype(BF16), tn=896)
    la = _gla_la(proj, w_a2, b_a).reshape(2, BATCH, S_ROWS, GLA_DK)
    proj = proj.reshape(BATCH, S_ROWS, GLA_IN + 128)
    o_f, o_b = _gla_scan(proj, la)
    y = _gla_post(o_f, o_b, proj, head_norm)
    out = _mm(y.reshape(M_ROWS, GLA_DV), w_out.astype(BF16), res=xs.reshape(M_ROWS, D_MODEL), mod=mod, gate_idx=2)
    return out.reshape(BATCH, S_ROWS, D_MODEL)


def _hy_proj_kernel(x_ref, g_ref, mb_ref, mc_ref, w0, w1, w2, bi0, bi1, bi2, cw0, cw1, cw2, cb0, cb1, cb2,
                    x0_ref, u_ref, a_ref, *, tm):
    @pl.when(pl.program_id(1) == 0)
    def _():
        _norm_mod_tile(x_ref, g_ref, mb_ref, mc_ref, a_ref, pl.program_id(0), 0, 1)

    rows = lax.broadcasted_iota(jnp.int32, (tm, 1), 0)
    last_tile = pl.program_id(0) % (S_ROWS // tm) == S_ROWS // tm - 1
    ctx0 = tm - CTX_LEN
    is_ctx = jnp.logical_and(last_tile, rows >= ctx0)
    pos = jnp.where(is_ctx, rows - ctx0, rows % GRID_W)
    last = jnp.where(is_ctx, CTX_LEN - 1, GRID_W - 1)
    has_prev = pos != 0
    has_next = pos != last
    a = a_ref[...]

    def branch(w_ref, bi_ref, cw_ref, cb_ref):
        p = jnp.dot(a, w_ref[...], preferred_element_type=F32) + bi_ref[...]
        prev = jnp.where(has_prev, pltpu.roll(p, 1, 0), 0.0)
        nxt = jnp.where(has_next, pltpu.roll(p, tm - 1, 0), 0.0)
        return prev * cw_ref[0:1, :] + p * cw_ref[1:2, :] + nxt * cw_ref[2:3, :] + cb_ref[...]

    x0_ref[...] = branch(w0, bi0, cw0, cb0).astype(BF16)
    u_ref[...] = (branch(w1, bi1, cw1, cb1) * branch(w2, bi2, cw2, cb2)).astype(BF16)


def _hy_proj(x, g_norm, mod, w_in, b_in, conv_w, conv_b):
    tm, tn = MM_TM, 512
    nj = D_MODEL // tn
    wspec = lambda k: pl.BlockSpec((D_MODEL, tn), lambda i, j: (0, k * nj + j))
    bspec = lambda k: pl.BlockSpec((1, tn), lambda i, j: (0, k * nj + j))
    cspec = lambda k: pl.BlockSpec((3, tn), lambda i, j: (0, k * nj + j))
    ospec = pl.BlockSpec((tm, tn), lambda i, j: (i, j))
    out_sd = jax.ShapeDtypeStruct((M_ROWS, D_MODEL), BF16)
    bi = b_in.reshape(1, 3 * D_MODEL)
    cb = conv_b.reshape(1, 3 * D_MODEL)
    return pl.pallas_call(
        functools.partial(_hy_proj_kernel, tm=tm),
        grid=(M_ROWS // tm, nj),
        in_specs=_norm_mod_specs(lambda i, j: i) + [wspec(0), wspec(1), wspec(2),
                  bspec(0), bspec(1), bspec(2), cspec(0), cspec(1), cspec(2), bspec(0), bspec(1), bspec(2)],
        out_specs=[ospec, ospec],
        out_shape=[out_sd, out_sd],
        scratch_shapes=[pltpu.VMEM((tm, D_MODEL), BF16)],
        compiler_params=_cparams(("arbitrary", "arbitrary"), 56),
        name="hy_proj",
    )(x, g_norm.reshape(1, D_MODEL), mod, mod, w_in, w_in, w_in, bi, bi, bi, conv_w, conv_w, conv_w, cb, cb, cb)


def _hy_filter_kernel(z_ref, w1, b1, w2, b2, w3, b3, w4f, w4b, dl_ref, o_ref):
    def layer(x, w_ref, b_ref):
        y = jnp.dot(x.astype(BF16), w_ref[...].astype(BF16), preferred_element_type=F32) + b_ref[...]
        return jnp.sin(HY_SIN_FREQ * y)

    z = z_ref[...]
    hdn = layer(layer(layer(z, w1, b1), w2, b2), w3, b3).astype(BF16)
    decay = jnp.exp(-z[:, 0:1] * dl_ref[...])
    h_fwd = jnp.dot(hdn, w4f[...].astype(BF16), preferred_element_type=F32) * decay
    h_bwd = jnp.dot(hdn, w4b[...].astype(BF16), preferred_element_type=F32) * decay
    lag0 = lax.broadcasted_iota(jnp.int32, (z.shape[0], 1), 0) == 0
    h_bwd = jnp.where(lag0, 0.0, h_bwd)
    inv = 1.0 / jnp.sum(jnp.abs(h_fwd) + jnp.abs(h_bwd), axis=0, keepdims=True)
    for k, sig in enumerate(((h_fwd + h_bwd) * inv, (h_bwd - h_fwd) * inv)):
        hi = sig.astype(BF16)
        o_ref[2 * k] = hi
        o_ref[2 * k + 1] = (sig - hi.astype(F32)).astype(BF16)


def _hy_filter(L, f_w1, f_b1, f_w2, f_b2, f_w3, f_b3, f_w4):
    tn = 512
    nj = D_MODEL // tn
    t = jnp.linspace(0.0, 1.0, L, dtype=F32)[:, None]
    w = 2 * math.pi * jnp.arange(L, dtype=F32)[:, None] / L
    f = jnp.linspace(1e-4, HY_BANDS - 1, HY_BANDS, dtype=F32)[None, :]
    emb = 1 + 2 * HY_BANDS
    z = jnp.concatenate([t, jnp.cos(f * w), -jnp.sin(f * w), jnp.zeros((L, 128 - emb), F32)], axis=-1)
    w1 = jnp.concatenate([f_w1, jnp.zeros((128 - emb, f_w1.shape[1]), F32)], axis=0)
    max_decay = math.log(HY_DECAY_TARGET) / HY_FAST_DECAY_PCT
    min_decay = math.log(HY_DECAY_TARGET) / HY_SLOW_DECAY_PCT
    deltas = jnp.abs(jnp.linspace(min_decay, max_decay, D_MODEL, dtype=F32))[None, :]
    hid = f_w2.shape[0]
    full = lambda shape: pl.BlockSpec(shape, lambda j: (0, 0))
    return pl.pallas_call(
        _hy_filter_kernel,
        grid=(nj,),
        in_specs=[full((L, 128)), full((128, hid)), full((1, hid)), full((hid, hid)), full((1, hid)),
                  full((hid, hid)), full((1, hid)),
                  pl.BlockSpec((hid, tn), lambda j: (0, j)), pl.BlockSpec((hid, tn), lambda j: (0, nj + j)),
                  pl.BlockSpec((1, tn), lambda j: (0, j))],
        out_specs=pl.BlockSpec((4, L, tn), lambda j: (0, 0, j)),
        out_shape=jax.ShapeDtypeStruct((4, L, D_MODEL), BF16),
        compiler_params=_cparams(("arbitrary",), 48),
        name="hy_filter",
    )(z, w1, f_b1.reshape(1, hid), f_w2, f_b2.reshape(1, hid), f_w3, f_b3.reshape(1, hid), f_w4, f_w4, deltas)


def _dft_tiles(L):
    return (min(2 * L, 1024), min(L, 1024))


def _dft_mats(L):
    n = 2 * L
    tf = _dft_tiles(L)[0] // 2
    f = jnp.arange(L, dtype=jnp.int32)
    ang = ((f[:, None] * f[None, :]) % n).astype(F32) * (2 * math.pi / n)
    cosm = jnp.cos(ang)
    sinm = jnp.sin(ang).at[0].set(jnp.where(f % 2 == 0, 1.0, -1.0))
    fwd = jnp.concatenate([cosm.reshape(L // tf, tf, L), sinm.reshape(L // tf, tf, L)], axis=1).reshape(n, L)
    wgt = jnp.where(f == 0, 1.0 / n, 2.0 / n)[None, :]
    inv = jnp.concatenate([cosm.T * wgt, -sinm.T * wgt], axis=1)
    fwd_hi = fwd.astype(BF16)
    return fwd_hi, (fwd - fwd_hi.astype(F32)).astype(BF16), inv.astype(BF16)


def _dft_fwd_kernel(f_ref, x_ref, ka, kb, kap, o_ref, *, tf):
    acc = jnp.dot(f_ref[...], x_ref[0], preferred_element_type=F32)
    c, s = acc[:tf], acc[tf:]
    o_ref[0, 0] = (c * ka[...] + s * kb[...]).astype(BF16)
    o_ref[0, 1] = (c * kb[...] - s * kap[...]).astype(BF16)


def _dft_fwd(fwd, x, row_blk, L, spec):
    nb = x.shape[0]
    tm = _dft_tiles(L)[0]
    tf, tn = tm // 2, 1024
    kspec = pl.BlockSpec((tf, tn), lambda b, j, i: (i, j))
    return pl.pallas_call(
        functools.partial(_dft_fwd_kernel, tf=tf),
        grid=(nb, D_MODEL // tn, 2 * L // tm),
        in_specs=[pl.BlockSpec((tm, L), lambda b, j, i: (i, 0)),
                  pl.BlockSpec((1, L, tn), lambda b, j, i: (b, row_blk, j)), kspec, kspec, kspec],
        out_specs=pl.BlockSpec((1, 2, tf, tn), lambda b, j, i: (b, 0, i, j)),
        out_shape=jax.ShapeDtypeStruct((nb, 2, L, D_MODEL), BF16),
        compiler_params=_cparams(("arbitrary", "arbitrary", "arbitrary"), 52),
        name="dft_fwd",
    )(fwd, x, *spec)


def _filter_spectrum_kernel(fh_ref, fl_ref, sig_ref, ka_ref, kb_ref, kap_ref, *, tf):
    fh, fl = fh_ref[...], fl_ref[...]

    def dft(rows, hi, lo):
        return (jnp.dot(fh[rows], hi, preferred_element_type=F32) + jnp.dot(fh[rows], lo, preferred_element_type=F32)
                + jnp.dot(fl[rows], hi, preferred_element_type=F32))

    ka = dft(slice(0, tf), sig_ref[0], sig_ref[1])
    kb = dft(slice(tf, 2 * tf), sig_ref[2], sig_ref[3])
    nyq = dft(slice(tf, tf + 16), sig_ref[0], sig_ref[1])[0:1]
    freq = lax.broadcasted_iota(jnp.int32, (tf, 1), 0) + pl.program_id(1) * tf
    ka_ref[...] = ka
    kb_ref[...] = jnp.where(freq == 0, 0.0, kb)
    kap_ref[...] = jnp.where(freq == 0, nyq, ka)


def _filter_spectrum(sig, fwd_hi, fwd_lo, L):
    tm = _dft_tiles(L)[0]
    tf, tn = tm // 2, 256
    fspec = pl.BlockSpec((tm, L), lambda j, i: (i, 0))
    ospec = pl.BlockSpec((tf, tn), lambda j, i: (i, j))
    out_sd = jax.ShapeDtypeStruct((L, D_MODEL), F32)
    return pl.pallas_call(
        functools.partial(_filter_spectrum_kernel, tf=tf),
        grid=(D_MODEL // tn, 2 * L // tm),
        in_specs=[fspec, fspec, pl.BlockSpec((4, L, tn), lambda j, i: (0, 0, j))],
        out_specs=[ospec, ospec, ospec],
        out_shape=[out_sd, out_sd, out_sd],
        compiler_params=_cparams(("arbitrary", "arbitrary"), 52),
        name="filter_spectrum",
    )(fwd_hi, fwd_lo, sig)


def _dft_inv_kernel(g_ref, p_ref, x0_ref, u_ref, skip_ref, *rest):
    o_ref = rest[-1]
    y = jnp.dot(g_ref[...], p_ref[0], preferred_element_type=F32)
    o_ref[0] = (x0_ref[0].astype(F32) * (y + u_ref[0].astype(F32) * skip_ref[...])).astype(BF16)


def _dft_inv(inv, p, x0, u, skip, L, prev=None):
    tm = _dft_tiles(L)[1]
    tn = 512
    off = 0 if L == SEQ else SEQ // tm
    row = lambda b, j, i: (b, off + i, j)
    in_specs = [pl.BlockSpec((tm, 2 * L), lambda b, j, i: (i, 0)),
                pl.BlockSpec((1, 2 * L, tn), lambda b, j, i: (b, 0, j)),
                pl.BlockSpec((1, tm, tn), row), pl.BlockSpec((1, tm, tn), row),
                pl.BlockSpec((1, tn), lambda b, j, i: (0, j))]
    args = [inv, p, x0, u, skip.reshape(1, D_MODEL)]
    aliases = {}
    if prev is not None:
        in_specs.append(pl.BlockSpec(memory_space=pl.ANY))
        args.append(prev)
        aliases = {len(args) - 1: 0}
    return pl.pallas_call(
        _dft_inv_kernel,
        grid=(BATCH, D_MODEL // tn, L // tm),
        in_specs=in_specs,
        out_specs=pl.BlockSpec((1, tm, tn), row),
        out_shape=jax.ShapeDtypeStruct((BATCH, S_ROWS, D_MODEL), BF16),
        input_output_aliases=aliases,
        compiler_params=_cparams(("arbitrary", "arbitrary", "arbitrary"), 52),
        name="dft_inv",
    )(*args)


def _hyena_layer(xs, g_norm, mod, w_in, b_in, conv_w, conv_b, fw, skip, w_out, b_out):
    x0, u = _hy_proj(xs.reshape(M_ROWS, D_MODEL), g_norm, mod, w_in.astype(BF16), b_in, conv_w, conv_b)
    x0 = x0.reshape(BATCH, S_ROWS, D_MODEL)
    u = u.reshape(BATCH, S_ROWS, D_MODEL)
    y = None
    for L in (SEQ, CTX_LEN):
        fwd_hi, fwd_lo, inv = _dft_mats(L)
        spec = _filter_spectrum(_hy_filter(L, *fw), fwd_hi, fwd_lo, L)
        row_blk = 0 if L == SEQ else SEQ // L
        p = _dft_fwd(fwd_hi, u, row_blk, L, spec=spec).reshape(BATCH, 2 * L, D_MODEL)
        y = _dft_inv(inv, p, x0, u, skip, L, prev=y)
    out = _mm(y.reshape(M_ROWS, D_MODEL), w_out.astype(BF16), bias=b_out,
              res=xs.reshape(M_ROWS, D_MODEL), mod=mod, gate_idx=2)
    return out.reshape(BATCH, S_ROWS, D_MODEL)


def _ffn_up_kernel(x_ref, w1_ref, w3_ref, o_ref):
    x = _unpack_bf16_pairs(x_ref[0])
    h1 = jnp.dot(x, w1_ref[0, 0].astype(BF16), preferred_element_type=F32)
    h3 = jnp.dot(x, w3_ref[0, 0].astype(BF16), preferred_element_type=F32)
    o_ref[0] = (h1 * jax.nn.sigmoid(h1) * h3).astype(BF16)


def _ffn_down_kernel(a_ref, w2_ref, s_ref, o_ref):
    y = jnp.dot(a_ref[0], w2_ref[0, 0].astype(BF16), preferred_element_type=F32) * s_ref[0]
    o_ref[0] = y.astype(o_ref.dtype)


def _ffn(xe, w1, w3, w2, layer, score):
    tm, tf, tn = FFN_TM, 512, 1024
    act = pl.pallas_call(
        _ffn_up_kernel,
        grid=(N_EXPERTS, EXPERT_FF // tf, EXP_ROWS // tm),
        in_specs=[
            pl.BlockSpec((1, tm, D_MODEL // 2), lambda e, f, i: (e, i, 0)),
            pl.BlockSpec((1, 1, D_MODEL, tf), lambda e, f, i: (layer, e, 0, f)),
            pl.BlockSpec((1, 1, D_MODEL, tf), lambda e, f, i: (layer, e, 0, f)),
        ],
        out_specs=pl.BlockSpec((1, tm, tf), lambda e, f, i: (e, i, f)),
        out_shape=jax.ShapeDtypeStruct((N_EXPERTS, EXP_ROWS, EXPERT_FF), BF16),
        compiler_params=_cparams(("arbitrary", "arbitrary", "arbitrary"), 52),
        name="moe_ffn_up",
    )(xe, w1, w3)
    return pl.pallas_call(
        _ffn_down_kernel,
        grid=(N_EXPERTS, D_MODEL // tn, EXP_ROWS // tm),
        in_specs=[
            pl.BlockSpec((1, tm, EXPERT_FF), lambda e, j, i: (e, i, 0)),
            pl.BlockSpec((1, 1, EXPERT_FF, tn), lambda e, j, i: (layer, e, 0, j)),
            pl.BlockSpec((1, tm, 1), lambda e, j, i: (e, i, 0)),
        ],
        out_specs=pl.BlockSpec((1, tm, tn), lambda e, j, i: (e, i, j)),
        out_shape=jax.ShapeDtypeStruct((N_EXPERTS, EXP_ROWS, D_MODEL), BF16),
        compiler_params=_cparams(("arbitrary", "arbitrary", "arbitrary"), 52),
        name="moe_ffn_down",
    )(act, w2, score)


def _moe_combine_kernel(il_ref, ic_ref, yl_ref, yc_ref, x_ref, mb_ref, mc_ref, o_ref, ql_ref, qc_ref):
    @pl.when(pl.program_id(1) == 0)
    def _():
        chunk = 512
        tok = lax.broadcasted_iota(jnp.int32, (SEQ, chunk), 0)
        for kk in range(N_EXPERTS * CAP_LAT // chunk):
            cols = slice(kk * chunk, (kk + 1) * chunk)
            ql_ref[:, cols] = jnp.where(il_ref[0, :, cols] == tok, 1.0, 0.0).astype(BF16)
        tok_c = lax.broadcasted_iota(jnp.int32, (CTX_LEN, N_EXPERTS * CAP_CTX), 0)
        qc_ref[...] = jnp.where(ic_ref[0] == tok_c, 1.0, 0.0).astype(BF16)

    tn = o_ref.shape[-1]
    lat = jnp.dot(ql_ref[...], yl_ref[...].reshape(N_EXPERTS * CAP_LAT, tn), preferred_element_type=F32)
    o_ref[0, :SEQ, :] = x_ref[0, :SEQ, :] + mb_ref[0, 5:6, :] * lat
    cx = jnp.dot(qc_ref[...], yc_ref[...].reshape(N_EXPERTS * CAP_CTX, tn), preferred_element_type=F32)
    o_ref[0, SEQ:, :] = x_ref[0, SEQ:, :] + mc_ref[0, 5:6, :] * cx


def _moe_combine(xs, y, idx_l, idx_c, mod):
    tn = 256
    return pl.pallas_call(
        _moe_combine_kernel,
        grid=(BATCH, D_MODEL // tn),
        in_specs=[
            pl.BlockSpec((1, 1, N_EXPERTS * CAP_LAT), lambda b, j: (b, 0, 0)),
            pl.BlockSpec((1, 1, N_EXPERTS * CAP_CTX), lambda b, j: (b, 0, 0)),
            pl.BlockSpec((N_EXPERTS, CAP_LAT, tn), lambda b, j: (0, b, j)),
            pl.BlockSpec((N_EXPERTS, CAP_CTX, tn), lambda b, j: (0, BATCH * CAP_LAT // CAP_CTX + b, j)),
            pl.BlockSpec((1, S_ROWS, tn), lambda b, j: (b, 0, j)),
            pl.BlockSpec((1, N_ADA, tn), lambda b, j: (b, 0, j)),
            pl.BlockSpec((1, N_ADA, tn), lambda b, j: (CTX_GROUP, 0, j)),
        ],
        out_specs=pl.BlockSpec((1, S_ROWS, tn), lambda b, j: (b, 0, j)),
        out_shape=jax.ShapeDtypeStruct((BATCH, S_ROWS, D_MODEL), F32),
        scratch_shapes=[pltpu.VMEM((SEQ, N_EXPERTS * CAP_LAT), BF16),
                        pltpu.VMEM((CTX_LEN, N_EXPERTS * CAP_CTX), BF16)],
        compiler_params=_cparams(("arbitrary", "arbitrary"), 56),
        name="moe_combine",
    )(idx_l, idx_c, y, y, xs, mod, mod)


def _route_stream(s_tok, s_exp, u_ref, cap, idx_ref, score_ref):
    n_tok = s_exp.shape[1]
    bits = lax.bitcast_convert_type(s_exp, jnp.int32)
    thr = jnp.zeros((N_EXPERTS, 1), jnp.int32)
    for bit in range(30, -1, -1):
        cand = thr | (1 << bit)
        cnt = jnp.sum(jnp.where(bits >= cand, 1.0, 0.0), axis=1, keepdims=True)
        thr = jnp.where(cnt >= cap, cand, thr)
    above = bits > thr
    tied = bits == thr
    n_above = jnp.sum(jnp.where(above, 1.0, 0.0), axis=1, keepdims=True)
    before = u_ref[:n_tok, :n_tok]
    tie_rank = jnp.dot(jnp.where(tied, 1.0, 0.0).astype(BF16), before, preferred_element_type=F32)
    chosen = jnp.logical_or(above, jnp.logical_and(tied, tie_rank < cap - n_above))
    pos = jnp.dot(jnp.where(chosen, 1.0, 0.0).astype(BF16), before, preferred_element_type=F32)
    slot = jnp.where(chosen, pos, -1.0)
    slot_tok = jnp.concatenate([slot, jnp.full((128 - N_EXPERTS, n_tok), -1.0, F32)], axis=0).T
    slot_ids = lax.broadcasted_iota(jnp.int32, (1, cap), 1).astype(F32)
    tok_ids = lax.broadcasted_iota(jnp.int32, (n_tok, 1), 0).astype(F32)
    for e in range(N_EXPERTS):
        hit = slot_tok[:, e:e + 1] == slot_ids
        idx_ref[0, e:e + 1, :] = jnp.sum(jnp.where(hit, tok_ids, 0.0), axis=0, keepdims=True).astype(jnp.int32)
        score_ref[0, e:e + 1, :] = jnp.sum(jnp.where(hit, s_tok[:, e:e + 1], 0.0), axis=0, keepdims=True)


def _router_kernel(lg_ref, u_ref, il_ref, sl_ref, ic_ref, sc_ref):
    lg = lg_ref[0]
    valid = lax.broadcasted_iota(jnp.int32, (1, lg.shape[1]), 1) < N_EXPERTS
    x = jnp.where(valid, lg, -jnp.inf)
    e = jnp.where(valid, jnp.exp(x - jnp.max(x, axis=-1, keepdims=True)), 0.0)
    s = e / jnp.sum(e, axis=-1, keepdims=True)
    s_exp = s.T[:N_EXPERTS]
    _route_stream(s[:SEQ], s_exp[:, :SEQ], u_ref, CAP_LAT, il_ref, sl_ref)
    _route_stream(s[SEQ:], s_exp[:, SEQ:], u_ref, CAP_CTX, ic_ref, sc_ref)


def _router(logits, before):
    out = lambda cap: pl.BlockSpec((1, N_EXPERTS, cap), lambda b: (b, 0, 0))
    sd = lambda cap, dt: jax.ShapeDtypeStruct((BATCH, N_EXPERTS, cap), dt)
    return pl.pallas_call(
        _router_kernel,
        grid=(BATCH,),
        in_specs=[pl.BlockSpec((1, S_ROWS, 128), lambda b: (b, 0, 0)),
                  pl.BlockSpec((SEQ, SEQ), lambda b: (0, 0))],
        out_specs=[out(CAP_LAT), out(CAP_LAT), out(CAP_CTX), out(CAP_CTX)],
        out_shape=[sd(CAP_LAT, jnp.int32), sd(CAP_LAT, F32), sd(CAP_CTX, jnp.int32), sd(CAP_CTX, F32)],
        compiler_params=_cparams(("arbitrary",), 52),
        name="router",
    )(logits, before)


def _moe_layer(xs, g_norm, mod, w_router, w1, w3, w2, layer, before):
    wr = jnp.zeros((D_MODEL, 128), F32).at[:, :N_EXPERTS].set(w_router)
    hp, logits = _norm_route(xs.reshape(M_ROWS, D_MODEL), g_norm, mod, wr.astype(BF16))
    idx_l, top_l, idx_c, top_c = _router(logits.reshape(BATCH, S_ROWS, 128), before)
    base = (jnp.arange(BATCH, dtype=jnp.int32) * S_ROWS)[:, None, None]

    def per_expert(t):
        return jnp.swapaxes(t, 0, 1).reshape(N_EXPERTS, -1)

    rows = jnp.concatenate([per_expert(idx_l + base), per_expert(idx_c + base + SEQ)], axis=1)
    score = jnp.concatenate([per_expert(top_l), per_expert(top_c)], axis=1)
    xe = _row_gather(hp, rows.reshape(-1))
    y = _ffn(xe.reshape(N_EXPERTS, EXP_ROWS, D_MODEL // 2), w1, w3, w2, layer, score[..., None])
    return _moe_combine(xs, y, idx_l.reshape(BATCH, 1, -1), idx_c.reshape(BATCH, 1, -1), mod)


def kernel(x, c, ctx, c_ctx, w_ada, b_ada, norm_mix, norm_ffn, norm_final, gla_w_in, gla_w_a1, gla_w_a2, gla_b_a, gla_head_norm, gla_w_out, hy_w_in, hy_b_in, hy_conv_w, hy_conv_b, hy_f_w1, hy_f_b1, hy_f_w2, hy_f_b2, hy_f_w3, hy_f_b3, hy_f_w4, hy_skip, hy_w_out, hy_b_out, moe_router, moe_w1, moe_w3, moe_w2):
    cond = jnp.zeros((N_GROUPS, D_MODEL), F32).at[:BATCH].set(c).at[CTX_GROUP].set(c_ctx)
    mods = _ada(cond, w_ada, b_ada).reshape(DEPTH, N_GROUPS, N_ADA, D_MODEL)
    xs = jnp.concatenate([x, ctx], axis=1)
    tok = jnp.arange(SEQ, dtype=jnp.int32)
    before = (tok[:, None] < tok[None, :]).astype(BF16)
    for i in range(DEPTH):
        mod = mods[i]
        j = i // 2
        if i % 2 == 0:
            xs = _gla_layer(xs, norm_mix[i], mod, gla_w_in[j], gla_w_a1[j], gla_w_a2[j], gla_b_a[j], gla_head_norm[j],
                            gla_w_out[j])
        else:
            fw = (hy_f_w1[j], hy_f_b1[j], hy_f_w2[j], hy_f_b2[j], hy_f_w3[j], hy_f_b3[j], hy_f_w4[j])
            xs = _hyena_layer(xs, norm_mix[i], mod, hy_w_in[j], hy_b_in[j], hy_conv_w[j], hy_conv_b[j], fw, hy_skip[j],
                              hy_w_out[j], hy_b_out[j])
        xs = _moe_layer(xs, norm_ffn[i], mod, moe_router[i], moe_w1, moe_w3, moe_w2, i, before)
    return _final_norm(xs, norm_final)
```

```python
import functools
import math

import jax
import jax.numpy as jnp
import numpy as np
from jax import lax
from jax.experimental import pallas as pl
from jax.experimental.pallas import tpu as pltpu
from jax.experimental.pallas import tpu_sc as plsc

D_MODEL = 2048
BATCH = 8
SEQ = 2048
DEPTH = 4
CTX_LEN = 256
GRID_W = 64
EPS = 1e-6
N_ADA = 6

GLA_HEADS = 4
GLA_DK = D_MODEL // 2
GLA_DV = D_MODEL
GLA_DK_HEAD = GLA_DK // GLA_HEADS
GLA_DV_HEAD = GLA_DV // GLA_HEADS
GLA_GATE_RANK = 16
GLA_GATE_TEMP = 16.0
GLA_IN = 2 * GLA_DK + 2 * GLA_DV

HY_BANDS = 16
HY_SIN_FREQ = 1.0
HY_FAST_DECAY_PCT = 0.3
HY_SLOW_DECAY_PCT = 1.5
HY_DECAY_TARGET = 1e-2

N_EXPERTS = 16
EXPERT_FF = D_MODEL // 2
CAPACITY_FACTOR = 2

S_ROWS = CTX_LEN + SEQ
M_ROWS = BATCH * S_ROWS
CTX_GROUP = BATCH
N_GROUPS = 16
ROW_BLK = 256
BLKS = S_ROWS // ROW_BLK
MM_TM = S_ROWS // 2
GLA_CHUNK = ROW_BLK
GLA_LEVELS = 8
GLA_SUBLANES = 8
LANES = 128
DFT_PHASE_SPLIT = 64
CAP_LAT = CAPACITY_FACTOR * SEQ // N_EXPERTS
CAP_CTX = CAPACITY_FACTOR * CTX_LEN // N_EXPERTS
EXP_ROWS = BATCH * (CAP_LAT + CAP_CTX)
FFN_TM = EXP_ROWS // 2

F32 = jnp.float32
BF16 = jnp.bfloat16
MIB = 1024 * 1024


def _cparams(sem, vmem_mib):
    return pltpu.CompilerParams(dimension_semantics=sem, vmem_limit_bytes=vmem_mib * MIB)


def _ada_kernel(c_ref, w_ref, b_ref, o_ref):
    c = c_ref[...]
    s = (c * jax.nn.sigmoid(c)).astype(BF16)
    o_ref[0] = jnp.dot(s, w_ref[0].astype(BF16), preferred_element_type=F32) + b_ref[0]


def _ada(cond, w_ada, b_ada):
    tn = 1024
    n = N_ADA * D_MODEL
    return pl.pallas_call(
        _ada_kernel,
        grid=(DEPTH, n // tn),
        in_specs=[
            pl.BlockSpec((N_GROUPS, D_MODEL), lambda l, j: (0, 0)),
            pl.BlockSpec((1, D_MODEL, tn), lambda l, j: (l, 0, j)),
            pl.BlockSpec((1, 1, tn), lambda l, j: (l, 0, j)),
        ],
        out_specs=pl.BlockSpec((1, N_GROUPS, tn), lambda l, j: (l, 0, j)),
        out_shape=jax.ShapeDtypeStruct((DEPTH, N_GROUPS, n), F32),
        compiler_params=_cparams(("arbitrary", "arbitrary"), 40),
        name="ada",
    )(cond, w_ada, b_ada.reshape(DEPTH, 1, n))


def _final_norm_kernel(x_ref, g_ref, o_ref):
    x = x_ref[0]
    y = x * lax.rsqrt(jnp.mean(x * x, axis=-1, keepdims=True) + EPS)
    o_ref[0] = y * g_ref[...]


def _final_norm(xs, g):
    return pl.pallas_call(
        _final_norm_kernel,
        grid=(BATCH, SEQ // ROW_BLK),
        in_specs=[
            pl.BlockSpec((1, ROW_BLK, D_MODEL), lambda b, t: (b, t, 0)),
            pl.BlockSpec((1, D_MODEL), lambda b, t: (0, 0)),
        ],
        out_specs=pl.BlockSpec((1, ROW_BLK, D_MODEL), lambda b, t: (b, t, 0)),
        out_shape=jax.ShapeDtypeStruct((BATCH, SEQ, D_MODEL), F32),
        compiler_params=_cparams(("arbitrary", "arbitrary"), 32),
        name="final_norm",
    )(xs, g.reshape(1, D_MODEL))


MIXER_GATE = 2
FFN_GATE = 5


def _out_proj_kernel(a_ref, w_ref, bias_ref, res_ref, mb_ref, mc_ref, o_ref):
    tm = a_ref.shape[0]
    acc = jnp.dot(a_ref[...], w_ref[...], preferred_element_type=F32) + bias_ref[...]
    rows = lax.broadcasted_iota(jnp.int32, (tm, 1), 0)
    last_tile = pl.program_id(1) % (S_ROWS // tm) == S_ROWS // tm - 1
    is_ctx = jnp.logical_and(last_tile, rows >= tm - CTX_LEN)
    gate = jnp.where(is_ctx, mc_ref[0, MIXER_GATE:MIXER_GATE + 1, :], mb_ref[0, MIXER_GATE:MIXER_GATE + 1, :])
    o_ref[...] = res_ref[...] + gate * acc


def _out_proj(a, w, bias, res, mod):
    m, k = a.shape
    n = w.shape[1]
    tm, tn = MM_TM, 1024
    per = S_ROWS // tm
    return pl.pallas_call(
        _out_proj_kernel,
        grid=(n // tn, m // tm),
        in_specs=[
            pl.BlockSpec((tm, k), lambda j, i: (i, 0)),
            pl.BlockSpec((k, tn), lambda j, i: (0, j)),
            pl.BlockSpec((1, tn), lambda j, i: (0, j)),
            pl.BlockSpec((tm, tn), lambda j, i: (i, j)),
            pl.BlockSpec((1, N_ADA, tn), lambda j, i: (i // per, 0, j)),
            pl.BlockSpec((1, N_ADA, tn), lambda j, i: (CTX_GROUP, 0, j)),
        ],
        out_specs=pl.BlockSpec((tm, tn), lambda j, i: (i, j)),
        out_shape=jax.ShapeDtypeStruct((m, n), F32),
        compiler_params=_cparams(("arbitrary", "arbitrary"), 52),
        name="out_proj",
    )(a, w, bias.reshape(1, n), res, mod, mod)


def _norm_mod_tile(x_ref, g_ref, mb_ref, mc_ref, hn_ref, tile, sh, sc):
    tm = x_ref.shape[0]
    per = S_ROWS // tm
    last_tile = tile % per == per - 1
    step = tm // 4
    for r in range(0, tm, step):
        x = x_ref[r:r + step, :]
        y = x * lax.rsqrt(jnp.mean(x * x, axis=-1, keepdims=True) + EPS) * g_ref[...]
        rows = lax.broadcasted_iota(jnp.int32, (step, 1), 0) + r
        is_ctx = jnp.logical_and(last_tile, rows >= tm - CTX_LEN)
        scale = jnp.where(is_ctx, mc_ref[0, sc:sc + 1, :], mb_ref[0, sc:sc + 1, :])
        shift = jnp.where(is_ctx, mc_ref[0, sh:sh + 1, :], mb_ref[0, sh:sh + 1, :])
        hn_ref[r:r + step, :] = (y * (1.0 + scale) + shift).astype(hn_ref.dtype)


def _norm_mod_specs(index_i):
    per = S_ROWS // MM_TM
    return [
        pl.BlockSpec((MM_TM, D_MODEL), lambda *ids: (index_i(*ids), 0)),
        pl.BlockSpec((1, D_MODEL), lambda *ids: (0, 0)),
        pl.BlockSpec((1, N_ADA, D_MODEL), lambda *ids: (index_i(*ids) // per, 0, 0)),
        pl.BlockSpec((1, N_ADA, D_MODEL), lambda *ids: (CTX_GROUP, 0, 0)),
    ]


def _nmm_kernel(x_ref, g_ref, mb_ref, mc_ref, w_ref, o_ref, hn_ref, *, sh, sc):
    @pl.when(pl.program_id(1) == 0)
    def _():
        _norm_mod_tile(x_ref, g_ref, mb_ref, mc_ref, hn_ref, pl.program_id(0), sh, sc)

    o_ref[...] = jnp.dot(hn_ref[...], w_ref[...], preferred_element_type=F32).astype(o_ref.dtype)


def _nmm(x, g, mod, sh, sc, w, tn):
    m, n = x.shape[0], w.shape[1]
    return pl.pallas_call(
        functools.partial(_nmm_kernel, sh=sh, sc=sc),
        grid=(m // MM_TM, n // tn),
        in_specs=_norm_mod_specs(lambda i, j: i) + [pl.BlockSpec((D_MODEL, tn), lambda i, j: (0, j))],
        out_specs=pl.BlockSpec((MM_TM, tn), lambda i, j: (i, j)),
        out_shape=jax.ShapeDtypeStruct((m, n), BF16),
        scratch_shapes=[pltpu.VMEM((MM_TM, D_MODEL), BF16)],
        compiler_params=_cparams(("arbitrary", "arbitrary"), 52),
        name="norm_mm",
    )(x, g.reshape(1, D_MODEL), mod, mod, w)


def _pack_bf16_pairs(h):
    bits = lax.bitcast_convert_type(h.astype(F32), jnp.uint32)
    half = h.shape[1] // 2
    return (bits[:, :half] >> 16) | bits[:, half:]


def _unpack_bf16_pairs(p):
    lo = lax.bitcast_convert_type(p << 16, F32)
    hi = lax.bitcast_convert_type(p & jnp.uint32(0xFFFF0000), F32)
    return jnp.concatenate([lo, hi], axis=1).astype(BF16)


def _norm_route_kernel(x_ref, g_ref, mb_ref, mc_ref, wr_ref, hp_ref, lg_ref, h_ref):
    _norm_mod_tile(x_ref, g_ref, mb_ref, mc_ref, h_ref, pl.program_id(0), 3, 4)
    h = h_ref[...]
    lg_ref[...] = jnp.dot(h, wr_ref[...], preferred_element_type=F32)
    hp_ref[...] = _pack_bf16_pairs(h)


def _norm_route(x, g, mod, wr):
    return pl.pallas_call(
        _norm_route_kernel,
        grid=(M_ROWS // MM_TM,),
        in_specs=_norm_mod_specs(lambda i: i) + [pl.BlockSpec((D_MODEL, LANES), lambda i: (0, 0))],
        out_specs=[pl.BlockSpec((MM_TM, D_MODEL // 2), lambda i: (i, 0)), pl.BlockSpec((MM_TM, LANES), lambda i: (i, 0))],
        out_shape=[jax.ShapeDtypeStruct((M_ROWS, D_MODEL // 2), jnp.uint32), jax.ShapeDtypeStruct((M_ROWS, LANES), F32)],
        scratch_shapes=[pltpu.VMEM((MM_TM, D_MODEL), BF16)],
        compiler_params=_cparams(("arbitrary",), 52),
        name="norm_route",
    )(x, g.reshape(1, D_MODEL), mod, mod, wr)


SC_GATHER_CHUNK = 96


def _row_gather(table, idx):
    info = plsc.get_sparse_core_info()
    nc, ns = info.num_cores, info.num_subcores
    n_rows = idx.shape[0]
    per_worker = n_rows // (nc * ns)
    n_chunks = per_worker // SC_GATHER_CHUNK
    assert per_worker * nc * ns == n_rows and n_chunks * SC_GATHER_CHUNK == per_worker
    mesh = plsc.VectorSubcoreMesh(core_axis_name="c", subcore_axis_name="s")

    @functools.partial(
        pl.kernel, mesh=mesh,
        out_type=jax.ShapeDtypeStruct((n_rows,) + table.shape[1:], table.dtype),
        scratch_types=[pltpu.VMEM((SC_GATHER_CHUNK,), jnp.int32),
                       pltpu.VMEM((SC_GATHER_CHUNK,) + table.shape[1:], table.dtype),
                       pltpu.SemaphoreType.DMA],
    )
    def gather(table_hbm, idx_hbm, out_hbm, idx_v, rows_v, sem):
        worker = lax.axis_index("s") * nc + lax.axis_index("c")

        @pl.loop(0, n_chunks)
        def _(j):
            base = pl.multiple_of(worker * per_worker + j * SC_GATHER_CHUNK, 8)
            pltpu.sync_copy(idx_hbm.at[pl.ds(base, SC_GATHER_CHUNK)], idx_v)
            pltpu.async_copy(table_hbm.at[idx_v], rows_v, sem).wait()
            pltpu.sync_copy(rows_v, out_hbm.at[pl.ds(base, SC_GATHER_CHUNK)])

    return gather(table, idx)


def _gla_la_kernel(low_ref, w2_ref, b_ref, o_ref):
    low = low_ref[...]
    for z in range(2):
        lz = low[:, z * GLA_GATE_RANK:(z + 1) * GLA_GATE_RANK].astype(BF16)
        logit = jnp.dot(lz, w2_ref[z].astype(BF16), preferred_element_type=F32) + b_ref[z]
        log_sig = jnp.minimum(logit, 0.0) - jnp.log1p(jnp.exp(-jnp.abs(logit)))
        o_ref[z] = log_sig / GLA_GATE_TEMP


def _gla_la(proj, w_a2, b_a):
    tm = MM_TM
    return pl.pallas_call(
        _gla_la_kernel,
        grid=(M_ROWS // tm,),
        in_specs=[
            pl.BlockSpec((tm, LANES), lambda i: (i, GLA_IN // LANES)),
            pl.BlockSpec((2, GLA_GATE_RANK, GLA_DK), lambda i: (0, 0, 0)),
            pl.BlockSpec((2, 1, GLA_DK), lambda i: (0, 0, 0)),
        ],
        out_specs=pl.BlockSpec((2, tm, GLA_DK), lambda i: (0, i, 0)),
        out_shape=jax.ShapeDtypeStruct((2, M_ROWS, GLA_DK), F32),
        compiler_params=_cparams(("arbitrary",), 48),
        name="gla_la",
    )(proj, w_a2, b_a.reshape(2, 1, GLA_DK))


def _gla_tables():
    c = GLA_CHUNK
    t = np.arange(c)
    pair = np.full((c, c), GLA_LEVELS + 1, np.int32)
    pair[t, t] = GLA_LEVELS
    for lvl in range(GLA_LEVELS):
        s = c >> (lvl + 1)
        second = (t % (2 * s)) >= s
        same = (t[:, None] // (2 * s)) == (t[None, :] // (2 * s))
        pair[same & second[:, None] & ~second[None, :]] = lvl
    lower = (t[None, :] <= t[:, None]).astype(np.float32)
    tri = np.stack([np.concatenate([lower, lower], axis=1), np.concatenate([lower.T, lower.T], axis=1)])
    return tri, np.stack([pair, pair.T])


def _gla_dir(q_ref, k_ref, v_ref, la_ref, tri_ref, pair_ref, st_ref, o_ref, reverse):
    c = GLA_CHUNK
    la = la_ref[0, 0] * math.log2(math.e)
    hi = la.astype(BF16)
    lo = (la - hi.astype(F32)).astype(BF16)
    b = jnp.dot(tri_ref[0], jnp.concatenate([hi, lo], axis=0), preferred_element_type=F32)
    q = q_ref[0].astype(F32) * (GLA_DK_HEAD ** -0.5)
    k = k_ref[0].astype(F32)
    v = v_ref[0]
    d = q.shape[-1]
    last = 0 if reverse else c - 1
    b_last = b[last:last + 1]
    nt = (((1,), (1,)), ((), ()))
    st = st_ref[...]
    o = lax.dot_general((q * jnp.exp2(b)).astype(BF16), st.astype(BF16), nt, preferred_element_type=F32)
    row = lax.broadcasted_iota(jnp.int32, (c, 1), 0)
    ms = []
    for lvl in range(GLA_LEVELS):
        s = c >> (lvl + 1)
        n = c // (2 * s)
        mid = s if reverse else s - 1
        b_mid = jnp.broadcast_to(b.reshape(n, 2 * s, d)[:, mid:mid + 1, :], (n, 2 * s, d)).reshape(c, d)
        first, second = (q, k) if reverse else (k, q)
        if s >= GLA_SUBLANES:
            mix = jnp.concatenate([first.reshape(n, 2 * s, d)[:, :s], second.reshape(n, 2 * s, d)[:, s:]],
                                  axis=1).reshape(c, d)
        else:
            mix = jnp.where(row % (2 * s) >= s, second, first)
        ms.append((mix * jnp.exp2(-jnp.abs(b - b_mid))).astype(BF16))
    h = c // 2
    lo_rows, hi_rows = slice(0, h), slice(h, c)
    q_rows, k_rows = (lo_rows, hi_rows) if reverse else (hi_rows, lo_rows)
    a_off = lax.dot_general(ms[0][q_rows], ms[0][k_rows], nt, preferred_element_type=F32).astype(BF16)
    pair = pair_ref[0, :h, :h]
    qb, kb = q.astype(BF16), k.astype(BF16)
    diag = []
    for rows in (lo_rows, hi_rows):
        acc = jnp.where(pair == GLA_LEVELS, lax.dot_general(qb[rows], kb[rows], nt, preferred_element_type=F32), 0.0)
        for lvl in range(1, GLA_LEVELS):
            m = ms[lvl][rows]
            acc = jnp.where(pair == lvl, lax.dot_general(m, m, nt, preferred_element_type=F32), acc)
        diag.append(acc.astype(BF16))
    o_lo = jnp.dot(diag[0], v[lo_rows], preferred_element_type=F32)
    o_hi = jnp.dot(diag[1], v[hi_rows], preferred_element_type=F32)
    if reverse:
        o_lo = o_lo + jnp.dot(a_off, v[hi_rows], preferred_element_type=F32)
    else:
        o_hi = o_hi + jnp.dot(a_off, v[lo_rows], preferred_element_type=F32)
    o_ref[0] = (o + jnp.concatenate([o_lo, o_hi], axis=0)).astype(o_ref.dtype)
    kt = (k * jnp.exp2(b_last - b)).astype(BF16)
    tn = (((0,), (0,)), ((), ()))
    st_ref[...] = st * jnp.exp2(b_last) + lax.dot_general(v, kt, tn, preferred_element_type=F32)


def _gla_scan_kernel(qf, kf, vf, laf, qb, kb, vb, lab, tri_ref, pair_ref, of_ref, ob_ref, sf_ref, sb_ref):
    @pl.when(pl.program_id(2) == 0)
    def _():
        sf_ref[...] = jnp.zeros_like(sf_ref)
        sb_ref[...] = jnp.zeros_like(sb_ref)

    _gla_dir(qf, kf, vf, laf, tri_ref.at[0:1], pair_ref.at[0:1], sf_ref, of_ref, False)
    _gla_dir(qb, kb, vb, lab, tri_ref.at[1:2], pair_ref.at[1:2], sb_ref, ob_ref, True)


def _fwd_block(t):
    return jnp.where(t == 0, BLKS - 1, t - 1)


def _bwd_block(t):
    return jnp.where(t == 0, BLKS - 1, BLKS - 1 - t)


def _gla_scan(proj, la):
    tri_np, pair_np = _gla_tables()
    tri = jnp.asarray(tri_np, BF16)
    pair = jnp.asarray(pair_np, jnp.int32)
    kq, kv = GLA_DK_HEAD, GLA_DV_HEAD
    k_off = GLA_DK // kq
    v_off = 2 * GLA_DK // kv
    c = GLA_CHUNK
    spec_f = [
        pl.BlockSpec((1, ROW_BLK, kq), lambda b, h, t: (b, _fwd_block(t), h)),
        pl.BlockSpec((1, ROW_BLK, kq), lambda b, h, t: (b, _fwd_block(t), k_off + h)),
        pl.BlockSpec((1, ROW_BLK, kv), lambda b, h, t: (b, _fwd_block(t), v_off + h)),
        pl.BlockSpec((1, 1, ROW_BLK, kq), lambda b, h, t: (0, b, _fwd_block(t), h)),
    ]
    spec_b = [
        pl.BlockSpec((1, ROW_BLK, kq), lambda b, h, t: (b, _bwd_block(t), h)),
        pl.BlockSpec((1, ROW_BLK, kq), lambda b, h, t: (b, _bwd_block(t), k_off + h)),
        pl.BlockSpec((1, ROW_BLK, kv), lambda b, h, t: (b, _bwd_block(t), v_off + h)),
        pl.BlockSpec((1, 1, ROW_BLK, kq), lambda b, h, t: (1, b, _bwd_block(t), h)),
    ]
    const = [
        pl.BlockSpec((2,) + tri_np.shape[1:], lambda b, h, t: (0, 0, 0)),
        pl.BlockSpec((2, c, c), lambda b, h, t: (0, 0, 0)),
    ]
    out_sd = jax.ShapeDtypeStruct((BATCH, S_ROWS, GLA_DV), BF16)
    return pl.pallas_call(
        _gla_scan_kernel,
        grid=(BATCH, GLA_HEADS, BLKS),
        in_specs=spec_f + spec_b + const,
        out_specs=[
            pl.BlockSpec((1, ROW_BLK, kv), lambda b, h, t: (b, _fwd_block(t), h)),
            pl.BlockSpec((1, ROW_BLK, kv), lambda b, h, t: (b, _bwd_block(t), h)),
        ],
        out_shape=[out_sd, out_sd],
        scratch_shapes=[pltpu.VMEM((kv, kq), F32), pltpu.VMEM((kv, kq), F32)],
        compiler_params=_cparams(("arbitrary", "arbitrary", "arbitrary"), 32),
        name="gla_scan",
    )(proj, proj, proj, la, proj, proj, proj, la, tri, pair)


def _gla_post_kernel(of_ref, ob_ref, g_ref, hn_ref, o_ref):
    hn = hn_ref[...]
    for h in range(GLA_HEADS):
        cols = slice(h * GLA_DV_HEAD, (h + 1) * GLA_DV_HEAD)
        o = of_ref[0, :, cols].astype(F32) + ob_ref[0, :, cols].astype(F32)
        o = o * lax.rsqrt(jnp.mean(o * o, axis=-1, keepdims=True) + EPS) * hn
        g = g_ref[0, :, cols].astype(F32)
        o_ref[0, :, cols] = (o.astype(BF16) * (g * jax.nn.sigmoid(g)).astype(BF16))


def _gla_post(o_f, o_b, proj, head_norm):
    g_off = 2 * GLA_DK // GLA_DV + 1
    blk = pl.BlockSpec((1, ROW_BLK, GLA_DV), lambda b, t: (b, t, 0))
    return pl.pallas_call(
        _gla_post_kernel,
        grid=(BATCH, BLKS),
        in_specs=[blk, blk,
                  pl.BlockSpec((1, ROW_BLK, GLA_DV), lambda b, t: (b, t, g_off)),
                  pl.BlockSpec((1, GLA_DV_HEAD), lambda b, t: (0, 0))],
        out_specs=blk,
        out_shape=jax.ShapeDtypeStruct((BATCH, S_ROWS, GLA_DV), BF16),
        compiler_params=_cparams(("arbitrary", "arbitrary"), 32),
        name="gla_post",
    )(o_f, o_b, proj, head_norm.reshape(1, GLA_DV_HEAD))


def _gla_layer(xs, g_norm, mod, w_in, w_a1, w_a2, b_a, head_norm, w_out):
    w_cat = jnp.concatenate([w_in, w_a1[0], w_a1[1], jnp.zeros((D_MODEL, LANES - 2 * GLA_GATE_RANK), F32)], axis=1)
    proj = _nmm(xs.reshape(M_ROWS, D_MODEL), g_norm, mod, 0, 1, w_cat.astype(BF16), tn=896)
    la = _gla_la(proj, w_a2, b_a).reshape(2, BATCH, S_ROWS, GLA_DK)
    proj = proj.reshape(BATCH, S_ROWS, GLA_IN + LANES)
    o_f, o_b = _gla_scan(proj, la)
    y = _gla_post(o_f, o_b, proj, head_norm)
    out = _out_proj(y.reshape(M_ROWS, GLA_DV), w_out.astype(BF16), jnp.zeros((D_MODEL,), F32),
                    xs.reshape(M_ROWS, D_MODEL), mod)
    return out.reshape(BATCH, S_ROWS, D_MODEL)


def _hy_proj_kernel(x_ref, g_ref, mb_ref, mc_ref, w0, w1, w2, bi0, bi1, bi2, cw0, cw1, cw2, cb0, cb1, cb2,
                    x0_ref, u_ref, a_ref, *, tm):
    @pl.when(pl.program_id(1) == 0)
    def _():
        _norm_mod_tile(x_ref, g_ref, mb_ref, mc_ref, a_ref, pl.program_id(0), 0, 1)

    rows = lax.broadcasted_iota(jnp.int32, (tm, 1), 0)
    last_tile = pl.program_id(0) % (S_ROWS // tm) == S_ROWS // tm - 1
    ctx0 = tm - CTX_LEN
    is_ctx = jnp.logical_and(last_tile, rows >= ctx0)
    pos = jnp.where(is_ctx, rows - ctx0, rows % GRID_W)
    last = jnp.where(is_ctx, CTX_LEN - 1, GRID_W - 1)
    has_prev = pos != 0
    has_next = pos != last
    a = a_ref[...]

    def branch(w_ref, bi_ref, cw_ref, cb_ref):
        p = jnp.dot(a, w_ref[...], preferred_element_type=F32) + bi_ref[...]
        prev = jnp.where(has_prev, pltpu.roll(p, 1, 0), 0.0)
        nxt = jnp.where(has_next, pltpu.roll(p, tm - 1, 0), 0.0)
        return prev * cw_ref[0:1, :] + p * cw_ref[1:2, :] + nxt * cw_ref[2:3, :] + cb_ref[...]

    x0_ref[...] = branch(w0, bi0, cw0, cb0).astype(BF16)
    u_ref[...] = (branch(w1, bi1, cw1, cb1) * branch(w2, bi2, cw2, cb2)).astype(BF16)


def _hy_proj(x, g_norm, mod, w_in, b_in, conv_w, conv_b):
    tm, tn = MM_TM, 512
    nj = D_MODEL // tn
    wspec = lambda k: pl.BlockSpec((D_MODEL, tn), lambda i, j: (0, k * nj + j))
    bspec = lambda k: pl.BlockSpec((1, tn), lambda i, j: (0, k * nj + j))
    cspec = lambda k: pl.BlockSpec((3, tn), lambda i, j: (0, k * nj + j))
    ospec = pl.BlockSpec((tm, tn), lambda i, j: (i, j))
    out_sd = jax.ShapeDtypeStruct((M_ROWS, D_MODEL), BF16)
    bi = b_in.reshape(1, 3 * D_MODEL)
    cb = conv_b.reshape(1, 3 * D_MODEL)
    return pl.pallas_call(
        functools.partial(_hy_proj_kernel, tm=tm),
        grid=(M_ROWS // tm, nj),
        in_specs=_norm_mod_specs(lambda i, j: i) + [wspec(0), wspec(1), wspec(2),
                  bspec(0), bspec(1), bspec(2), cspec(0), cspec(1), cspec(2), bspec(0), bspec(1), bspec(2)],
        out_specs=[ospec, ospec],
        out_shape=[out_sd, out_sd],
        scratch_shapes=[pltpu.VMEM((tm, D_MODEL), BF16)],
        compiler_params=_cparams(("arbitrary", "arbitrary"), 56),
        name="hy_proj",
    )(x, g_norm.reshape(1, D_MODEL), mod, mod, w_in, w_in, w_in, bi, bi, bi, conv_w, conv_w, conv_w, cb, cb, cb)


def _hy_filter_kernel(z_ref, w1, b1, w2, b2, w3, b3, w4f, w4b, dl_ref, o_ref):
    def layer(x, w_ref, b_ref):
        y = jnp.dot(x.astype(BF16), w_ref[...].astype(BF16), preferred_element_type=F32) + b_ref[...]
        return jnp.sin(HY_SIN_FREQ * y)

    z = z_ref[...]
    hdn = layer(layer(layer(z, w1, b1), w2, b2), w3, b3).astype(BF16)
    decay = jnp.exp(-z[:, 0:1] * dl_ref[...])
    h_fwd = jnp.dot(hdn, w4f[...].astype(BF16), preferred_element_type=F32) * decay
    h_bwd = jnp.dot(hdn, w4b[...].astype(BF16), preferred_element_type=F32) * decay
    lag0 = lax.broadcasted_iota(jnp.int32, (z.shape[0], 1), 0) == 0
    h_bwd = jnp.where(lag0, 0.0, h_bwd)
    inv = 1.0 / jnp.sum(jnp.abs(h_fwd) + jnp.abs(h_bwd), axis=0, keepdims=True)
    for k, sig in enumerate(((h_fwd + h_bwd) * inv, (h_bwd - h_fwd) * inv)):
        hi = sig.astype(BF16)
        o_ref[2 * k] = hi
        o_ref[2 * k + 1] = (sig - hi.astype(F32)).astype(BF16)


def _hy_filter(L, f_w1, f_b1, f_w2, f_b2, f_w3, f_b3, f_w4):
    tn = 512
    nj = D_MODEL // tn
    t = jnp.linspace(0.0, 1.0, L, dtype=F32)[:, None]
    w = 2 * math.pi * jnp.arange(L, dtype=F32)[:, None] / L
    f = jnp.linspace(1e-4, HY_BANDS - 1, HY_BANDS, dtype=F32)[None, :]
    emb = 1 + 2 * HY_BANDS
    z = jnp.concatenate([t, jnp.cos(f * w), -jnp.sin(f * w), jnp.zeros((L, LANES - emb), F32)], axis=-1)
    w1 = jnp.concatenate([f_w1, jnp.zeros((LANES - emb, f_w1.shape[1]), F32)], axis=0)
    max_decay = math.log(HY_DECAY_TARGET) / HY_FAST_DECAY_PCT
    min_decay = math.log(HY_DECAY_TARGET) / HY_SLOW_DECAY_PCT
    deltas = jnp.abs(jnp.linspace(min_decay, max_decay, D_MODEL, dtype=F32))[None, :]
    hid = f_w2.shape[0]
    full = lambda shape: pl.BlockSpec(shape, lambda j: (0, 0))
    return pl.pallas_call(
        _hy_filter_kernel,
        grid=(nj,),
        in_specs=[full((L, LANES)), full((LANES, hid)), full((1, hid)), full((hid, hid)), full((1, hid)),
                  full((hid, hid)), full((1, hid)),
                  pl.BlockSpec((hid, tn), lambda j: (0, j)), pl.BlockSpec((hid, tn), lambda j: (0, nj + j)),
                  pl.BlockSpec((1, tn), lambda j: (0, j))],
        out_specs=pl.BlockSpec((4, L, tn), lambda j: (0, 0, j)),
        out_shape=jax.ShapeDtypeStruct((4, L, D_MODEL), BF16),
        compiler_params=_cparams(("arbitrary",), 48),
        name="hy_filter",
    )(z, w1, f_b1.reshape(1, hid), f_w2, f_b2.reshape(1, hid), f_w3, f_b3.reshape(1, hid), f_w4, f_w4, deltas)


def _dft_tiles(L):
    return (min(2 * L, 1024), min(L, 1024))


def _dft_mats(L):
    n = 2 * L
    tf = _dft_tiles(L)[0] // 2
    f = jnp.arange(L, dtype=jnp.int32)
    w = DFT_PHASE_SPLIT
    phase = lambda k: (k % n).astype(F32) * (2 * math.pi / n)
    coarse = phase(f[:, None] * (jnp.arange(L // w, dtype=jnp.int32) * w)[None, :])[:, :, None]
    fine = phase(f[:, None] * jnp.arange(w, dtype=jnp.int32)[None, :])[:, None, :]
    cosm = (jnp.cos(coarse) * jnp.cos(fine) - jnp.sin(coarse) * jnp.sin(fine)).reshape(L, L)
    sinm = (jnp.sin(coarse) * jnp.cos(fine) + jnp.cos(coarse) * jnp.sin(fine)).reshape(L, L)
    sinm = sinm.at[0].set(jnp.where(f % 2 == 0, 1.0, -1.0))
    fwd = jnp.concatenate([cosm.reshape(L // tf, tf, L), sinm.reshape(L // tf, tf, L)], axis=1).reshape(n, L)
    wgt = jnp.where(f == 0, 1.0 / n, 2.0 / n)[None, :]
    inv = jnp.concatenate([cosm.T * wgt, -sinm.T * wgt], axis=1)
    fwd_hi = fwd.astype(BF16)
    return fwd_hi, (fwd - fwd_hi.astype(F32)).astype(BF16), inv.astype(BF16)


def _dft_fwd_kernel(f_ref, x_ref, ka, kb, kap, o_ref, *, tf):
    acc = jnp.dot(f_ref[...], x_ref[0], preferred_element_type=F32)
    c, s = acc[:tf], acc[tf:]
    o_ref[0, 0] = (c * ka[...] + s * kb[...]).astype(BF16)
    o_ref[0, 1] = (c * kb[...] - s * kap[...]).astype(BF16)


def _dft_fwd(fwd, x, row_blk, L, spec):
    nb = x.shape[0]
    tm = _dft_tiles(L)[0]
    tf, tn = tm // 2, 1024
    kspec = pl.BlockSpec((tf, tn), lambda b, j, i: (i, j))
    return pl.pallas_call(
        functools.partial(_dft_fwd_kernel, tf=tf),
        grid=(nb, D_MODEL // tn, 2 * L // tm),
        in_specs=[pl.BlockSpec((tm, L), lambda b, j, i: (i, 0)),
                  pl.BlockSpec((1, L, tn), lambda b, j, i: (b, row_blk, j)), kspec, kspec, kspec],
        out_specs=pl.BlockSpec((1, 2, tf, tn), lambda b, j, i: (b, 0, i, j)),
        out_shape=jax.ShapeDtypeStruct((nb, 2, L, D_MODEL), BF16),
        compiler_params=_cparams(("arbitrary", "arbitrary", "arbitrary"), 52),
        name="dft_fwd",
    )(fwd, x, *spec)


def _filter_spectrum_kernel(fh_ref, fl_ref, sig_ref, ka_ref, kb_ref, kap_ref, *, tf):
    fh, fl = fh_ref[...], fl_ref[...]

    def dft(rows, hi, lo):
        return (jnp.dot(fh[rows], hi, preferred_element_type=F32) + jnp.dot(fh[rows], lo, preferred_element_type=F32)
                + jnp.dot(fl[rows], hi, preferred_element_type=F32))

    ka = dft(slice(0, tf), sig_ref[0], sig_ref[1])
    kb = dft(slice(tf, 2 * tf), sig_ref[2], sig_ref[3])
    nyq = dft(slice(tf, tf + 16), sig_ref[0], sig_ref[1])[0:1]
    freq = lax.broadcasted_iota(jnp.int32, (tf, 1), 0) + pl.program_id(1) * tf
    ka_ref[...] = ka
    kb_ref[...] = jnp.where(freq == 0, 0.0, kb)
    kap_ref[...] = jnp.where(freq == 0, nyq, ka)


def _filter_spectrum(sig, fwd_hi, fwd_lo, L):
    tm = _dft_tiles(L)[0]
    tf, tn = tm // 2, 256
    fspec = pl.BlockSpec((tm, L), lambda j, i: (i, 0))
    ospec = pl.BlockSpec((tf, tn), lambda j, i: (i, j))
    out_sd = jax.ShapeDtypeStruct((L, D_MODEL), F32)
    return pl.pallas_call(
        functools.partial(_filter_spectrum_kernel, tf=tf),
        grid=(D_MODEL // tn, 2 * L // tm),
        in_specs=[fspec, fspec, pl.BlockSpec((4, L, tn), lambda j, i: (0, 0, j))],
        out_specs=[ospec, ospec, ospec],
        out_shape=[out_sd, out_sd, out_sd],
        compiler_params=_cparams(("arbitrary", "arbitrary"), 52),
        name="filter_spectrum",
    )(fwd_hi, fwd_lo, sig)


def _dft_inv_kernel(g_ref, p_ref, x0_ref, u_ref, skip_ref, *rest):
    o_ref = rest[-1]
    y = jnp.dot(g_ref[...], p_ref[0], preferred_element_type=F32)
    o_ref[0] = (x0_ref[0].astype(F32) * (y + u_ref[0].astype(F32) * skip_ref[...])).astype(BF16)


def _dft_inv(inv, p, x0, u, skip, L, prev=None):
    tm = _dft_tiles(L)[1]
    tn = 512
    off = 0 if L == SEQ else SEQ // tm
    row = lambda b, j, i: (b, off + i, j)
    in_specs = [pl.BlockSpec((tm, 2 * L), lambda b, j, i: (i, 0)),
                pl.BlockSpec((1, 2 * L, tn), lambda b, j, i: (b, 0, j)),
                pl.BlockSpec((1, tm, tn), row), pl.BlockSpec((1, tm, tn), row),
                pl.BlockSpec((1, tn), lambda b, j, i: (0, j))]
    args = [inv, p, x0, u, skip.reshape(1, D_MODEL)]
    aliases = {}
    if prev is not None:
        in_specs.append(pl.BlockSpec(memory_space=pl.ANY))
        args.append(prev)
        aliases = {len(args) - 1: 0}
    return pl.pallas_call(
        _dft_inv_kernel,
        grid=(BATCH, D_MODEL // tn, L // tm),
        in_specs=in_specs,
        out_specs=pl.BlockSpec((1, tm, tn), row),
        out_shape=jax.ShapeDtypeStruct((BATCH, S_ROWS, D_MODEL), BF16),
        input_output_aliases=aliases,
        compiler_params=_cparams(("arbitrary", "arbitrary", "arbitrary"), 52),
        name="dft_inv",
    )(*args)


def _hyena_layer(xs, g_norm, mod, w_in, b_in, conv_w, conv_b, fw, skip, w_out, b_out):
    x0, u = _hy_proj(xs.reshape(M_ROWS, D_MODEL), g_norm, mod, w_in.astype(BF16), b_in, conv_w, conv_b)
    x0 = x0.reshape(BATCH, S_ROWS, D_MODEL)
    u = u.reshape(BATCH, S_ROWS, D_MODEL)
    y = None
    for L in (SEQ, CTX_LEN):
        fwd_hi, fwd_lo, inv = _dft_mats(L)
        spec = _filter_spectrum(_hy_filter(L, *fw), fwd_hi, fwd_lo, L)
        row_blk = 0 if L == SEQ else SEQ // L
        p = _dft_fwd(fwd_hi, u, row_blk, L, spec=spec).reshape(BATCH, 2 * L, D_MODEL)
        y = _dft_inv(inv, p, x0, u, skip, L, prev=y)
    out = _out_proj(y.reshape(M_ROWS, D_MODEL), w_out.astype(BF16), b_out, xs.reshape(M_ROWS, D_MODEL), mod)
    return out.reshape(BATCH, S_ROWS, D_MODEL)


def _ffn_up_kernel(x_ref, w1_ref, w3_ref, o_ref):
    x = _unpack_bf16_pairs(x_ref[0])
    h1 = jnp.dot(x, w1_ref[0, 0].astype(BF16), preferred_element_type=F32)
    h3 = jnp.dot(x, w3_ref[0, 0].astype(BF16), preferred_element_type=F32)
    o_ref[0] = (h1 * jax.nn.sigmoid(h1) * h3).astype(BF16)


def _ffn_down_kernel(a_ref, w2_ref, s_ref, o_ref):
    y = jnp.dot(a_ref[0], w2_ref[0, 0].astype(BF16), preferred_element_type=F32) * s_ref[0]
    o_ref[0] = y.astype(o_ref.dtype)


def _ffn(xe, w1, w3, w2, layer, score):
    tm, tf, tn = FFN_TM, 512, 1024
    act = pl.pallas_call(
        _ffn_up_kernel,
        grid=(N_EXPERTS, EXPERT_FF // tf, EXP_ROWS // tm),
        in_specs=[
            pl.BlockSpec((1, tm, D_MODEL // 2), lambda e, f, i: (e, i, 0)),
            pl.BlockSpec((1, 1, D_MODEL, tf), lambda e, f, i: (layer, e, 0, f)),
            pl.BlockSpec((1, 1, D_MODEL, tf), lambda e, f, i: (layer, e, 0, f)),
        ],
        out_specs=pl.BlockSpec((1, tm, tf), lambda e, f, i: (e, i, f)),
        out_shape=jax.ShapeDtypeStruct((N_EXPERTS, EXP_ROWS, EXPERT_FF), BF16),
        compiler_params=_cparams(("arbitrary", "arbitrary", "arbitrary"), 52),
        name="moe_ffn_up",
    )(xe, w1, w3)
    return pl.pallas_call(
        _ffn_down_kernel,
        grid=(N_EXPERTS, D_MODEL // tn, EXP_ROWS // tm),
        in_specs=[
            pl.BlockSpec((1, tm, EXPERT_FF), lambda e, j, i: (e, i, 0)),
            pl.BlockSpec((1, 1, EXPERT_FF, tn), lambda e, j, i: (layer, e, 0, j)),
            pl.BlockSpec((1, tm, 1), lambda e, j, i: (e, i, 0)),
        ],
        out_specs=pl.BlockSpec((1, tm, tn), lambda e, j, i: (e, i, j)),
        out_shape=jax.ShapeDtypeStruct((N_EXPERTS, EXP_ROWS, D_MODEL), BF16),
        compiler_params=_cparams(("arbitrary", "arbitrary", "arbitrary"), 52),
        name="moe_ffn_down",
    )(act, w2, score)


def _moe_combine_kernel(il_ref, ic_ref, yl_ref, yc_ref, x_ref, mb_ref, mc_ref, o_ref, ql_ref, qc_ref):
    @pl.when(pl.program_id(1) == 0)
    def _():
        chunk = 512
        tok = lax.broadcasted_iota(jnp.int32, (SEQ, chunk), 0)
        for kk in range(N_EXPERTS * CAP_LAT // chunk):
            cols = slice(kk * chunk, (kk + 1) * chunk)
            ql_ref[:, cols] = jnp.where(il_ref[0, :, cols] == tok, 1.0, 0.0).astype(BF16)
        tok_c = lax.broadcasted_iota(jnp.int32, (CTX_LEN, N_EXPERTS * CAP_CTX), 0)
        qc_ref[...] = jnp.where(ic_ref[0] == tok_c, 1.0, 0.0).astype(BF16)

    tn = o_ref.shape[-1]
    lat = jnp.dot(ql_ref[...], yl_ref[...].reshape(N_EXPERTS * CAP_LAT, tn), preferred_element_type=F32)
    o_ref[0, :SEQ, :] = x_ref[0, :SEQ, :] + mb_ref[0, FFN_GATE:FFN_GATE + 1, :] * lat
    cx = jnp.dot(qc_ref[...], yc_ref[...].reshape(N_EXPERTS * CAP_CTX, tn), preferred_element_type=F32)
    o_ref[0, SEQ:, :] = x_ref[0, SEQ:, :] + mc_ref[0, FFN_GATE:FFN_GATE + 1, :] * cx


def _moe_combine(xs, y, idx_l, idx_c, mod):
    tn = 256
    return pl.pallas_call(
        _moe_combine_kernel,
        grid=(BATCH, D_MODEL // tn),
        in_specs=[
            pl.BlockSpec((1, 1, N_EXPERTS * CAP_LAT), lambda b, j: (b, 0, 0)),
            pl.BlockSpec((1, 1, N_EXPERTS * CAP_CTX), lambda b, j: (b, 0, 0)),
            pl.BlockSpec((N_EXPERTS, CAP_LAT, tn), lambda b, j: (0, b, j)),
            pl.BlockSpec((N_EXPERTS, CAP_CTX, tn), lambda b, j: (0, BATCH * CAP_LAT // CAP_CTX + b, j)),
            pl.BlockSpec((1, S_ROWS, tn), lambda b, j: (b, 0, j)),
            pl.BlockSpec((1, N_ADA, tn), lambda b, j: (b, 0, j)),
            pl.BlockSpec((1, N_ADA, tn), lambda b, j: (CTX_GROUP, 0, j)),
        ],
        out_specs=pl.BlockSpec((1, S_ROWS, tn), lambda b, j: (b, 0, j)),
        out_shape=jax.ShapeDtypeStruct((BATCH, S_ROWS, D_MODEL), F32),
        scratch_shapes=[pltpu.VMEM((SEQ, N_EXPERTS * CAP_LAT), BF16),
                        pltpu.VMEM((CTX_LEN, N_EXPERTS * CAP_CTX), BF16)],
        compiler_params=_cparams(("arbitrary", "arbitrary"), 56),
        name="moe_combine",
    )(idx_l, idx_c, y, y, xs, mod, mod)


def _route_stream(s_tok, s_exp, u_ref, cap, idx_ref, score_ref):
    n_tok = s_exp.shape[1]
    bits = lax.bitcast_convert_type(s_exp, jnp.int32)
    thr = jnp.zeros((N_EXPERTS, 1), jnp.int32)
    for bit in range(30, -1, -1):
        cand = thr | (1 << bit)
        cnt = jnp.sum(jnp.where(bits >= cand, 1.0, 0.0), axis=1, keepdims=True)
        thr = jnp.where(cnt >= cap, cand, thr)
    above = bits > thr
    tied = bits == thr
    n_above = jnp.sum(jnp.where(above, 1.0, 0.0), axis=1, keepdims=True)
    before = u_ref[:n_tok, :n_tok]
    tie_rank = jnp.dot(jnp.where(tied, 1.0, 0.0).astype(BF16), before, preferred_element_type=F32)
    chosen = jnp.logical_or(above, jnp.logical_and(tied, tie_rank < cap - n_above))
    pos = jnp.dot(jnp.where(chosen, 1.0, 0.0).astype(BF16), before, preferred_element_type=F32)
    slot = jnp.where(chosen, pos, -1.0)
    slot_tok = jnp.concatenate([slot, jnp.full((LANES - N_EXPERTS, n_tok), -1.0, F32)], axis=0).T
    slot_ids = lax.broadcasted_iota(jnp.int32, (1, cap), 1).astype(F32)
    tok_ids = lax.broadcasted_iota(jnp.int32, (n_tok, 1), 0).astype(F32)
    for e in range(N_EXPERTS):
        hit = slot_tok[:, e:e + 1] == slot_ids
        idx_ref[0, e:e + 1, :] = jnp.sum(jnp.where(hit, tok_ids, 0.0), axis=0, keepdims=True).astype(jnp.int32)
        score_ref[0, e:e + 1, :] = jnp.sum(jnp.where(hit, s_tok[:, e:e + 1], 0.0), axis=0, keepdims=True)


def _router_kernel(lg_ref, u_ref, il_ref, sl_ref, ic_ref, sc_ref):
    lg = lg_ref[0]
    valid = lax.broadcasted_iota(jnp.int32, (1, lg.shape[1]), 1) < N_EXPERTS
    x = jnp.where(valid, lg, -jnp.inf)
    e = jnp.where(valid, jnp.exp(x - jnp.max(x, axis=-1, keepdims=True)), 0.0)
    s = e / jnp.sum(e, axis=-1, keepdims=True)
    s_exp = s.T[:N_EXPERTS]
    _route_stream(s[:SEQ], s_exp[:, :SEQ], u_ref, CAP_LAT, il_ref, sl_ref)
    _route_stream(s[SEQ:], s_exp[:, SEQ:], u_ref, CAP_CTX, ic_ref, sc_ref)


def _router(logits, before):
    out = lambda cap: pl.BlockSpec((1, N_EXPERTS, cap), lambda b: (b, 0, 0))
    sd = lambda cap, dt: jax.ShapeDtypeStruct((BATCH, N_EXPERTS, cap), dt)
    return pl.pallas_call(
        _router_kernel,
        grid=(BATCH,),
        in_specs=[pl.BlockSpec((1, S_ROWS, LANES), lambda b: (b, 0, 0)),
                  pl.BlockSpec((SEQ, SEQ), lambda b: (0, 0))],
        out_specs=[out(CAP_LAT), out(CAP_LAT), out(CAP_CTX), out(CAP_CTX)],
        out_shape=[sd(CAP_LAT, jnp.int32), sd(CAP_LAT, F32), sd(CAP_CTX, jnp.int32), sd(CAP_CTX, F32)],
        compiler_params=_cparams(("arbitrary",), 52),
        name="router",
    )(logits, before)


def _moe_layer(xs, g_norm, mod, w_router, w1, w3, w2, layer, before):
    wr = jnp.zeros((D_MODEL, LANES), F32).at[:, :N_EXPERTS].set(w_router)
    hp, logits = _norm_route(xs.reshape(M_ROWS, D_MODEL), g_norm, mod, wr.astype(BF16))
    idx_l, top_l, idx_c, top_c = _router(logits.reshape(BATCH, S_ROWS, LANES), before)
    base = (jnp.arange(BATCH, dtype=jnp.int32) * S_ROWS)[:, None, None]

    def per_expert(t):
        return jnp.swapaxes(t, 0, 1).reshape(N_EXPERTS, -1)

    rows = jnp.concatenate([per_expert(idx_l + base), per_expert(idx_c + base + SEQ)], axis=1)
    score = jnp.concatenate([per_expert(top_l), per_expert(top_c)], axis=1)
    xe = _row_gather(hp, rows.reshape(-1))
    y = _ffn(xe.reshape(N_EXPERTS, EXP_ROWS, D_MODEL // 2), w1, w3, w2, layer, score[..., None])
    return _moe_combine(xs, y, idx_l.reshape(BATCH, 1, -1), idx_c.reshape(BATCH, 1, -1), mod)


def kernel(x, c, ctx, c_ctx, w_ada, b_ada, norm_mix, norm_ffn, norm_final, gla_w_in, gla_w_a1, gla_w_a2, gla_b_a, gla_head_norm, gla_w_out, hy_w_in, hy_b_in, hy_conv_w, hy_conv_b, hy_f_w1, hy_f_b1, hy_f_w2, hy_f_b2, hy_f_w3, hy_f_b3, hy_f_w4, hy_skip, hy_w_out, hy_b_out, moe_router, moe_w1, moe_w3, moe_w2):
    cond = jnp.zeros((N_GROUPS, D_MODEL), F32).at[:BATCH].set(c).at[CTX_GROUP].set(c_ctx)
    mods = _ada(cond, w_ada, b_ada).reshape(DEPTH, N_GROUPS, N_ADA, D_MODEL)
    xs = jnp.concatenate([x, ctx], axis=1)
    tok = jnp.arange(SEQ, dtype=jnp.int32)
    before = (tok[:, None] < tok[None, :]).astype(BF16)
    for i in range(DEPTH):
        mod = mods[i]
        j = i // 2
        if i % 2 == 0:
            xs = _gla_layer(xs, norm_mix[i], mod, gla_w_in[j], gla_w_a1[j], gla_w_a2[j], gla_b_a[j], gla_head_norm[j],
                            gla_w_out[j])
        else:
            fw = (hy_f_w1[j], hy_f_b1[j], hy_f_w2[j], hy_f_b2[j], hy_f_w3[j], hy_f_b3[j], hy_f_w4[j])
            xs = _hyena_layer(xs, norm_mix[i], mod, hy_w_in[j], hy_b_in[j], hy_conv_w[j], hy_conv_b[j], fw, hy_skip[j],
                              hy_w_out[j], hy_b_out[j])
        xs = _moe_layer(xs, norm_ffn[i], mod, moe_router[i], moe_w1, moe_w3, moe_w2, i, before)
    return _final_norm(xs, norm_final)
```

```python
import functools
import math

import jax
import jax.numpy as jnp
import numpy as np
from jax import lax
from jax.experimental import pallas as pl
from jax.experimental.pallas import tpu as pltpu
from jax.experimental.pallas import tpu_sc as plsc

D_MODEL = 2048
BATCH = 8
SEQ = 2048
DEPTH = 4
CTX_LEN = 256
GRID_W = 64
EPS = 1e-6
N_ADA = 6

GLA_HEADS = 4
GLA_DK = D_MODEL // 2
GLA_DV = D_MODEL
GLA_DK_HEAD = GLA_DK // GLA_HEADS
GLA_DV_HEAD = GLA_DV // GLA_HEADS
GLA_GATE_RANK = 16
GLA_GATE_TEMP = 16.0
GLA_IN = 2 * GLA_DK + 2 * GLA_DV

HY_BANDS = 16
HY_SIN_FREQ = 1.0
HY_FAST_DECAY_PCT = 0.3
HY_SLOW_DECAY_PCT = 1.5
HY_DECAY_TARGET = 1e-2

N_EXPERTS = 16
EXPERT_FF = D_MODEL // 2
CAPACITY_FACTOR = 2

S_ROWS = CTX_LEN + SEQ
M_ROWS = BATCH * S_ROWS
CTX_GROUP = BATCH
N_GROUPS = 16
ROW_BLK = 256
BLKS = S_ROWS // ROW_BLK
MM_TM = S_ROWS // 2
GLA_CHUNK = ROW_BLK
GLA_LEVELS = 8
GLA_SUBLANES = 8
LANES = 128
DFT_PHASE_SPLIT = 64
CAP_LAT = CAPACITY_FACTOR * SEQ // N_EXPERTS
CAP_CTX = CAPACITY_FACTOR * CTX_LEN // N_EXPERTS
EXP_ROWS = BATCH * (CAP_LAT + CAP_CTX)
FFN_TM = EXP_ROWS // 2

F32 = jnp.float32
BF16 = jnp.bfloat16
MIB = 1024 * 1024


def _cparams(sem, vmem_mib):
    return pltpu.CompilerParams(dimension_semantics=sem, vmem_limit_bytes=vmem_mib * MIB)


def _ada_kernel(c_ref, w_ref, b_ref, o_ref):
    c = c_ref[...]
    s = (c * jax.nn.sigmoid(c)).astype(BF16)
    o_ref[0] = jnp.dot(s, w_ref[0].astype(BF16), preferred_element_type=F32) + b_ref[0]


def _ada(cond, w_ada, b_ada):
    tn = 1024
    n = N_ADA * D_MODEL
    return pl.pallas_call(
        _ada_kernel,
        grid=(DEPTH, n // tn),
        in_specs=[
            pl.BlockSpec((N_GROUPS, D_MODEL), lambda l, j: (0, 0)),
            pl.BlockSpec((1, D_MODEL, tn), lambda l, j: (l, 0, j)),
            pl.BlockSpec((1, 1, tn), lambda l, j: (l, 0, j)),
        ],
        out_specs=pl.BlockSpec((1, N_GROUPS, tn), lambda l, j: (l, 0, j)),
        out_shape=jax.ShapeDtypeStruct((DEPTH, N_GROUPS, n), F32),
        compiler_params=_cparams(("arbitrary", "arbitrary"), 40),
        name="ada",
    )(cond, w_ada, b_ada.reshape(DEPTH, 1, n))


def _final_norm_kernel(x_ref, g_ref, o_ref):
    x = x_ref[0]
    y = x * lax.rsqrt(jnp.mean(x * x, axis=-1, keepdims=True) + EPS)
    o_ref[0] = y * g_ref[...]


def _final_norm(xs, g):
    return pl.pallas_call(
        _final_norm_kernel,
        grid=(BATCH, SEQ // ROW_BLK),
        in_specs=[
            pl.BlockSpec((1, ROW_BLK, D_MODEL), lambda b, t: (b, t, 0)),
            pl.BlockSpec((1, D_MODEL), lambda b, t: (0, 0)),
        ],
        out_specs=pl.BlockSpec((1, ROW_BLK, D_MODEL), lambda b, t: (b, t, 0)),
        out_shape=jax.ShapeDtypeStruct((BATCH, SEQ, D_MODEL), F32),
        compiler_params=_cparams(("arbitrary", "arbitrary"), 32),
        name="final_norm",
    )(xs, g.reshape(1, D_MODEL))


MIXER_GATE = 2
FFN_GATE = 5


def _out_proj_kernel(a_ref, w_ref, bias_ref, res_ref, mb_ref, mc_ref, o_ref):
    tm = a_ref.shape[0]
    acc = jnp.dot(a_ref[...], w_ref[...], preferred_element_type=F32) + bias_ref[...]
    rows = lax.broadcasted_iota(jnp.int32, (tm, 1), 0)
    last_tile = pl.program_id(1) % (S_ROWS // tm) == S_ROWS // tm - 1
    is_ctx = jnp.logical_and(last_tile, rows >= tm - CTX_LEN)
    gate = jnp.where(is_ctx, mc_ref[0, MIXER_GATE:MIXER_GATE + 1, :], mb_ref[0, MIXER_GATE:MIXER_GATE + 1, :])
    o_ref[...] = res_ref[...] + gate * acc


def _out_proj(a, w, bias, res, mod):
    m, k = a.shape
    n = w.shape[1]
    tm, tn = MM_TM, 1024
    per = S_ROWS // tm
    return pl.pallas_call(
        _out_proj_kernel,
        grid=(n // tn, m // tm),
        in_specs=[
            pl.BlockSpec((tm, k), lambda j, i: (i, 0)),
            pl.BlockSpec((k, tn), lambda j, i: (0, j)),
            pl.BlockSpec((1, tn), lambda j, i: (0, j)),
            pl.BlockSpec((tm, tn), lambda j, i: (i, j)),
            pl.BlockSpec((1, N_ADA, tn), lambda j, i: (i // per, 0, j)),
            pl.BlockSpec((1, N_ADA, tn), lambda j, i: (CTX_GROUP, 0, j)),
        ],
        out_specs=pl.BlockSpec((tm, tn), lambda j, i: (i, j)),
        out_shape=jax.ShapeDtypeStruct((m, n), F32),
        compiler_params=_cparams(("arbitrary", "arbitrary"), 52),
        name="out_proj",
    )(a, w, bias.reshape(1, n), res, mod, mod)


def _norm_mod_tile(x_ref, g_ref, mb_ref, mc_ref, hn_ref, tile, sh, sc):
    tm = x_ref.shape[0]
    per = S_ROWS // tm
    last_tile = tile % per == per - 1
    step = tm // 4
    for r in range(0, tm, step):
        x = x_ref[r:r + step, :]
        y = x * lax.rsqrt(jnp.mean(x * x, axis=-1, keepdims=True) + EPS) * g_ref[...]
        rows = lax.broadcasted_iota(jnp.int32, (step, 1), 0) + r
        is_ctx = jnp.logical_and(last_tile, rows >= tm - CTX_LEN)
        scale = jnp.where(is_ctx, mc_ref[0, sc:sc + 1, :], mb_ref[0, sc:sc + 1, :])
        shift = jnp.where(is_ctx, mc_ref[0, sh:sh + 1, :], mb_ref[0, sh:sh + 1, :])
        hn_ref[r:r + step, :] = (y * (1.0 + scale) + shift).astype(hn_ref.dtype)


def _norm_mod_specs(index_i):
    per = S_ROWS // MM_TM
    return [
        pl.BlockSpec((MM_TM, D_MODEL), lambda *ids: (index_i(*ids), 0)),
        pl.BlockSpec((1, D_MODEL), lambda *ids: (0, 0)),
        pl.BlockSpec((1, N_ADA, D_MODEL), lambda *ids: (index_i(*ids) // per, 0, 0)),
        pl.BlockSpec((1, N_ADA, D_MODEL), lambda *ids: (CTX_GROUP, 0, 0)),
    ]


def _nmm_kernel(x_ref, g_ref, mb_ref, mc_ref, w_ref, o_ref, hn_ref, *, sh, sc):
    @pl.when(pl.program_id(1) == 0)
    def _():
        _norm_mod_tile(x_ref, g_ref, mb_ref, mc_ref, hn_ref, pl.program_id(0), sh, sc)

    o_ref[...] = jnp.dot(hn_ref[...], w_ref[...], preferred_element_type=F32).astype(o_ref.dtype)


def _nmm(x, g, mod, sh, sc, w, tn):
    m, n = x.shape[0], w.shape[1]
    return pl.pallas_call(
        functools.partial(_nmm_kernel, sh=sh, sc=sc),
        grid=(m // MM_TM, n // tn),
        in_specs=_norm_mod_specs(lambda i, j: i) + [pl.BlockSpec((D_MODEL, tn), lambda i, j: (0, j))],
        out_specs=pl.BlockSpec((MM_TM, tn), lambda i, j: (i, j)),
        out_shape=jax.ShapeDtypeStruct((m, n), BF16),
        scratch_shapes=[pltpu.VMEM((MM_TM, D_MODEL), BF16)],
        compiler_params=_cparams(("arbitrary", "arbitrary"), 52),
        name="norm_mm",
    )(x, g.reshape(1, D_MODEL), mod, mod, w)


def _pack_bf16_pairs(h):
    bits = lax.bitcast_convert_type(h.astype(F32), jnp.uint32)
    half = h.shape[1] // 2
    return (bits[:, :half] >> 16) | bits[:, half:]


def _unpack_bf16_pairs(p):
    lo = lax.bitcast_convert_type(p << 16, F32)
    hi = lax.bitcast_convert_type(p & jnp.uint32(0xFFFF0000), F32)
    return jnp.concatenate([lo, hi], axis=1).astype(BF16)


def _norm_route_kernel(x_ref, g_ref, mb_ref, mc_ref, wr_ref, hp_ref, lg_ref, h_ref):
    _norm_mod_tile(x_ref, g_ref, mb_ref, mc_ref, h_ref, pl.program_id(0), 3, 4)
    h = h_ref[...]
    lg_ref[...] = jnp.dot(h, wr_ref[...], preferred_element_type=F32)
    hp_ref[...] = _pack_bf16_pairs(h)


def _norm_route(x, g, mod, wr):
    return pl.pallas_call(
        _norm_route_kernel,
        grid=(M_ROWS // MM_TM,),
        in_specs=_norm_mod_specs(lambda i: i) + [pl.BlockSpec((D_MODEL, LANES), lambda i: (0, 0))],
        out_specs=[pl.BlockSpec((MM_TM, D_MODEL // 2), lambda i: (i, 0)), pl.BlockSpec((MM_TM, LANES), lambda i: (i, 0))],
        out_shape=[jax.ShapeDtypeStruct((M_ROWS, D_MODEL // 2), jnp.uint32), jax.ShapeDtypeStruct((M_ROWS, LANES), F32)],
        scratch_shapes=[pltpu.VMEM((MM_TM, D_MODEL), BF16)],
        compiler_params=_cparams(("arbitrary",), 52),
        name="norm_route",
    )(x, g.reshape(1, D_MODEL), mod, mod, wr)


SC_GATHER_CHUNK = 96


def _row_gather(table, idx):
    info = plsc.get_sparse_core_info()
    nc, ns = info.num_cores, info.num_subcores
    n_rows = idx.shape[0]
    per_worker = n_rows // (nc * ns)
    n_chunks = per_worker // SC_GATHER_CHUNK
    assert per_worker * nc * ns == n_rows and n_chunks * SC_GATHER_CHUNK == per_worker
    mesh = plsc.VectorSubcoreMesh(core_axis_name="c", subcore_axis_name="s")

    @functools.partial(
        pl.kernel, mesh=mesh,
        out_type=jax.ShapeDtypeStruct((n_rows,) + table.shape[1:], table.dtype),
        scratch_types=[pltpu.VMEM((SC_GATHER_CHUNK,), jnp.int32),
                       pltpu.VMEM((SC_GATHER_CHUNK,) + table.shape[1:], table.dtype),
                       pltpu.SemaphoreType.DMA],
    )
    def gather(table_hbm, idx_hbm, out_hbm, idx_v, rows_v, sem):
        worker = lax.axis_index("s") * nc + lax.axis_index("c")

        @pl.loop(0, n_chunks)
        def _(j):
            base = pl.multiple_of(worker * per_worker + j * SC_GATHER_CHUNK, 8)
            pltpu.sync_copy(idx_hbm.at[pl.ds(base, SC_GATHER_CHUNK)], idx_v)
            pltpu.async_copy(table_hbm.at[idx_v], rows_v, sem).wait()
            pltpu.sync_copy(rows_v, out_hbm.at[pl.ds(base, SC_GATHER_CHUNK)])

    return gather(table, idx)


def _gla_la_kernel(low_ref, w2_ref, b_ref, o_ref):
    low = low_ref[...]
    for z in range(2):
        lz = low[:, z * GLA_GATE_RANK:(z + 1) * GLA_GATE_RANK].astype(BF16)
        logit = jnp.dot(lz, w2_ref[z].astype(BF16), preferred_element_type=F32) + b_ref[z]
        log_sig = jnp.minimum(logit, 0.0) - jnp.log1p(jnp.exp(-jnp.abs(logit)))
        o_ref[z] = log_sig / GLA_GATE_TEMP


def _gla_la(proj, w_a2, b_a):
    tm = MM_TM
    return pl.pallas_call(
        _gla_la_kernel,
        grid=(M_ROWS // tm,),
        in_specs=[
            pl.BlockSpec((tm, LANES), lambda i: (i, GLA_IN // LANES)),
            pl.BlockSpec((2, GLA_GATE_RANK, GLA_DK), lambda i: (0, 0, 0)),
            pl.BlockSpec((2, 1, GLA_DK), lambda i: (0, 0, 0)),
        ],
        out_specs=pl.BlockSpec((2, tm, GLA_DK), lambda i: (0, i, 0)),
        out_shape=jax.ShapeDtypeStruct((2, M_ROWS, GLA_DK), F32),
        compiler_params=_cparams(("arbitrary",), 48),
        name="gla_la",
    )(proj, w_a2, b_a.reshape(2, 1, GLA_DK))


def _gla_tables():
    c = GLA_CHUNK
    t = np.arange(c)
    pair = np.full((c, c), GLA_LEVELS + 1, np.int32)
    pair[t, t] = GLA_LEVELS
    for lvl in range(GLA_LEVELS):
        s = c >> (lvl + 1)
        second = (t % (2 * s)) >= s
        same = (t[:, None] // (2 * s)) == (t[None, :] // (2 * s))
        pair[same & second[:, None] & ~second[None, :]] = lvl
    lower = (t[None, :] <= t[:, None]).astype(np.float32)
    tri = np.stack([np.concatenate([lower, lower], axis=1), np.concatenate([lower.T, lower.T], axis=1)])
    return tri, np.stack([pair, pair.T])


def _gla_dir(q_ref, k_ref, v_ref, la_ref, tri_ref, pair_ref, st_ref, o_ref, reverse):
    c = GLA_CHUNK
    la = la_ref[0, 0] * math.log2(math.e)
    hi = la.astype(BF16)
    lo = (la - hi.astype(F32)).astype(BF16)
    b = jnp.dot(tri_ref[0], jnp.concatenate([hi, lo], axis=0), preferred_element_type=F32)
    q = q_ref[0].astype(F32) * (GLA_DK_HEAD ** -0.5)
    k = k_ref[0].astype(F32)
    v = v_ref[0]
    d = q.shape[-1]
    last = 0 if reverse else c - 1
    b_last = b[last:last + 1]
    nt = (((1,), (1,)), ((), ()))
    st = st_ref[...]
    o = lax.dot_general((q * jnp.exp2(b)).astype(BF16), st.astype(BF16), nt, preferred_element_type=F32)
    row = lax.broadcasted_iota(jnp.int32, (c, 1), 0)
    ms = []
    for lvl in range(GLA_LEVELS):
        s = c >> (lvl + 1)
        n = c // (2 * s)
        mid = s if reverse else s - 1
        b_mid = jnp.broadcast_to(b.reshape(n, 2 * s, d)[:, mid:mid + 1, :], (n, 2 * s, d)).reshape(c, d)
        first, second = (q, k) if reverse else (k, q)
        if s >= GLA_SUBLANES:
            mix = jnp.concatenate([first.reshape(n, 2 * s, d)[:, :s], second.reshape(n, 2 * s, d)[:, s:]],
                                  axis=1).reshape(c, d)
        else:
            mix = jnp.where(row % (2 * s) >= s, second, first)
        ms.append((mix * jnp.exp2(-jnp.abs(b - b_mid))).astype(BF16))
    h = c // 2
    lo_rows, hi_rows = slice(0, h), slice(h, c)
    q_rows, k_rows = (lo_rows, hi_rows) if reverse else (hi_rows, lo_rows)
    a_off = lax.dot_general(ms[0][q_rows], ms[0][k_rows], nt, preferred_element_type=F32).astype(BF16)
    pair = pair_ref[0, :h, :h]
    qb, kb = q.astype(BF16), k.astype(BF16)
    diag = []
    for rows in (lo_rows, hi_rows):
        acc = jnp.where(pair == GLA_LEVELS, lax.dot_general(qb[rows], kb[rows], nt, preferred_element_type=F32), 0.0)
        for lvl in range(1, GLA_LEVELS):
            m = ms[lvl][rows]
            acc = jnp.where(pair == lvl, lax.dot_general(m, m, nt, preferred_element_type=F32), acc)
        diag.append(acc.astype(BF16))
    o_lo = jnp.dot(diag[0], v[lo_rows], preferred_element_type=F32)
    o_hi = jnp.dot(diag[1], v[hi_rows], preferred_element_type=F32)
    if reverse:
        o_lo = o_lo + jnp.dot(a_off, v[hi_rows], preferred_element_type=F32)
    else:
        o_hi = o_hi + jnp.dot(a_off, v[lo_rows], preferred_element_type=F32)
    o_ref[0] = (o + jnp.concatenate([o_lo, o_hi], axis=0)).astype(o_ref.dtype)
    kt = (k * jnp.exp2(b_last - b)).astype(BF16)
    tn = (((0,), (0,)), ((), ()))
    st_ref[...] = st * jnp.exp2(b_last) + lax.dot_general(v, kt, tn, preferred_element_type=F32)


def _gla_scan_kernel(qf, kf, vf, laf, qb, kb, vb, lab, tri_ref, pair_ref, of_ref, ob_ref, sf_ref, sb_ref):
    @pl.when(pl.program_id(2) == 0)
    def _():
        sf_ref[...] = jnp.zeros_like(sf_ref)
        sb_ref[...] = jnp.zeros_like(sb_ref)

    _gla_dir(qf, kf, vf, laf, tri_ref.at[0:1], pair_ref.at[0:1], sf_ref, of_ref, False)
    _gla_dir(qb, kb, vb, lab, tri_ref.at[1:2], pair_ref.at[1:2], sb_ref, ob_ref, True)


def _fwd_block(t):
    return jnp.where(t == 0, BLKS - 1, t - 1)


def _bwd_block(t):
    return jnp.where(t == 0, BLKS - 1, BLKS - 1 - t)


def _gla_scan(proj, la):
    tri_np, pair_np = _gla_tables()
    tri = jnp.asarray(tri_np, BF16)
    pair = jnp.asarray(pair_np, jnp.int32)
    kq, kv = GLA_DK_HEAD, GLA_DV_HEAD
    k_off = GLA_DK // kq
    v_off = 2 * GLA_DK // kv
    c = GLA_CHUNK
    spec_f = [
        pl.BlockSpec((1, ROW_BLK, kq), lambda b, h, t: (b, _fwd_block(t), h)),
        pl.BlockSpec((1, ROW_BLK, kq), lambda b, h, t: (b, _fwd_block(t), k_off + h)),
        pl.BlockSpec((1, ROW_BLK, kv), lambda b, h, t: (b, _fwd_block(t), v_off + h)),
        pl.BlockSpec((1, 1, ROW_BLK, kq), lambda b, h, t: (0, b, _fwd_block(t), h)),
    ]
    spec_b = [
        pl.BlockSpec((1, ROW_BLK, kq), lambda b, h, t: (b, _bwd_block(t), h)),
        pl.BlockSpec((1, ROW_BLK, kq), lambda b, h, t: (b, _bwd_block(t), k_off + h)),
        pl.BlockSpec((1, ROW_BLK, kv), lambda b, h, t: (b, _bwd_block(t), v_off + h)),
        pl.BlockSpec((1, 1, ROW_BLK, kq), lambda b, h, t: (1, b, _bwd_block(t), h)),
    ]
    const = [
        pl.BlockSpec((2,) + tri_np.shape[1:], lambda b, h, t: (0, 0, 0)),
        pl.BlockSpec((2, c, c), lambda b, h, t: (0, 0, 0)),
    ]
    out_sd = jax.ShapeDtypeStruct((BATCH, S_ROWS, GLA_DV), BF16)
    return pl.pallas_call(
        _gla_scan_kernel,
        grid=(BATCH, GLA_HEADS, BLKS),
        in_specs=spec_f + spec_b + const,
        out_specs=[
            pl.BlockSpec((1, ROW_BLK, kv), lambda b, h, t: (b, _fwd_block(t), h)),
            pl.BlockSpec((1, ROW_BLK, kv), lambda b, h, t: (b, _bwd_block(t), h)),
        ],
        out_shape=[out_sd, out_sd],
        scratch_shapes=[pltpu.VMEM((kv, kq), F32), pltpu.VMEM((kv, kq), F32)],
        compiler_params=_cparams(("arbitrary", "arbitrary", "arbitrary"), 32),
        name="gla_scan",
    )(proj, proj, proj, la, proj, proj, proj, la, tri, pair)


def _gla_post_kernel(of_ref, ob_ref, g_ref, hn_ref, o_ref):
    hn = hn_ref[...]
    for h in range(GLA_HEADS):
        cols = slice(h * GLA_DV_HEAD, (h + 1) * GLA_DV_HEAD)
        o = of_ref[0, :, cols].astype(F32) + ob_ref[0, :, cols].astype(F32)
        o = o * lax.rsqrt(jnp.mean(o * o, axis=-1, keepdims=True) + EPS) * hn
        g = g_ref[0, :, cols].astype(F32)
        o_ref[0, :, cols] = (o.astype(BF16) * (g * jax.nn.sigmoid(g)).astype(BF16))


def _gla_post(o_f, o_b, proj, head_norm):
    g_off = 2 * GLA_DK // GLA_DV + 1
    blk = pl.BlockSpec((1, ROW_BLK, GLA_DV), lambda b, t: (b, t, 0))
    return pl.pallas_call(
        _gla_post_kernel,
        grid=(BATCH, BLKS),
        in_specs=[blk, blk,
                  pl.BlockSpec((1, ROW_BLK, GLA_DV), lambda b, t: (b, t, g_off)),
                  pl.BlockSpec((1, GLA_DV_HEAD), lambda b, t: (0, 0))],
        out_specs=blk,
        out_shape=jax.ShapeDtypeStruct((BATCH, S_ROWS, GLA_DV), BF16),
        compiler_params=_cparams(("arbitrary", "arbitrary"), 32),
        name="gla_post",
    )(o_f, o_b, proj, head_norm.reshape(1, GLA_DV_HEAD))


def _gla_layer(xs, g_norm, mod, w_in, w_a1, w_a2, b_a, head_norm, w_out):
    w_cat = jnp.concatenate([w_in, w_a1[0], w_a1[1], jnp.zeros((D_MODEL, LANES - 2 * GLA_GATE_RANK), F32)], axis=1)
    proj = _nmm(xs.reshape(M_ROWS, D_MODEL), g_norm, mod, 0, 1, w_cat.astype(BF16), tn=896)
    la = _gla_la(proj, w_a2, b_a).reshape(2, BATCH, S_ROWS, GLA_DK)
    proj = proj.reshape(BATCH, S_ROWS, GLA_IN + LANES)
    o_f, o_b = _gla_scan(proj, la)
    y = _gla_post(o_f, o_b, proj, head_norm)
    out = _out_proj(y.reshape(M_ROWS, GLA_DV), w_out.astype(BF16), jnp.zeros((D_MODEL,), F32),
                    xs.reshape(M_ROWS, D_MODEL), mod)
    return out.reshape(BATCH, S_ROWS, D_MODEL)


def _hy_proj_kernel(x_ref, g_ref, mb_ref, mc_ref, w0, w1, w2, bi0, bi1, bi2, cw0, cw1, cw2, cb0, cb1, cb2,
                    x0_ref, u_ref, a_ref, *, tm):
    @pl.when(pl.program_id(1) == 0)
    def _():
        _norm_mod_tile(x_ref, g_ref, mb_ref, mc_ref, a_ref, pl.program_id(0), 0, 1)

    rows = lax.broadcasted_iota(jnp.int32, (tm, 1), 0)
    last_tile = pl.program_id(0) % (S_ROWS // tm) == S_ROWS // tm - 1
    ctx0 = tm - CTX_LEN
    is_ctx = jnp.logical_and(last_tile, rows >= ctx0)
    pos = jnp.where(is_ctx, rows - ctx0, rows % GRID_W)
    last = jnp.where(is_ctx, CTX_LEN - 1, GRID_W - 1)
    has_prev = pos != 0
    has_next = pos != last
    a = a_ref[...]

    def branch(w_ref, bi_ref, cw_ref, cb_ref):
        p = jnp.dot(a, w_ref[...], preferred_element_type=F32) + bi_ref[...]
        prev = jnp.where(has_prev, pltpu.roll(p, 1, 0), 0.0)
        nxt = jnp.where(has_next, pltpu.roll(p, tm - 1, 0), 0.0)
        return prev * cw_ref[0:1, :] + p * cw_ref[1:2, :] + nxt * cw_ref[2:3, :] + cb_ref[...]

    x0_ref[...] = branch(w0, bi0, cw0, cb0).astype(BF16)
    u_ref[...] = (branch(w1, bi1, cw1, cb1) * branch(w2, bi2, cw2, cb2)).astype(BF16)


def _hy_proj(x, g_norm, mod, w_in, b_in, conv_w, conv_b):
    tm, tn = MM_TM, 512
    nj = D_MODEL // tn
    wspec = lambda k: pl.BlockSpec((D_MODEL, tn), lambda i, j: (0, k * nj + j))
    bspec = lambda k: pl.BlockSpec((1, tn), lambda i, j: (0, k * nj + j))
    cspec = lambda k: pl.BlockSpec((3, tn), lambda i, j: (0, k * nj + j))
    ospec = pl.BlockSpec((tm, tn), lambda i, j: (i, j))
    out_sd = jax.ShapeDtypeStruct((M_ROWS, D_MODEL), BF16)
    bi = b_in.reshape(1, 3 * D_MODEL)
    cb = conv_b.reshape(1, 3 * D_MODEL)
    return pl.pallas_call(
        functools.partial(_hy_proj_kernel, tm=tm),
        grid=(M_ROWS // tm, nj),
        in_specs=_norm_mod_specs(lambda i, j: i) + [wspec(0), wspec(1), wspec(2),
                  bspec(0), bspec(1), bspec(2), cspec(0), cspec(1), cspec(2), bspec(0), bspec(1), bspec(2)],
        out_specs=[ospec, ospec],
        out_shape=[out_sd, out_sd],
        scratch_shapes=[pltpu.VMEM((tm, D_MODEL), BF16)],
        compiler_params=_cparams(("arbitrary", "arbitrary"), 56),
        name="hy_proj",
    )(x, g_norm.reshape(1, D_MODEL), mod, mod, w_in, w_in, w_in, bi, bi, bi, conv_w, conv_w, conv_w, cb, cb, cb)


def _hy_filter_kernel(z_ref, w1, b1, w2, b2, w3, b3, w4f, w4b, dl_ref, o_ref):
    def layer(x, w_ref, b_ref):
        y = jnp.dot(x.astype(BF16), w_ref[...].astype(BF16), preferred_element_type=F32) + b_ref[...]
        return jnp.sin(HY_SIN_FREQ * y)

    z = z_ref[...]
    hdn = layer(layer(layer(z, w1, b1), w2, b2), w3, b3).astype(BF16)
    decay = jnp.exp(-z[:, 0:1] * dl_ref[...])
    h_fwd = jnp.dot(hdn, w4f[...].astype(BF16), preferred_element_type=F32) * decay
    h_bwd = jnp.dot(hdn, w4b[...].astype(BF16), preferred_element_type=F32) * decay
    lag0 = lax.broadcasted_iota(jnp.int32, (z.shape[0], 1), 0) == 0
    h_bwd = jnp.where(lag0, 0.0, h_bwd)
    inv = 1.0 / jnp.sum(jnp.abs(h_fwd) + jnp.abs(h_bwd), axis=0, keepdims=True)
    for k, sig in enumerate(((h_fwd + h_bwd) * inv, (h_bwd - h_fwd) * inv)):
        hi = sig.astype(BF16)
        o_ref[2 * k] = hi
        o_ref[2 * k + 1] = (sig - hi.astype(F32)).astype(BF16)


def _hy_filter(L, f_w1, f_b1, f_w2, f_b2, f_w3, f_b3, f_w4):
    tn = 512
    nj = D_MODEL // tn
    t = jnp.linspace(0.0, 1.0, L, dtype=F32)[:, None]
    w = 2 * math.pi * jnp.arange(L, dtype=F32)[:, None] / L
    f = jnp.linspace(1e-4, HY_BANDS - 1, HY_BANDS, dtype=F32)[None, :]
    emb = 1 + 2 * HY_BANDS
    z = jnp.concatenate([t, jnp.cos(f * w), -jnp.sin(f * w), jnp.zeros((L, LANES - emb), F32)], axis=-1)
    w1 = jnp.concatenate([f_w1, jnp.zeros((LANES - emb, f_w1.shape[1]), F32)], axis=0)
    max_decay = math.log(HY_DECAY_TARGET) / HY_FAST_DECAY_PCT
    min_decay = math.log(HY_DECAY_TARGET) / HY_SLOW_DECAY_PCT
    deltas = jnp.abs(jnp.linspace(min_decay, max_decay, D_MODEL, dtype=F32))[None, :]
    hid = f_w2.shape[0]
    full = lambda shape: pl.BlockSpec(shape, lambda j: (0, 0))
    return pl.pallas_call(
        _hy_filter_kernel,
        grid=(nj,),
        in_specs=[full((L, LANES)), full((LANES, hid)), full((1, hid)), full((hid, hid)), full((1, hid)),
                  full((hid, hid)), full((1, hid)),
                  pl.BlockSpec((hid, tn), lambda j: (0, j)), pl.BlockSpec((hid, tn), lambda j: (0, nj + j)),
                  pl.BlockSpec((1, tn), lambda j: (0, j))],
        out_specs=pl.BlockSpec((4, L, tn), lambda j: (0, 0, j)),
        out_shape=jax.ShapeDtypeStruct((4, L, D_MODEL), BF16),
        compiler_params=_cparams(("arbitrary",), 48),
        name="hy_filter",
    )(z, w1, f_b1.reshape(1, hid), f_w2, f_b2.reshape(1, hid), f_w3, f_b3.reshape(1, hid), f_w4, f_w4, deltas)


def _dft_tiles(L):
    return (min(2 * L, 1024), min(L, 1024))


def _dft_mats(L):
    n = 2 * L
    tf = _dft_tiles(L)[0] // 2
    f = jnp.arange(L, dtype=jnp.int32)
    w = DFT_PHASE_SPLIT
    phase = lambda k: (k % n).astype(F32) * (2 * math.pi / n)
    coarse = phase(f[:, None] * (jnp.arange(L // w, dtype=jnp.int32) * w)[None, :])[:, :, None]
    fine = phase(f[:, None] * jnp.arange(w, dtype=jnp.int32)[None, :])[:, None, :]
    cosm = (jnp.cos(coarse) * jnp.cos(fine) - jnp.sin(coarse) * jnp.sin(fine)).reshape(L, L)
    sinm = (jnp.sin(coarse) * jnp.cos(fine) + jnp.cos(coarse) * jnp.sin(fine)).reshape(L, L)
    sinm = sinm.at[0].set(jnp.where(f % 2 == 0, 1.0, -1.0))
    fwd = jnp.concatenate([cosm.reshape(L // tf, tf, L), sinm.reshape(L // tf, tf, L)], axis=1).reshape(n, L)
    wgt = jnp.where(f == 0, 1.0 / n, 2.0 / n)[None, :]
    inv = jnp.concatenate([cosm.T * wgt, -sinm.T * wgt], axis=1)
    fwd_hi = fwd.astype(BF16)
    return fwd_hi, (fwd - fwd_hi.astype(F32)).astype(BF16), inv.astype(BF16)


def _dft_fwd_kernel(f_ref, x_ref, ka, kb, kap, o_ref, *, tf):
    acc = jnp.dot(f_ref[...], x_ref[0], preferred_element_type=F32)
    c, s = acc[:tf], acc[tf:]
    o_ref[0, 0] = (c * ka[...] + s * kb[...]).astype(BF16)
    o_ref[0, 1] = (c * kb[...] - s * kap[...]).astype(BF16)


def _dft_fwd(fwd, x, row_blk, L, spec):
    nb = x.shape[0]
    tm = _dft_tiles(L)[0]
    tf, tn = tm // 2, 1024
    kspec = pl.BlockSpec((tf, tn), lambda b, j, i: (i, j))
    return pl.pallas_call(
        functools.partial(_dft_fwd_kernel, tf=tf),
        grid=(nb, D_MODEL // tn, 2 * L // tm),
        in_specs=[pl.BlockSpec((tm, L), lambda b, j, i: (i, 0)),
                  pl.BlockSpec((1, L, tn), lambda b, j, i: (b, row_blk, j)), kspec, kspec, kspec],
        out_specs=pl.BlockSpec((1, 2, tf, tn), lambda b, j, i: (b, 0, i, j)),
        out_shape=jax.ShapeDtypeStruct((nb, 2, L, D_MODEL), BF16),
        compiler_params=_cparams(("arbitrary", "arbitrary", "arbitrary"), 52),
        name="dft_fwd",
    )(fwd, x, *spec)


def _filter_spectrum_kernel(fh_ref, fl_ref, sig_ref, ka_ref, kb_ref, kap_ref, *, tf):
    fh, fl = fh_ref[...], fl_ref[...]

    def dft(rows, hi, lo):
        return (jnp.dot(fh[rows], hi, preferred_element_type=F32) + jnp.dot(fh[rows], lo, preferred_element_type=F32)
                + jnp.dot(fl[rows], hi, preferred_element_type=F32))

    ka = dft(slice(0, tf), sig_ref[0], sig_ref[1])
    kb = dft(slice(tf, 2 * tf), sig_ref[2], sig_ref[3])
    nyq = dft(slice(tf, tf + 16), sig_ref[0], sig_ref[1])[0:1]
    freq = lax.broadcasted_iota(jnp.int32, (tf, 1), 0) + pl.program_id(1) * tf
    ka_ref[...] = ka
    kb_ref[...] = jnp.where(freq == 0, 0.0, kb)
    kap_ref[...] = jnp.where(freq == 0, nyq, ka)


def _filter_spectrum(sig, fwd_hi, fwd_lo, L):
    tm = _dft_tiles(L)[0]
    tf, tn = tm // 2, 256
    fspec = pl.BlockSpec((tm, L), lambda j, i: (i, 0))
    ospec = pl.BlockSpec((tf, tn), lambda j, i: (i, j))
    out_sd = jax.ShapeDtypeStruct((L, D_MODEL), F32)
    return pl.pallas_call(
        functools.partial(_filter_spectrum_kernel, tf=tf),
        grid=(D_MODEL // tn, 2 * L // tm),
        in_specs=[fspec, fspec, pl.BlockSpec((4, L, tn), lambda j, i: (0, 0, j))],
        out_specs=[ospec, ospec, ospec],
        out_shape=[out_sd, out_sd, out_sd],
        compiler_params=_cparams(("arbitrary", "arbitrary"), 52),
        name="filter_spectrum",
    )(fwd_hi, fwd_lo, sig)


def _dft_inv_kernel(g_ref, p_ref, x0_ref, u_ref, skip_ref, *rest):
    o_ref = rest[-1]
    y = jnp.dot(g_ref[...], p_ref[0], preferred_element_type=F32)
    o_ref[0] = (x0_ref[0].astype(F32) * (y + u_ref[0].astype(F32) * skip_ref[...])).astype(BF16)


def _dft_inv(inv, p, x0, u, skip, L, prev=None):
    tm = _dft_tiles(L)[1]
    tn = 512
    off = 0 if L == SEQ else SEQ // tm
    row = lambda b, j, i: (b, off + i, j)
    in_specs = [pl.BlockSpec((tm, 2 * L), lambda b, j, i: (i, 0)),
                pl.BlockSpec((1, 2 * L, tn), lambda b, j, i: (b, 0, j)),
                pl.BlockSpec((1, tm, tn), row), pl.BlockSpec((1, tm, tn), row),
                pl.BlockSpec((1, tn), lambda b, j, i: (0, j))]
    args = [inv, p, x0, u, skip.reshape(1, D_MODEL)]
    aliases = {}
    if prev is not None:
        in_specs.append(pl.BlockSpec(memory_space=pl.ANY))
        args.append(prev)
        aliases = {len(args) - 1: 0}
    return pl.pallas_call(
        _dft_inv_kernel,
        grid=(BATCH, D_MODEL // tn, L // tm),
        in_specs=in_specs,
        out_specs=pl.BlockSpec((1, tm, tn), row),
        out_shape=jax.ShapeDtypeStruct((BATCH, S_ROWS, D_MODEL), BF16),
        input_output_aliases=aliases,
        compiler_params=_cparams(("arbitrary", "arbitrary", "arbitrary"), 52),
        name="dft_inv",
    )(*args)


def _hyena_layer(xs, g_norm, mod, w_in, b_in, conv_w, conv_b, fw, skip, w_out, b_out):
    x0, u = _hy_proj(xs.reshape(M_ROWS, D_MODEL), g_norm, mod, w_in.astype(BF16), b_in, conv_w, conv_b)
    x0 = x0.reshape(BATCH, S_ROWS, D_MODEL)
    u = u.reshape(BATCH, S_ROWS, D_MODEL)
    y = None
    for L in (SEQ, CTX_LEN):
        fwd_hi, fwd_lo, inv = _dft_mats(L)
        spec = _filter_spectrum(_hy_filter(L, *fw), fwd_hi, fwd_lo, L)
        row_blk = 0 if L == SEQ else SEQ // L
        p = _dft_fwd(fwd_hi, u, row_blk, L, spec=spec).reshape(BATCH, 2 * L, D_MODEL)
        y = _dft_inv(inv, p, x0, u, skip, L, prev=y)
    out = _out_proj(y.reshape(M_ROWS, D_MODEL), w_out.astype(BF16), b_out, xs.reshape(M_ROWS, D_MODEL), mod)
    return out.reshape(BATCH, S_ROWS, D_MODEL)


def _ffn_up_kernel(x_ref, w1_ref, w3_ref, o_ref):
    x = _unpack_bf16_pairs(x_ref[0])
    h1 = jnp.dot(x, w1_ref[0, 0].astype(BF16), preferred_element_type=F32)
    h3 = jnp.dot(x, w3_ref[0, 0].astype(BF16), preferred_element_type=F32)
    o_ref[0] = (h1 * jax.nn.sigmoid(h1) * h3).astype(BF16)


def _ffn_down_kernel(a_ref, w2_ref, s_ref, o_ref):
    y = jnp.dot(a_ref[0], w2_ref[0, 0].astype(BF16), preferred_element_type=F32) * s_ref[0]
    o_ref[0] = y.astype(o_ref.dtype)


def _ffn(xe, w1, w3, w2, layer, score):
    tm, tf, tn = FFN_TM, 512, D_MODEL
    act = pl.pallas_call(
        _ffn_up_kernel,
        grid=(N_EXPERTS, EXPERT_FF // tf, EXP_ROWS // tm),
        in_specs=[
            pl.BlockSpec((1, tm, D_MODEL // 2), lambda e, f, i: (e, i, 0)),
            pl.BlockSpec((1, 1, D_MODEL, tf), lambda e, f, i: (layer, e, 0, f)),
            pl.BlockSpec((1, 1, D_MODEL, tf), lambda e, f, i: (layer, e, 0, f)),
        ],
        out_specs=pl.BlockSpec((1, tm, tf), lambda e, f, i: (e, i, f)),
        out_shape=jax.ShapeDtypeStruct((N_EXPERTS, EXP_ROWS, EXPERT_FF), BF16),
        compiler_params=_cparams(("arbitrary", "arbitrary", "arbitrary"), 52),
        name="moe_ffn_up",
    )(xe, w1, w3)
    return pl.pallas_call(
        _ffn_down_kernel,
        grid=(N_EXPERTS, D_MODEL // tn, EXP_ROWS // tm),
        in_specs=[
            pl.BlockSpec((1, tm, EXPERT_FF), lambda e, j, i: (e, i, 0)),
            pl.BlockSpec((1, 1, EXPERT_FF, tn), lambda e, j, i: (layer, e, 0, j)),
            pl.BlockSpec((1, tm, 1), lambda e, j, i: (e, i, 0)),
        ],
        out_specs=pl.BlockSpec((1, tm, tn), lambda e, j, i: (e, i, j)),
        out_shape=jax.ShapeDtypeStruct((N_EXPERTS, EXP_ROWS, D_MODEL), BF16),
        compiler_params=_cparams(("arbitrary", "arbitrary", "arbitrary"), 52),
        name="moe_ffn_down",
    )(act, w2, score)


def _moe_combine_kernel(il_ref, ic_ref, yl_ref, yc_ref, x_ref, mb_ref, mc_ref, o_ref, ql_ref, qc_ref):
    @pl.when(pl.program_id(1) == 0)
    def _():
        chunk = 512
        tok = lax.broadcasted_iota(jnp.int32, (SEQ, chunk), 0)
        for kk in range(N_EXPERTS * CAP_LAT // chunk):
            cols = slice(kk * chunk, (kk + 1) * chunk)
            ql_ref[:, cols] = jnp.where(il_ref[0, :, cols] == tok, 1.0, 0.0).astype(BF16)
        tok_c = lax.broadcasted_iota(jnp.int32, (CTX_LEN, N_EXPERTS * CAP_CTX), 0)
        qc_ref[...] = jnp.where(ic_ref[0] == tok_c, 1.0, 0.0).astype(BF16)

    tn = o_ref.shape[-1]
    lat = jnp.dot(ql_ref[...], yl_ref[...].reshape(N_EXPERTS * CAP_LAT, tn), preferred_element_type=F32)
    o_ref[0, :SEQ, :] = x_ref[0, :SEQ, :] + mb_ref[0, FFN_GATE:FFN_GATE + 1, :] * lat
    cx = jnp.dot(qc_ref[...], yc_ref[...].reshape(N_EXPERTS * CAP_CTX, tn), preferred_element_type=F32)
    o_ref[0, SEQ:, :] = x_ref[0, SEQ:, :] + mc_ref[0, FFN_GATE:FFN_GATE + 1, :] * cx


def _moe_combine(xs, y, idx_l, idx_c, mod):
    tn = 256
    return pl.pallas_call(
        _moe_combine_kernel,
        grid=(BATCH, D_MODEL // tn),
        in_specs=[
            pl.BlockSpec((1, 1, N_EXPERTS * CAP_LAT), lambda b, j: (b, 0, 0)),
            pl.BlockSpec((1, 1, N_EXPERTS * CAP_CTX), lambda b, j: (b, 0, 0)),
            pl.BlockSpec((N_EXPERTS, CAP_LAT, tn), lambda b, j: (0, b, j)),
            pl.BlockSpec((N_EXPERTS, CAP_CTX, tn), lambda b, j: (0, BATCH * CAP_LAT // CAP_CTX + b, j)),
            pl.BlockSpec((1, S_ROWS, tn), lambda b, j: (b, 0, j)),
            pl.BlockSpec((1, N_ADA, tn), lambda b, j: (b, 0, j)),
            pl.BlockSpec((1, N_ADA, tn), lambda b, j: (CTX_GROUP, 0, j)),
        ],
        out_specs=pl.BlockSpec((1, S_ROWS, tn), lambda b, j: (b, 0, j)),
        out_shape=jax.ShapeDtypeStruct((BATCH, S_ROWS, D_MODEL), F32),
        scratch_shapes=[pltpu.VMEM((SEQ, N_EXPERTS * CAP_LAT), BF16),
                        pltpu.VMEM((CTX_LEN, N_EXPERTS * CAP_CTX), BF16)],
        compiler_params=_cparams(("arbitrary", "arbitrary"), 56),
        name="moe_combine",
    )(idx_l, idx_c, y, y, xs, mod, mod)


def _route_stream(s_tok, s_exp, u_ref, cap, idx_ref, score_ref):
    n_tok = s_exp.shape[1]
    bits = lax.bitcast_convert_type(s_exp, jnp.int32)
    thr = jnp.zeros((N_EXPERTS, 1), jnp.int32)
    for bit in range(30, -1, -1):
        cand = thr | (1 << bit)
        cnt = jnp.sum(jnp.where(bits >= cand, 1.0, 0.0), axis=1, keepdims=True)
        thr = jnp.where(cnt >= cap, cand, thr)
    above = bits > thr
    tied = bits == thr
    n_above = jnp.sum(jnp.where(above, 1.0, 0.0), axis=1, keepdims=True)
    before = u_ref[:n_tok, :n_tok]
    tie_rank = jnp.dot(jnp.where(tied, 1.0, 0.0).astype(BF16), before, preferred_element_type=F32)
    chosen = jnp.logical_or(above, jnp.logical_and(tied, tie_rank < cap - n_above))
    pos = jnp.dot(jnp.where(chosen, 1.0, 0.0).astype(BF16), before, preferred_element_type=F32)
    slot = jnp.where(chosen, pos, -1.0)
    slot_tok = jnp.concatenate([slot, jnp.full((LANES - N_EXPERTS, n_tok), -1.0, F32)], axis=0).T
    slot_ids = lax.broadcasted_iota(jnp.int32, (1, cap), 1).astype(F32)
    tok_ids = lax.broadcasted_iota(jnp.int32, (n_tok, 1), 0).astype(F32)
    for e in range(N_EXPERTS):
        hit = slot_tok[:, e:e + 1] == slot_ids
        idx_ref[0, e:e + 1, :] = jnp.sum(jnp.where(hit, tok_ids, 0.0), axis=0, keepdims=True).astype(jnp.int32)
        score_ref[0, e:e + 1, :] = jnp.sum(jnp.where(hit, s_tok[:, e:e + 1], 0.0), axis=0, keepdims=True)


def _router_kernel(lg_ref, u_ref, il_ref, sl_ref, ic_ref, sc_ref):
    lg = lg_ref[0]
    valid = lax.broadcasted_iota(jnp.int32, (1, lg.shape[1]), 1) < N_EXPERTS
    x = jnp.where(valid, lg, -jnp.inf)
    e = jnp.where(valid, jnp.exp(x - jnp.max(x, axis=-1, keepdims=True)), 0.0)
    s = e / jnp.sum(e, axis=-1, keepdims=True)
    s_exp = s.T[:N_EXPERTS]
    _route_stream(s[:SEQ], s_exp[:, :SEQ], u_ref, CAP_LAT, il_ref, sl_ref)
    _route_stream(s[SEQ:], s_exp[:, SEQ:], u_ref, CAP_CTX, ic_ref, sc_ref)


def _router(logits, before):
    out = lambda cap: pl.BlockSpec((1, N_EXPERTS, cap), lambda b: (b, 0, 0))
    sd = lambda cap, dt: jax.ShapeDtypeStruct((BATCH, N_EXPERTS, cap), dt)
    return pl.pallas_call(
        _router_kernel,
        grid=(BATCH,),
        in_specs=[pl.BlockSpec((1, S_ROWS, LANES), lambda b: (b, 0, 0)),
                  pl.BlockSpec((SEQ, SEQ), lambda b: (0, 0))],
        out_specs=[out(CAP_LAT), out(CAP_LAT), out(CAP_CTX), out(CAP_CTX)],
        out_shape=[sd(CAP_LAT, jnp.int32), sd(CAP_LAT, F32), sd(CAP_CTX, jnp.int32), sd(CAP_CTX, F32)],
        compiler_params=_cparams(("arbitrary",), 52),
        name="router",
    )(logits, before)


def _moe_layer(xs, g_norm, mod, w_router, w1, w3, w2, layer, before):
    wr = jnp.zeros((D_MODEL, LANES), F32).at[:, :N_EXPERTS].set(w_router)
    hp, logits = _norm_route(xs.reshape(M_ROWS, D_MODEL), g_norm, mod, wr.astype(BF16))
    idx_l, top_l, idx_c, top_c = _router(logits.reshape(BATCH, S_ROWS, LANES), before)
    base = (jnp.arange(BATCH, dtype=jnp.int32) * S_ROWS)[:, None, None]

    def per_expert(t):
        return jnp.swapaxes(t, 0, 1).reshape(N_EXPERTS, -1)

    rows = jnp.concatenate([per_expert(idx_l + base), per_expert(idx_c + base + SEQ)], axis=1)
    score = jnp.concatenate([per_expert(top_l), per_expert(top_c)], axis=1)
    xe = _row_gather(hp, rows.reshape(-1))
    y = _ffn(xe.reshape(N_EXPERTS, EXP_ROWS, D_MODEL // 2), w1, w3, w2, layer, score[..., None])
    return _moe_combine(xs, y, idx_l.reshape(BATCH, 1, -1), idx_c.reshape(BATCH, 1, -1), mod)


def kernel(x, c, ctx, c_ctx, w_ada, b_ada, norm_mix, norm_ffn, norm_final, gla_w_in, gla_w_a1, gla_w_a2, gla_b_a, gla_head_norm, gla_w_out, hy_w_in, hy_b_in, hy_conv_w, hy_conv_b, hy_f_w1, hy_f_b1, hy_f_w2, hy_f_b2, hy_f_w3, hy_f_b3, hy_f_w4, hy_skip, hy_w_out, hy_b_out, moe_router, moe_w1, moe_w3, moe_w2):
    cond = jnp.zeros((N_GROUPS, D_MODEL), F32).at[:BATCH].set(c).at[CTX_GROUP].set(c_ctx)
    mods = _ada(cond, w_ada, b_ada).reshape(DEPTH, N_GROUPS, N_ADA, D_MODEL)
    xs = jnp.concatenate([x, ctx], axis=1)
    tok = jnp.arange(SEQ, dtype=jnp.int32)
    before = (tok[:, None] < tok[None, :]).astype(BF16)
    for i in range(DEPTH):
        mod = mods[i]
        j = i // 2
        if i % 2 == 0:
            xs = _gla_layer(xs, norm_mix[i], mod, gla_w_in[j], gla_w_a1[j], gla_w_a2[j], gla_b_a[j], gla_head_norm[j],
                            gla_w_out[j])
        else:
            fw = (hy_f_w1[j], hy_f_b1[j], hy_f_w2[j], hy_f_b2[j], hy_f_w3[j], hy_f_b3[j], hy_f_w4[j])
            xs = _hyena_layer(xs, norm_mix[i], mod, hy_w_in[j], hy_b_in[j], hy_conv_w[j], hy_conv_b[j], fw, hy_skip[j],
                              hy_w_out[j], hy_b_out[j])
        xs = _moe_layer(xs, norm_ffn[i], mod, moe_router[i], moe_w1, moe_w3, moe_w2, i, before)
    return _final_norm(xs, norm_final)
```

```python
import functools
import math

import jax
import jax.numpy as jnp
import numpy as np
from jax import lax
from jax.experimental import pallas as pl
from jax.experimental.pallas import tpu as pltpu
from jax.experimental.pallas import tpu_sc as plsc

D_MODEL = 2048
BATCH = 8
SEQ = 2048
DEPTH = 4
CTX_LEN = 256
GRID_W = 64
EPS = 1e-6
N_ADA = 6

GLA_HEADS = 4
GLA_DK = D_MODEL // 2
GLA_DV = D_MODEL
GLA_DK_HEAD = GLA_DK // GLA_HEADS
GLA_DV_HEAD = GLA_DV // GLA_HEADS
GLA_GATE_RANK = 16
GLA_GATE_TEMP = 16.0
GLA_IN = 2 * GLA_DK + 2 * GLA_DV

HY_BANDS = 16
HY_SIN_FREQ = 1.0
HY_FAST_DECAY_PCT = 0.3
HY_SLOW_DECAY_PCT = 1.5
HY_DECAY_TARGET = 1e-2

N_EXPERTS = 16
EXPERT_FF = D_MODEL // 2
CAPACITY_FACTOR = 2

S_ROWS = CTX_LEN + SEQ
M_ROWS = BATCH * S_ROWS
CTX_GROUP = BATCH
N_GROUPS = 16
ROW_BLK = 256
BLKS = S_ROWS // ROW_BLK
MM_TM = S_ROWS // 2
GLA_CHUNK = ROW_BLK
GLA_LEVELS = 8
GLA_SUBLANES = 8
LANES = 128
DFT_PHASE_SPLIT = 64
CAP_LAT = CAPACITY_FACTOR * SEQ // N_EXPERTS
CAP_CTX = CAPACITY_FACTOR * CTX_LEN // N_EXPERTS
EXP_ROWS = BATCH * (CAP_LAT + CAP_CTX)
FFN_TM = EXP_ROWS // 2

F32 = jnp.float32
BF16 = jnp.bfloat16
MIB = 1024 * 1024


def _cparams(sem, vmem_mib):
    return pltpu.CompilerParams(dimension_semantics=sem, vmem_limit_bytes=vmem_mib * MIB)


def _ada_kernel(c_ref, w_ref, b_ref, o_ref):
    c = c_ref[...]
    s = (c * jax.nn.sigmoid(c)).astype(BF16)
    o_ref[0] = jnp.dot(s, w_ref[0].astype(BF16), preferred_element_type=F32) + b_ref[0]


def _ada(cond, w_ada, b_ada):
    tn = 1024
    n = N_ADA * D_MODEL
    return pl.pallas_call(
        _ada_kernel,
        grid=(DEPTH, n // tn),
        in_specs=[
            pl.BlockSpec((N_GROUPS, D_MODEL), lambda l, j: (0, 0)),
            pl.BlockSpec((1, D_MODEL, tn), lambda l, j: (l, 0, j)),
            pl.BlockSpec((1, 1, tn), lambda l, j: (l, 0, j)),
        ],
        out_specs=pl.BlockSpec((1, N_GROUPS, tn), lambda l, j: (l, 0, j)),
        out_shape=jax.ShapeDtypeStruct((DEPTH, N_GROUPS, n), F32),
        compiler_params=_cparams(("arbitrary", "arbitrary"), 40),
        name="ada",
    )(cond, w_ada, b_ada.reshape(DEPTH, 1, n))


def _final_norm_kernel(x_ref, g_ref, o_ref):
    x = x_ref[0]
    y = x * lax.rsqrt(jnp.mean(x * x, axis=-1, keepdims=True) + EPS)
    o_ref[0] = y * g_ref[...]


def _final_norm(xs, g):
    return pl.pallas_call(
        _final_norm_kernel,
        grid=(BATCH, SEQ // ROW_BLK),
        in_specs=[
            pl.BlockSpec((1, ROW_BLK, D_MODEL), lambda b, t: (b, t, 0)),
            pl.BlockSpec((1, D_MODEL), lambda b, t: (0, 0)),
        ],
        out_specs=pl.BlockSpec((1, ROW_BLK, D_MODEL), lambda b, t: (b, t, 0)),
        out_shape=jax.ShapeDtypeStruct((BATCH, SEQ, D_MODEL), F32),
        compiler_params=_cparams(("arbitrary", "arbitrary"), 32),
        name="final_norm",
    )(xs, g.reshape(1, D_MODEL))


MIXER_GATE = 2
FFN_GATE = 5


def _out_proj_kernel(a_ref, w_ref, bias_ref, res_ref, mb_ref, mc_ref, o_ref):
    tm = a_ref.shape[0]
    acc = jnp.dot(a_ref[...], w_ref[...], preferred_element_type=F32) + bias_ref[...]
    rows = lax.broadcasted_iota(jnp.int32, (tm, 1), 0)
    last_tile = pl.program_id(1) % (S_ROWS // tm) == S_ROWS // tm - 1
    is_ctx = jnp.logical_and(last_tile, rows >= tm - CTX_LEN)
    gate = jnp.where(is_ctx, mc_ref[0, MIXER_GATE:MIXER_GATE + 1, :], mb_ref[0, MIXER_GATE:MIXER_GATE + 1, :])
    o_ref[...] = res_ref[...] + gate * acc


def _out_proj(a, w, bias, res, mod):
    m, k = a.shape
    n = w.shape[1]
    tm, tn = MM_TM, 1024
    per = S_ROWS // tm
    return pl.pallas_call(
        _out_proj_kernel,
        grid=(n // tn, m // tm),
        in_specs=[
            pl.BlockSpec((tm, k), lambda j, i: (i, 0)),
            pl.BlockSpec((k, tn), lambda j, i: (0, j)),
            pl.BlockSpec((1, tn), lambda j, i: (0, j)),
            pl.BlockSpec((tm, tn), lambda j, i: (i, j)),
            pl.BlockSpec((1, N_ADA, tn), lambda j, i: (i // per, 0, j)),
            pl.BlockSpec((1, N_ADA, tn), lambda j, i: (CTX_GROUP, 0, j)),
        ],
        out_specs=pl.BlockSpec((tm, tn), lambda j, i: (i, j)),
        out_shape=jax.ShapeDtypeStruct((m, n), F32),
        compiler_params=_cparams(("arbitrary", "arbitrary"), 52),
        name="out_proj",
    )(a, w, bias.reshape(1, n), res, mod, mod)


def _norm_mod_tile(x_ref, g_ref, mb_ref, mc_ref, hn_ref, tile, sh, sc):
    tm = x_ref.shape[0]
    per = S_ROWS // tm
    last_tile = tile % per == per - 1
    step = tm // 4
    for r in range(0, tm, step):
        x = x_ref[r:r + step, :]
        y = x * lax.rsqrt(jnp.mean(x * x, axis=-1, keepdims=True) + EPS) * g_ref[...]
        rows = lax.broadcasted_iota(jnp.int32, (step, 1), 0) + r
        is_ctx = jnp.logical_and(last_tile, rows >= tm - CTX_LEN)
        scale = jnp.where(is_ctx, mc_ref[0, sc:sc + 1, :], mb_ref[0, sc:sc + 1, :])
        shift = jnp.where(is_ctx, mc_ref[0, sh:sh + 1, :], mb_ref[0, sh:sh + 1, :])
        hn_ref[r:r + step, :] = (y * (1.0 + scale) + shift).astype(hn_ref.dtype)


def _norm_mod_specs(index_i):
    per = S_ROWS // MM_TM
    return [
        pl.BlockSpec((MM_TM, D_MODEL), lambda *ids: (index_i(*ids), 0)),
        pl.BlockSpec((1, D_MODEL), lambda *ids: (0, 0)),
        pl.BlockSpec((1, N_ADA, D_MODEL), lambda *ids: (index_i(*ids) // per, 0, 0)),
        pl.BlockSpec((1, N_ADA, D_MODEL), lambda *ids: (CTX_GROUP, 0, 0)),
    ]


def _nmm_kernel(x_ref, g_ref, mb_ref, mc_ref, w_ref, o_ref, hn_ref, *, sh, sc):
    @pl.when(pl.program_id(1) == 0)
    def _():
        _norm_mod_tile(x_ref, g_ref, mb_ref, mc_ref, hn_ref, pl.program_id(0), sh, sc)

    o_ref[...] = jnp.dot(hn_ref[...], w_ref[...], preferred_element_type=F32).astype(o_ref.dtype)


def _nmm(x, g, mod, sh, sc, w, tn):
    m, n = x.shape[0], w.shape[1]
    return pl.pallas_call(
        functools.partial(_nmm_kernel, sh=sh, sc=sc),
        grid=(m // MM_TM, n // tn),
        in_specs=_norm_mod_specs(lambda i, j: i) + [pl.BlockSpec((D_MODEL, tn), lambda i, j: (0, j))],
        out_specs=pl.BlockSpec((MM_TM, tn), lambda i, j: (i, j)),
        out_shape=jax.ShapeDtypeStruct((m, n), BF16),
        scratch_shapes=[pltpu.VMEM((MM_TM, D_MODEL), BF16)],
        compiler_params=_cparams(("arbitrary", "arbitrary"), 52),
        name="norm_mm",
    )(x, g.reshape(1, D_MODEL), mod, mod, w)


def _pack_bf16_pairs(h):
    bits = lax.bitcast_convert_type(h.astype(F32), jnp.uint32)
    half = h.shape[1] // 2
    return (bits[:, :half] >> 16) | bits[:, half:]


def _unpack_bf16_pairs(p):
    lo = lax.bitcast_convert_type(p << 16, F32)
    hi = lax.bitcast_convert_type(p & jnp.uint32(0xFFFF0000), F32)
    return jnp.concatenate([lo, hi], axis=1).astype(BF16)


def _norm_route_kernel(x_ref, g_ref, mb_ref, mc_ref, wr_ref, hp_ref, lg_ref, h_ref):
    _norm_mod_tile(x_ref, g_ref, mb_ref, mc_ref, h_ref, pl.program_id(0), 3, 4)
    h = h_ref[...]
    lg_ref[...] = jnp.dot(h, wr_ref[...], preferred_element_type=F32)
    hp_ref[...] = _pack_bf16_pairs(h)


def _norm_route(x, g, mod, wr):
    return pl.pallas_call(
        _norm_route_kernel,
        grid=(M_ROWS // MM_TM,),
        in_specs=_norm_mod_specs(lambda i: i) + [pl.BlockSpec((D_MODEL, LANES), lambda i: (0, 0))],
        out_specs=[pl.BlockSpec((MM_TM, D_MODEL // 2), lambda i: (i, 0)), pl.BlockSpec((MM_TM, LANES), lambda i: (i, 0))],
        out_shape=[jax.ShapeDtypeStruct((M_ROWS, D_MODEL // 2), jnp.uint32), jax.ShapeDtypeStruct((M_ROWS, LANES), F32)],
        scratch_shapes=[pltpu.VMEM((MM_TM, D_MODEL), BF16)],
        compiler_params=_cparams(("arbitrary",), 52),
        name="norm_route",
    )(x, g.reshape(1, D_MODEL), mod, mod, wr)


SC_GATHER_CHUNK = 96


def _row_gather(table, idx):
    info = plsc.get_sparse_core_info()
    nc, ns = info.num_cores, info.num_subcores
    n_rows = idx.shape[0]
    per_worker = n_rows // (nc * ns)
    n_chunks = per_worker // SC_GATHER_CHUNK
    assert per_worker * nc * ns == n_rows and n_chunks * SC_GATHER_CHUNK == per_worker
    mesh = plsc.VectorSubcoreMesh(core_axis_name="c", subcore_axis_name="s")

    @functools.partial(
        pl.kernel, mesh=mesh,
        out_type=jax.ShapeDtypeStruct((n_rows,) + table.shape[1:], table.dtype),
        scratch_types=[pltpu.VMEM((SC_GATHER_CHUNK,), jnp.int32),
                       pltpu.VMEM((SC_GATHER_CHUNK,) + table.shape[1:], table.dtype),
                       pltpu.SemaphoreType.DMA],
    )
    def gather(table_hbm, idx_hbm, out_hbm, idx_v, rows_v, sem):
        worker = lax.axis_index("s") * nc + lax.axis_index("c")

        @pl.loop(0, n_chunks)
        def _(j):
            base = pl.multiple_of(worker * per_worker + j * SC_GATHER_CHUNK, 8)
            pltpu.sync_copy(idx_hbm.at[pl.ds(base, SC_GATHER_CHUNK)], idx_v)
            pltpu.async_copy(table_hbm.at[idx_v], rows_v, sem).wait()
            pltpu.sync_copy(rows_v, out_hbm.at[pl.ds(base, SC_GATHER_CHUNK)])

    return gather(table, idx)


def _gla_la_kernel(low_ref, w2_ref, b_ref, o_ref):
    low = low_ref[...]
    for z in range(2):
        lz = low[:, z * GLA_GATE_RANK:(z + 1) * GLA_GATE_RANK].astype(BF16)
        logit = jnp.dot(lz, w2_ref[z].astype(BF16), preferred_element_type=F32) + b_ref[z]
        log_sig = jnp.minimum(logit, 0.0) - jnp.log1p(jnp.exp(-jnp.abs(logit)))
        o_ref[z] = log_sig / GLA_GATE_TEMP


def _gla_la(proj, w_a2, b_a):
    tm = MM_TM
    return pl.pallas_call(
        _gla_la_kernel,
        grid=(M_ROWS // tm,),
        in_specs=[
            pl.BlockSpec((tm, LANES), lambda i: (i, GLA_IN // LANES)),
            pl.BlockSpec((2, GLA_GATE_RANK, GLA_DK), lambda i: (0, 0, 0)),
            pl.BlockSpec((2, 1, GLA_DK), lambda i: (0, 0, 0)),
        ],
        out_specs=pl.BlockSpec((2, tm, GLA_DK), lambda i: (0, i, 0)),
        out_shape=jax.ShapeDtypeStruct((2, M_ROWS, GLA_DK), F32),
        compiler_params=_cparams(("arbitrary",), 48),
        name="gla_la",
    )(proj, w_a2, b_a.reshape(2, 1, GLA_DK))


def _gla_tables():
    c = GLA_CHUNK
    t = np.arange(c)
    pair = np.full((c, c), GLA_LEVELS + 1, np.int32)
    pair[t, t] = GLA_LEVELS
    for lvl in range(GLA_LEVELS):
        s = c >> (lvl + 1)
        second = (t % (2 * s)) >= s
        same = (t[:, None] // (2 * s)) == (t[None, :] // (2 * s))
        pair[same & second[:, None] & ~second[None, :]] = lvl
    lower = (t[None, :] <= t[:, None]).astype(np.float32)
    tri = np.stack([np.concatenate([lower, lower], axis=1), np.concatenate([lower.T, lower.T], axis=1)])
    return tri, np.stack([pair, pair.T])


def _gla_dir(q_ref, k_ref, v_ref, la_ref, tri_ref, pair_ref, st_ref, o_ref, reverse):
    c = GLA_CHUNK
    la = la_ref[0, 0] * math.log2(math.e)
    hi = la.astype(BF16)
    lo = (la - hi.astype(F32)).astype(BF16)
    b = jnp.dot(tri_ref[0], jnp.concatenate([hi, lo], axis=0), preferred_element_type=F32)
    q = q_ref[0].astype(F32) * (GLA_DK_HEAD ** -0.5)
    k = k_ref[0].astype(F32)
    v = v_ref[0]
    d = q.shape[-1]
    last = 0 if reverse else c - 1
    b_last = b[last:last + 1]
    nt = (((1,), (1,)), ((), ()))
    st = st_ref[...]
    o = lax.dot_general((q * jnp.exp2(b)).astype(BF16), st.astype(BF16), nt, preferred_element_type=F32)
    row = lax.broadcasted_iota(jnp.int32, (c, 1), 0)
    ms = []
    for lvl in range(GLA_LEVELS):
        s = c >> (lvl + 1)
        n = c // (2 * s)
        mid = s if reverse else s - 1
        b_mid = jnp.broadcast_to(b.reshape(n, 2 * s, d)[:, mid:mid + 1, :], (n, 2 * s, d)).reshape(c, d)
        first, second = (q, k) if reverse else (k, q)
        if s >= GLA_SUBLANES:
            mix = jnp.concatenate([first.reshape(n, 2 * s, d)[:, :s], second.reshape(n, 2 * s, d)[:, s:]],
                                  axis=1).reshape(c, d)
        else:
            mix = jnp.where(row % (2 * s) >= s, second, first)
        ms.append((mix * jnp.exp2(-jnp.abs(b - b_mid))).astype(BF16))
    h = c // 2
    lo_rows, hi_rows = slice(0, h), slice(h, c)
    q_rows, k_rows = (lo_rows, hi_rows) if reverse else (hi_rows, lo_rows)
    a_off = lax.dot_general(ms[0][q_rows], ms[0][k_rows], nt, preferred_element_type=F32).astype(BF16)
    pair = pair_ref[0, :h, :h]
    qb, kb = q.astype(BF16), k.astype(BF16)
    diag = []
    for rows in (lo_rows, hi_rows):
        acc = jnp.where(pair == GLA_LEVELS, lax.dot_general(qb[rows], kb[rows], nt, preferred_element_type=F32), 0.0)
        for lvl in range(1, GLA_LEVELS):
            m = ms[lvl][rows]
            acc = jnp.where(pair == lvl, lax.dot_general(m, m, nt, preferred_element_type=F32), acc)
        diag.append(acc.astype(BF16))
    o_lo = jnp.dot(diag[0], v[lo_rows], preferred_element_type=F32)
    o_hi = jnp.dot(diag[1], v[hi_rows], preferred_element_type=F32)
    if reverse:
        o_lo = o_lo + jnp.dot(a_off, v[hi_rows], preferred_element_type=F32)
    else:
        o_hi = o_hi + jnp.dot(a_off, v[lo_rows], preferred_element_type=F32)
    o_ref[0] = (o + jnp.concatenate([o_lo, o_hi], axis=0)).astype(o_ref.dtype)
    kt = (k * jnp.exp2(b_last - b)).astype(BF16)
    tn = (((0,), (0,)), ((), ()))
    st_ref[...] = st * jnp.exp2(b_last) + lax.dot_general(v, kt, tn, preferred_element_type=F32)


def _gla_scan_kernel(qf, kf, vf, laf, qb, kb, vb, lab, tri_ref, pair_ref, of_ref, ob_ref, sf_ref, sb_ref):
    @pl.when(pl.program_id(2) == 0)
    def _():
        sf_ref[...] = jnp.zeros_like(sf_ref)
        sb_ref[...] = jnp.zeros_like(sb_ref)

    _gla_dir(qf, kf, vf, laf, tri_ref.at[0:1], pair_ref.at[0:1], sf_ref, of_ref, False)
    _gla_dir(qb, kb, vb, lab, tri_ref.at[1:2], pair_ref.at[1:2], sb_ref, ob_ref, True)


def _fwd_block(t):
    return jnp.where(t == 0, BLKS - 1, t - 1)


def _bwd_block(t):
    return jnp.where(t == 0, BLKS - 1, BLKS - 1 - t)


def _gla_scan(proj, la):
    tri_np, pair_np = _gla_tables()
    tri = jnp.asarray(tri_np, BF16)
    pair = jnp.asarray(pair_np, jnp.int32)
    kq, kv = GLA_DK_HEAD, GLA_DV_HEAD
    k_off = GLA_DK // kq
    v_off = 2 * GLA_DK // kv
    c = GLA_CHUNK
    spec_f = [
        pl.BlockSpec((1, ROW_BLK, kq), lambda b, h, t: (b, _fwd_block(t), h)),
        pl.BlockSpec((1, ROW_BLK, kq), lambda b, h, t: (b, _fwd_block(t), k_off + h)),
        pl.BlockSpec((1, ROW_BLK, kv), lambda b, h, t: (b, _fwd_block(t), v_off + h)),
        pl.BlockSpec((1, 1, ROW_BLK, kq), lambda b, h, t: (0, b, _fwd_block(t), h)),
    ]
    spec_b = [
        pl.BlockSpec((1, ROW_BLK, kq), lambda b, h, t: (b, _bwd_block(t), h)),
        pl.BlockSpec((1, ROW_BLK, kq), lambda b, h, t: (b, _bwd_block(t), k_off + h)),
        pl.BlockSpec((1, ROW_BLK, kv), lambda b, h, t: (b, _bwd_block(t), v_off + h)),
        pl.BlockSpec((1, 1, ROW_BLK, kq), lambda b, h, t: (1, b, _bwd_block(t), h)),
    ]
    const = [
        pl.BlockSpec((2,) + tri_np.shape[1:], lambda b, h, t: (0, 0, 0)),
        pl.BlockSpec((2, c, c), lambda b, h, t: (0, 0, 0)),
    ]
    out_sd = jax.ShapeDtypeStruct((BATCH, S_ROWS, GLA_DV), BF16)
    return pl.pallas_call(
        _gla_scan_kernel,
        grid=(BATCH, GLA_HEADS, BLKS),
        in_specs=spec_f + spec_b + const,
        out_specs=[
            pl.BlockSpec((1, ROW_BLK, kv), lambda b, h, t: (b, _fwd_block(t), h)),
            pl.BlockSpec((1, ROW_BLK, kv), lambda b, h, t: (b, _bwd_block(t), h)),
        ],
        out_shape=[out_sd, out_sd],
        scratch_shapes=[pltpu.VMEM((kv, kq), F32), pltpu.VMEM((kv, kq), F32)],
        compiler_params=_cparams(("arbitrary", "arbitrary", "arbitrary"), 32),
        name="gla_scan",
    )(proj, proj, proj, la, proj, proj, proj, la, tri, pair)


def _gla_post_kernel(of_ref, ob_ref, g_ref, hn_ref, o_ref):
    hn = hn_ref[...]
    for h in range(GLA_HEADS):
        cols = slice(h * GLA_DV_HEAD, (h + 1) * GLA_DV_HEAD)
        o = of_ref[0, :, cols].astype(F32) + ob_ref[0, :, cols].astype(F32)
        o = o * lax.rsqrt(jnp.mean(o * o, axis=-1, keepdims=True) + EPS) * hn
        g = g_ref[0, :, cols].astype(F32)
        o_ref[0, :, cols] = (o.astype(BF16) * (g * jax.nn.sigmoid(g)).astype(BF16))


def _gla_post(o_f, o_b, proj, head_norm):
    g_off = 2 * GLA_DK // GLA_DV + 1
    blk = pl.BlockSpec((1, ROW_BLK, GLA_DV), lambda b, t: (b, t, 0))
    return pl.pallas_call(
        _gla_post_kernel,
        grid=(BATCH, BLKS),
        in_specs=[blk, blk,
                  pl.BlockSpec((1, ROW_BLK, GLA_DV), lambda b, t: (b, t, g_off)),
                  pl.BlockSpec((1, GLA_DV_HEAD), lambda b, t: (0, 0))],
        out_specs=blk,
        out_shape=jax.ShapeDtypeStruct((BATCH, S_ROWS, GLA_DV), BF16),
        compiler_params=_cparams(("arbitrary", "arbitrary"), 32),
        name="gla_post",
    )(o_f, o_b, proj, head_norm.reshape(1, GLA_DV_HEAD))


def _gla_layer(xs, g_norm, mod, w_in, w_a1, w_a2, b_a, head_norm, w_out):
    w_cat = jnp.concatenate([w_in, w_a1[0], w_a1[1], jnp.zeros((D_MODEL, LANES - 2 * GLA_GATE_RANK), F32)], axis=1)
    proj = _nmm(xs.reshape(M_ROWS, D_MODEL), g_norm, mod, 0, 1, w_cat.astype(BF16), tn=896)
    la = _gla_la(proj, w_a2, b_a).reshape(2, BATCH, S_ROWS, GLA_DK)
    proj = proj.reshape(BATCH, S_ROWS, GLA_IN + LANES)
    o_f, o_b = _gla_scan(proj, la)
    y = _gla_post(o_f, o_b, proj, head_norm)
    out = _out_proj(y.reshape(M_ROWS, GLA_DV), w_out.astype(BF16), jnp.zeros((D_MODEL,), F32),
                    xs.reshape(M_ROWS, D_MODEL), mod)
    return out.reshape(BATCH, S_ROWS, D_MODEL)


def _hy_proj_kernel(x_ref, g_ref, mb_ref, mc_ref, w0, w1, w2, bi0, bi1, bi2, cw0, cw1, cw2, cb0, cb1, cb2,
                    x0_ref, u_ref, a_ref, *, tm):
    @pl.when(pl.program_id(1) == 0)
    def _():
        _norm_mod_tile(x_ref, g_ref, mb_ref, mc_ref, a_ref, pl.program_id(0), 0, 1)

    rows = lax.broadcasted_iota(jnp.int32, (tm, 1), 0)
    last_tile = pl.program_id(0) % (S_ROWS // tm) == S_ROWS // tm - 1
    ctx0 = tm - CTX_LEN
    is_ctx = jnp.logical_and(last_tile, rows >= ctx0)
    pos = jnp.where(is_ctx, rows - ctx0, rows % GRID_W)
    last = jnp.where(is_ctx, CTX_LEN - 1, GRID_W - 1)
    has_prev = pos != 0
    has_next = pos != last
    a = a_ref[...]

    def branch(w_ref, bi_ref, cw_ref, cb_ref):
        p = jnp.dot(a, w_ref[...], preferred_element_type=F32) + bi_ref[...]
        prev = jnp.where(has_prev, pltpu.roll(p, 1, 0), 0.0)
        nxt = jnp.where(has_next, pltpu.roll(p, tm - 1, 0), 0.0)
        return prev * cw_ref[0:1, :] + p * cw_ref[1:2, :] + nxt * cw_ref[2:3, :] + cb_ref[...]

    x0_ref[...] = branch(w0, bi0, cw0, cb0).astype(BF16)
    u_ref[...] = (branch(w1, bi1, cw1, cb1) * branch(w2, bi2, cw2, cb2)).astype(BF16)


def _hy_proj(x, g_norm, mod, w_in, b_in, conv_w, conv_b):
    tm, tn = MM_TM, 512
    nj = D_MODEL // tn
    wspec = lambda k: pl.BlockSpec((D_MODEL, tn), lambda i, j: (0, k * nj + j))
    bspec = lambda k: pl.BlockSpec((1, tn), lambda i, j: (0, k * nj + j))
    cspec = lambda k: pl.BlockSpec((3, tn), lambda i, j: (0, k * nj + j))
    ospec = pl.BlockSpec((tm, tn), lambda i, j: (i, j))
    out_sd = jax.ShapeDtypeStruct((M_ROWS, D_MODEL), BF16)
    bi = b_in.reshape(1, 3 * D_MODEL)
    cb = conv_b.reshape(1, 3 * D_MODEL)
    return pl.pallas_call(
        functools.partial(_hy_proj_kernel, tm=tm),
        grid=(M_ROWS // tm, nj),
        in_specs=_norm_mod_specs(lambda i, j: i) + [wspec(0), wspec(1), wspec(2),
                  bspec(0), bspec(1), bspec(2), cspec(0), cspec(1), cspec(2), bspec(0), bspec(1), bspec(2)],
        out_specs=[ospec, ospec],
        out_shape=[out_sd, out_sd],
        scratch_shapes=[pltpu.VMEM((tm, D_MODEL), BF16)],
        compiler_params=_cparams(("arbitrary", "arbitrary"), 56),
        name="hy_proj",
    )(x, g_norm.reshape(1, D_MODEL), mod, mod, w_in, w_in, w_in, bi, bi, bi, conv_w, conv_w, conv_w, cb, cb, cb)


def _hy_filter_kernel(z_ref, w1, b1, w2, b2, w3, b3, w4f, w4b, dl_ref, o_ref):
    def layer(x, w_ref, b_ref):
        y = jnp.dot(x.astype(BF16), w_ref[...].astype(BF16), preferred_element_type=F32) + b_ref[...]
        return jnp.sin(HY_SIN_FREQ * y)

    z = z_ref[...]
    hdn = layer(layer(layer(z, w1, b1), w2, b2), w3, b3).astype(BF16)
    decay = jnp.exp(-z[:, 0:1] * dl_ref[...])
    h_fwd = jnp.dot(hdn, w4f[...].astype(BF16), preferred_element_type=F32) * decay
    h_bwd = jnp.dot(hdn, w4b[...].astype(BF16), preferred_element_type=F32) * decay
    lag0 = lax.broadcasted_iota(jnp.int32, (z.shape[0], 1), 0) == 0
    h_bwd = jnp.where(lag0, 0.0, h_bwd)
    inv = 1.0 / jnp.sum(jnp.abs(h_fwd) + jnp.abs(h_bwd), axis=0, keepdims=True)
    for k, sig in enumerate(((h_fwd + h_bwd) * inv, (h_bwd - h_fwd) * inv)):
        hi = sig.astype(BF16)
        o_ref[2 * k] = hi
        o_ref[2 * k + 1] = (sig - hi.astype(F32)).astype(BF16)


def _hy_filter(L, f_w1, f_b1, f_w2, f_b2, f_w3, f_b3, f_w4):
    tn = 512
    nj = D_MODEL // tn
    t = jnp.linspace(0.0, 1.0, L, dtype=F32)[:, None]
    w = 2 * math.pi * jnp.arange(L, dtype=F32)[:, None] / L
    f = jnp.linspace(1e-4, HY_BANDS - 1, HY_BANDS, dtype=F32)[None, :]
    emb = 1 + 2 * HY_BANDS
    z = jnp.concatenate([t, jnp.cos(f * w), -jnp.sin(f * w), jnp.zeros((L, LANES - emb), F32)], axis=-1)
    w1 = jnp.concatenate([f_w1, jnp.zeros((LANES - emb, f_w1.shape[1]), F32)], axis=0)
    max_decay = math.log(HY_DECAY_TARGET) / HY_FAST_DECAY_PCT
    min_decay = math.log(HY_DECAY_TARGET) / HY_SLOW_DECAY_PCT
    deltas = jnp.abs(jnp.linspace(min_decay, max_decay, D_MODEL, dtype=F32))[None, :]
    hid = f_w2.shape[0]
    full = lambda shape: pl.BlockSpec(shape, lambda j: (0, 0))
    return pl.pallas_call(
        _hy_filter_kernel,
        grid=(nj,),
        in_specs=[full((L, LANES)), full((LANES, hid)), full((1, hid)), full((hid, hid)), full((1, hid)),
                  full((hid, hid)), full((1, hid)),
                  pl.BlockSpec((hid, tn), lambda j: (0, j)), pl.BlockSpec((hid, tn), lambda j: (0, nj + j)),
                  pl.BlockSpec((1, tn), lambda j: (0, j))],
        out_specs=pl.BlockSpec((4, L, tn), lambda j: (0, 0, j)),
        out_shape=jax.ShapeDtypeStruct((4, L, D_MODEL), BF16),
        compiler_params=_cparams(("arbitrary",), 48),
        name="hy_filter",
    )(z, w1, f_b1.reshape(1, hid), f_w2, f_b2.reshape(1, hid), f_w3, f_b3.reshape(1, hid), f_w4, f_w4, deltas)


def _dft_tiles(L):
    return (min(2 * L, 1024), min(L, 1024))


def _dft_mats(L):
    n = 2 * L
    tf = _dft_tiles(L)[0] // 2
    f = jnp.arange(L, dtype=jnp.int32)
    w = DFT_PHASE_SPLIT
    phase = lambda k: (k % n).astype(F32) * (2 * math.pi / n)
    coarse = phase(f[:, None] * (jnp.arange(L // w, dtype=jnp.int32) * w)[None, :])[:, :, None]
    fine = phase(f[:, None] * jnp.arange(w, dtype=jnp.int32)[None, :])[:, None, :]
    cosm = (jnp.cos(coarse) * jnp.cos(fine) - jnp.sin(coarse) * jnp.sin(fine)).reshape(L, L)
    sinm = (jnp.sin(coarse) * jnp.cos(fine) + jnp.cos(coarse) * jnp.sin(fine)).reshape(L, L)
    sinm = sinm.at[0].set(jnp.where(f % 2 == 0, 1.0, -1.0))
    fwd = jnp.concatenate([cosm.reshape(L // tf, tf, L), sinm.reshape(L // tf, tf, L)], axis=1).reshape(n, L)
    wgt = jnp.where(f == 0, 1.0 / n, 2.0 / n)[None, :]
    inv = jnp.concatenate([cosm.T * wgt, -sinm.T * wgt], axis=1)
    fwd_hi = fwd.astype(BF16)
    return fwd_hi, (fwd - fwd_hi.astype(F32)).astype(BF16), inv.astype(BF16)


def _dft_fwd_kernel(f_ref, x_ref, ka, kb, kap, o_ref, *, tf):
    acc = jnp.dot(f_ref[...], x_ref[0], preferred_element_type=F32)
    c, s = acc[:tf], acc[tf:]
    o_ref[0, 0] = (c * ka[...] + s * kb[...]).astype(BF16)
    o_ref[0, 1] = (c * kb[...] - s * kap[...]).astype(BF16)


def _dft_fwd(fwd, x, row_blk, L, spec):
    nb = x.shape[0]
    tm = _dft_tiles(L)[0]
    tf, tn = tm // 2, 1024
    kspec = pl.BlockSpec((tf, tn), lambda b, j, i: (i, j))
    return pl.pallas_call(
        functools.partial(_dft_fwd_kernel, tf=tf),
        grid=(nb, D_MODEL // tn, 2 * L // tm),
        in_specs=[pl.BlockSpec((tm, L), lambda b, j, i: (i, 0)),
                  pl.BlockSpec((1, L, tn), lambda b, j, i: (b, row_blk, j)), kspec, kspec, kspec],
        out_specs=pl.BlockSpec((1, 2, tf, tn), lambda b, j, i: (b, 0, i, j)),
        out_shape=jax.ShapeDtypeStruct((nb, 2, L, D_MODEL), BF16),
        compiler_params=_cparams(("arbitrary", "arbitrary", "arbitrary"), 52),
        name="dft_fwd",
    )(fwd, x, *spec)


def _filter_spectrum_kernel(fh_ref, fl_ref, sig_ref, ka_ref, kb_ref, kap_ref, *, tf):
    fh, fl = fh_ref[...], fl_ref[...]

    def dft(rows, hi, lo):
        return (jnp.dot(fh[rows], hi, preferred_element_type=F32) + jnp.dot(fh[rows], lo, preferred_element_type=F32)
                + jnp.dot(fl[rows], hi, preferred_element_type=F32))

    ka = dft(slice(0, tf), sig_ref[0], sig_ref[1])
    kb = dft(slice(tf, 2 * tf), sig_ref[2], sig_ref[3])
    nyq = dft(slice(tf, tf + 16), sig_ref[0], sig_ref[1])[0:1]
    freq = lax.broadcasted_iota(jnp.int32, (tf, 1), 0) + pl.program_id(1) * tf
    ka_ref[...] = ka
    kb_ref[...] = jnp.where(freq == 0, 0.0, kb)
    kap_ref[...] = jnp.where(freq == 0, nyq, ka)


def _filter_spectrum(sig, fwd_hi, fwd_lo, L):
    tm = _dft_tiles(L)[0]
    tf, tn = tm // 2, 256
    fspec = pl.BlockSpec((tm, L), lambda j, i: (i, 0))
    ospec = pl.BlockSpec((tf, tn), lambda j, i: (i, j))
    out_sd = jax.ShapeDtypeStruct((L, D_MODEL), F32)
    return pl.pallas_call(
        functools.partial(_filter_spectrum_kernel, tf=tf),
        grid=(D_MODEL // tn, 2 * L // tm),
        in_specs=[fspec, fspec, pl.BlockSpec((4, L, tn), lambda j, i: (0, 0, j))],
        out_specs=[ospec, ospec, ospec],
        out_shape=[out_sd, out_sd, out_sd],
        compiler_params=_cparams(("arbitrary", "arbitrary"), 52),
        name="filter_spectrum",
    )(fwd_hi, fwd_lo, sig)


def _dft_inv_kernel(g_ref, p_ref, x0_ref, u_ref, skip_ref, *rest):
    o_ref = rest[-1]
    y = jnp.dot(g_ref[...], p_ref[0], preferred_element_type=F32)
    o_ref[0] = (x0_ref[0].astype(F32) * (y + u_ref[0].astype(F32) * skip_ref[...])).astype(BF16)


def _dft_inv(inv, p, x0, u, skip, L, prev=None):
    tm = _dft_tiles(L)[1]
    tn = 512
    off = 0 if L == SEQ else SEQ // tm
    row = lambda b, j, i: (b, off + i, j)
    in_specs = [pl.BlockSpec((tm, 2 * L), lambda b, j, i: (i, 0)),
                pl.BlockSpec((1, 2 * L, tn), lambda b, j, i: (b, 0, j)),
                pl.BlockSpec((1, tm, tn), row), pl.BlockSpec((1, tm, tn), row),
                pl.BlockSpec((1, tn), lambda b, j, i: (0, j))]
    args = [inv, p, x0, u, skip.reshape(1, D_MODEL)]
    aliases = {}
    if prev is not None:
        in_specs.append(pl.BlockSpec(memory_space=pl.ANY))
        args.append(prev)
        aliases = {len(args) - 1: 0}
    return pl.pallas_call(
        _dft_inv_kernel,
        grid=(BATCH, D_MODEL // tn, L // tm),
        in_specs=in_specs,
        out_specs=pl.BlockSpec((1, tm, tn), row),
        out_shape=jax.ShapeDtypeStruct((BATCH, S_ROWS, D_MODEL), BF16),
        input_output_aliases=aliases,
        compiler_params=_cparams(("arbitrary", "arbitrary", "arbitrary"), 52),
        name="dft_inv",
    )(*args)


def _hyena_layer(xs, g_norm, mod, w_in, b_in, conv_w, conv_b, fw, skip, w_out, b_out):
    x0, u = _hy_proj(xs.reshape(M_ROWS, D_MODEL), g_norm, mod, w_in.astype(BF16), b_in, conv_w, conv_b)
    x0 = x0.reshape(BATCH, S_ROWS, D_MODEL)
    u = u.reshape(BATCH, S_ROWS, D_MODEL)
    y = None
    for L in (SEQ, CTX_LEN):
        fwd_hi, fwd_lo, inv = _dft_mats(L)
        spec = _filter_spectrum(_hy_filter(L, *fw), fwd_hi, fwd_lo, L)
        row_blk = 0 if L == SEQ else SEQ // L
        p = _dft_fwd(fwd_hi, u, row_blk, L, spec=spec).reshape(BATCH, 2 * L, D_MODEL)
        y = _dft_inv(inv, p, x0, u, skip, L, prev=y)
    out = _out_proj(y.reshape(M_ROWS, D_MODEL), w_out.astype(BF16), b_out, xs.reshape(M_ROWS, D_MODEL), mod)
    return out.reshape(BATCH, S_ROWS, D_MODEL)


def _ffn_up_kernel(x_ref, w1_ref, w3_ref, o_ref):
    x = _unpack_bf16_pairs(x_ref[0])
    h1 = jnp.dot(x, w1_ref[0, 0].astype(BF16), preferred_element_type=F32)
    h3 = jnp.dot(x, w3_ref[0, 0].astype(BF16), preferred_element_type=F32)
    o_ref[0] = (h1 * jax.nn.sigmoid(h1) * h3).astype(BF16)


def _ffn_down_kernel(a_ref, w2_ref, s_ref, o_ref):
    y = jnp.dot(a_ref[0], w2_ref[0, 0].astype(BF16), preferred_element_type=F32) * s_ref[0]
    o_ref[0] = y.astype(o_ref.dtype)


def _ffn(xe, w1, w3, w2, layer, score):
    tm, tf, tn = FFN_TM, 512, D_MODEL
    act = pl.pallas_call(
        _ffn_up_kernel,
        grid=(N_EXPERTS, EXPERT_FF // tf, EXP_ROWS // tm),
        in_specs=[
            pl.BlockSpec((1, tm, D_MODEL // 2), lambda e, f, i: (e, i, 0)),
            pl.BlockSpec((1, 1, D_MODEL, tf), lambda e, f, i: (layer, e, 0, f)),
            pl.BlockSpec((1, 1, D_MODEL, tf), lambda e, f, i: (layer, e, 0, f)),
        ],
        out_specs=pl.BlockSpec((1, tm, tf), lambda e, f, i: (e, i, f)),
        out_shape=jax.ShapeDtypeStruct((N_EXPERTS, EXP_ROWS, EXPERT_FF), BF16),
        compiler_params=_cparams(("arbitrary", "arbitrary", "arbitrary"), 52),
        name="moe_ffn_up",
    )(xe, w1, w3)
    return pl.pallas_call(
        _ffn_down_kernel,
        grid=(N_EXPERTS, D_MODEL // tn, EXP_ROWS // tm),
        in_specs=[
            pl.BlockSpec((1, tm, EXPERT_FF), lambda e, j, i: (e, i, 0)),
            pl.BlockSpec((1, 1, EXPERT_FF, tn), lambda e, j, i: (layer, e, 0, j)),
            pl.BlockSpec((1, tm, 1), lambda e, j, i: (e, i, 0)),
        ],
        out_specs=pl.BlockSpec((1, tm, tn), lambda e, j, i: (e, i, j)),
        out_shape=jax.ShapeDtypeStruct((N_EXPERTS, EXP_ROWS, D_MODEL), BF16),
        compiler_params=_cparams(("arbitrary", "arbitrary", "arbitrary"), 52),
        name="moe_ffn_down",
    )(act, w2, score)


def _moe_combine_kernel(il_ref, ic_ref, yl_ref, yc_ref, x_ref, mb_ref, mc_ref, o_ref, ql_ref, qc_ref):
    @pl.when(pl.program_id(1) == 0)
    def _():
        chunk = 512
        tok = lax.broadcasted_iota(jnp.int32, (SEQ, chunk), 0)
        for kk in range(N_EXPERTS * CAP_LAT // chunk):
            cols = slice(kk * chunk, (kk + 1) * chunk)
            ql_ref[:, cols] = jnp.where(il_ref[0, :, cols] == tok, 1.0, 0.0).astype(BF16)
        tok_c = lax.broadcasted_iota(jnp.int32, (CTX_LEN, N_EXPERTS * CAP_CTX), 0)
        qc_ref[...] = jnp.where(ic_ref[0] == tok_c, 1.0, 0.0).astype(BF16)

    tn = o_ref.shape[-1]
    lat = jnp.dot(ql_ref[...], yl_ref[...].reshape(N_EXPERTS * CAP_LAT, tn), preferred_element_type=F32)
    o_ref[0, :SEQ, :] = x_ref[0, :SEQ, :] + mb_ref[0, FFN_GATE:FFN_GATE + 1, :] * lat
    cx = jnp.dot(qc_ref[...], yc_ref[...].reshape(N_EXPERTS * CAP_CTX, tn), preferred_element_type=F32)
    o_ref[0, SEQ:, :] = x_ref[0, SEQ:, :] + mc_ref[0, FFN_GATE:FFN_GATE + 1, :] * cx


def _moe_combine(xs, y, idx_l, idx_c, mod):
    tn = 512
    return pl.pallas_call(
        _moe_combine_kernel,
        grid=(BATCH, D_MODEL // tn),
        in_specs=[
            pl.BlockSpec((1, 1, N_EXPERTS * CAP_LAT), lambda b, j: (b, 0, 0)),
            pl.BlockSpec((1, 1, N_EXPERTS * CAP_CTX), lambda b, j: (b, 0, 0)),
            pl.BlockSpec((N_EXPERTS, CAP_LAT, tn), lambda b, j: (0, b, j)),
            pl.BlockSpec((N_EXPERTS, CAP_CTX, tn), lambda b, j: (0, BATCH * CAP_LAT // CAP_CTX + b, j)),
            pl.BlockSpec((1, S_ROWS, tn), lambda b, j: (b, 0, j)),
            pl.BlockSpec((1, N_ADA, tn), lambda b, j: (b, 0, j)),
            pl.BlockSpec((1, N_ADA, tn), lambda b, j: (CTX_GROUP, 0, j)),
        ],
        out_specs=pl.BlockSpec((1, S_ROWS, tn), lambda b, j: (b, 0, j)),
        out_shape=jax.ShapeDtypeStruct((BATCH, S_ROWS, D_MODEL), F32),
        scratch_shapes=[pltpu.VMEM((SEQ, N_EXPERTS * CAP_LAT), BF16),
                        pltpu.VMEM((CTX_LEN, N_EXPERTS * CAP_CTX), BF16)],
        compiler_params=_cparams(("arbitrary", "arbitrary"), 56),
        name="moe_combine",
    )(idx_l, idx_c, y, y, xs, mod, mod)


def _route_stream(s_tok, s_exp, u_ref, cap, idx_ref, score_ref):
    n_tok = s_exp.shape[1]
    bits = lax.bitcast_convert_type(s_exp, jnp.int32)
    thr = jnp.zeros((N_EXPERTS, 1), jnp.int32)
    for bit in range(30, -1, -1):
        cand = thr | (1 << bit)
        cnt = jnp.sum(jnp.where(bits >= cand, 1.0, 0.0), axis=1, keepdims=True)
        thr = jnp.where(cnt >= cap, cand, thr)
    above = bits > thr
    tied = bits == thr
    n_above = jnp.sum(jnp.where(above, 1.0, 0.0), axis=1, keepdims=True)
    before = u_ref[:n_tok, :n_tok]
    tie_rank = jnp.dot(jnp.where(tied, 1.0, 0.0).astype(BF16), before, preferred_element_type=F32)
    chosen = jnp.logical_or(above, jnp.logical_and(tied, tie_rank < cap - n_above))
    pos = jnp.dot(jnp.where(chosen, 1.0, 0.0).astype(BF16), before, preferred_element_type=F32)
    slot = jnp.where(chosen, pos, -1.0)
    slot_tok = jnp.concatenate([slot, jnp.full((LANES - N_EXPERTS, n_tok), -1.0, F32)], axis=0).T
    slot_ids = lax.broadcasted_iota(jnp.int32, (1, cap), 1).astype(F32)
    tok_ids = lax.broadcasted_iota(jnp.int32, (n_tok, 1), 0).astype(F32)
    for e in range(N_EXPERTS):
        hit = slot_tok[:, e:e + 1] == slot_ids
        idx_ref[0, e:e + 1, :] = jnp.sum(jnp.where(hit, tok_ids, 0.0), axis=0, keepdims=True).astype(jnp.int32)
        score_ref[0, e:e + 1, :] = jnp.sum(jnp.where(hit, s_tok[:, e:e + 1], 0.0), axis=0, keepdims=True)


def _router_kernel(lg_ref, u_ref, il_ref, sl_ref, ic_ref, sc_ref):
    lg = lg_ref[0]
    valid = lax.broadcasted_iota(jnp.int32, (1, lg.shape[1]), 1) < N_EXPERTS
    x = jnp.where(valid, lg, -jnp.inf)
    e = jnp.where(valid, jnp.exp(x - jnp.max(x, axis=-1, keepdims=True)), 0.0)
    s = e / jnp.sum(e, axis=-1, keepdims=True)
    s_exp = s.T[:N_EXPERTS]
    _route_stream(s[:SEQ], s_exp[:, :SEQ], u_ref, CAP_LAT, il_ref, sl_ref)
    _route_stream(s[SEQ:], s_exp[:, SEQ:], u_ref, CAP_CTX, ic_ref, sc_ref)


def _router(logits, before):
    out = lambda cap: pl.BlockSpec((1, N_EXPERTS, cap), lambda b: (b, 0, 0))
    sd = lambda cap, dt: jax.ShapeDtypeStruct((BATCH, N_EXPERTS, cap), dt)
    return pl.pallas_call(
        _router_kernel,
        grid=(BATCH,),
        in_specs=[pl.BlockSpec((1, S_ROWS, LANES), lambda b: (b, 0, 0)),
                  pl.BlockSpec((SEQ, SEQ), lambda b: (0, 0))],
        out_specs=[out(CAP_LAT), out(CAP_LAT), out(CAP_CTX), out(CAP_CTX)],
        out_shape=[sd(CAP_LAT, jnp.int32), sd(CAP_LAT, F32), sd(CAP_CTX, jnp.int32), sd(CAP_CTX, F32)],
        compiler_params=_cparams(("arbitrary",), 52),
        name="router",
    )(logits, before)


def _moe_layer(xs, g_norm, mod, w_router, w1, w3, w2, layer, before):
    wr = jnp.zeros((D_MODEL, LANES), F32).at[:, :N_EXPERTS].set(w_router)
    hp, logits = _norm_route(xs.reshape(M_ROWS, D_MODEL), g_norm, mod, wr.astype(BF16))
    idx_l, top_l, idx_c, top_c = _router(logits.reshape(BATCH, S_ROWS, LANES), before)
    base = (jnp.arange(BATCH, dtype=jnp.int32) * S_ROWS)[:, None, None]

    def per_expert(t):
        return jnp.swapaxes(t, 0, 1).reshape(N_EXPERTS, -1)

    rows = jnp.concatenate([per_expert(idx_l + base), per_expert(idx_c + base + SEQ)], axis=1)
    score = jnp.concatenate([per_expert(top_l), per_expert(top_c)], axis=1)
    xe = _row_gather(hp, rows.reshape(-1))
    y = _ffn(xe.reshape(N_EXPERTS, EXP_ROWS, D_MODEL // 2), w1, w3, w2, layer, score[..., None])
    return _moe_combine(xs, y, idx_l.reshape(BATCH, 1, -1), idx_c.reshape(BATCH, 1, -1), mod)


def kernel(x, c, ctx, c_ctx, w_ada, b_ada, norm_mix, norm_ffn, norm_final, gla_w_in, gla_w_a1, gla_w_a2, gla_b_a, gla_head_norm, gla_w_out, hy_w_in, hy_b_in, hy_conv_w, hy_conv_b, hy_f_w1, hy_f_b1, hy_f_w2, hy_f_b2, hy_f_w3, hy_f_b3, hy_f_w4, hy_skip, hy_w_out, hy_b_out, moe_router, moe_w1, moe_w3, moe_w2):
    cond = jnp.zeros((N_GROUPS, D_MODEL), F32).at[:BATCH].set(c).at[CTX_GROUP].set(c_ctx)
    mods = _ada(cond, w_ada, b_ada).reshape(DEPTH, N_GROUPS, N_ADA, D_MODEL)
    xs = jnp.concatenate([x, ctx], axis=1)
    tok = jnp.arange(SEQ, dtype=jnp.int32)
    before = (tok[:, None] < tok[None, :]).astype(BF16)
    for i in range(DEPTH):
        mod = mods[i]
        j = i // 2
        if i % 2 == 0:
            xs = _gla_layer(xs, norm_mix[i], mod, gla_w_in[j], gla_w_a1[j], gla_w_a2[j], gla_b_a[j], gla_head_norm[j],
                            gla_w_out[j])
        else:
            fw = (hy_f_w1[j], hy_f_b1[j], hy_f_w2[j], hy_f_b2[j], hy_f_w3[j], hy_f_b3[j], hy_f_w4[j])
            xs = _hyena_layer(xs, norm_mix[i], mod, hy_w_in[j], hy_b_in[j], hy_conv_w[j], hy_conv_b[j], fw, hy_skip[j],
                              hy_w_out[j], hy_b_out[j])
        xs = _moe_layer(xs, norm_ffn[i], mod, moe_router[i], moe_w1, moe_w3, moe_w2, i, before)
    return _final_norm(xs, norm_final)
```

```python
import functools
import math

import jax
import jax.numpy as jnp
import numpy as np
from jax import lax
from jax.experimental import pallas as pl
from jax.experimental.pallas import tpu as pltpu
from jax.experimental.pallas import tpu_sc as plsc

D_MODEL = 2048
BATCH = 8
SEQ = 2048
DEPTH = 4
CTX_LEN = 256
GRID_W = 64
EPS = 1e-6
N_ADA = 6

GLA_HEADS = 4
GLA_DK = D_MODEL // 2
GLA_DV = D_MODEL
GLA_DK_HEAD = GLA_DK // GLA_HEADS
GLA_DV_HEAD = GLA_DV // GLA_HEADS
GLA_GATE_RANK = 16
GLA_GATE_TEMP = 16.0
GLA_IN = 2 * GLA_DK + 2 * GLA_DV

HY_BANDS = 16
HY_SIN_FREQ = 1.0
HY_FAST_DECAY_PCT = 0.3
HY_SLOW_DECAY_PCT = 1.5
HY_DECAY_TARGET = 1e-2

N_EXPERTS = 16
EXPERT_FF = D_MODEL // 2
CAPACITY_FACTOR = 2

S_ROWS = CTX_LEN + SEQ
M_ROWS = BATCH * S_ROWS
CTX_GROUP = BATCH
N_GROUPS = 16
ROW_BLK = 256
BLKS = S_ROWS // ROW_BLK
MM_TM = S_ROWS // 2
GLA_CHUNK = ROW_BLK
GLA_LEVELS = 8
GLA_SUBLANES = 8
LANES = 128
DFT_PHASE_SPLIT = 64
CAP_LAT = CAPACITY_FACTOR * SEQ // N_EXPERTS
CAP_CTX = CAPACITY_FACTOR * CTX_LEN // N_EXPERTS
EXP_ROWS = BATCH * (CAP_LAT + CAP_CTX)
FFN_TM = EXP_ROWS // 2

F32 = jnp.float32
BF16 = jnp.bfloat16
MIB = 1024 * 1024


def _cparams(sem, vmem_mib):
    return pltpu.CompilerParams(dimension_semantics=sem, vmem_limit_bytes=vmem_mib * MIB)


def _ada_kernel(c_ref, w_ref, b_ref, o_ref):
    c = c_ref[...]
    s = (c * jax.nn.sigmoid(c)).astype(BF16)
    o_ref[0] = jnp.dot(s, w_ref[0].astype(BF16), preferred_element_type=F32) + b_ref[0]


def _ada(cond, w_ada, b_ada):
    tn = 1024
    n = N_ADA * D_MODEL
    return pl.pallas_call(
        _ada_kernel,
        grid=(DEPTH, n // tn),
        in_specs=[
            pl.BlockSpec((N_GROUPS, D_MODEL), lambda l, j: (0, 0)),
            pl.BlockSpec((1, D_MODEL, tn), lambda l, j: (l, 0, j)),
            pl.BlockSpec((1, 1, tn), lambda l, j: (l, 0, j)),
        ],
        out_specs=pl.BlockSpec((1, N_GROUPS, tn), lambda l, j: (l, 0, j)),
        out_shape=jax.ShapeDtypeStruct((DEPTH, N_GROUPS, n), F32),
        compiler_params=_cparams(("arbitrary", "arbitrary"), 40),
        name="ada",
    )(cond, w_ada, b_ada.reshape(DEPTH, 1, n))


def _final_norm_kernel(x_ref, g_ref, o_ref):
    x = x_ref[0]
    y = x * lax.rsqrt(jnp.mean(x * x, axis=-1, keepdims=True) + EPS)
    o_ref[0] = y * g_ref[...]


def _final_norm(xs, g):
    return pl.pallas_call(
        _final_norm_kernel,
        grid=(BATCH, SEQ // ROW_BLK),
        in_specs=[
            pl.BlockSpec((1, ROW_BLK, D_MODEL), lambda b, t: (b, t, 0)),
            pl.BlockSpec((1, D_MODEL), lambda b, t: (0, 0)),
        ],
        out_specs=pl.BlockSpec((1, ROW_BLK, D_MODEL), lambda b, t: (b, t, 0)),
        out_shape=jax.ShapeDtypeStruct((BATCH, SEQ, D_MODEL), F32),
        compiler_params=_cparams(("arbitrary", "arbitrary"), 32),
        name="final_norm",
    )(xs, g.reshape(1, D_MODEL))


MIXER_GATE = 2
FFN_GATE = 5


def _out_proj_kernel(a_ref, w_ref, bias_ref, res_ref, mb_ref, mc_ref, o_ref):
    tm = a_ref.shape[0]
    acc = jnp.dot(a_ref[...], w_ref[...], preferred_element_type=F32) + bias_ref[...]
    rows = lax.broadcasted_iota(jnp.int32, (tm, 1), 0)
    last_tile = pl.program_id(1) % (S_ROWS // tm) == S_ROWS // tm - 1
    is_ctx = jnp.logical_and(last_tile, rows >= tm - CTX_LEN)
    gate = jnp.where(is_ctx, mc_ref[0, MIXER_GATE:MIXER_GATE + 1, :], mb_ref[0, MIXER_GATE:MIXER_GATE + 1, :])
    o_ref[...] = res_ref[...] + gate * acc


def _out_proj(a, w, bias, res, mod):
    m, k = a.shape
    n = w.shape[1]
    tm, tn = MM_TM, 1024
    per = S_ROWS // tm
    return pl.pallas_call(
        _out_proj_kernel,
        grid=(n // tn, m // tm),
        in_specs=[
            pl.BlockSpec((tm, k), lambda j, i: (i, 0)),
            pl.BlockSpec((k, tn), lambda j, i: (0, j)),
            pl.BlockSpec((1, tn), lambda j, i: (0, j)),
            pl.BlockSpec((tm, tn), lambda j, i: (i, j)),
            pl.BlockSpec((1, N_ADA, tn), lambda j, i: (i // per, 0, j)),
            pl.BlockSpec((1, N_ADA, tn), lambda j, i: (CTX_GROUP, 0, j)),
        ],
        out_specs=pl.BlockSpec((tm, tn), lambda j, i: (i, j)),
        out_shape=jax.ShapeDtypeStruct((m, n), F32),
        compiler_params=_cparams(("arbitrary", "arbitrary"), 52),
        name="out_proj",
    )(a, w, bias.reshape(1, n), res, mod, mod)


def _norm_mod_tile(x_ref, g_ref, mb_ref, mc_ref, hn_ref, tile, sh, sc):
    tm = x_ref.shape[0]
    per = S_ROWS // tm
    last_tile = tile % per == per - 1
    step = tm // 4
    for r in range(0, tm, step):
        x = x_ref[r:r + step, :]
        y = x * lax.rsqrt(jnp.mean(x * x, axis=-1, keepdims=True) + EPS) * g_ref[...]
        rows = lax.broadcasted_iota(jnp.int32, (step, 1), 0) + r
        is_ctx = jnp.logical_and(last_tile, rows >= tm - CTX_LEN)
        scale = jnp.where(is_ctx, mc_ref[0, sc:sc + 1, :], mb_ref[0, sc:sc + 1, :])
        shift = jnp.where(is_ctx, mc_ref[0, sh:sh + 1, :], mb_ref[0, sh:sh + 1, :])
        hn_ref[r:r + step, :] = (y * (1.0 + scale) + shift).astype(hn_ref.dtype)


def _norm_mod_specs(index_i):
    per = S_ROWS // MM_TM
    return [
        pl.BlockSpec((MM_TM, D_MODEL), lambda *ids: (index_i(*ids), 0)),
        pl.BlockSpec((1, D_MODEL), lambda *ids: (0, 0)),
        pl.BlockSpec((1, N_ADA, D_MODEL), lambda *ids: (index_i(*ids) // per, 0, 0)),
        pl.BlockSpec((1, N_ADA, D_MODEL), lambda *ids: (CTX_GROUP, 0, 0)),
    ]


def _nmm_kernel(x_ref, g_ref, mb_ref, mc_ref, w_ref, o_ref, hn_ref, *, sh, sc):
    @pl.when(pl.program_id(1) == 0)
    def _():
        _norm_mod_tile(x_ref, g_ref, mb_ref, mc_ref, hn_ref, pl.program_id(0), sh, sc)

    o_ref[...] = jnp.dot(hn_ref[...], w_ref[...], preferred_element_type=F32).astype(o_ref.dtype)


def _nmm(x, g, mod, sh, sc, w, tn):
    m, n = x.shape[0], w.shape[1]
    return pl.pallas_call(
        functools.partial(_nmm_kernel, sh=sh, sc=sc),
        grid=(m // MM_TM, n // tn),
        in_specs=_norm_mod_specs(lambda i, j: i) + [pl.BlockSpec((D_MODEL, tn), lambda i, j: (0, j))],
        out_specs=pl.BlockSpec((MM_TM, tn), lambda i, j: (i, j)),
        out_shape=jax.ShapeDtypeStruct((m, n), BF16),
        scratch_shapes=[pltpu.VMEM((MM_TM, D_MODEL), BF16)],
        compiler_params=_cparams(("arbitrary", "arbitrary"), 52),
        name="norm_mm",
    )(x, g.reshape(1, D_MODEL), mod, mod, w)


def _pack_bf16_pairs(h):
    bits = lax.bitcast_convert_type(h.astype(F32), jnp.uint32)
    half = h.shape[1] // 2
    return (bits[:, :half] >> 16) | bits[:, half:]


def _unpack_bf16_pairs(p):
    lo = lax.bitcast_convert_type(p << 16, F32)
    hi = lax.bitcast_convert_type(p & jnp.uint32(0xFFFF0000), F32)
    return jnp.concatenate([lo, hi], axis=1).astype(BF16)


def _norm_route_kernel(x_ref, g_ref, mb_ref, mc_ref, wr_ref, hp_ref, lg_ref, h_ref):
    _norm_mod_tile(x_ref, g_ref, mb_ref, mc_ref, h_ref, pl.program_id(0), 3, 4)
    h = h_ref[...]
    lg_ref[...] = jnp.dot(h, wr_ref[...], preferred_element_type=F32)
    hp_ref[...] = _pack_bf16_pairs(h)


def _norm_route(x, g, mod, wr):
    return pl.pallas_call(
        _norm_route_kernel,
        grid=(M_ROWS // MM_TM,),
        in_specs=_norm_mod_specs(lambda i: i) + [pl.BlockSpec((D_MODEL, LANES), lambda i: (0, 0))],
        out_specs=[pl.BlockSpec((MM_TM, D_MODEL // 2), lambda i: (i, 0)), pl.BlockSpec((MM_TM, LANES), lambda i: (i, 0))],
        out_shape=[jax.ShapeDtypeStruct((M_ROWS, D_MODEL // 2), jnp.uint32), jax.ShapeDtypeStruct((M_ROWS, LANES), F32)],
        scratch_shapes=[pltpu.VMEM((MM_TM, D_MODEL), BF16)],
        compiler_params=_cparams(("arbitrary",), 52),
        name="norm_route",
    )(x, g.reshape(1, D_MODEL), mod, mod, wr)


SC_GATHER_CHUNK = 96


def _row_gather(table, idx):
    info = plsc.get_sparse_core_info()
    nc, ns = info.num_cores, info.num_subcores
    n_rows = idx.shape[0]
    per_worker = n_rows // (nc * ns)
    n_chunks = per_worker // SC_GATHER_CHUNK
    assert per_worker * nc * ns == n_rows and n_chunks * SC_GATHER_CHUNK == per_worker
    mesh = plsc.VectorSubcoreMesh(core_axis_name="c", subcore_axis_name="s")

    @functools.partial(
        pl.kernel, mesh=mesh,
        out_type=jax.ShapeDtypeStruct((n_rows,) + table.shape[1:], table.dtype),
        scratch_types=[pltpu.VMEM((SC_GATHER_CHUNK,), jnp.int32),
                       pltpu.VMEM((SC_GATHER_CHUNK,) + table.shape[1:], table.dtype),
                       pltpu.SemaphoreType.DMA],
    )
    def gather(table_hbm, idx_hbm, out_hbm, idx_v, rows_v, sem):
        worker = lax.axis_index("s") * nc + lax.axis_index("c")

        @pl.loop(0, n_chunks)
        def _(j):
            base = pl.multiple_of(worker * per_worker + j * SC_GATHER_CHUNK, 8)
            pltpu.sync_copy(idx_hbm.at[pl.ds(base, SC_GATHER_CHUNK)], idx_v)
            pltpu.async_copy(table_hbm.at[idx_v], rows_v, sem).wait()
            pltpu.sync_copy(rows_v, out_hbm.at[pl.ds(base, SC_GATHER_CHUNK)])

    return gather(table, idx)


def _gla_la_kernel(low_ref, w2_ref, b_ref, o_ref):
    low = low_ref[...]
    for z in range(2):
        lz = low[:, z * GLA_GATE_RANK:(z + 1) * GLA_GATE_RANK].astype(BF16)
        logit = jnp.dot(lz, w2_ref[z].astype(BF16), preferred_element_type=F32) + b_ref[z]
        log_sig = jnp.minimum(logit, 0.0) - jnp.log1p(jnp.exp(-jnp.abs(logit)))
        o_ref[z] = log_sig / GLA_GATE_TEMP


def _gla_la(proj, w_a2, b_a):
    tm = MM_TM
    return pl.pallas_call(
        _gla_la_kernel,
        grid=(M_ROWS // tm,),
        in_specs=[
            pl.BlockSpec((tm, LANES), lambda i: (i, GLA_IN // LANES)),
            pl.BlockSpec((2, GLA_GATE_RANK, GLA_DK), lambda i: (0, 0, 0)),
            pl.BlockSpec((2, 1, GLA_DK), lambda i: (0, 0, 0)),
        ],
        out_specs=pl.BlockSpec((2, tm, GLA_DK), lambda i: (0, i, 0)),
        out_shape=jax.ShapeDtypeStruct((2, M_ROWS, GLA_DK), F32),
        compiler_params=_cparams(("arbitrary",), 48),
        name="gla_la",
    )(proj, w_a2, b_a.reshape(2, 1, GLA_DK))


def _gla_tables():
    c = GLA_CHUNK
    t = np.arange(c)
    pair = np.full((c, c), GLA_LEVELS + 1, np.int32)
    pair[t, t] = GLA_LEVELS
    for lvl in range(GLA_LEVELS):
        s = c >> (lvl + 1)
        second = (t % (2 * s)) >= s
        same = (t[:, None] // (2 * s)) == (t[None, :] // (2 * s))
        pair[same & second[:, None] & ~second[None, :]] = lvl
    lower = (t[None, :] <= t[:, None]).astype(np.float32)
    tri = np.stack([np.concatenate([lower, lower], axis=1), np.concatenate([lower.T, lower.T], axis=1)])
    return tri, np.stack([pair, pair.T])


def _gla_dir(q_ref, k_ref, v_ref, la_ref, tri_ref, pair_ref, st_ref, o_ref, reverse):
    c = GLA_CHUNK
    la = la_ref[0, 0] * math.log2(math.e)
    hi = la.astype(BF16)
    lo = (la - hi.astype(F32)).astype(BF16)
    b = jnp.dot(tri_ref[0], jnp.concatenate([hi, lo], axis=0), preferred_element_type=F32)
    q = q_ref[0].astype(F32) * (GLA_DK_HEAD ** -0.5)
    k = k_ref[0].astype(F32)
    v = v_ref[0]
    d = q.shape[-1]
    last = 0 if reverse else c - 1
    b_last = b[last:last + 1]
    nt = (((1,), (1,)), ((), ()))
    st = st_ref[...]
    o = lax.dot_general((q * jnp.exp2(b)).astype(BF16), st.astype(BF16), nt, preferred_element_type=F32)
    row = lax.broadcasted_iota(jnp.int32, (c, 1), 0)
    ms = []
    for lvl in range(GLA_LEVELS):
        s = c >> (lvl + 1)
        n = c // (2 * s)
        mid = s if reverse else s - 1
        b_mid = jnp.broadcast_to(b.reshape(n, 2 * s, d)[:, mid:mid + 1, :], (n, 2 * s, d)).reshape(c, d)
        first, second = (q, k) if reverse else (k, q)
        if s >= GLA_SUBLANES:
            mix = jnp.concatenate([first.reshape(n, 2 * s, d)[:, :s], second.reshape(n, 2 * s, d)[:, s:]],
                                  axis=1).reshape(c, d)
        else:
            mix = jnp.where(row % (2 * s) >= s, second, first)
        ms.append((mix * jnp.exp2(-jnp.abs(b - b_mid))).astype(BF16))
    h = c // 2
    lo_rows, hi_rows = slice(0, h), slice(h, c)
    q_rows, k_rows = (lo_rows, hi_rows) if reverse else (hi_rows, lo_rows)
    a_off = lax.dot_general(ms[0][q_rows], ms[0][k_rows], nt, preferred_element_type=F32).astype(BF16)
    pair = pair_ref[0, :h, :h]
    qb, kb = q.astype(BF16), k.astype(BF16)
    diag = []
    for rows in (lo_rows, hi_rows):
        acc = jnp.where(pair == GLA_LEVELS, lax.dot_general(qb[rows], kb[rows], nt, preferred_element_type=F32), 0.0)
        for lvl in range(1, GLA_LEVELS):
            m = ms[lvl][rows]
            acc = jnp.where(pair == lvl, lax.dot_general(m, m, nt, preferred_element_type=F32), acc)
        diag.append(acc.astype(BF16))
    o_lo = jnp.dot(diag[0], v[lo_rows], preferred_element_type=F32)
    o_hi = jnp.dot(diag[1], v[hi_rows], preferred_element_type=F32)
    if reverse:
        o_lo = o_lo + jnp.dot(a_off, v[hi_rows], preferred_element_type=F32)
    else:
        o_hi = o_hi + jnp.dot(a_off, v[lo_rows], preferred_element_type=F32)
    o_ref[0] = (o + jnp.concatenate([o_lo, o_hi], axis=0)).astype(o_ref.dtype)
    kt = (k * jnp.exp2(b_last - b)).astype(BF16)
    tn = (((0,), (0,)), ((), ()))
    st_ref[...] = st * jnp.exp2(b_last) + lax.dot_general(v, kt, tn, preferred_element_type=F32)


def _gla_scan_kernel(qf, kf, vf, laf, qb, kb, vb, lab, tri_ref, pair_ref, of_ref, ob_ref, sf_ref, sb_ref):
    @pl.when(pl.program_id(2) == 0)
    def _():
        sf_ref[...] = jnp.zeros_like(sf_ref)
        sb_ref[...] = jnp.zeros_like(sb_ref)

    _gla_dir(qf, kf, vf, laf, tri_ref.at[0:1], pair_ref.at[0:1], sf_ref, of_ref, False)
    _gla_dir(qb, kb, vb, lab, tri_ref.at[1:2], pair_ref.at[1:2], sb_ref, ob_ref, True)


def _fwd_block(t):
    return jnp.where(t == 0, BLKS - 1, t - 1)


def _bwd_block(t):
    return jnp.where(t == 0, BLKS - 1, BLKS - 1 - t)


def _gla_scan(proj, la):
    tri_np, pair_np = _gla_tables()
    tri = jnp.asarray(tri_np, BF16)
    pair = jnp.asarray(pair_np, jnp.int32)
    kq, kv = GLA_DK_HEAD, GLA_DV_HEAD
    k_off = GLA_DK // kq
    v_off = 2 * GLA_DK // kv
    c = GLA_CHUNK
    spec_f = [
        pl.BlockSpec((1, ROW_BLK, kq), lambda b, h, t: (b, _fwd_block(t), h)),
        pl.BlockSpec((1, ROW_BLK, kq), lambda b, h, t: (b, _fwd_block(t), k_off + h)),
        pl.BlockSpec((1, ROW_BLK, kv), lambda b, h, t: (b, _fwd_block(t), v_off + h)),
        pl.BlockSpec((1, 1, ROW_BLK, kq), lambda b, h, t: (0, b, _fwd_block(t), h)),
    ]
    spec_b = [
        pl.BlockSpec((1, ROW_BLK, kq), lambda b, h, t: (b, _bwd_block(t), h)),
        pl.BlockSpec((1, ROW_BLK, kq), lambda b, h, t: (b, _bwd_block(t), k_off + h)),
        pl.BlockSpec((1, ROW_BLK, kv), lambda b, h, t: (b, _bwd_block(t), v_off + h)),
        pl.BlockSpec((1, 1, ROW_BLK, kq), lambda b, h, t: (1, b, _bwd_block(t), h)),
    ]
    const = [
        pl.BlockSpec((2,) + tri_np.shape[1:], lambda b, h, t: (0, 0, 0)),
        pl.BlockSpec((2, c, c), lambda b, h, t: (0, 0, 0)),
    ]
    out_sd = jax.ShapeDtypeStruct((BATCH, S_ROWS, GLA_DV), BF16)
    return pl.pallas_call(
        _gla_scan_kernel,
        grid=(BATCH, GLA_HEADS, BLKS),
        in_specs=spec_f + spec_b + const,
        out_specs=[
            pl.BlockSpec((1, ROW_BLK, kv), lambda b, h, t: (b, _fwd_block(t), h)),
            pl.BlockSpec((1, ROW_BLK, kv), lambda b, h, t: (b, _bwd_block(t), h)),
        ],
        out_shape=[out_sd, out_sd],
        scratch_shapes=[pltpu.VMEM((kv, kq), F32), pltpu.VMEM((kv, kq), F32)],
        compiler_params=_cparams(("arbitrary", "arbitrary", "arbitrary"), 32),
        name="gla_scan",
    )(proj, proj, proj, la, proj, proj, proj, la, tri, pair)


def _gla_post_kernel(of_ref, ob_ref, g_ref, hn_ref, o_ref):
    hn = hn_ref[...]
    for h in range(GLA_HEADS):
        cols = slice(h * GLA_DV_HEAD, (h + 1) * GLA_DV_HEAD)
        o = of_ref[0, :, cols].astype(F32) + ob_ref[0, :, cols].astype(F32)
        o = o * lax.rsqrt(jnp.mean(o * o, axis=-1, keepdims=True) + EPS) * hn
        g = g_ref[0, :, cols].astype(F32)
        o_ref[0, :, cols] = (o.astype(BF16) * (g * jax.nn.sigmoid(g)).astype(BF16))


def _gla_post(o_f, o_b, proj, head_norm):
    g_off = 2 * GLA_DK // GLA_DV + 1
    blk = pl.BlockSpec((1, ROW_BLK, GLA_DV), lambda b, t: (b, t, 0))
    return pl.pallas_call(
        _gla_post_kernel,
        grid=(BATCH, BLKS),
        in_specs=[blk, blk,
                  pl.BlockSpec((1, ROW_BLK, GLA_DV), lambda b, t: (b, t, g_off)),
                  pl.BlockSpec((1, GLA_DV_HEAD), lambda b, t: (0, 0))],
        out_specs=blk,
        out_shape=jax.ShapeDtypeStruct((BATCH, S_ROWS, GLA_DV), BF16),
        compiler_params=_cparams(("arbitrary", "arbitrary"), 32),
        name="gla_post",
    )(o_f, o_b, proj, head_norm.reshape(1, GLA_DV_HEAD))


def _gla_layer(xs, g_norm, mod, w_in, w_a1, w_a2, b_a, head_norm, w_out):
    w_cat = jnp.concatenate([w_in, w_a1[0], w_a1[1], jnp.zeros((D_MODEL, LANES - 2 * GLA_GATE_RANK), F32)], axis=1)
    proj = _nmm(xs.reshape(M_ROWS, D_MODEL), g_norm, mod, 0, 1, w_cat.astype(BF16), tn=896)
    la = _gla_la(proj, w_a2, b_a).reshape(2, BATCH, S_ROWS, GLA_DK)
    proj = proj.reshape(BATCH, S_ROWS, GLA_IN + LANES)
    o_f, o_b = _gla_scan(proj, la)
    y = _gla_post(o_f, o_b, proj, head_norm)
    out = _out_proj(y.reshape(M_ROWS, GLA_DV), w_out.astype(BF16), jnp.zeros((D_MODEL,), F32),
                    xs.reshape(M_ROWS, D_MODEL), mod)
    return out.reshape(BATCH, S_ROWS, D_MODEL)


def _hy_proj_kernel(x_ref, g_ref, mb_ref, mc_ref, w0, w1, w2, bi0, bi1, bi2, cw0, cw1, cw2, cb0, cb1, cb2,
                    x0_ref, u_ref, a_ref, *, tm):
    @pl.when(pl.program_id(1) == 0)
    def _():
        _norm_mod_tile(x_ref, g_ref, mb_ref, mc_ref, a_ref, pl.program_id(0), 0, 1)

    rows = lax.broadcasted_iota(jnp.int32, (tm, 1), 0)
    last_tile = pl.program_id(0) % (S_ROWS // tm) == S_ROWS // tm - 1
    ctx0 = tm - CTX_LEN
    is_ctx = jnp.logical_and(last_tile, rows >= ctx0)
    pos = jnp.where(is_ctx, rows - ctx0, rows % GRID_W)
    last = jnp.where(is_ctx, CTX_LEN - 1, GRID_W - 1)
    has_prev = pos != 0
    has_next = pos != last
    a = a_ref[...]

    def branch(w_ref, bi_ref, cw_ref, cb_ref):
        p = jnp.dot(a, w_ref[...], preferred_element_type=F32) + bi_ref[...]
        prev = jnp.where(has_prev, pltpu.roll(p, 1, 0), 0.0)
        nxt = jnp.where(has_next, pltpu.roll(p, tm - 1, 0), 0.0)
        return prev * cw_ref[0:1, :] + p * cw_ref[1:2, :] + nxt * cw_ref[2:3, :] + cb_ref[...]

    x0_ref[...] = branch(w0, bi0, cw0, cb0).astype(BF16)
    u_ref[...] = (branch(w1, bi1, cw1, cb1) * branch(w2, bi2, cw2, cb2)).astype(BF16)


def _hy_proj(x, g_norm, mod, w_in, b_in, conv_w, conv_b):
    tm, tn = MM_TM, 512
    nj = D_MODEL // tn
    wspec = lambda k: pl.BlockSpec((D_MODEL, tn), lambda i, j: (0, k * nj + j))
    bspec = lambda k: pl.BlockSpec((1, tn), lambda i, j: (0, k * nj + j))
    cspec = lambda k: pl.BlockSpec((3, tn), lambda i, j: (0, k * nj + j))
    ospec = pl.BlockSpec((tm, tn), lambda i, j: (i, j))
    out_sd = jax.ShapeDtypeStruct((M_ROWS, D_MODEL), BF16)
    bi = b_in.reshape(1, 3 * D_MODEL)
    cb = conv_b.reshape(1, 3 * D_MODEL)
    return pl.pallas_call(
        functools.partial(_hy_proj_kernel, tm=tm),
        grid=(M_ROWS // tm, nj),
        in_specs=_norm_mod_specs(lambda i, j: i) + [wspec(0), wspec(1), wspec(2),
                  bspec(0), bspec(1), bspec(2), cspec(0), cspec(1), cspec(2), bspec(0), bspec(1), bspec(2)],
        out_specs=[ospec, ospec],
        out_shape=[out_sd, out_sd],
        scratch_shapes=[pltpu.VMEM((tm, D_MODEL), BF16)],
        compiler_params=_cparams(("arbitrary", "arbitrary"), 56),
        name="hy_proj",
    )(x, g_norm.reshape(1, D_MODEL), mod, mod, w_in, w_in, w_in, bi, bi, bi, conv_w, conv_w, conv_w, cb, cb, cb)


def _hy_filter_kernel(z_ref, w1, b1, w2, b2, w3, b3, w4f, w4b, dl_ref, o_ref):
    def layer(x, w_ref, b_ref):
        y = jnp.dot(x.astype(BF16), w_ref[...].astype(BF16), preferred_element_type=F32) + b_ref[...]
        return jnp.sin(HY_SIN_FREQ * y)

    z = z_ref[...]
    hdn = layer(layer(layer(z, w1, b1), w2, b2), w3, b3).astype(BF16)
    decay = jnp.exp(-z[:, 0:1] * dl_ref[...])
    h_fwd = jnp.dot(hdn, w4f[...].astype(BF16), preferred_element_type=F32) * decay
    h_bwd = jnp.dot(hdn, w4b[...].astype(BF16), preferred_element_type=F32) * decay
    lag0 = lax.broadcasted_iota(jnp.int32, (z.shape[0], 1), 0) == 0
    h_bwd = jnp.where(lag0, 0.0, h_bwd)
    inv = 1.0 / jnp.sum(jnp.abs(h_fwd) + jnp.abs(h_bwd), axis=0, keepdims=True)
    for k, sig in enumerate(((h_fwd + h_bwd) * inv, (h_bwd - h_fwd) * inv)):
        hi = sig.astype(BF16)
        o_ref[2 * k] = hi
        o_ref[2 * k + 1] = (sig - hi.astype(F32)).astype(BF16)


def _hy_filter(L, f_w1, f_b1, f_w2, f_b2, f_w3, f_b3, f_w4):
    tn = 512
    nj = D_MODEL // tn
    t = jnp.linspace(0.0, 1.0, L, dtype=F32)[:, None]
    w = 2 * math.pi * jnp.arange(L, dtype=F32)[:, None] / L
    f = jnp.linspace(1e-4, HY_BANDS - 1, HY_BANDS, dtype=F32)[None, :]
    emb = 1 + 2 * HY_BANDS
    z = jnp.concatenate([t, jnp.cos(f * w), -jnp.sin(f * w), jnp.zeros((L, LANES - emb), F32)], axis=-1)
    w1 = jnp.concatenate([f_w1, jnp.zeros((LANES - emb, f_w1.shape[1]), F32)], axis=0)
    max_decay = math.log(HY_DECAY_TARGET) / HY_FAST_DECAY_PCT
    min_decay = math.log(HY_DECAY_TARGET) / HY_SLOW_DECAY_PCT
    deltas = jnp.abs(jnp.linspace(min_decay, max_decay, D_MODEL, dtype=F32))[None, :]
    hid = f_w2.shape[0]
    full = lambda shape: pl.BlockSpec(shape, lambda j: (0, 0))
    return pl.pallas_call(
        _hy_filter_kernel,
        grid=(nj,),
        in_specs=[full((L, LANES)), full((LANES, hid)), full((1, hid)), full((hid, hid)), full((1, hid)),
                  full((hid, hid)), full((1, hid)),
                  pl.BlockSpec((hid, tn), lambda j: (0, j)), pl.BlockSpec((hid, tn), lambda j: (0, nj + j)),
                  pl.BlockSpec((1, tn), lambda j: (0, j))],
        out_specs=pl.BlockSpec((4, L, tn), lambda j: (0, 0, j)),
        out_shape=jax.ShapeDtypeStruct((4, L, D_MODEL), BF16),
        compiler_params=_cparams(("arbitrary",), 48),
        name="hy_filter",
    )(z, w1, f_b1.reshape(1, hid), f_w2, f_b2.reshape(1, hid), f_w3, f_b3.reshape(1, hid), f_w4, f_w4, deltas)


def _dft_tiles(L):
    return (min(2 * L, 1024), min(L, 1024))


def _dft_mats(L):
    n = 2 * L
    tf = _dft_tiles(L)[0] // 2
    f = jnp.arange(L, dtype=jnp.int32)
    w = DFT_PHASE_SPLIT
    phase = lambda k: (k % n).astype(F32) * (2 * math.pi / n)
    coarse = phase(f[:, None] * (jnp.arange(L // w, dtype=jnp.int32) * w)[None, :])[:, :, None]
    fine = phase(f[:, None] * jnp.arange(w, dtype=jnp.int32)[None, :])[:, None, :]
    cosm = (jnp.cos(coarse) * jnp.cos(fine) - jnp.sin(coarse) * jnp.sin(fine)).reshape(L, L)
    sinm = (jnp.sin(coarse) * jnp.cos(fine) + jnp.cos(coarse) * jnp.sin(fine)).reshape(L, L)
    sinm = sinm.at[0].set(jnp.where(f % 2 == 0, 1.0, -1.0))
    fwd = jnp.concatenate([cosm.reshape(L // tf, tf, L), sinm.reshape(L // tf, tf, L)], axis=1).reshape(n, L)
    wgt = jnp.where(f == 0, 1.0 / n, 2.0 / n)[None, :]
    inv = jnp.concatenate([cosm.T * wgt, -sinm.T * wgt], axis=1)
    fwd_hi = fwd.astype(BF16)
    return fwd_hi, (fwd - fwd_hi.astype(F32)).astype(BF16), inv.astype(BF16)


def _dft_fwd_kernel(f_ref, x_ref, ka, kb, kap, o_ref, *, tf):
    acc = jnp.dot(f_ref[...], x_ref[0], preferred_element_type=F32)
    c, s = acc[:tf], acc[tf:]
    o_ref[0, 0] = (c * ka[...] + s * kb[...]).astype(BF16)
    o_ref[0, 1] = (c * kb[...] - s * kap[...]).astype(BF16)


def _dft_fwd(fwd, x, row_blk, L, spec):
    nb = x.shape[0]
    tm = _dft_tiles(L)[0]
    tf, tn = tm // 2, 1024
    kspec = pl.BlockSpec((tf, tn), lambda b, j, i: (i, j))
    return pl.pallas_call(
        functools.partial(_dft_fwd_kernel, tf=tf),
        grid=(nb, D_MODEL // tn, 2 * L // tm),
        in_specs=[pl.BlockSpec((tm, L), lambda b, j, i: (i, 0)),
                  pl.BlockSpec((1, L, tn), lambda b, j, i: (b, row_blk, j)), kspec, kspec, kspec],
        out_specs=pl.BlockSpec((1, 2, tf, tn), lambda b, j, i: (b, 0, i, j)),
        out_shape=jax.ShapeDtypeStruct((nb, 2, L, D_MODEL), BF16),
        compiler_params=_cparams(("arbitrary", "arbitrary", "arbitrary"), 52),
        name="dft_fwd",
    )(fwd, x, *spec)


def _filter_spectrum_kernel(fh_ref, fl_ref, sig_ref, ka_ref, kb_ref, kap_ref, *, tf):
    fh, fl = fh_ref[...], fl_ref[...]

    def dft(rows, hi, lo):
        return (jnp.dot(fh[rows], hi, preferred_element_type=F32) + jnp.dot(fh[rows], lo, preferred_element_type=F32)
                + jnp.dot(fl[rows], hi, preferred_element_type=F32))

    ka = dft(slice(0, tf), sig_ref[0], sig_ref[1])
    kb = dft(slice(tf, 2 * tf), sig_ref[2], sig_ref[3])
    nyq = dft(slice(tf, tf + 16), sig_ref[0], sig_ref[1])[0:1]
    freq = lax.broadcasted_iota(jnp.int32, (tf, 1), 0) + pl.program_id(1) * tf
    ka_ref[...] = ka
    kb_ref[...] = jnp.where(freq == 0, 0.0, kb)
    kap_ref[...] = jnp.where(freq == 0, nyq, ka)


def _filter_spectrum(sig, fwd_hi, fwd_lo, L):
    tm = _dft_tiles(L)[0]
    tf, tn = tm // 2, 512
    fspec = pl.BlockSpec((tm, L), lambda j, i: (i, 0))
    ospec = pl.BlockSpec((tf, tn), lambda j, i: (i, j))
    out_sd = jax.ShapeDtypeStruct((L, D_MODEL), F32)
    return pl.pallas_call(
        functools.partial(_filter_spectrum_kernel, tf=tf),
        grid=(D_MODEL // tn, 2 * L // tm),
        in_specs=[fspec, fspec, pl.BlockSpec((4, L, tn), lambda j, i: (0, 0, j))],
        out_specs=[ospec, ospec, ospec],
        out_shape=[out_sd, out_sd, out_sd],
        compiler_params=_cparams(("arbitrary", "arbitrary"), 52),
        name="filter_spectrum",
    )(fwd_hi, fwd_lo, sig)


def _dft_inv_kernel(g_ref, p_ref, x0_ref, u_ref, skip_ref, *rest):
    o_ref = rest[-1]
    y = jnp.dot(g_ref[...], p_ref[0], preferred_element_type=F32)
    o_ref[0] = (x0_ref[0].astype(F32) * (y + u_ref[0].astype(F32) * skip_ref[...])).astype(BF16)


def _dft_inv(inv, p, x0, u, skip, L, prev=None):
    tm = _dft_tiles(L)[1]
    tn = 512
    off = 0 if L == SEQ else SEQ // tm
    row = lambda b, j, i: (b, off + i, j)
    in_specs = [pl.BlockSpec((tm, 2 * L), lambda b, j, i: (i, 0)),
                pl.BlockSpec((1, 2 * L, tn), lambda b, j, i: (b, 0, j)),
                pl.BlockSpec((1, tm, tn), row), pl.BlockSpec((1, tm, tn), row),
                pl.BlockSpec((1, tn), lambda b, j, i: (0, j))]
    args = [inv, p, x0, u, skip.reshape(1, D_MODEL)]
    aliases = {}
    if prev is not None:
        in_specs.append(pl.BlockSpec(memory_space=pl.ANY))
        args.append(prev)
        aliases = {len(args) - 1: 0}
    return pl.pallas_call(
        _dft_inv_kernel,
        grid=(BATCH, D_MODEL // tn, L // tm),
        in_specs=in_specs,
        out_specs=pl.BlockSpec((1, tm, tn), row),
        out_shape=jax.ShapeDtypeStruct((BATCH, S_ROWS, D_MODEL), BF16),
        input_output_aliases=aliases,
        compiler_params=_cparams(("arbitrary", "arbitrary", "arbitrary"), 52),
        name="dft_inv",
    )(*args)


def _hyena_layer(xs, g_norm, mod, w_in, b_in, conv_w, conv_b, fw, skip, w_out, b_out):
    x0, u = _hy_proj(xs.reshape(M_ROWS, D_MODEL), g_norm, mod, w_in.astype(BF16), b_in, conv_w, conv_b)
    x0 = x0.reshape(BATCH, S_ROWS, D_MODEL)
    u = u.reshape(BATCH, S_ROWS, D_MODEL)
    y = None
    for L in (SEQ, CTX_LEN):
        fwd_hi, fwd_lo, inv = _dft_mats(L)
        spec = _filter_spectrum(_hy_filter(L, *fw), fwd_hi, fwd_lo, L)
        row_blk = 0 if L == SEQ else SEQ // L
        p = _dft_fwd(fwd_hi, u, row_blk, L, spec=spec).reshape(BATCH, 2 * L, D_MODEL)
        y = _dft_inv(inv, p, x0, u, skip, L, prev=y)
    out = _out_proj(y.reshape(M_ROWS, D_MODEL), w_out.astype(BF16), b_out, xs.reshape(M_ROWS, D_MODEL), mod)
    return out.reshape(BATCH, S_ROWS, D_MODEL)


def _ffn_up_kernel(x_ref, w1_ref, w3_ref, o_ref):
    x = _unpack_bf16_pairs(x_ref[0])
    h1 = jnp.dot(x, w1_ref[0, 0].astype(BF16), preferred_element_type=F32)
    h3 = jnp.dot(x, w3_ref[0, 0].astype(BF16), preferred_element_type=F32)
    o_ref[0] = (h1 * jax.nn.sigmoid(h1) * h3).astype(BF16)


def _ffn_down_kernel(a_ref, w2_ref, s_ref, o_ref):
    y = jnp.dot(a_ref[0], w2_ref[0, 0].astype(BF16), preferred_element_type=F32) * s_ref[0]
    o_ref[0] = y.astype(o_ref.dtype)


def _ffn(xe, w1, w3, w2, layer, score):
    tm, tf, tn = FFN_TM, 512, D_MODEL
    act = pl.pallas_call(
        _ffn_up_kernel,
        grid=(N_EXPERTS, EXPERT_FF // tf, EXP_ROWS // tm),
        in_specs=[
            pl.BlockSpec((1, tm, D_MODEL // 2), lambda e, f, i: (e, i, 0)),
            pl.BlockSpec((1, 1, D_MODEL, tf), lambda e, f, i: (layer, e, 0, f)),
            pl.BlockSpec((1, 1, D_MODEL, tf), lambda e, f, i: (layer, e, 0, f)),
        ],
        out_specs=pl.BlockSpec((1, tm, tf), lambda e, f, i: (e, i, f)),
        out_shape=jax.ShapeDtypeStruct((N_EXPERTS, EXP_ROWS, EXPERT_FF), BF16),
        compiler_params=_cparams(("arbitrary", "arbitrary", "arbitrary"), 52),
        name="moe_ffn_up",
    )(xe, w1, w3)
    return pl.pallas_call(
        _ffn_down_kernel,
        grid=(N_EXPERTS, D_MODEL // tn, EXP_ROWS // tm),
        in_specs=[
            pl.BlockSpec((1, tm, EXPERT_FF), lambda e, j, i: (e, i, 0)),
            pl.BlockSpec((1, 1, EXPERT_FF, tn), lambda e, j, i: (layer, e, 0, j)),
            pl.BlockSpec((1, tm, 1), lambda e, j, i: (e, i, 0)),
        ],
        out_specs=pl.BlockSpec((1, tm, tn), lambda e, j, i: (e, i, j)),
        out_shape=jax.ShapeDtypeStruct((N_EXPERTS, EXP_ROWS, D_MODEL), BF16),
        compiler_params=_cparams(("arbitrary", "arbitrary", "arbitrary"), 52),
        name="moe_ffn_down",
    )(act, w2, score)


def _moe_combine_kernel(il_ref, ic_ref, yl_ref, yc_ref, x_ref, mb_ref, mc_ref, o_ref, ql_ref, qc_ref):
    @pl.when(pl.program_id(1) == 0)
    def _():
        chunk = 512
        tok = lax.broadcasted_iota(jnp.int32, (SEQ, chunk), 0)
        for kk in range(N_EXPERTS * CAP_LAT // chunk):
            cols = slice(kk * chunk, (kk + 1) * chunk)
            ql_ref[:, cols] = jnp.where(il_ref[0, :, cols] == tok, 1.0, 0.0).astype(BF16)
        tok_c = lax.broadcasted_iota(jnp.int32, (CTX_LEN, N_EXPERTS * CAP_CTX), 0)
        qc_ref[...] = jnp.where(ic_ref[0] == tok_c, 1.0, 0.0).astype(BF16)

    tn = o_ref.shape[-1]
    lat = jnp.dot(ql_ref[...], yl_ref[...].reshape(N_EXPERTS * CAP_LAT, tn), preferred_element_type=F32)
    o_ref[0, :SEQ, :] = x_ref[0, :SEQ, :] + mb_ref[0, FFN_GATE:FFN_GATE + 1, :] * lat
    cx = jnp.dot(qc_ref[...], yc_ref[...].reshape(N_EXPERTS * CAP_CTX, tn), preferred_element_type=F32)
    o_ref[0, SEQ:, :] = x_ref[0, SEQ:, :] + mc_ref[0, FFN_GATE:FFN_GATE + 1, :] * cx


def _moe_combine(xs, y, idx_l, idx_c, mod):
    tn = 512
    return pl.pallas_call(
        _moe_combine_kernel,
        grid=(BATCH, D_MODEL // tn),
        in_specs=[
            pl.BlockSpec((1, 1, N_EXPERTS * CAP_LAT), lambda b, j: (b, 0, 0)),
            pl.BlockSpec((1, 1, N_EXPERTS * CAP_CTX), lambda b, j: (b, 0, 0)),
            pl.BlockSpec((N_EXPERTS, CAP_LAT, tn), lambda b, j: (0, b, j)),
            pl.BlockSpec((N_EXPERTS, CAP_CTX, tn), lambda b, j: (0, BATCH * CAP_LAT // CAP_CTX + b, j)),
            pl.BlockSpec((1, S_ROWS, tn), lambda b, j: (b, 0, j)),
            pl.BlockSpec((1, N_ADA, tn), lambda b, j: (b, 0, j)),
            pl.BlockSpec((1, N_ADA, tn), lambda b, j: (CTX_GROUP, 0, j)),
        ],
        out_specs=pl.BlockSpec((1, S_ROWS, tn), lambda b, j: (b, 0, j)),
        out_shape=jax.ShapeDtypeStruct((BATCH, S_ROWS, D_MODEL), F32),
        scratch_shapes=[pltpu.VMEM((SEQ, N_EXPERTS * CAP_LAT), BF16),
                        pltpu.VMEM((CTX_LEN, N_EXPERTS * CAP_CTX), BF16)],
        compiler_params=_cparams(("arbitrary", "arbitrary"), 56),
        name="moe_combine",
    )(idx_l, idx_c, y, y, xs, mod, mod)


def _route_stream(s_tok, s_exp, u_ref, cap, idx_ref, score_ref):
    n_tok = s_exp.shape[1]
    bits = lax.bitcast_convert_type(s_exp, jnp.int32)
    thr = jnp.zeros((N_EXPERTS, 1), jnp.int32)
    for bit in range(30, -1, -1):
        cand = thr | (1 << bit)
        cnt = jnp.sum(jnp.where(bits >= cand, 1.0, 0.0), axis=1, keepdims=True)
        thr = jnp.where(cnt >= cap, cand, thr)
    above = bits > thr
    tied = bits == thr
    n_above = jnp.sum(jnp.where(above, 1.0, 0.0), axis=1, keepdims=True)
    before = u_ref[:n_tok, :n_tok]
    tie_rank = jnp.dot(jnp.where(tied, 1.0, 0.0).astype(BF16), before, preferred_element_type=F32)
    chosen = jnp.logical_or(above, jnp.logical_and(tied, tie_rank < cap - n_above))
    pos = jnp.dot(jnp.where(chosen, 1.0, 0.0).astype(BF16), before, preferred_element_type=F32)
    slot = jnp.where(chosen, pos, -1.0)
    slot_tok = jnp.concatenate([slot, jnp.full((LANES - N_EXPERTS, n_tok), -1.0, F32)], axis=0).T
    slot_ids = lax.broadcasted_iota(jnp.int32, (1, cap), 1).astype(F32)
    tok_ids = lax.broadcasted_iota(jnp.int32, (n_tok, 1), 0).astype(F32)
    for e in range(N_EXPERTS):
        hit = slot_tok[:, e:e + 1] == slot_ids
        idx_ref[0, e:e + 1, :] = jnp.sum(jnp.where(hit, tok_ids, 0.0), axis=0, keepdims=True).astype(jnp.int32)
        score_ref[0, e:e + 1, :] = jnp.sum(jnp.where(hit, s_tok[:, e:e + 1], 0.0), axis=0, keepdims=True)


def _router_kernel(lg_ref, u_ref, il_ref, sl_ref, ic_ref, sc_ref):
    lg = lg_ref[0]
    valid = lax.broadcasted_iota(jnp.int32, (1, lg.shape[1]), 1) < N_EXPERTS
    x = jnp.where(valid, lg, -jnp.inf)
    e = jnp.where(valid, jnp.exp(x - jnp.max(x, axis=-1, keepdims=True)), 0.0)
    s = e / jnp.sum(e, axis=-1, keepdims=True)
    s_exp = s.T[:N_EXPERTS]
    _route_stream(s[:SEQ], s_exp[:, :SEQ], u_ref, CAP_LAT, il_ref, sl_ref)
    _route_stream(s[SEQ:], s_exp[:, SEQ:], u_ref, CAP_CTX, ic_ref, sc_ref)


def _router(logits, before):
    out = lambda cap: pl.BlockSpec((1, N_EXPERTS, cap), lambda b: (b, 0, 0))
    sd = lambda cap, dt: jax.ShapeDtypeStruct((BATCH, N_EXPERTS, cap), dt)
    return pl.pallas_call(
        _router_kernel,
        grid=(BATCH,),
        in_specs=[pl.BlockSpec((1, S_ROWS, LANES), lambda b: (b, 0, 0)),
                  pl.BlockSpec((SEQ, SEQ), lambda b: (0, 0))],
        out_specs=[out(CAP_LAT), out(CAP_LAT), out(CAP_CTX), out(CAP_CTX)],
        out_shape=[sd(CAP_LAT, jnp.int32), sd(CAP_LAT, F32), sd(CAP_CTX, jnp.int32), sd(CAP_CTX, F32)],
        compiler_params=_cparams(("arbitrary",), 52),
        name="router",
    )(logits, before)


def _moe_layer(xs, g_norm, mod, w_router, w1, w3, w2, layer, before):
    wr = jnp.zeros((D_MODEL, LANES), F32).at[:, :N_EXPERTS].set(w_router)
    hp, logits = _norm_route(xs.reshape(M_ROWS, D_MODEL), g_norm, mod, wr.astype(BF16))
    idx_l, top_l, idx_c, top_c = _router(logits.reshape(BATCH, S_ROWS, LANES), before)
    base = (jnp.arange(BATCH, dtype=jnp.int32) * S_ROWS)[:, None, None]

    def per_expert(t):
        return jnp.swapaxes(t, 0, 1).reshape(N_EXPERTS, -1)

    rows = jnp.concatenate([per_expert(idx_l + base), per_expert(idx_c + base + SEQ)], axis=1)
    score = jnp.concatenate([per_expert(top_l), per_expert(top_c)], axis=1)
    xe = _row_gather(hp, rows.reshape(-1))
    y = _ffn(xe.reshape(N_EXPERTS, EXP_ROWS, D_MODEL // 2), w1, w3, w2, layer, score[..., None])
    return _moe_combine(xs, y, idx_l.reshape(BATCH, 1, -1), idx_c.reshape(BATCH, 1, -1), mod)


def kernel(x, c, ctx, c_ctx, w_ada, b_ada, norm_mix, norm_ffn, norm_final, gla_w_in, gla_w_a1, gla_w_a2, gla_b_a, gla_head_norm, gla_w_out, hy_w_in, hy_b_in, hy_conv_w, hy_conv_b, hy_f_w1, hy_f_b1, hy_f_w2, hy_f_b2, hy_f_w3, hy_f_b3, hy_f_w4, hy_skip, hy_w_out, hy_b_out, moe_router, moe_w1, moe_w3, moe_w2):
    cond = jnp.zeros((N_GROUPS, D_MODEL), F32).at[:BATCH].set(c).at[CTX_GROUP].set(c_ctx)
    mods = _ada(cond, w_ada, b_ada).reshape(DEPTH, N_GROUPS, N_ADA, D_MODEL)
    xs = jnp.concatenate([x, ctx], axis=1)
    tok = jnp.arange(SEQ, dtype=jnp.int32)
    before = (tok[:, None] < tok[None, :]).astype(BF16)
    for i in range(DEPTH):
        mod = mods[i]
        j = i // 2
        if i % 2 == 0:
            xs = _gla_layer(xs, norm_mix[i], mod, gla_w_in[j], gla_w_a1[j], gla_w_a2[j], gla_b_a[j], gla_head_norm[j],
                            gla_w_out[j])
        else:
            fw = (hy_f_w1[j], hy_f_b1[j], hy_f_w2[j], hy_f_b2[j], hy_f_w3[j], hy_f_b3[j], hy_f_w4[j])
            xs = _hyena_layer(xs, norm_mix[i], mod, hy_w_in[j], hy_b_in[j], hy_conv_w[j], hy_conv_b[j], fw, hy_skip[j],
                              hy_w_out[j], hy_b_out[j])
        xs = _moe_layer(xs, norm_ffn[i], mod, moe_router[i], moe_w1, moe_w3, moe_w2, i, before)
    return _final_norm(xs, norm_final)
```
